```python
import math
import jax, jax.numpy as jnp
from jax import lax
import numpy as np

D_MODEL = 1024
BATCH = 32
SEQ = 256
DEPTH = 1
DEC_BATCH = 4
DEC_SEQ = 1024
PAST_LEN = 256

GRID_W = 64
NA_HEADS = 8
HEAD_DIM = 64
NA_WIDTH = NA_HEADS * HEAD_DIM
WIN_R = 8
WIN_C = 16
ROPE_BASE = 10000.0
DN_HEADS = 8
DN_DK = 64
DN_DV = 64
DN_QK_WIDTH = DN_HEADS * DN_DK
DN_V_WIDTH = DN_HEADS * DN_DV
DN_CONV_CH = 2 * DN_QK_WIDTH + DN_V_WIDTH
CONV_W = 5
CHUNK = 64
D_FF = 4 * D_MODEL
N_MOD = 6
EPS = 1e-6
NEG_INF = -1e30
IN_SIZES = (NA_WIDTH, NA_WIDTH, NA_WIDTH, DN_CONV_CH, 2 * DN_HEADS, 2 * DN_HEADS, DN_V_WIDTH, D_MODEL, D_MODEL)
IN_COLS = 3 * NA_WIDTH + DN_CONV_CH + 4 * DN_HEADS + DN_V_WIDTH + 2 * D_MODEL

kernel_name = 'neighbourhood_deltanet_flow_step'


def _rms(x, w):
    xf = x.astype(jnp.float32)
    y = xf * lax.rsqrt(jnp.mean(xf * xf, axis=-1, keepdims=True) + EPS)
    return (y * w.astype(jnp.float32)).astype(x.dtype)


def _l2n(x):
    xf = x.astype(jnp.float32)
    return xf * lax.rsqrt(jnp.sum(xf * xf, axis=-1, keepdims=True) + EPS)


def _adaln(cvec, w_ada, b_ada):
    m = jax.nn.silu(cvec) @ w_ada + b_ada
    m = m.reshape(m.shape[:-1] + (1, N_MOD, D_MODEL))
    return [m[..., i, :] for i in range(N_MOD)]


def _axial_rope(x):
    T = x.shape[1]
    pos = jnp.arange(T)
    half = HEAD_DIM // 2
    nf = half // 2
    inv = ROPE_BASE ** (-jnp.arange(nf, dtype=jnp.float32) / nf)

    def rot(xa, p):
        ang = p.astype(jnp.float32)[:, None] * inv
        cos = jnp.cos(ang)[None, :, None, :]
        sin = jnp.sin(ang)[None, :, None, :]
        x1, x2 = xa[..., :nf], xa[..., nf:]
        return jnp.concatenate([x1 * cos - x2 * sin, x1 * sin + x2 * cos], axis=-1)

    xf = x.astype(jnp.float32)
    out = jnp.concatenate([rot(xf[..., :half], pos // GRID_W), rot(xf[..., half:], pos % GRID_W)], axis=-1)
    return out.astype(x.dtype)


def _dwconv(x, w):
    return lax.conv_general_dilated(
        x, w[:, None, :].astype(x.dtype), window_strides=(1,),
        padding=[(CONV_W // 2, CONV_W // 2)],
        dimension_numbers=('NWC', 'WIO', 'NWC'),
        feature_group_count=x.shape[-1])


def _project(h, w_in, conv_w):
    B, T, _ = h.shape
    p = h @ w_in
    idx = [int(i) for i in np.cumsum(IN_SIZES)[:-1]]
    na_q, na_k, na_v, dn_qkv, dn_b, dn_a, dn_z, gate_na, gate_dn = jnp.split(p, idx, axis=-1)
    dn_qkv = jax.nn.silu(_dwconv(dn_qkv, conv_w))
    dq, dk, dv = jnp.split(dn_qkv, [DN_QK_WIDTH, 2 * DN_QK_WIDTH], axis=-1)
    na_q = na_q.reshape(B, T, NA_HEADS, HEAD_DIM)
    na_k = na_k.reshape(B, T, NA_HEADS, HEAD_DIM)
    na_v = na_v.reshape(B, T, NA_HEADS, HEAD_DIM)
    dq = _l2n(dq.reshape(B, T, DN_HEADS, DN_DK))
    dk = _l2n(dk.reshape(B, T, DN_HEADS, DN_DK))
    dv = dv.reshape(B, T, DN_HEADS, DN_DV)
    return na_q, na_k, na_v, dq, dk, dv, dn_b, dn_a, dn_z, gate_na, gate_dn


def _na_context(q, k, v):
    s = jnp.einsum('bqhd,bkhd->bhqk', q, k).astype(jnp.float32) * (HEAD_DIM ** -0.5)
    p = jax.nn.softmax(s, axis=-1).astype(v.dtype)
    return jnp.einsum('bhqk,bkhd->bqhd', p, v)


def _na_latent(q, k, v, ck, cv, rpb):
    B, T, H, Dh = q.shape
    rows = T // GRID_W
    kr = min(WIN_R, rows)
    r = np.arange(rows)
    rs = np.clip(r - kr // 2, 0, rows - kr)
    krow = rs[:, None] + np.arange(kr)
    col = np.arange(GRID_W)
    cs = np.clip(col - WIN_C // 2, 0, GRID_W - WIN_C)
    kcol = np.broadcast_to(col, (kr, GRID_W)).reshape(-1)
    valid = (kcol[None, :] >= cs[:, None]) & (kcol[None, :] < cs[:, None] + WIN_C)
    dr_idx = (np.repeat(krow, GRID_W, axis=1) - r[:, None]) + (WIN_R - 1)
    dc_idx = np.clip(kcol[None, :] - col[:, None] + (WIN_C - 1), 0, 2 * WIN_C - 2)
    bias = rpb[:, dr_idx[:, None, :], dc_idx[None, :, :]].astype(jnp.float32)
    K = kr * GRID_W
    qg = q.reshape(B, rows, GRID_W, H, Dh)
    kg = k.reshape(B, rows, GRID_W, H, Dh)[:, krow].reshape(B, rows, K, H, Dh)
    vg = v.reshape(B, rows, GRID_W, H, Dh)[:, krow].reshape(B, rows, K, H, Dh)
    scale = HEAD_DIM ** -0.5
    s_loc = jnp.einsum('brqhd,brkhd->bhrqk', qg, kg).astype(jnp.float32) * scale + bias[None]
    s_loc = jnp.where(valid[None, None, None], s_loc, NEG_INF)
    s_ctx = jnp.einsum('brqhd,bhkd->bhrqk', qg, ck).astype(jnp.float32) * scale
    p = jax.nn.softmax(jnp.concatenate([s_loc, s_ctx], axis=-1), axis=-1).astype(v.dtype)
    o = jnp.einsum('bhrqk,brkhd->brqhd', p[..., :K], vg) + jnp.einsum('bhrqk,bhkd->brqhd', p[..., K:], cv)
    return o.reshape(B, T, H, Dh)


def _gated_delta_chunked(q, k, v, g, beta, s0):
    B, T, H, DK = q.shape
    DV = v.shape[-1]
    n = T // CHUNK
    f32 = jnp.float32

    def blk(a):
        a = a.astype(f32).reshape((B, n, CHUNK, H) + a.shape[3:])
        return jnp.moveaxis(a, (1, 3), (0, 2))

    qb = blk(q) * (DK ** -0.5)
    kb, vb, gb, bb = blk(k), blk(v), blk(g), blk(beta)
    gc = jnp.cumsum(gb, axis=-1)
    tril = jnp.tril(jnp.ones((CHUNK, CHUNK), bool))
    strict = jnp.tril(jnp.ones((CHUNK, CHUNK), bool), -1)
    diff = gc[..., :, None] - gc[..., None, :]
    decay = jnp.where(tril, jnp.exp(jnp.where(tril, diff, 0.0)), 0.0)
    kbeta = kb * bb[..., None]
    a_mat = jnp.where(strict, jnp.einsum('nbhid,nbhjd->nbhij', kbeta, kb) * decay, 0.0) + jnp.eye(CHUNK, dtype=f32)
    rhs = jnp.concatenate([vb * bb[..., None], kbeta * jnp.exp(gc)[..., None]], axis=-1)
    sol = lax.linalg.triangular_solve(a_mat, rhs, left_side=True, lower=True, unit_diagonal=True)
    u, w = sol[..., :DV], sol[..., DV:]
    qk = jnp.einsum('nbhid,nbhjd->nbhij', qb, kb) * decay

    def step(S, xs):
        qc, kc, uc, wc, gcc, qkc = xs
        v_new = uc - jnp.einsum('bhck,bhkv->bhcv', wc, S)
        o = (jnp.einsum('bhck,bhkv->bhcv', qc * jnp.exp(gcc)[..., None], S)
             + jnp.einsum('bhij,bhjv->bhiv', qkc, v_new))
        glast = gcc[..., -1:]
        S = S * jnp.exp(glast)[..., None] + jnp.einsum('bhck,bhcv->bhkv', kc * jnp.exp(glast - gcc)[..., None], v_new)
        return S, o

    s_fin, o = lax.scan(step, s0.astype(f32), (qb, kb, u, w, gc, qk))
    o = jnp.moveaxis(o, (0, 2), (1, 3)).reshape(B, T, H, DV)
    return o.astype(v.dtype), s_fin


def _delta_dir(q, k, v, b_logit, a_logit, a_log, dt_bias, s0, reverse):
    beta = jax.nn.sigmoid(b_logit.astype(jnp.float32))
    g = -jnp.exp(a_log.astype(jnp.float32)) * jax.nn.softplus(a_logit.astype(jnp.float32) + dt_bias.astype(jnp.float32))
    if reverse:
        q, k, v, beta, g = (jnp.flip(t, axis=1) for t in (q, k, v, beta, g))
    o, s = _gated_delta_chunked(q, k, v, g, beta, s0)
    if reverse:
        o = jnp.flip(o, axis=1)
    return o, s


def _merge(o_na, o_dn, z, gate_na, gate_dn, dn_norm, w_ao, w_do, w_out):
    B, T = o_na.shape[:2]
    o_dn = _rms(o_dn, dn_norm) * jax.nn.silu(z).reshape(B, T, DN_HEADS, DN_DV)
    br_na = o_na.reshape(B, T, NA_WIDTH) @ w_ao
    br_dn = o_dn.reshape(B, T, DN_V_WIDTH) @ w_do
    m = jax.nn.sigmoid(gate_na) * br_na + jax.nn.sigmoid(gate_dn) * br_dn
    return m @ w_out


def _mixer_context(h, w_in, conv_w, a_log, dt_bias, dn_norm, w_ao, w_do, w_out):
    q, k, v, dq, dk, dv, b, a, z, ga, gd = _project(h, w_in, conv_w)
    o_na = _na_context(q, k, v)
    s0 = jnp.zeros((h.shape[0], DN_HEADS, DN_DK, DN_DV), jnp.float32)
    o_f, s_f = _delta_dir(dq, dk, dv, b[..., :DN_HEADS], a[..., :DN_HEADS], a_log[0], dt_bias[0], s0, False)
    o_b, s_b = _delta_dir(dq, dk, dv, b[..., DN_HEADS:], a[..., DN_HEADS:], a_log[1], dt_bias[1], s0, True)
    y = _merge(o_na, o_f + o_b, z, ga, gd, dn_norm, w_ao, w_do, w_out)
    return y, jnp.transpose(k, (0, 2, 1, 3)), jnp.transpose(v, (0, 2, 1, 3)), jnp.stack([s_f, s_b], axis=1)


def _mixer_latent(h, ck, cv, st, w_in, conv_w, a_log, dt_bias, dn_norm, rpb, w_ao, w_do, w_out):
    q, k, v, dq, dk, dv, b, a, z, ga, gd = _project(h, w_in, conv_w)
    o_na = _na_latent(_axial_rope(q), _axial_rope(k), v, ck, cv, rpb)
    s = st.astype(jnp.float32)
    o_f, _ = _delta_dir(dq, dk, dv, b[..., :DN_HEADS], a[..., :DN_HEADS], a_log[0], dt_bias[0], s[:, 0], False)
    o_b, _ = _delta_dir(dq, dk, dv, b[..., DN_HEADS:], a[..., DN_HEADS:], a_log[1], dt_bias[1], s[:, 1], True)
    return _merge(o_na, o_f + o_b, z, ga, gd, dn_norm, w_ao, w_do, w_out)


def _ffn_sub(x, shift, scale, gate, w_pre, w_post, w1, w2):
    h = _rms(x, w_pre) * (1 + scale) + shift
    f = jnp.square(jax.nn.relu(h @ w1)) @ w2
    return x + gate * _rms(f, w_post)


def setup_inputs(seed: int = 0) -> dict:
    key = jax.random.key(seed)
    ks = jax.random.split(key, 24)
    f32 = jnp.float32

    def nrm(k, shape, s):
        return jax.random.normal(k, shape, f32) * s

    dt = jnp.exp(jax.random.uniform(ks[16], (DEPTH, 2, DN_HEADS), f32, math.log(1e-3), math.log(1e-1)))
    return {
        'x_prompt': nrm(ks[0], (BATCH, SEQ, D_MODEL), 1.0),
        'x_sample': nrm(ks[1], (DEC_BATCH, DEC_SEQ, D_MODEL), 1.0),
        'c': nrm(ks[2], (DEC_BATCH, D_MODEL), 1.0),
        'cache_na_k': nrm(ks[3], (DEC_BATCH, DEPTH, NA_HEADS, PAST_LEN, HEAD_DIM), 1.0),
        'cache_na_v': nrm(ks[4], (DEC_BATCH, DEPTH, NA_HEADS, PAST_LEN, HEAD_DIM), 1.0),
        'state_delta': nrm(ks[5], (DEC_BATCH, DEPTH, 2, DN_HEADS, DN_DK, DN_DV), DN_DK ** -0.5),
        'c_ctx': nrm(ks[6], (D_MODEL,), 1.0),
        'w_ada': nrm(ks[7], (DEPTH, D_MODEL, N_MOD * D_MODEL), 0.5 * D_MODEL ** -0.5),
        'b_ada': nrm(ks[8], (DEPTH, N_MOD * D_MODEL), 0.02),
        'norm_pre1': 1.0 + nrm(ks[9], (DEPTH, D_MODEL), 0.05),
        'norm_post1': 1.0 + nrm(ks[10], (DEPTH, D_MODEL), 0.05),
        'norm_pre2': 1.0 + nrm(ks[11], (DEPTH, D_MODEL), 0.05),
        'norm_post2': 1.0 + nrm(ks[12], (DEPTH, D_MODEL), 0.05),
        'w_in': nrm(ks[13], (DEPTH, D_MODEL, IN_COLS), D_MODEL ** -0.5),
        'conv_w': nrm(ks[14], (DEPTH, CONV_W, DN_CONV_CH), CONV_W ** -0.5),
        'a_log': jnp.log(jax.random.uniform(ks[15], (DEPTH, 2, DN_HEADS), f32, 1.0, 16.0)),
        'dt_bias': dt + jnp.log(-jnp.expm1(-dt)),
        'dn_norm': 1.0 + nrm(ks[17], (DEPTH, DN_DV), 0.05),
        'na_rpb': nrm(ks[18], (DEPTH, NA_HEADS, 2 * WIN_R - 1, 2 * WIN_C - 1), 0.02),
        'w_ao': nrm(ks[19], (DEPTH, NA_WIDTH, D_MODEL), NA_WIDTH ** -0.5),
        'w_do': nrm(ks[20], (DEPTH, DN_V_WIDTH, D_MODEL), DN_V_WIDTH ** -0.5),
        'w_out': nrm(ks[21], (DEPTH, D_MODEL, D_MODEL), D_MODEL ** -0.5),
        'w_ff1': nrm(ks[22], (DEPTH, D_MODEL, D_FF), D_MODEL ** -0.5),
        'w_ff2': nrm(ks[23], (DEPTH, D_FF, D_MODEL), D_FF ** -0.5),
    }


def reference(x_prompt, x_sample, c, cache_na_k, cache_na_v, state_delta, c_ctx,
              w_ada, b_ada, norm_pre1, norm_post1, norm_pre2, norm_post2,
              w_in, conv_w, a_log, dt_bias, dn_norm, na_rpb,
              w_ao, w_do, w_out, w_ff1, w_ff2):
    xp = x_prompt
    xs = x_sample
    new_k, new_v, new_s = [], [], []
    for l in range(DEPTH):
        sh1, sc1, g1, sh2, sc2, g2 = _adaln(c_ctx, w_ada[l], b_ada[l])
        h = _rms(xp, norm_pre1[l]) * (1 + sc1) + sh1
        y, kc, vc, st = _mixer_context(h, w_in[l], conv_w[l], a_log[l], dt_bias[l], dn_norm[l],
                                       w_ao[l], w_do[l], w_out[l])
        xp = xp + g1 * _rms(y, norm_post1[l])
        xp = _ffn_sub(xp, sh2, sc2, g2, norm_pre2[l], norm_post2[l], w_ff1[l], w_ff2[l])
        new_k.append(kc)
        new_v.append(vc)
        new_s.append(st.astype(x_prompt.dtype))
        sh1, sc1, g1, sh2, sc2, g2 = _adaln(c, w_ada[l], b_ada[l])
        h = _rms(xs, norm_pre1[l]) * (1 + sc1) + sh1
        y = _mixer_latent(h, cache_na_k[:, l], cache_na_v[:, l], state_delta[:, l], w_in[l], conv_w[l],
                          a_log[l], dt_bias[l], dn_norm[l], na_rpb[l], w_ao[l], w_do[l], w_out[l])
        xs = xs + g1 * _rms(y, norm_post1[l])
        xs = _ffn_sub(xs, sh2, sc2, g2, norm_pre2[l], norm_post2[l], w_ff1[l], w_ff2[l])
    new_cache_na_k = jnp.stack(new_k, axis=1)
    new_cache_na_v = jnp.stack(new_v, axis=1)
    new_state_delta = jnp.stack(new_s, axis=1)
    return (xp, xs, new_cache_na_k, new_cache_na_v, new_state_delta)
```

```python
import functools
import math

import numpy as np
import jax
import jax.numpy as jnp
from jax import lax
from jax.experimental import pallas as pl
from jax.experimental.pallas import tpu as pltpu

F32 = jnp.float32
BF16 = jnp.bfloat16

D_MODEL = 1024
N_MOD = 6
GRID_W = 64
NA_HEADS = 8
HEAD_DIM = 64
NA_WIDTH = NA_HEADS * HEAD_DIM
WIN_R = 8
WIN_C = 16
ROPE_BASE = 10000.0
DN_HEADS = 8
DN_DK = 64
DN_DV = 64
CONV_W = 5
CHUNK = 64
D_FF = 4 * D_MODEL
EPS = 1e-6
NEG_INF = -1e30

LANES = 128
N_PAIRS = NA_HEADS // 2
TM = 256
VMEM_LIMIT = 56 * 1024 * 1024

COL_Q, COL_K, COL_V = 0, 512, 1024
COL_DQ, COL_DK, COL_DV = 1536, 2048, 2560
COL_GNA, COL_GDN, COL_Z, COL_BA = 3072, 4096, 5120, 5632
P_COLS = 5760


def _params(n_grid):
    return pltpu.CompilerParams(dimension_semantics=("arbitrary",) * n_grid,
                                vmem_limit_bytes=VMEM_LIMIT)


def _bdot(a, b):
    return jnp.dot(a, b, preferred_element_type=F32)


def _split3(a):
    hi = a.astype(BF16)
    r1 = a - hi.astype(F32)
    mid = r1.astype(BF16)
    lo = (r1 - mid.astype(F32)).astype(BF16)
    return hi, mid, lo


def _dot3_l(a, b_exact):
    hi, mid, lo = _split3(a)
    return _bdot(hi, b_exact) + _bdot(mid, b_exact) + _bdot(lo, b_exact)


def _dot3_r(a_exact, b):
    hi, mid, lo = _split3(b)
    return _bdot(a_exact, hi) + _bdot(a_exact, mid) + _bdot(a_exact, lo)


def _sigmoid(x):
    return 1.0 / (1.0 + jnp.exp(-x))


def _silu(x):
    return x * _sigmoid(x)


def _softplus(x):
    return jnp.maximum(x, 0.0) + jnp.log(1.0 + jnp.exp(-jnp.abs(x)))


def _rms_rows(x, w):
    ms = jnp.mean(x * x, axis=-1, keepdims=True)
    return x * lax.rsqrt(ms + EPS) * w


def _adaln_kernel(c_ref, w_ref, b_ref, o_ref):
    c = c_ref[...]
    s = _silu(c).astype(BF16)
    o_ref[...] = _bdot(s, w_ref[...].astype(BF16)) + b_ref[...]


def _adaln(cvecs, w_ada, b_ada):
    tn = 1024
    n = w_ada.shape[1]
    return pl.pallas_call(
        _adaln_kernel,
        grid=(n // tn,),
        in_specs=[pl.BlockSpec((8, D_MODEL), lambda j: (0, 0)),
                  pl.BlockSpec((D_MODEL, tn), lambda j: (0, j)),
                  pl.BlockSpec((1, tn), lambda j: (0, j))],
        out_specs=pl.BlockSpec((8, tn), lambda j: (0, j)),
        out_shape=jax.ShapeDtypeStruct((8, n), F32),
        compiler_params=_params(1),
        name="adaln",
    )(cvecs, w_ada, b_ada.reshape(1, n))


_PROJ_BOUNDS = (0, 1024, 2304, 3456, 4608, 5760)


def _inproj_kernel(*refs, rope):
    if rope:
        x_ref, mod_ref, wpre_ref, w_ref, cos_ref, sin_ref, o_ref = refs
    else:
        x_ref, mod_ref, wpre_ref, w_ref, o_ref = refs
    x = x_ref[...]
    mod = mod_ref[0]
    h = _rms_rows(x, wpre_ref[...]) * (1.0 + mod[1:2]) + mod[0:1]
    hb = h.astype(BF16)
    for lo, hi in zip(_PROJ_BOUNDS[:-1], _PROJ_BOUNDS[1:]):
        p = _bdot(hb, w_ref[:, lo:hi])
        if rope and lo == 0:
            lane = lax.broadcasted_iota(jnp.int32, p.shape, 1)
            first = (lane % 32) < 16
            width = hi - lo
            partner = jnp.where(first, pltpu.roll(p, width - 16, 1), pltpu.roll(p, 16, 1))
            p = p * cos_ref[...] + partner * sin_ref[...]
        o_ref[:, lo:hi] = p


def _inproj(x2d, mod3, mod_row_of_tile, wpre, w_packed, rope_tabs=None):
    n_tok = x2d.shape[0]
    rope = rope_tabs is not None
    in_specs = [pl.BlockSpec((TM, D_MODEL), lambda i: (i, 0)),
                pl.BlockSpec((1, N_MOD, D_MODEL), lambda i: (mod_row_of_tile(i), 0, 0)),
                pl.BlockSpec((1, D_MODEL), lambda i: (0, 0)),
                pl.BlockSpec((D_MODEL, P_COLS), lambda i: (0, 0))]
    args = [x2d, mod3, wpre, w_packed]
    if rope:
        tiles_per_seq = rope_tabs[0].shape[0] // TM
        for t in rope_tabs:
            in_specs.append(pl.BlockSpec((TM, 1024), lambda i: (i % tiles_per_seq, 0)))
            args.append(t)
    return pl.pallas_call(
        functools.partial(_inproj_kernel, rope=rope),
        grid=(n_tok // TM,),
        in_specs=in_specs,
        out_specs=pl.BlockSpec((TM, P_COLS), lambda i: (i, 0)),
        out_shape=jax.ShapeDtypeStruct((n_tok, P_COLS), F32),
        compiler_params=_params(1),
        name="inproj_lat" if rope else "inproj_ctx",
    )(*args)


def _rope_tables(seq):
    half = HEAD_DIM // 2
    nf = half // 2
    pos = np.arange(seq)
    inv = (np.float32(ROPE_BASE) ** (-np.arange(nf, dtype=np.float32) / np.float32(nf))).astype(np.float32)
    ang_r = ((pos // GRID_W).astype(np.float32)[:, None] * inv).astype(np.float64)
    ang_c = ((pos % GRID_W).astype(np.float32)[:, None] * inv).astype(np.float64)

    def grp(ang):
        return (np.concatenate([np.cos(ang), np.cos(ang)], 1),
                np.concatenate([-np.sin(ang), np.sin(ang)], 1))

    cr, sr = grp(ang_r)
    cc, sc = grp(ang_c)
    cos_h = np.concatenate([cr, cc], 1)
    sin_h = np.concatenate([sr, sc], 1)
    reps = 2 * NA_HEADS
    return (jnp.asarray(np.tile(cos_h, (1, reps)), F32),
            jnp.asarray(np.tile(sin_h, (1, reps)), F32))


def _ctx_attn_kernel(q_ref, k_ref, v_ref, o_ref, kc_ref, vc_ref):
    lane = lax.broadcasted_iota(jnp.int32, (1, LANES), 1)
    scale = HEAD_DIM ** -0.5
    for hp in range(N_PAIRS):
        sl = slice(hp * LANES, (hp + 1) * LANES)
        q = q_ref[:, sl]
        k = k_ref[:, sl]
        v = v_ref[:, sl]
        kb = k.astype(BF16)
        acc = None
        for e in range(2):
            m_e = (lane >= 64) if e else (lane < 64)
            qe = jnp.where(m_e, q * scale, 0.0).astype(BF16)
            ve = jnp.where(m_e, v, 0.0).astype(BF16)
            s = lax.dot_general(qe, kb, (((1,), (1,)), ((), ())), preferred_element_type=F32)
            s = s - jnp.max(s, axis=-1, keepdims=True)
            p = jnp.exp(s)
            l = jnp.sum(p, axis=-1, keepdims=True)
            o = _bdot(p.astype(BF16), ve) / l
            acc = o if acc is None else acc + o
            kc_ref[0, 0, 2 * hp + e] = k[:, e * 64:(e + 1) * 64]
            vc_ref[0, 0, 2 * hp + e] = v[:, e * 64:(e + 1) * 64]
        o_ref[:, sl] = acc


def _ctx_attn(p_ctx, batch, seq):
    def col(cb):
        return pl.BlockSpec((seq, NA_WIDTH), lambda b: (b, cb))
    cache_shape = jax.ShapeDtypeStruct((batch, 1, NA_HEADS, seq, HEAD_DIM), F32)
    cache_spec = pl.BlockSpec((1, 1, NA_HEADS, seq, HEAD_DIM), lambda b: (b, 0, 0, 0, 0))
    return pl.pallas_call(
        _ctx_attn_kernel,
        grid=(batch,),
        in_specs=[col(COL_Q // NA_WIDTH), col(COL_K // NA_WIDTH), col(COL_V // NA_WIDTH)],
        out_specs=[pl.BlockSpec((seq, NA_WIDTH), lambda b: (b, 0)), cache_spec, cache_spec],
        out_shape=[jax.ShapeDtypeStruct((batch * seq, NA_WIDTH), F32), cache_shape, cache_shape],
        compiler_params=_params(1),
        name="ctx_attn",
    )(p_ctx, p_ctx, p_ctx)


Q_HALF = 512
K_WIN = 768


def _lat_attn_kernel(rpb_ref, q_ref, k_ref, v_ref, ck_ref, cv_ref, o_ref, bias_ref, *, rows):
    hp = pl.program_id(0)
    b = pl.program_id(1)
    half = pl.program_id(2)
    kr_n = min(WIN_R, rows)
    rows_half = Q_HALF // GRID_W
    krows_win = K_WIN // GRID_W

    @pl.when(jnp.logical_and(b == 0, half == 0))
    def _build_bias():
        qc = lax.broadcasted_iota(jnp.int32, (GRID_W, LANES), 0)
        ln = lax.broadcasted_iota(jnp.int32, (GRID_W, LANES), 1)
        kc = ln % GRID_W
        cs = jnp.clip(qc - WIN_C // 2, 0, GRID_W - WIN_C)
        valid = jnp.logical_and(kc >= cs, kc < cs + WIN_C)
        dc = jnp.clip(kc - qc + (WIN_C - 1), 0, 2 * WIN_C - 2)
        neg = jnp.full((GRID_W, LANES), NEG_INF, F32)
        for e in range(2):
            h = 2 * hp + e
            blocks = []
            for a in range(2 * WIN_R - 1):
                def body(j, acc, a=a, h=h):
                    return jnp.where(dc == j, rpb_ref[(h * (2 * WIN_R - 1) + a) * (2 * WIN_C - 1) + j], acc)
                t = lax.fori_loop(0, 2 * WIN_C - 1, body, jnp.zeros((GRID_W, LANES), F32))
                blocks.append(jnp.where(valid, t, NEG_INF))
            for hf in range(2):
                for rl in range(rows_half):
                    r = hf * rows_half + rl
                    rs = min(max(r - kr_n // 2, 0), rows - kr_n)
                    for kp in range(krows_win // 2):
                        parts = []
                        for kl in (2 * kp, 2 * kp + 1):
                            kr = kl + hf * (rows - krows_win)
                            if rs <= kr < rs + kr_n:
                                parts.append(blocks[kr - r + WIN_R - 1])
                            else:
                                parts.append(neg)
                        blk = jnp.where(ln < GRID_W, parts[0], parts[1])
                        bias_ref[e, hf, rl * GRID_W:(rl + 1) * GRID_W, kp * LANES:(kp + 1) * LANES] = blk

    lane = lax.broadcasted_iota(jnp.int32, (1, LANES), 1)
    scale = HEAD_DIM ** -0.5
    k0 = pl.multiple_of(half * ((rows - krows_win) * GRID_W), GRID_W)
    kw = k_ref[pl.ds(k0, K_WIN), :].astype(BF16)
    vw = v_ref[pl.ds(k0, K_WIN), :]
    ck = ck_ref[0].astype(BF16)
    cv = cv_ref[0]
    SUB = 256
    for qs in range(Q_HALF // SUB):
        q = q_ref[qs * SUB:(qs + 1) * SUB, :] * scale
        acc = None
        for e in range(2):
            m_e = (lane >= 64) if e else (lane < 64)
            qe = jnp.where(m_e, q, 0.0).astype(BF16)
            s_loc = lax.dot_general(qe, kw, (((1,), (1,)), ((), ())), preferred_element_type=F32)
            s_loc = s_loc + bias_ref[e, half, qs * SUB:(qs + 1) * SUB, :]
            s_ctx = lax.dot_general(qe, ck, (((1,), (1,)), ((), ())), preferred_element_type=F32)
            m = jnp.maximum(jnp.max(s_loc, axis=-1, keepdims=True), jnp.max(s_ctx, axis=-1, keepdims=True))
            p_loc = jnp.exp(s_loc - m)
            p_ctx = jnp.exp(s_ctx - m)
            l = jnp.sum(p_loc, axis=-1, keepdims=True) + jnp.sum(p_ctx, axis=-1, keepdims=True)
            ve = jnp.where(m_e, vw, 0.0).astype(BF16)
            cve = jnp.where(m_e, cv, 0.0).astype(BF16)
            o = (_bdot(p_loc.astype(BF16), ve) + _bdot(p_ctx.astype(BF16), cve)) / l
            acc = o if acc is None else acc + o
        o_ref[qs * SUB:(qs + 1) * SUB, :] = acc


def _lat_attn(p_lat, ck_tm, cv_tm, rpb, batch, seq):
    rows = seq // GRID_W
    assert rows * GRID_W == seq and rows - K_WIN // GRID_W == 4 and Q_HALF * 2 == seq
    halves = seq // Q_HALF
    past = ck_tm.shape[1]
    kernel = functools.partial(_lat_attn_kernel, rows=rows)
    return pl.pallas_call(
        kernel,
        grid=(N_PAIRS, batch, halves),
        in_specs=[pl.BlockSpec(memory_space=pltpu.SMEM),
                  pl.BlockSpec((Q_HALF, LANES), lambda hp, b, hf: (b * halves + hf, COL_Q // LANES + hp)),
                  pl.BlockSpec((seq, LANES), lambda hp, b, hf: (b, COL_K // LANES + hp)),
                  pl.BlockSpec((seq, LANES), lambda hp, b, hf: (b, COL_V // LANES + hp)),
                  pl.BlockSpec((1, past, LANES), lambda hp, b, hf: (b, 0, hp)),
                  pl.BlockSpec((1, past, LANES), lambda hp, b, hf: (b, 0, hp))],
        out_specs=pl.BlockSpec((Q_HALF, LANES), lambda hp, b, hf: (b * halves + hf, hp)),
        out_shape=jax.ShapeDtypeStruct((batch * seq, NA_WIDTH), F32),
        scratch_shapes=[pltpu.VMEM((2, 2, Q_HALF, K_WIN), F32)],
        compiler_params=_params(3),
        name="lat_attn",
    )(rpb.reshape(-1), p_lat, p_lat, p_lat, ck_tm, cv_tm)


def _dn_kernel(*refs, seq, has_s0, emit_state):
    it = iter(refs)
    dq_ref, dk_ref, dv_ref, ba_ref = next(it), next(it), next(it), next(it)
    cwq_ref, cwk_ref, cwv_ref, cst_ref = next(it), next(it), next(it), next(it)
    s0_ref = next(it) if has_s0 else None
    o_ref = next(it)
    st_ref = next(it) if emit_state else None
    (pad_s, q_s, k_s, v_s, beta_s, g_s, u_s, wq_s, lb_s, el_s, oacc_s) = it

    hp = pl.program_id(1)
    n = seq // CHUNK

    r128 = lax.broadcasted_iota(jnp.int32, (LANES, LANES), 0)
    c128 = lax.broadcasted_iota(jnp.int32, (LANES, LANES), 1)
    bd_mask = (r128 // 64) == (c128 // 64)
    bd_ones = jnp.where(bd_mask, 1.0, 0.0).astype(BF16)

    zeros8 = jnp.zeros((8, LANES), F32)
    pad_s[0:8, :] = zeros8
    pad_s[seq + 8:seq + 16, :] = zeros8

    def conv_silu(x_ref, w_ref):
        pad_s[8:seq + 8, :] = x_ref[...]
        acc = None
        for j in range(CONV_W):
            term = pad_s[8 - CONV_W // 2 + j: 8 - CONV_W // 2 + j + seq, :] * w_ref[j:j + 1, :]
            acc = term if acc is None else acc + term
        return _silu(acc)

    def l2n(x):
        ss = _dot3_l(x * x, bd_ones)
        return x * lax.rsqrt(ss + EPS)

    q_s[...] = l2n(conv_silu(dq_ref, cwq_ref))
    k_s[...] = l2n(conv_silu(dk_ref, cwk_ref))
    v_s[...] = conv_silu(dv_ref, cwv_ref)

    ba = ba_ref[...]
    beta_all = _sigmoid(ba)
    g_all = -jnp.exp(cst_ref[0:1, :]) * _softplus(ba + cst_ref[1:2, :])
    e_col = c128 // 64
    for d in range(2):
        sel_b = jnp.where(r128 == d * DN_HEADS + 2 * hp + e_col, 1.0, 0.0).astype(BF16)
        sel_g = jnp.where(r128 == 2 * DN_HEADS + d * DN_HEADS + 2 * hp + e_col, 1.0, 0.0).astype(BF16)
        beta_s[d] = _dot3_l(beta_all, sel_b)
        g_s[d] = _dot3_l(g_all, sel_g)

    ri = lax.broadcasted_iota(jnp.int32, (CHUNK, LANES), 0)
    li = lax.broadcasted_iota(jnp.int32, (CHUNK, LANES), 1)
    lj = li % 64
    lo_half = li < 64
    diag = ri == lj
    incl = (ri >= lj, ri <= lj)
    strict = (ri > lj, ri < lj)
    r64 = lax.broadcasted_iota(jnp.int32, (CHUNK, CHUNK), 0)
    c64 = lax.broadcasted_iota(jnp.int32, (CHUNK, CHUNK), 1)
    tri = (jnp.where(c64 <= r64, 1.0, 0.0).astype(BF16), jnp.where(c64 >= r64, 1.0, 0.0).astype(BF16))
    li2 = lax.broadcasted_iota(jnp.int32, (CHUNK, 2 * LANES), 1)
    lo_half2 = (li2 % LANES) < 64

    off_blk = ([], [])
    for lvl in range(6):
        s = 1 << lvl
        same2 = (ri // (2 * s)) == (lj // (2 * s))
        off_blk[0].append(jnp.logical_and(same2, (ri // s) == (lj // s) + 1))
        off_blk[1].append(jnp.logical_and(same2, (lj // s) == (ri // s) + 1))

    def stack_f32(x, lo_mask):
        return jnp.concatenate([jnp.where(lo_mask, x, 0.0), jnp.where(lo_mask, 0.0, x)], axis=0)

    def stack_heads(x, lo_mask):
        return stack_f32(x, lo_mask).astype(BF16)

    def pair_dot(a, b, lo_mask):
        a_hi = a.astype(BF16)
        a_lo = (a - a_hi.astype(F32)).astype(BF16)
        bs = stack_f32(b, lo_mask)
        b_hi = bs.astype(BF16)
        b_lo = (bs - b_hi.astype(F32)).astype(BF16)
        return _bdot(a_hi, b_hi) + _bdot(a_hi, b_lo) + _bdot(a_lo, b_hi)

    def phase_a(c, carry):
        rows = pl.ds(pl.multiple_of(c * CHUNK, CHUNK), CHUNK)
        qc = q_s[rows, :] * (DN_DK ** -0.5)
        kc = k_s[rows, :]
        vc = v_s[rows, :]
        ktt = jnp.transpose(jnp.concatenate([kc, kc], axis=0))
        kt_bd = jnp.where(bd_mask, ktt, 0.0)
        kt_bd_b = kt_bd.astype(BF16)
        gcs, decs, kbetas, rrows = [], [], [], []
        for d in range(2):
            bt = beta_s[d, rows, :]
            g = g_s[d, rows, :]
            gc = _dot3_r(tri[d], g)
            rrow = jnp.sum(jnp.where(diag, gc, 0.0), axis=0, keepdims=True)
            dec = jnp.exp(jnp.minimum(gc - rrow, 0.0))
            gcs.append(gc); decs.append(dec); kbetas.append(kc * bt); rrows.append(rrow)
        lhs1 = jnp.concatenate([qc, kbetas[0], kbetas[1]], axis=0).astype(BF16)
        m1 = _bdot(lhs1, kt_bd_b)
        qk_raw = m1[0:CHUNK]
        for d in range(2):
            gc, dec, kbeta, rrow = gcs[d], decs[d], kbetas[d], rrows[d]
            bt = beta_s[d, rows, :]
            amat = jnp.where(strict[d], m1[CHUNK * (1 + d):CHUNK * (2 + d)] * dec, 0.0)
            eg = jnp.exp(gc)
            rhs = jnp.concatenate([vc * bt, kbeta * eg], axis=1)
            tmat = jnp.where(diag, 1.0, 0.0) - jnp.where(off_blk[d][0], amat, 0.0)
            for lvl in range(1, len(off_blk[d])):
                tmp = pair_dot(jnp.where(off_blk[d][lvl], amat, 0.0), tmat, lo_half)
                tmat = tmat - pair_dot(tmat, tmp, lo_half)
            x = _bdot(tmat.astype(BF16), stack_heads(rhs, lo_half2))
            u = x[:, :LANES]
            w = x[:, LANES:]
            qk = jnp.where(incl[d], qk_raw * dec, 0.0)
            qg = qc * eg
            glast = gc[CHUNK - 1:CHUNK, :] if d == 0 else gc[0:1, :]
            kd_bd = kt_bd * jnp.exp(glast - rrow)
            u_s[d, c] = u
            wq_s[d, c] = jnp.concatenate([w, qg], axis=0).astype(BF16)
            lb_s[d, c] = jnp.concatenate([qk, kd_bd], axis=0).astype(BF16)
            el_s[d, c] = jnp.broadcast_to(jnp.exp(glast), (8, LANES))
        return carry

    lax.fori_loop(0, n, phase_a, 0)

    oacc_s[...] = jnp.zeros((seq, LANES), F32)

    if has_s0:
        def bd_state(d):
            top = jnp.concatenate([s0_ref[0, 0, d, 0], jnp.zeros((DN_DK, DN_DV), F32)], axis=1)
            bot = jnp.concatenate([jnp.zeros((DN_DK, DN_DV), F32), s0_ref[0, 0, d, 1]], axis=1)
            return jnp.concatenate([top, bot], axis=0)
        s_init = (bd_state(0), bd_state(1))
    else:
        s_init = (jnp.zeros((LANES, LANES), F32), jnp.zeros((LANES, LANES), F32))

    def phase_b(s, states):
        new_states = []
        for d in range(2):
            c = s if d == 0 else n - 1 - s
            rows = pl.ds(pl.multiple_of(c * CHUNK, CHUNK), CHUNK)
            st = states[d]
            ws = _bdot(wq_s[d, c], st.astype(BF16))
            vn = u_s[d, c] - ws[0:CHUNK]
            r2 = _bdot(lb_s[d, c], stack_heads(vn, lo_half))
            oacc_s[rows, :] += ws[CHUNK:] + r2[0:CHUNK]
            new_states.append(st * el_s[d, c, 0:1, :] + r2[CHUNK:])
        return tuple(new_states)

    s_fin = lax.fori_loop(0, n, phase_b, s_init)

    o = oacc_s[...]
    msq = _dot3_l(o * o, bd_ones) * (1.0 / DN_DV)
    o_ref[...] = o * lax.rsqrt(msq + EPS) * cst_ref[2:3, :]

    if emit_state:
        for d in range(2):
            st_ref[0, 0, d, 0] = s_fin[d][0:DN_DK, 0:DN_DV]
            st_ref[0, 0, d, 1] = s_fin[d][DN_DK:, DN_DV:]


def _deltanet(p, conv_w8, cst, batch, seq, s0=None, emit_state=False):
    n = seq // CHUNK
    has_s0 = s0 is not None

    def col(cbase):
        return pl.BlockSpec((seq, LANES), lambda b, hp: (b, cbase // LANES + hp))

    def cw(cbase):
        return pl.BlockSpec((8, LANES), lambda b, hp: (0, cbase // LANES + hp))

    st_spec = pl.BlockSpec((1, 1, 2, 2, DN_DK, DN_DV), lambda b, hp: (b, 0, 0, hp, 0, 0))
    in_specs = [col(COL_DQ), col(COL_DK), col(COL_DV),
                pl.BlockSpec((seq, LANES), lambda b, hp: (b, COL_BA // LANES)),
                cw(0), cw(512), cw(1024),
                pl.BlockSpec((8, LANES), lambda b, hp: (0, 0))]
    args = [p, p, p, p, conv_w8, conv_w8, conv_w8, cst]
    if has_s0:
        in_specs.append(st_spec)
        args.append(s0)
    out_specs = [pl.BlockSpec((seq, LANES), lambda b, hp: (b, hp))]
    out_shape = [jax.ShapeDtypeStruct((batch * seq, DN_HEADS * DN_DV), F32)]
    if emit_state:
        out_specs.append(st_spec)
        out_shape.append(jax.ShapeDtypeStruct((batch, 1, 2, DN_HEADS, DN_DK, DN_DV), F32))
    scratch = [pltpu.VMEM((seq + 16, LANES), F32),
               pltpu.VMEM((seq, LANES), F32), pltpu.VMEM((seq, LANES), F32), pltpu.VMEM((seq, LANES), F32),
               pltpu.VMEM((2, seq, LANES), F32), pltpu.VMEM((2, seq, LANES), F32),
               pltpu.VMEM((2, n, CHUNK, LANES), F32),
               pltpu.VMEM((2, n, 2 * CHUNK, LANES), BF16),
               pltpu.VMEM((2, n, 3 * CHUNK, LANES), BF16),
               pltpu.VMEM((2, n, 8, LANES), F32),
               pltpu.VMEM((seq, LANES), F32)]
    res = pl.pallas_call(
        functools.partial(_dn_kernel, seq=seq, has_s0=has_s0, emit_state=emit_state),
        grid=(batch, N_PAIRS),
        in_specs=in_specs,
        out_specs=out_specs,
        out_shape=out_shape,
        scratch_shapes=scratch,
        compiler_params=_params(2),
        name="deltanet_lat" if has_s0 else "deltanet_ctx",
    )(*args)
    return res


def _merge_kernel(x_ref, ona_ref, odn_ref, z_ref, gna_ref, gdn_ref, mod_ref, wpost_ref,
                  wao_ref, wdo_ref, wout_ref, o_ref):
    mod = mod_ref[0]
    odn = odn_ref[...] * _silu(z_ref[...])
    br_na = _bdot(ona_ref[...].astype(BF16), wao_ref[...])
    br_dn = _bdot(odn.astype(BF16), wdo_ref[...])
    m = _sigmoid(gna_ref[...]) * br_na + _sigmoid(gdn_ref[...]) * br_dn
    y = _bdot(m.astype(BF16), wout_ref[...])
    o_ref[...] = x_ref[...] + mod[2:3] * _rms_rows(y, wpost_ref[...])


def _merge(x2d, o_na, o_dn, p, mod3, mod_row_of_tile, wpost, w_ao, w_do, w_out):
    n_tok = x2d.shape[0]
    row = lambda i: (i, 0)
    const = lambda i: (0, 0)
    return pl.pallas_call(
        _merge_kernel,
        grid=(n_tok // TM,),
        in_specs=[pl.BlockSpec((TM, D_MODEL), row),
                  pl.BlockSpec((TM, NA_WIDTH), row),
                  pl.BlockSpec((TM, NA_WIDTH), row),
                  pl.BlockSpec((TM, 512), lambda i: (i, COL_Z // 512)),
                  pl.BlockSpec((TM, D_MODEL), lambda i: (i, COL_GNA // D_MODEL)),
                  pl.BlockSpec((TM, D_MODEL), lambda i: (i, COL_GDN // D_MODEL)),
                  pl.BlockSpec((1, N_MOD, D_MODEL), lambda i: (mod_row_of_tile(i), 0, 0)),
                  pl.BlockSpec((1, D_MODEL), const),
                  pl.BlockSpec((NA_WIDTH, D_MODEL), const),
                  pl.BlockSpec((NA_WIDTH, D_MODEL), const),
                  pl.BlockSpec((D_MODEL, D_MODEL), const)],
        out_specs=pl.BlockSpec((TM, D_MODEL), row),
        out_shape=jax.ShapeDtypeStruct((n_tok, D_MODEL), F32),
        compiler_params=_params(1),
        name="merge",
    )(x2d, o_na, o_dn, p, p, p, mod3, wpost, w_ao, w_do, w_out)


def _ffn_kernel(x_ref, mod_ref, wpre_ref, wpost_ref, w1_ref, w2_ref, o_ref):
    x = x_ref[...]
    mod = mod_ref[0]
    h = _rms_rows(x, wpre_ref[...]) * (1.0 + mod[4:5]) + mod[3:4]
    f1 = _bdot(h.astype(BF16), w1_ref[...])
    r = jnp.maximum(f1, 0.0)
    f = _bdot((r * r).astype(BF16), w2_ref[...])
    o_ref[...] = x + mod[5:6] * _rms_rows(f, wpost_ref[...])


def _ffn(x2d, mod3, mod_row_of_tile, wpre, wpost, w1, w2):
    n_tok = x2d.shape[0]
    row = lambda i: (i, 0)
    const = lambda i: (0, 0)
    return pl.pallas_call(
        _ffn_kernel,
        grid=(n_tok // TM,),
        in_specs=[pl.BlockSpec((TM, D_MODEL), row),
                  pl.BlockSpec((1, N_MOD, D_MODEL), lambda i: (mod_row_of_tile(i), 0, 0)),
                  pl.BlockSpec((1, D_MODEL), const),
                  pl.BlockSpec((1, D_MODEL), const),
                  pl.BlockSpec((D_MODEL, D_FF), const),
                  pl.BlockSpec((D_FF, D_MODEL), const)],
        out_specs=pl.BlockSpec((TM, D_MODEL), row),
        out_shape=jax.ShapeDtypeStruct((n_tok, D_MODEL), F32),
        compiler_params=_params(1),
        name="ffn",
    )(x2d, mod3, wpre, wpost, w1, w2)


def _pack_w_in(w):
    pad = jnp.zeros((w.shape[0], LANES - 32), w.dtype)
    packed = jnp.concatenate([w[:, 0:3072], w[:, 3616:5664], w[:, 3104:3616], w[:, 3072:3104], pad], axis=1)
    return packed.astype(BF16)


def kernel(x_prompt, x_sample, c, cache_na_k, cache_na_v, state_delta, c_ctx, w_ada, b_ada, norm_pre1, norm_post1, norm_pre2, norm_post2, w_in, conv_w, a_log, dt_bias, dn_norm, na_rpb, w_ao, w_do, w_out, w_ff1, w_ff2):
    batch, seq, _ = x_prompt.shape
    dec_batch, dec_seq, _ = x_sample.shape
    depth = w_in.shape[0]
    assert depth == 1 and seq == TM and dec_seq % TM == 0 and dec_batch < 8

    xp = x_prompt.reshape(batch * seq, D_MODEL)
    xs = x_sample.reshape(dec_batch * dec_seq, D_MODEL)
    l = 0

    cvecs = jnp.zeros((8, D_MODEL), F32).at[:dec_batch].set(c).at[dec_batch].set(c_ctx)
    mod3 = _adaln(cvecs, w_ada[l], b_ada[l]).reshape(8, N_MOD, D_MODEL)
    ctx_row = lambda i: dec_batch
    tiles_per_lat = dec_seq // TM
    lat_row = lambda i: i // tiles_per_lat

    w_in_p = _pack_w_in(w_in[l])
    w_ao_b = w_ao[l].astype(BF16)
    w_do_b = w_do[l].astype(BF16)
    w_out_b = w_out[l].astype(BF16)
    w1_b = w_ff1[l].astype(BF16)
    w2_b = w_ff2[l].astype(BF16)
    wpre1 = norm_pre1[l].reshape(1, D_MODEL)
    wpost1 = norm_post1[l].reshape(1, D_MODEL)
    wpre2 = norm_pre2[l].reshape(1, D_MODEL)
    wpost2 = norm_post2[l].reshape(1, D_MODEL)

    conv_w8 = jnp.zeros((8, conv_w.shape[2]), F32).at[:CONV_W].set(conv_w[l])
    cst = jnp.zeros((8, LANES), F32)
    cst = cst.at[0, 2 * DN_HEADS:4 * DN_HEADS].set(a_log[l].reshape(-1))
    cst = cst.at[1, 2 * DN_HEADS:4 * DN_HEADS].set(dt_bias[l].reshape(-1))
    cst = cst.at[2].set(jnp.tile(dn_norm[l], 2))

    p_ctx = _inproj(xp, mod3, ctx_row, wpre1, w_in_p)
    o_na_c, new_k, new_v = _ctx_attn(p_ctx, batch, seq)
    o_dn_c, new_s = _deltanet(p_ctx, conv_w8, cst, batch, seq, emit_state=True)
    x1_c = _merge(xp, o_na_c, o_dn_c, p_ctx, mod3, ctx_row, wpost1, w_ao_b, w_do_b, w_out_b)
    y_c = _ffn(x1_c, mod3, ctx_row, wpre2, wpost2, w1_b, w2_b)

    p_lat = _inproj(xs, mod3, lat_row, wpre1, w_in_p, rope_tabs=_rope_tables(dec_seq))
    past = cache_na_k.shape[3]
    ck_tm = jnp.transpose(cache_na_k[:, l], (0, 2, 1, 3)).reshape(dec_batch, past, NA_WIDTH)
    cv_tm = jnp.transpose(cache_na_v[:, l], (0, 2, 1, 3)).reshape(dec_batch, past, NA_WIDTH)
    o_na_l = _lat_attn(p_lat, ck_tm, cv_tm, na_rpb[l], dec_batch, dec_seq)
    (o_dn_l,) = _deltanet(p_lat, conv_w8, cst, dec_batch, dec_seq, s0=state_delta)
    x1_l = _merge(xs, o_na_l, o_dn_l, p_lat, mod3, lat_row, wpost1, w_ao_b, w_do_b, w_out_b)
    y_l = _ffn(x1_l, mod3, lat_row, wpre2, wpost2, w1_b, w2_b)

    return (y_c.reshape(batch, seq, D_MODEL), y_l.reshape(dec_batch, dec_seq, D_MODEL),
            new_k, new_v, new_s)
```

```python
import functools
import math

import numpy as np
import jax
import jax.numpy as jnp
from jax import lax
from jax.experimental import pallas as pl
from jax.experimental.pallas import tpu as pltpu

F32 = jnp.float32
BF16 = jnp.bfloat16

D_MODEL = 1024
N_MOD = 6
GRID_W = 64
NA_HEADS = 8
HEAD_DIM = 64
NA_WIDTH = NA_HEADS * HEAD_DIM
WIN_R = 8
WIN_C = 16
ROPE_BASE = 10000.0
DN_HEADS = 8
DN_DK = 64
DN_DV = 64
CONV_W = 5
CHUNK = 64
D_FF = 4 * D_MODEL
EPS = 1e-6
NEG_INF = -1e30

LANES = 128
N_PAIRS = NA_HEADS // 2
TM = 256
A_GROUP = 4
VMEM_LIMIT = 56 * 1024 * 1024

COL_Q, COL_K, COL_V = 0, 512, 1024
COL_DQ, COL_DK, COL_DV = 1536, 2048, 2560
COL_GNA, COL_GDN, COL_Z, COL_BA = 3072, 4096, 5120, 5632
P_COLS = 5760


def _params(n_grid):
    return pltpu.CompilerParams(dimension_semantics=("arbitrary",) * n_grid,
                                vmem_limit_bytes=VMEM_LIMIT)


def _bdot(a, b):
    return jnp.dot(a, b, preferred_element_type=F32)


def _split3(a):
    hi = a.astype(BF16)
    r1 = a - hi.astype(F32)
    mid = r1.astype(BF16)
    lo = (r1 - mid.astype(F32)).astype(BF16)
    return hi, mid, lo


def _dot3_l(a, b_exact):
    hi, mid, lo = _split3(a)
    return _bdot(hi, b_exact) + _bdot(mid, b_exact) + _bdot(lo, b_exact)


def _dot3_r(a_exact, b):
    hi, mid, lo = _split3(b)
    return _bdot(a_exact, hi) + _bdot(a_exact, mid) + _bdot(a_exact, lo)


def _sigmoid(x):
    return 1.0 / (1.0 + jnp.exp(-x))


def _silu(x):
    return x * _sigmoid(x)


def _softplus(x):
    return jnp.maximum(x, 0.0) + jnp.log(1.0 + jnp.exp(-jnp.abs(x)))


def _rms_rows(x, w):
    ms = jnp.mean(x * x, axis=-1, keepdims=True)
    return x * lax.rsqrt(ms + EPS) * w


def _adaln_kernel(c_ref, w_ref, b_ref, o_ref):
    c = c_ref[...]
    s = _silu(c).astype(BF16)
    o_ref[...] = _bdot(s, w_ref[...].astype(BF16)) + b_ref[...]


def _adaln(cvecs, w_ada, b_ada):
    tn = 1024
    n = w_ada.shape[1]
    return pl.pallas_call(
        _adaln_kernel,
        grid=(n // tn,),
        in_specs=[pl.BlockSpec((8, D_MODEL), lambda j: (0, 0)),
                  pl.BlockSpec((D_MODEL, tn), lambda j: (0, j)),
                  pl.BlockSpec((1, tn), lambda j: (0, j))],
        out_specs=pl.BlockSpec((8, tn), lambda j: (0, j)),
        out_shape=jax.ShapeDtypeStruct((8, n), F32),
        compiler_params=_params(1),
        name="adaln",
    )(cvecs, w_ada, b_ada.reshape(1, n))


_PROJ_BOUNDS = (0, 1024, 2304, 3456, 4608, 5760)


def _inproj_kernel(*refs, rope):
    if rope:
        x_ref, mod_ref, wpre_ref, w_ref, cos_ref, sin_ref, o_ref = refs
    else:
        x_ref, mod_ref, wpre_ref, w_ref, o_ref = refs
    x = x_ref[...]
    mod = mod_ref[0]
    h = _rms_rows(x, wpre_ref[...]) * (1.0 + mod[1:2]) + mod[0:1]
    hb = h.astype(BF16)
    for lo, hi in zip(_PROJ_BOUNDS[:-1], _PROJ_BOUNDS[1:]):
        p = _bdot(hb, w_ref[:, lo:hi])
        if rope and lo == 0:
            lane = lax.broadcasted_iota(jnp.int32, p.shape, 1)
            first = (lane % 32) < 16
            width = hi - lo
            partner = jnp.where(first, pltpu.roll(p, width - 16, 1), pltpu.roll(p, 16, 1))
            p = p * cos_ref[...] + partner * sin_ref[...]
        o_ref[:, lo:hi] = p


def _inproj(x2d, mod3, mod_row_of_tile, wpre, w_packed, rope_tabs=None):
    n_tok = x2d.shape[0]
    rope = rope_tabs is not None
    in_specs = [pl.BlockSpec((TM, D_MODEL), lambda i: (i, 0)),
                pl.BlockSpec((1, N_MOD, D_MODEL), lambda i: (mod_row_of_tile(i), 0, 0)),
                pl.BlockSpec((1, D_MODEL), lambda i: (0, 0)),
                pl.BlockSpec((D_MODEL, P_COLS), lambda i: (0, 0))]
    args = [x2d, mod3, wpre, w_packed]
    if rope:
        tiles_per_seq = rope_tabs[0].shape[0] // TM
        for t in rope_tabs:
            in_specs.append(pl.BlockSpec((TM, 1024), lambda i: (i % tiles_per_seq, 0)))
            args.append(t)
    return pl.pallas_call(
        functools.partial(_inproj_kernel, rope=rope),
        grid=(n_tok // TM,),
        in_specs=in_specs,
        out_specs=pl.BlockSpec((TM, P_COLS), lambda i: (i, 0)),
        out_shape=jax.ShapeDtypeStruct((n_tok, P_COLS), F32),
        compiler_params=_params(1),
        name="inproj_lat" if rope else "inproj_ctx",
    )(*args)


def _rope_tables(seq):
    half = HEAD_DIM // 2
    nf = half // 2
    pos = np.arange(seq)
    inv = (np.float32(ROPE_BASE) ** (-np.arange(nf, dtype=np.float32) / np.float32(nf))).astype(np.float32)
    ang_r = ((pos // GRID_W).astype(np.float32)[:, None] * inv).astype(np.float64)
    ang_c = ((pos % GRID_W).astype(np.float32)[:, None] * inv).astype(np.float64)

    def grp(ang):
        return (np.concatenate([np.cos(ang), np.cos(ang)], 1),
                np.concatenate([-np.sin(ang), np.sin(ang)], 1))

    cr, sr = grp(ang_r)
    cc, sc = grp(ang_c)
    cos_h = np.concatenate([cr, cc], 1)
    sin_h = np.concatenate([sr, sc], 1)
    reps = 2 * NA_HEADS
    return (jnp.asarray(np.tile(cos_h, (1, reps)), F32),
            jnp.asarray(np.tile(sin_h, (1, reps)), F32))


def _ctx_attn_kernel(q_ref, k_ref, v_ref, o_ref, kc_ref, vc_ref):
    lane = lax.broadcasted_iota(jnp.int32, (1, LANES), 1)
    scale = HEAD_DIM ** -0.5
    for hp in range(N_PAIRS):
        sl = slice(hp * LANES, (hp + 1) * LANES)
        q = q_ref[:, sl]
        k = k_ref[:, sl]
        v = v_ref[:, sl]
        kb = k.astype(BF16)
        acc = None
        for e in range(2):
            m_e = (lane >= 64) if e else (lane < 64)
            qe = jnp.where(m_e, q * scale, 0.0).astype(BF16)
            ve = jnp.where(m_e, v, 0.0).astype(BF16)
            s = lax.dot_general(qe, kb, (((1,), (1,)), ((), ())), preferred_element_type=F32)
            s = s - jnp.max(s, axis=-1, keepdims=True)
            p = jnp.exp(s)
            l = jnp.sum(p, axis=-1, keepdims=True)
            o = _bdot(p.astype(BF16), ve) / l
            acc = o if acc is None else acc + o
            kc_ref[0, 0, 2 * hp + e] = k[:, e * 64:(e + 1) * 64]
            vc_ref[0, 0, 2 * hp + e] = v[:, e * 64:(e + 1) * 64]
        o_ref[:, sl] = acc


def _ctx_attn(p_ctx, batch, seq):
    def col(cb):
        return pl.BlockSpec((seq, NA_WIDTH), lambda b: (b, cb))
    cache_shape = jax.ShapeDtypeStruct((batch, 1, NA_HEADS, seq, HEAD_DIM), F32)
    cache_spec = pl.BlockSpec((1, 1, NA_HEADS, seq, HEAD_DIM), lambda b: (b, 0, 0, 0, 0))
    return pl.pallas_call(
        _ctx_attn_kernel,
        grid=(batch,),
        in_specs=[col(COL_Q // NA_WIDTH), col(COL_K // NA_WIDTH), col(COL_V // NA_WIDTH)],
        out_specs=[pl.BlockSpec((seq, NA_WIDTH), lambda b: (b, 0)), cache_spec, cache_spec],
        out_shape=[jax.ShapeDtypeStruct((batch * seq, NA_WIDTH), F32), cache_shape, cache_shape],
        compiler_params=_params(1),
        name="ctx_attn",
    )(p_ctx, p_ctx, p_ctx)


Q_HALF = 512
K_WIN = 768


def _lat_attn_kernel(rpb_ref, q_ref, k_ref, v_ref, ck_ref, cv_ref, o_ref, bias_ref, *, rows):
    hp = pl.program_id(0)
    b = pl.program_id(1)
    half = pl.program_id(2)
    kr_n = min(WIN_R, rows)
    rows_half = Q_HALF // GRID_W
    krows_win = K_WIN // GRID_W

    @pl.when(jnp.logical_and(b == 0, half == 0))
    def _build_bias():
        qc = lax.broadcasted_iota(jnp.int32, (GRID_W, LANES), 0)
        ln = lax.broadcasted_iota(jnp.int32, (GRID_W, LANES), 1)
        kc = ln % GRID_W
        cs = jnp.clip(qc - WIN_C // 2, 0, GRID_W - WIN_C)
        valid = jnp.logical_and(kc >= cs, kc < cs + WIN_C)
        dc = jnp.clip(kc - qc + (WIN_C - 1), 0, 2 * WIN_C - 2)
        neg = jnp.full((GRID_W, LANES), NEG_INF, F32)
        for e in range(2):
            h = 2 * hp + e
            blocks = []
            for a in range(2 * WIN_R - 1):
                def body(j, acc, a=a, h=h):
                    return jnp.where(dc == j, rpb_ref[(h * (2 * WIN_R - 1) + a) * (2 * WIN_C - 1) + j], acc)
                t = lax.fori_loop(0, 2 * WIN_C - 1, body, jnp.zeros((GRID_W, LANES), F32))
                blocks.append(jnp.where(valid, t, NEG_INF))
            for hf in range(2):
                for rl in range(rows_half):
                    r = hf * rows_half + rl
                    rs = min(max(r - kr_n // 2, 0), rows - kr_n)
                    for kp in range(krows_win // 2):
                        parts = []
                        for kl in (2 * kp, 2 * kp + 1):
                            kr = kl + hf * (rows - krows_win)
                            if rs <= kr < rs + kr_n:
                                parts.append(blocks[kr - r + WIN_R - 1])
                            else:
                                parts.append(neg)
                        blk = jnp.where(ln < GRID_W, parts[0], parts[1])
                        bias_ref[e, hf, rl * GRID_W:(rl + 1) * GRID_W, kp * LANES:(kp + 1) * LANES] = blk

    lane = lax.broadcasted_iota(jnp.int32, (1, LANES), 1)
    scale = HEAD_DIM ** -0.5
    k0 = pl.multiple_of(half * ((rows - krows_win) * GRID_W), GRID_W)
    kw = k_ref[pl.ds(k0, K_WIN), :].astype(BF16)
    vw = v_ref[pl.ds(k0, K_WIN), :]
    ck = ck_ref[0].astype(BF16)
    cv = cv_ref[0]
    SUB = 256
    for qs in range(Q_HALF // SUB):
        q = q_ref[qs * SUB:(qs + 1) * SUB, :] * scale
        acc = None
        for e in range(2):
            m_e = (lane >= 64) if e else (lane < 64)
            qe = jnp.where(m_e, q, 0.0).astype(BF16)
            s_loc = lax.dot_general(qe, kw, (((1,), (1,)), ((), ())), preferred_element_type=F32)
            s_loc = s_loc + bias_ref[e, half, qs * SUB:(qs + 1) * SUB, :]
            s_ctx = lax.dot_general(qe, ck, (((1,), (1,)), ((), ())), preferred_element_type=F32)
            m = jnp.maximum(jnp.max(s_loc, axis=-1, keepdims=True), jnp.max(s_ctx, axis=-1, keepdims=True))
            p_loc = jnp.exp(s_loc - m)
            p_ctx = jnp.exp(s_ctx - m)
            l = jnp.sum(p_loc, axis=-1, keepdims=True) + jnp.sum(p_ctx, axis=-1, keepdims=True)
            ve = jnp.where(m_e, vw, 0.0).astype(BF16)
            cve = jnp.where(m_e, cv, 0.0).astype(BF16)
            o = (_bdot(p_loc.astype(BF16), ve) + _bdot(p_ctx.astype(BF16), cve)) / l
            acc = o if acc is None else acc + o
        o_ref[qs * SUB:(qs + 1) * SUB, :] = acc


def _lat_attn(p_lat, ck_tm, cv_tm, rpb, batch, seq):
    rows = seq // GRID_W
    assert rows * GRID_W == seq and rows - K_WIN // GRID_W == 4 and Q_HALF * 2 == seq
    halves = seq // Q_HALF
    past = ck_tm.shape[1]
    kernel = functools.partial(_lat_attn_kernel, rows=rows)
    return pl.pallas_call(
        kernel,
        grid=(N_PAIRS, batch, halves),
        in_specs=[pl.BlockSpec(memory_space=pltpu.SMEM),
                  pl.BlockSpec((Q_HALF, LANES), lambda hp, b, hf: (b * halves + hf, COL_Q // LANES + hp)),
                  pl.BlockSpec((seq, LANES), lambda hp, b, hf: (b, COL_K // LANES + hp)),
                  pl.BlockSpec((seq, LANES), lambda hp, b, hf: (b, COL_V // LANES + hp)),
                  pl.BlockSpec((1, past, LANES), lambda hp, b, hf: (b, 0, hp)),
                  pl.BlockSpec((1, past, LANES), lambda hp, b, hf: (b, 0, hp))],
        out_specs=pl.BlockSpec((Q_HALF, LANES), lambda hp, b, hf: (b * halves + hf, hp)),
        out_shape=jax.ShapeDtypeStruct((batch * seq, NA_WIDTH), F32),
        scratch_shapes=[pltpu.VMEM((2, 2, Q_HALF, K_WIN), F32)],
        compiler_params=_params(3),
        name="lat_attn",
    )(rpb.reshape(-1), p_lat, p_lat, p_lat, ck_tm, cv_tm)


def _dn_kernel(*refs, seq, has_s0, emit_state):
    it = iter(refs)
    dq_ref, dk_ref, dv_ref, ba_ref = next(it), next(it), next(it), next(it)
    cwq_ref, cwk_ref, cwv_ref, cst_ref = next(it), next(it), next(it), next(it)
    s0_ref = next(it) if has_s0 else None
    o_ref = next(it)
    st_ref = next(it) if emit_state else None
    (pad_s, q_s, k_s, v_s, beta_s, g_s, u_s, wq_s, lb_s, el_s, odir_s) = it

    hp = pl.program_id(1)
    n = seq // CHUNK

    r128 = lax.broadcasted_iota(jnp.int32, (LANES, LANES), 0)
    c128 = lax.broadcasted_iota(jnp.int32, (LANES, LANES), 1)
    bd_mask = (r128 // 64) == (c128 // 64)
    bd_ones = jnp.where(bd_mask, 1.0, 0.0).astype(BF16)

    zeros8 = jnp.zeros((8, LANES), F32)
    pad_s[0:8, :] = zeros8
    pad_s[seq + 8:seq + 16, :] = zeros8

    def conv_silu(x_ref, w_ref):
        pad_s[8:seq + 8, :] = x_ref[...]
        acc = None
        for j in range(CONV_W):
            term = pad_s[8 - CONV_W // 2 + j: 8 - CONV_W // 2 + j + seq, :] * w_ref[j:j + 1, :]
            acc = term if acc is None else acc + term
        return _silu(acc)

    def l2n(x):
        ss = _dot3_l(x * x, bd_ones)
        return x * lax.rsqrt(ss + EPS)

    q_s[...] = l2n(conv_silu(dq_ref, cwq_ref))
    k_s[...] = l2n(conv_silu(dk_ref, cwk_ref))
    v_s[...] = conv_silu(dv_ref, cwv_ref)

    ba = ba_ref[...]
    beta_all = _sigmoid(ba)
    g_all = -jnp.exp(cst_ref[0:1, :]) * _softplus(ba + cst_ref[1:2, :])
    e_col = c128 // 64
    for d in range(2):
        sel_b = jnp.where(r128 == d * DN_HEADS + 2 * hp + e_col, 1.0, 0.0).astype(BF16)
        sel_g = jnp.where(r128 == 2 * DN_HEADS + d * DN_HEADS + 2 * hp + e_col, 1.0, 0.0).astype(BF16)
        beta_s[d] = _dot3_l(beta_all, sel_b)
        g_s[d] = _dot3_l(g_all, sel_g)

    ri = lax.broadcasted_iota(jnp.int32, (CHUNK, LANES), 0)
    li = lax.broadcasted_iota(jnp.int32, (CHUNK, LANES), 1)
    lj = li % 64
    lo_half = li < 64
    diag = ri == lj
    incl = (ri >= lj, ri <= lj)
    strict = (ri > lj, ri < lj)
    r64 = lax.broadcasted_iota(jnp.int32, (CHUNK, CHUNK), 0)
    c64 = lax.broadcasted_iota(jnp.int32, (CHUNK, CHUNK), 1)
    tri = (jnp.where(c64 <= r64, 1.0, 0.0).astype(BF16), jnp.where(c64 >= r64, 1.0, 0.0).astype(BF16))
    li2 = lax.broadcasted_iota(jnp.int32, (CHUNK, 2 * LANES), 1)
    lo_half2 = (li2 % LANES) < 64

    off_blk = ([], [])
    for lvl in range(6):
        s = 1 << lvl
        same2 = (ri // (2 * s)) == (lj // (2 * s))
        off_blk[0].append(jnp.logical_and(same2, (ri // s) == (lj // s) + 1))
        off_blk[1].append(jnp.logical_and(same2, (lj // s) == (ri // s) + 1))

    def stack_f32(x, lo_mask):
        return jnp.concatenate([jnp.where(lo_mask, x, 0.0), jnp.where(lo_mask, 0.0, x)], axis=0)

    def stack_heads(x, lo_mask):
        return stack_f32(x, lo_mask).astype(BF16)

    def chunk_rows(c):
        return pl.ds(c * CHUNK if isinstance(c, int) else pl.multiple_of(c * CHUNK, CHUNK), CHUNK)

    def chunk_load(c):
        rows = chunk_rows(c)
        return (q_s[rows, :], k_s[rows, :], v_s[rows, :],
                [beta_s[d, rows, :] for d in range(2)], [g_s[d, rows, :] for d in range(2)])

    def group_a(loaded):
        nc = len(loaded)
        units = [(i, d) for i in range(nc) for d in range(2)]
        qcs = [ld[0] * (DN_DK ** -0.5) for ld in loaded]
        kcs = [ld[1] for ld in loaded]
        vcs = [ld[2] for ld in loaded]
        kt_bd = []
        for kc in kcs:
            ktt = jnp.transpose(jnp.concatenate([kc, kc], axis=0))
            kt_bd.append(jnp.where(bd_mask, ktt, 0.0))
        gsp = {(i, d): _split3(loaded[i][4][d]) for i, d in units}
        gc = {}
        for part in range(3):
            for i, d in units:
                t = _bdot(tri[d], gsp[i, d][part])
                gc[i, d] = t if part == 0 else gc[i, d] + t
        rrow = {u: jnp.sum(jnp.where(diag, gc[u], 0.0), axis=0, keepdims=True) for u in units}
        dec = {u: jnp.exp(jnp.minimum(gc[u] - rrow[u], 0.0)) for u in units}
        kbeta = {(i, d): kcs[i] * loaded[i][3][d] for i, d in units}
        m1 = [_bdot(jnp.concatenate([qcs[i], kbeta[i, 0], kbeta[i, 1]], axis=0).astype(BF16),
                    kt_bd[i].astype(BF16)) for i in range(nc)]
        amat = {(i, d): jnp.where(strict[d], m1[i][CHUNK * (1 + d):CHUNK * (2 + d)] * dec[i, d], 0.0)
                for i, d in units}
        a_st = {u: stack_heads(amat[u], lo_half) for u in units}
        tmat = {(i, d): jnp.where(diag, 1.0, 0.0) - jnp.where(off_blk[d][0], amat[i, d], 0.0) for i, d in units}
        for lvl in range(1, 6):
            ta = {u: _bdot(tmat[u].astype(BF16), a_st[u]) for u in units}
            y = {u: _bdot(ta[u].astype(BF16), stack_heads(tmat[u], lo_half)) for u in units}
            tmat = {(i, d): tmat[i, d] - jnp.where(off_blk[d][lvl], y[i, d], 0.0) for i, d in units}
        eg = {u: jnp.exp(gc[u]) for u in units}
        x = {}
        for i, d in units:
            rhs = jnp.concatenate([vcs[i] * loaded[i][3][d], kbeta[i, d] * eg[i, d]], axis=1)
            x[i, d] = _bdot(tmat[i, d].astype(BF16), stack_heads(rhs, lo_half2))
        results = []
        for i in range(nc):
            out = []
            for d in range(2):
                u = x[i, d][:, :LANES]
                w = x[i, d][:, LANES:]
                qk = jnp.where(incl[d], m1[i][0:CHUNK] * dec[i, d], 0.0)
                qg = qcs[i] * eg[i, d]
                glast = gc[i, d][CHUNK - 1:CHUNK, :] if d == 0 else gc[i, d][0:1, :]
                kd_bd = kt_bd[i] * jnp.exp(glast - rrow[i, d])
                out.append((u,
                            jnp.concatenate([w, qg], axis=0).astype(BF16),
                            jnp.concatenate([qk, kd_bd], axis=0).astype(BF16),
                            jnp.broadcast_to(jnp.exp(glast), (8, LANES))))
            results.append(out)
        return results

    group = min(A_GROUP, n)

    def phase_a(gi, carry):
        cs = [gi * group + j for j in range(group)]
        loaded = [chunk_load(c) for c in cs]
        results = group_a(loaded)
        for c, res in zip(cs, results):
            for d in range(2):
                u_s[d, c], wq_s[d, c], lb_s[d, c], el_s[d, c] = res[d]
        return carry

    if n == group:
        phase_a(0, 0)
    else:
        lax.fori_loop(0, n // group, phase_a, 0)

    if has_s0:
        def bd_state(d):
            top = jnp.concatenate([s0_ref[0, 0, d, 0], jnp.zeros((DN_DK, DN_DV), F32)], axis=1)
            bot = jnp.concatenate([jnp.zeros((DN_DK, DN_DV), F32), s0_ref[0, 0, d, 1]], axis=1)
            return jnp.concatenate([top, bot], axis=0)
        s_init = (bd_state(0), bd_state(1))
    else:
        s_init = (jnp.zeros((LANES, LANES), F32), jnp.zeros((LANES, LANES), F32))

    def phase_b(s, states):
        cs = (s, n - 1 - s)
        loaded = [(wq_s[d, cs[d]], u_s[d, cs[d]], lb_s[d, cs[d]], el_s[d, cs[d], 0:1, :]) for d in range(2)]
        new_states, outs = [], []
        for d in range(2):
            wq, u, lb, el = loaded[d]
            st = states[d]
            ws = _bdot(wq, st.astype(BF16))
            vn = u - ws[0:CHUNK]
            r2 = _bdot(lb, stack_heads(vn, lo_half))
            outs.append(ws[CHUNK:] + r2[0:CHUNK])
            new_states.append(st * el + r2[CHUNK:])
        for d in range(2):
            odir_s[d, chunk_rows(cs[d]), :] = outs[d]
        return tuple(new_states)

    s_fin = lax.fori_loop(0, n, phase_b, s_init)

    o = odir_s[0] + odir_s[1]
    msq = _dot3_l(o * o, bd_ones) * (1.0 / DN_DV)
    o_ref[...] = o * lax.rsqrt(msq + EPS) * cst_ref[2:3, :]

    if emit_state:
        for d in range(2):
            st_ref[0, 0, d, 0] = s_fin[d][0:DN_DK, 0:DN_DV]
            st_ref[0, 0, d, 1] = s_fin[d][DN_DK:, DN_DV:]


def _deltanet(p, conv_w8, cst, batch, seq, s0=None, emit_state=False):
    n = seq // CHUNK
    has_s0 = s0 is not None

    def col(cbase):
        return pl.BlockSpec((seq, LANES), lambda b, hp: (b, cbase // LANES + hp))

    def cw(cbase):
        return pl.BlockSpec((8, LANES), lambda b, hp: (0, cbase // LANES + hp))

    st_spec = pl.BlockSpec((1, 1, 2, 2, DN_DK, DN_DV), lambda b, hp: (b, 0, 0, hp, 0, 0))
    in_specs = [col(COL_DQ), col(COL_DK), col(COL_DV),
                pl.BlockSpec((seq, LANES), lambda b, hp: (b, COL_BA // LANES)),
                cw(0), cw(512), cw(1024),
                pl.BlockSpec((8, LANES), lambda b, hp: (0, 0))]
    args = [p, p, p, p, conv_w8, conv_w8, conv_w8, cst]
    if has_s0:
        in_specs.append(st_spec)
        args.append(s0)
    out_specs = [pl.BlockSpec((seq, LANES), lambda b, hp: (b, hp))]
    out_shape = [jax.ShapeDtypeStruct((batch * seq, DN_HEADS * DN_DV), F32)]
    if emit_state:
        out_specs.append(st_spec)
        out_shape.append(jax.ShapeDtypeStruct((batch, 1, 2, DN_HEADS, DN_DK, DN_DV), F32))
    scratch = [pltpu.VMEM((seq + 16, LANES), F32),
               pltpu.VMEM((seq, LANES), F32), pltpu.VMEM((seq, LANES), F32), pltpu.VMEM((seq, LANES), F32),
               pltpu.VMEM((2, seq, LANES), F32), pltpu.VMEM((2, seq, LANES), F32),
               pltpu.VMEM((2, n, CHUNK, LANES), F32),
               pltpu.VMEM((2, n, 2 * CHUNK, LANES), BF16),
               pltpu.VMEM((2, n, 3 * CHUNK, LANES), BF16),
               pltpu.VMEM((2, n, 8, LANES), F32),
               pltpu.VMEM((2, seq, LANES), F32)]
    res = pl.pallas_call(
        functools.partial(_dn_kernel, seq=seq, has_s0=has_s0, emit_state=emit_state),
        grid=(batch, N_PAIRS),
        in_specs=in_specs,
        out_specs=out_specs,
        out_shape=out_shape,
        scratch_shapes=scratch,
        compiler_params=_params(2),
        name="deltanet_lat" if has_s0 else "deltanet_ctx",
    )(*args)
    return res


def _merge_kernel(x_ref, ona_ref, odn_ref, z_ref, gna_ref, gdn_ref, mod_ref, wpost_ref,
                  wao_ref, wdo_ref, wout_ref, o_ref):
    mod = mod_ref[0]
    odn = odn_ref[...] * _silu(z_ref[...])
    br_na = _bdot(ona_ref[...].astype(BF16), wao_ref[...])
    br_dn = _bdot(odn.astype(BF16), wdo_ref[...])
    m = _sigmoid(gna_ref[...]) * br_na + _sigmoid(gdn_ref[...]) * br_dn
    y = _bdot(m.astype(BF16), wout_ref[...])
    o_ref[...] = x_ref[...] + mod[2:3] * _rms_rows(y, wpost_ref[...])


def _merge(x2d, o_na, o_dn, p, mod3, mod_row_of_tile, wpost, w_ao, w_do, w_out):
    n_tok = x2d.shape[0]
    row = lambda i: (i, 0)
    const = lambda i: (0, 0)
    return pl.pallas_call(
        _merge_kernel,
        grid=(n_tok // TM,),
        in_specs=[pl.BlockSpec((TM, D_MODEL), row),
                  pl.BlockSpec((TM, NA_WIDTH), row),
                  pl.BlockSpec((TM, NA_WIDTH), row),
                  pl.BlockSpec((TM, 512), lambda i: (i, COL_Z // 512)),
                  pl.BlockSpec((TM, D_MODEL), lambda i: (i, COL_GNA // D_MODEL)),
                  pl.BlockSpec((TM, D_MODEL), lambda i: (i, COL_GDN // D_MODEL)),
                  pl.BlockSpec((1, N_MOD, D_MODEL), lambda i: (mod_row_of_tile(i), 0, 0)),
                  pl.BlockSpec((1, D_MODEL), const),
                  pl.BlockSpec((NA_WIDTH, D_MODEL), const),
                  pl.BlockSpec((NA_WIDTH, D_MODEL), const),
                  pl.BlockSpec((D_MODEL, D_MODEL), const)],
        out_specs=pl.BlockSpec((TM, D_MODEL), row),
        out_shape=jax.ShapeDtypeStruct((n_tok, D_MODEL), F32),
        compiler_params=_params(1),
        name="merge",
    )(x2d, o_na, o_dn, p, p, p, mod3, wpost, w_ao, w_do, w_out)


def _ffn_kernel(x_ref, mod_ref, wpre_ref, wpost_ref, w1_ref, w2_ref, o_ref):
    x = x_ref[...]
    mod = mod_ref[0]
    h = _rms_rows(x, wpre_ref[...]) * (1.0 + mod[4:5]) + mod[3:4]
    f1 = _bdot(h.astype(BF16), w1_ref[...])
    r = jnp.maximum(f1, 0.0)
    f = _bdot((r * r).astype(BF16), w2_ref[...])
    o_ref[...] = x + mod[5:6] * _rms_rows(f, wpost_ref[...])


def _ffn(x2d, mod3, mod_row_of_tile, wpre, wpost, w1, w2):
    n_tok = x2d.shape[0]
    row = lambda i: (i, 0)
    const = lambda i: (0, 0)
    return pl.pallas_call(
        _ffn_kernel,
        grid=(n_tok // TM,),
        in_specs=[pl.BlockSpec((TM, D_MODEL), row),
                  pl.BlockSpec((1, N_MOD, D_MODEL), lambda i: (mod_row_of_tile(i), 0, 0)),
                  pl.BlockSpec((1, D_MODEL), const),
                  pl.BlockSpec((1, D_MODEL), const),
                  pl.BlockSpec((D_MODEL, D_FF), const),
                  pl.BlockSpec((D_FF, D_MODEL), const)],
        out_specs=pl.BlockSpec((TM, D_MODEL), row),
        out_shape=jax.ShapeDtypeStruct((n_tok, D_MODEL), F32),
        compiler_params=_params(1),
        name="ffn",
    )(x2d, mod3, wpre, wpost, w1, w2)


def _pack_w_in(w):
    pad = jnp.zeros((w.shape[0], LANES - 32), w.dtype)
    packed = jnp.concatenate([w[:, 0:3072], w[:, 3616:5664], w[:, 3104:3616], w[:, 3072:3104], pad], axis=1)
    return packed.astype(BF16)


def kernel(x_prompt, x_sample, c, cache_na_k, cache_na_v, state_delta, c_ctx, w_ada, b_ada, norm_pre1, norm_post1, norm_pre2, norm_post2, w_in, conv_w, a_log, dt_bias, dn_norm, na_rpb, w_ao, w_do, w_out, w_ff1, w_ff2):
    batch, seq, _ = x_prompt.shape
    dec_batch, dec_seq, _ = x_sample.shape
    depth = w_in.shape[0]
    assert depth == 1 and seq == TM and dec_seq % TM == 0 and dec_batch < 8

    xp = x_prompt.reshape(batch * seq, D_MODEL)
    xs = x_sample.reshape(dec_batch * dec_seq, D_MODEL)
    l = 0

    cvecs = jnp.zeros((8, D_MODEL), F32).at[:dec_batch].set(c).at[dec_batch].set(c_ctx)
    mod3 = _adaln(cvecs, w_ada[l], b_ada[l]).reshape(8, N_MOD, D_MODEL)
    ctx_row = lambda i: dec_batch
    tiles_per_lat = dec_seq // TM
    lat_row = lambda i: i // tiles_per_lat

    w_in_p = _pack_w_in(w_in[l])
    w_ao_b = w_ao[l].astype(BF16)
    w_do_b = w_do[l].astype(BF16)
    w_out_b = w_out[l].astype(BF16)
    w1_b = w_ff1[l].astype(BF16)
    w2_b = w_ff2[l].astype(BF16)
    wpre1 = norm_pre1[l].reshape(1, D_MODEL)
    wpost1 = norm_post1[l].reshape(1, D_MODEL)
    wpre2 = norm_pre2[l].reshape(1, D_MODEL)
    wpost2 = norm_post2[l].reshape(1, D_MODEL)

    conv_w8 = jnp.zeros((8, conv_w.shape[2]), F32).at[:CONV_W].set(conv_w[l])
    cst = jnp.zeros((8, LANES), F32)
    cst = cst.at[0, 2 * DN_HEADS:4 * DN_HEADS].set(a_log[l].reshape(-1))
    cst = cst.at[1, 2 * DN_HEADS:4 * DN_HEADS].set(dt_bias[l].reshape(-1))
    cst = cst.at[2].set(jnp.tile(dn_norm[l], 2))

    p_ctx = _inproj(xp, mod3, ctx_row, wpre1, w_in_p)
    o_na_c, new_k, new_v = _ctx_attn(p_ctx, batch, seq)
    o_dn_c, new_s = _deltanet(p_ctx, conv_w8, cst, batch, seq, emit_state=True)
    x1_c = _merge(xp, o_na_c, o_dn_c, p_ctx, mod3, ctx_row, wpost1, w_ao_b, w_do_b, w_out_b)
    y_c = _ffn(x1_c, mod3, ctx_row, wpre2, wpost2, w1_b, w2_b)

    p_lat = _inproj(xs, mod3, lat_row, wpre1, w_in_p, rope_tabs=_rope_tables(dec_seq))
    past = cache_na_k.shape[3]
    ck_tm = jnp.transpose(cache_na_k[:, l], (0, 2, 1, 3)).reshape(dec_batch, past, NA_WIDTH)
    cv_tm = jnp.transpose(cache_na_v[:, l], (0, 2, 1, 3)).reshape(dec_batch, past, NA_WIDTH)
    o_na_l = _lat_attn(p_lat, ck_tm, cv_tm, na_rpb[l], dec_batch, dec_seq)
    (o_dn_l,) = _deltanet(p_lat, conv_w8, cst, dec_batch, dec_seq, s0=state_delta)
    x1_l = _merge(xs, o_na_l, o_dn_l, p_lat, mod3, lat_row, wpost1, w_ao_b, w_do_b, w_out_b)
    y_l = _ffn(x1_l, mod3, lat_row, wpre2, wpost2, w1_b, w2_b)

    return (y_c.reshape(batch, seq, D_MODEL), y_l.reshape(dec_batch, dec_seq, D_MODEL),
            new_k, new_v, new_s)
```

```python
import functools
import math

import numpy as np
import jax
import jax.numpy as jnp
from jax import lax
from jax.experimental import pallas as pl
from jax.experimental.pallas import tpu as pltpu

F32 = jnp.float32
BF16 = jnp.bfloat16

D_MODEL = 1024
N_MOD = 6
GRID_W = 64
NA_HEADS = 8
HEAD_DIM = 64
NA_WIDTH = NA_HEADS * HEAD_DIM
WIN_R = 8
WIN_C = 16
ROPE_BASE = 10000.0
DN_HEADS = 8
DN_DK = 64
DN_DV = 64
CONV_W = 5
CHUNK = 64
D_FF = 4 * D_MODEL
EPS = 1e-6
NEG_INF = -1e30

LANES = 128
N_PAIRS = NA_HEADS // 2
TM = 256
VMEM_LIMIT = 56 * 1024 * 1024

COL_Q, COL_K, COL_V = 0, 512, 1024
COL_DQ, COL_DK, COL_DV = 1536, 2048, 2560
COL_GNA, COL_GDN, COL_Z, COL_BA = 3072, 4096, 5120, 5632
P_COLS = 5760


def _params(n_grid):
    return pltpu.CompilerParams(dimension_semantics=("arbitrary",) * n_grid,
                                vmem_limit_bytes=VMEM_LIMIT)


def _bdot(a, b):
    return jnp.dot(a, b, preferred_element_type=F32)


def _split3(a):
    hi = a.astype(BF16)
    r1 = a - hi.astype(F32)
    mid = r1.astype(BF16)
    lo = (r1 - mid.astype(F32)).astype(BF16)
    return hi, mid, lo


def _dot3_l(a, b_exact):
    hi, mid, lo = _split3(a)
    return _bdot(hi, b_exact) + _bdot(mid, b_exact) + _bdot(lo, b_exact)


def _dot2_l(a, b_exact):
    hi = a.astype(BF16)
    lo = (a - hi.astype(F32)).astype(BF16)
    return _bdot(hi, b_exact) + _bdot(lo, b_exact)


def _dot3_r(a_exact, b):
    hi, mid, lo = _split3(b)
    return _bdot(a_exact, hi) + _bdot(a_exact, mid) + _bdot(a_exact, lo)


def _sigmoid(x):
    return 0.5 * jnp.tanh(0.5 * x) + 0.5


def _silu(x):
    return x * _sigmoid(x)


def _softplus(x):
    return jnp.maximum(x, 0.0) + jnp.log(1.0 + jnp.exp(-jnp.abs(x)))


def _rms_rows(x, w):
    ms = jnp.mean(x * x, axis=-1, keepdims=True)
    return x * lax.rsqrt(ms + EPS) * w


def _adaln_kernel(c_ref, w_ref, b_ref, o_ref):
    c = c_ref[...]
    s = _silu(c).astype(BF16)
    o_ref[...] = _bdot(s, w_ref[...].astype(BF16)) + b_ref[...]


def _adaln(cvecs, w_ada, b_ada):
    tn = 1024
    n = w_ada.shape[1]
    return pl.pallas_call(
        _adaln_kernel,
        grid=(n // tn,),
        in_specs=[pl.BlockSpec((8, D_MODEL), lambda j: (0, 0)),
                  pl.BlockSpec((D_MODEL, tn), lambda j: (0, j)),
                  pl.BlockSpec((1, tn), lambda j: (0, j))],
        out_specs=pl.BlockSpec((8, tn), lambda j: (0, j)),
        out_shape=jax.ShapeDtypeStruct((8, n), F32),
        compiler_params=_params(1),
        name="adaln",
    )(cvecs, w_ada, b_ada.reshape(1, n))


_PROJ_BOUNDS = (0, 1024, 2304, 3456, 4608, 5760)


def _inproj_kernel(*refs, rope):
    if rope:
        x_ref, mod_ref, wpre_ref, w_ref, cos_ref, sin_ref, o_ref = refs
    else:
        x_ref, mod_ref, wpre_ref, w_ref, o_ref = refs
    x = x_ref[...]
    mod = mod_ref[0]
    h = _rms_rows(x, wpre_ref[...]) * (1.0 + mod[1:2]) + mod[0:1]
    hb = h.astype(BF16)
    for lo, hi in zip(_PROJ_BOUNDS[:-1], _PROJ_BOUNDS[1:]):
        p = _bdot(hb, w_ref[:, lo:hi])
        if rope and lo == 0:
            lane = lax.broadcasted_iota(jnp.int32, p.shape, 1)
            first = (lane % 32) < 16
            width = hi - lo
            partner = jnp.where(first, pltpu.roll(p, width - 16, 1), pltpu.roll(p, 16, 1))
            p = p * cos_ref[...] + partner * sin_ref[...]
        o_ref[:, lo:hi] = p


def _inproj(x2d, mod3, mod_row_of_tile, wpre, w_packed, rope_tabs=None):
    n_tok = x2d.shape[0]
    rope = rope_tabs is not None
    in_specs = [pl.BlockSpec((TM, D_MODEL), lambda i: (i, 0)),
                pl.BlockSpec((1, N_MOD, D_MODEL), lambda i: (mod_row_of_tile(i), 0, 0)),
                pl.BlockSpec((1, D_MODEL), lambda i: (0, 0)),
                pl.BlockSpec((D_MODEL, P_COLS), lambda i: (0, 0))]
    args = [x2d, mod3, wpre, w_packed]
    if rope:
        tiles_per_seq = rope_tabs[0].shape[0] // TM
        for t in rope_tabs:
            in_specs.append(pl.BlockSpec((TM, 1024), lambda i: (i % tiles_per_seq, 0)))
            args.append(t)
    return pl.pallas_call(
        functools.partial(_inproj_kernel, rope=rope),
        grid=(n_tok // TM,),
        in_specs=in_specs,
        out_specs=pl.BlockSpec((TM, P_COLS), lambda i: (i, 0)),
        out_shape=jax.ShapeDtypeStruct((n_tok, P_COLS), F32),
        compiler_params=_params(1),
        name="inproj_lat" if rope else "inproj_ctx",
    )(*args)


def _rope_tables(seq):
    half = HEAD_DIM // 2
    nf = half // 2
    pos = np.arange(seq)
    inv = (np.float32(ROPE_BASE) ** (-np.arange(nf, dtype=np.float32) / np.float32(nf))).astype(np.float32)
    ang_r = ((pos // GRID_W).astype(np.float32)[:, None] * inv).astype(np.float64)
    ang_c = ((pos % GRID_W).astype(np.float32)[:, None] * inv).astype(np.float64)

    def grp(ang):
        return (np.concatenate([np.cos(ang), np.cos(ang)], 1),
                np.concatenate([-np.sin(ang), np.sin(ang)], 1))

    cr, sr = grp(ang_r)
    cc, sc = grp(ang_c)
    cos_h = np.concatenate([cr, cc], 1)
    sin_h = np.concatenate([sr, sc], 1)
    reps = 2 * NA_HEADS
    return (jnp.asarray(np.tile(cos_h, (1, reps)), F32),
            jnp.asarray(np.tile(sin_h, (1, reps)), F32))


def _ctx_attn_kernel(q_ref, k_ref, v_ref, o_ref, kc_ref, vc_ref):
    lane = lax.broadcasted_iota(jnp.int32, (1, LANES), 1)
    scale = HEAD_DIM ** -0.5
    heads = [(hp, e) for hp in range(N_PAIRS) for e in range(2)]
    sls = [slice(hp * LANES, (hp + 1) * LANES) for hp in range(N_PAIRS)]
    q = [q_ref[:, sl] for sl in sls]
    k = [k_ref[:, sl] for sl in sls]
    v = [v_ref[:, sl] for sl in sls]
    kb = [x.astype(BF16) for x in k]
    m_e = (lane < 64, lane >= 64)
    s = {(hp, e): lax.dot_general(jnp.where(m_e[e], q[hp] * scale, 0.0).astype(BF16), kb[hp],
                                  (((1,), (1,)), ((), ())), preferred_element_type=F32) for hp, e in heads}
    p = {u: jnp.exp(s[u] - jnp.max(s[u], axis=-1, keepdims=True)) for u in heads}
    l = {u: jnp.sum(p[u], axis=-1, keepdims=True) for u in heads}
    o = {(hp, e): _bdot(p[hp, e].astype(BF16), jnp.where(m_e[e], v[hp], 0.0).astype(BF16)) for hp, e in heads}
    for hp in range(N_PAIRS):
        o_ref[:, sls[hp]] = o[hp, 0] / l[hp, 0] + o[hp, 1] / l[hp, 1]
        for e in range(2):
            kc_ref[0, 0, 2 * hp + e] = k[hp][:, e * 64:(e + 1) * 64]
            vc_ref[0, 0, 2 * hp + e] = v[hp][:, e * 64:(e + 1) * 64]


def _ctx_attn(p_ctx, batch, seq):
    def col(cb):
        return pl.BlockSpec((seq, NA_WIDTH), lambda b: (b, cb))
    cache_shape = jax.ShapeDtypeStruct((batch, 1, NA_HEADS, seq, HEAD_DIM), F32)
    cache_spec = pl.BlockSpec((1, 1, NA_HEADS, seq, HEAD_DIM), lambda b: (b, 0, 0, 0, 0))
    return pl.pallas_call(
        _ctx_attn_kernel,
        grid=(batch,),
        in_specs=[col(COL_Q // NA_WIDTH), col(COL_K // NA_WIDTH), col(COL_V // NA_WIDTH)],
        out_specs=[pl.BlockSpec((seq, NA_WIDTH), lambda b: (b, 0)), cache_spec, cache_spec],
        out_shape=[jax.ShapeDtypeStruct((batch * seq, NA_WIDTH), F32), cache_shape, cache_shape],
        compiler_params=_params(1),
        name="ctx_attn",
    )(p_ctx, p_ctx, p_ctx)


Q_HALF = 512
K_WIN = 768


def _lat_attn_kernel(rpb_ref, q_ref, k_ref, v_ref, ck_ref, cv_ref, o_ref, bias_ref, *, rows):
    hp = pl.program_id(0)
    b = pl.program_id(1)
    half = pl.program_id(2)
    kr_n = min(WIN_R, rows)
    rows_half = Q_HALF // GRID_W
    krows_win = K_WIN // GRID_W

    @pl.when(jnp.logical_and(b == 0, half == 0))
    def _build_bias():
        qc = lax.broadcasted_iota(jnp.int32, (GRID_W, LANES), 0)
        ln = lax.broadcasted_iota(jnp.int32, (GRID_W, LANES), 1)
        kc = ln % GRID_W
        cs = jnp.clip(qc - WIN_C // 2, 0, GRID_W - WIN_C)
        valid = jnp.logical_and(kc >= cs, kc < cs + WIN_C)
        dc = jnp.clip(kc - qc + (WIN_C - 1), 0, 2 * WIN_C - 2)
        neg = jnp.full((GRID_W, LANES), NEG_INF, F32)
        for e in range(2):
            h = 2 * hp + e
            blocks = []
            for a in range(2 * WIN_R - 1):
                def body(j, acc, a=a, h=h):
                    return jnp.where(dc == j, rpb_ref[(h * (2 * WIN_R - 1) + a) * (2 * WIN_C - 1) + j], acc)
                t = lax.fori_loop(0, 2 * WIN_C - 1, body, jnp.zeros((GRID_W, LANES), F32))
                blocks.append(jnp.where(valid, t, NEG_INF))
            for hf in range(2):
                for rl in range(rows_half):
                    r = hf * rows_half + rl
                    rs = min(max(r - kr_n // 2, 0), rows - kr_n)
                    for kp in range(krows_win // 2):
                        parts = []
                        for kl in (2 * kp, 2 * kp + 1):
                            kr = kl + hf * (rows - krows_win)
                            if rs <= kr < rs + kr_n:
                                parts.append(blocks[kr - r + WIN_R - 1])
                            else:
                                parts.append(neg)
                        blk = jnp.where(ln < GRID_W, parts[0], parts[1])
                        bias_ref[e, hf, rl * GRID_W:(rl + 1) * GRID_W, kp * LANES:(kp + 1) * LANES] = blk

    lane = lax.broadcasted_iota(jnp.int32, (1, LANES), 1)
    scale = HEAD_DIM ** -0.5
    k0 = pl.multiple_of(half * ((rows - krows_win) * GRID_W), GRID_W)
    kw = k_ref[pl.ds(k0, K_WIN), :].astype(BF16)
    vw = v_ref[pl.ds(k0, K_WIN), :]
    ck = ck_ref[0].astype(BF16)
    cv = cv_ref[0]
    SUB = 256
    m_e = (lane < 64, lane >= 64)
    ve = [jnp.where(m, vw, 0.0).astype(BF16) for m in m_e]
    cve = [jnp.where(m, cv, 0.0).astype(BF16) for m in m_e]
    units = [(qs, e) for qs in range(Q_HALF // SUB) for e in range(2)]
    nt = (((1,), (1,)), ((), ()))
    qe = {(qs, e): jnp.where(m_e[e], q_ref[qs * SUB:(qs + 1) * SUB, :] * scale, 0.0).astype(BF16) for qs, e in units}
    s_loc = {(qs, e): lax.dot_general(qe[qs, e], kw, nt, preferred_element_type=F32)
             + bias_ref[e, half, qs * SUB:(qs + 1) * SUB, :] for qs, e in units}
    s_ctx = {u: lax.dot_general(qe[u], ck, nt, preferred_element_type=F32) for u in units}
    m = {u: jnp.maximum(jnp.max(s_loc[u], axis=-1, keepdims=True), jnp.max(s_ctx[u], axis=-1, keepdims=True))
         for u in units}
    p_loc = {u: jnp.exp(s_loc[u] - m[u]) for u in units}
    p_ctx = {u: jnp.exp(s_ctx[u] - m[u]) for u in units}
    l = {u: jnp.sum(p_loc[u], axis=-1, keepdims=True) + jnp.sum(p_ctx[u], axis=-1, keepdims=True) for u in units}
    o = {(qs, e): (_bdot(p_loc[qs, e].astype(BF16), ve[e]) + _bdot(p_ctx[qs, e].astype(BF16), cve[e])) / l[qs, e]
         for qs, e in units}
    for qs in range(Q_HALF // SUB):
        o_ref[qs * SUB:(qs + 1) * SUB, :] = o[qs, 0] + o[qs, 1]


def _lat_attn(p_lat, ck_tm, cv_tm, rpb, batch, seq):
    rows = seq // GRID_W
    assert rows * GRID_W == seq and rows - K_WIN // GRID_W == 4 and Q_HALF * 2 == seq
    halves = seq // Q_HALF
    past = ck_tm.shape[1]
    kernel = functools.partial(_lat_attn_kernel, rows=rows)
    return pl.pallas_call(
        kernel,
        grid=(N_PAIRS, batch, halves),
        in_specs=[pl.BlockSpec(memory_space=pltpu.SMEM),
                  pl.BlockSpec((Q_HALF, LANES), lambda hp, b, hf: (b * halves + hf, COL_Q // LANES + hp)),
                  pl.BlockSpec((seq, LANES), lambda hp, b, hf: (b, COL_K // LANES + hp)),
                  pl.BlockSpec((seq, LANES), lambda hp, b, hf: (b, COL_V // LANES + hp)),
                  pl.BlockSpec((1, past, LANES), lambda hp, b, hf: (b, 0, hp)),
                  pl.BlockSpec((1, past, LANES), lambda hp, b, hf: (b, 0, hp))],
        out_specs=pl.BlockSpec((Q_HALF, LANES), lambda hp, b, hf: (b * halves + hf, hp)),
        out_shape=jax.ShapeDtypeStruct((batch * seq, NA_WIDTH), F32),
        scratch_shapes=[pltpu.VMEM((2, 2, Q_HALF, K_WIN), F32)],
        compiler_params=_params(3),
        name="lat_attn",
    )(rpb.reshape(-1), p_lat, p_lat, p_lat, ck_tm, cv_tm)


def _dn_kernel(*refs, seq, hps, a_chunks, has_s0, emit_state):
    it = iter(refs)
    dq_ref, dk_ref, dv_ref, ba_ref = next(it), next(it), next(it), next(it)
    cwq_ref, cwk_ref, cwv_ref, cst_ref = next(it), next(it), next(it), next(it)
    s0_ref = next(it) if has_s0 else None
    o_ref = next(it)
    st_ref = next(it) if emit_state else None
    (pad_s, q_s, k_s, v_s, beta_s, g_s, u_s, wq_s, lb_s, el_s, odir_s) = it

    hg = pl.program_id(1)
    wl = hps * LANES
    n = seq // CHUNK
    hp_sl = [slice(h * LANES, (h + 1) * LANES) for h in range(hps)]

    def lanes_cat(parts):
        return parts[0] if len(parts) == 1 else jnp.concatenate(parts, axis=1)

    r128 = lax.broadcasted_iota(jnp.int32, (LANES, LANES), 0)
    c128 = lax.broadcasted_iota(jnp.int32, (LANES, LANES), 1)
    bd_mask = (r128 // 64) == (c128 // 64)
    bd_ones = jnp.where(bd_mask, 1.0, 0.0).astype(BF16)

    def head_sums(x):
        return _dot2_l(x, bd_ones)

    zeros8 = jnp.zeros((8, wl), F32)
    pad_s[0:8, :] = zeros8
    pad_s[seq + 8:seq + 16, :] = zeros8

    def conv_silu(x_ref, w_ref):
        pad_s[8:seq + 8, :] = x_ref[...]
        acc = None
        for j in sorted(range(CONV_W), key=lambda t: abs(t - CONV_W // 2)):
            term = pad_s[8 - CONV_W // 2 + j: 8 - CONV_W // 2 + j + seq, :] * w_ref[j:j + 1, :]
            acc = term if acc is None else acc + term
        return _silu(acc)

    def l2n(x):
        return lanes_cat([x[:, sl] * lax.rsqrt(head_sums(x[:, sl] * x[:, sl]) + EPS) for sl in hp_sl])

    q_s[...] = l2n(conv_silu(dq_ref, cwq_ref))
    k_s[...] = l2n(conv_silu(dk_ref, cwk_ref))
    v_s[...] = conv_silu(dv_ref, cwv_ref)

    ba = ba_ref[...]
    beta_all = _sigmoid(ba)
    g_all = -jnp.exp(cst_ref[0:1, :]) * _softplus(ba + cst_ref[1:2, :])
    src = lax.broadcasted_iota(jnp.int32, (LANES, wl), 0)
    dst_head = 2 * hps * hg + lax.broadcasted_iota(jnp.int32, (LANES, wl), 1) // 64
    for d in range(2):
        sel_b = jnp.where(src == d * DN_HEADS + dst_head, 1.0, 0.0).astype(BF16)
        sel_g = jnp.where(src == (2 + d) * DN_HEADS + dst_head, 1.0, 0.0).astype(BF16)
        beta_s[d] = _dot2_l(beta_all, sel_b)
        g_s[d] = _dot3_l(g_all, sel_g)

    ri = lax.broadcasted_iota(jnp.int32, (CHUNK, LANES), 0)
    lj = lax.broadcasted_iota(jnp.int32, (CHUNK, LANES), 1) % 64
    lo_half = lax.broadcasted_iota(jnp.int32, (CHUNK, LANES), 1) < 64
    lo_half2 = (lax.broadcasted_iota(jnp.int32, (CHUNK, 2 * LANES), 1) % LANES) < 64
    diag = ri == lj
    strict = (ri > lj, ri < lj)
    riw = lax.broadcasted_iota(jnp.int32, (CHUNK, wl), 0)
    ljw = lax.broadcasted_iota(jnp.int32, (CHUNK, wl), 1) % 64
    diag_w = riw == ljw
    incl_w = (riw >= ljw, riw <= ljw)
    r64 = lax.broadcasted_iota(jnp.int32, (CHUNK, CHUNK), 0)
    c64 = lax.broadcasted_iota(jnp.int32, (CHUNK, CHUNK), 1)
    tri = (jnp.where(c64 <= r64, 1.0, 0.0).astype(BF16), jnp.where(c64 >= r64, 1.0, 0.0).astype(BF16))

    off_blk = ([], [])
    for lvl in range(6):
        s = 1 << lvl
        same2 = (ri // (2 * s)) == (lj // (2 * s))
        off_blk[0].append(jnp.logical_and(same2, (ri // s) == (lj // s) + 1))
        off_blk[1].append(jnp.logical_and(same2, (lj // s) == (ri // s) + 1))

    def stack_heads(x, lo_mask):
        return jnp.concatenate([jnp.where(lo_mask, x, 0.0), jnp.where(lo_mask, 0.0, x)], axis=0).astype(BF16)

    def chunk_rows(c):
        return pl.ds(c * CHUNK if isinstance(c, int) else pl.multiple_of(c * CHUNK, CHUNK), CHUNK)

    def chunk_load(c):
        rows = chunk_rows(c)
        return (q_s[rows, :], k_s[rows, :], v_s[rows, :],
                [beta_s[d, rows, :] for d in range(2)], [g_s[d, rows, :] for d in range(2)])

    def group_a(loaded):
        nc = len(loaded)
        cd = [(i, d) for i in range(nc) for d in range(2)]
        units = [(i, h, d) for i in range(nc) for h in range(hps) for d in range(2)]
        qcs = [ld[0] * (DN_DK ** -0.5) for ld in loaded]
        kcs = [ld[1] for ld in loaded]
        vcs = [ld[2] for ld in loaded]
        kt_bd = {}
        for i in range(nc):
            for h in range(hps):
                kc = kcs[i][:, hp_sl[h]]
                ktt = jnp.transpose(jnp.concatenate([kc, kc], axis=0))
                kt_bd[i, h] = jnp.where(bd_mask, ktt, 0.0)
        gsp = {(i, d): _split3(loaded[i][4][d]) for i, d in cd}
        gc = {}
        for part in range(3):
            for i, d in cd:
                t = _bdot(tri[d], gsp[i, d][part])
                gc[i, d] = t if part == 0 else gc[i, d] + t
        rrow = {u: jnp.sum(jnp.where(diag_w, gc[u], 0.0), axis=0, keepdims=True) for u in cd}
        dec = {u: jnp.exp(jnp.minimum(gc[u] - rrow[u], 0.0)) for u in cd}
        eg = {u: jnp.exp(gc[u]) for u in cd}
        kbeta = {(i, d): kcs[i] * loaded[i][3][d] for i, d in cd}
        vbeta = {(i, d): vcs[i] * loaded[i][3][d] for i, d in cd}
        kbeg = {u: kbeta[u] * eg[u] for u in cd}
        m1 = {(i, h): _bdot(jnp.concatenate([qcs[i][:, hp_sl[h]], kbeta[i, 0][:, hp_sl[h]],
                                             kbeta[i, 1][:, hp_sl[h]]], axis=0).astype(BF16),
                            kt_bd[i, h].astype(BF16))
              for i in range(nc) for h in range(hps)}
        amat = {(i, h, d): jnp.where(strict[d], m1[i, h][CHUNK * (1 + d):CHUNK * (2 + d)] * dec[i, d][:, hp_sl[h]], 0.0)
                for i, h, d in units}
        o_st = {(i, h, d): stack_heads(jnp.where(off_blk[d][0], amat[i, h, d], 0.0), lo_half) for i, h, d in units}
        tmat = {(i, h, d): jnp.where(diag, 1.0, 0.0) - jnp.where(off_blk[d][0], amat[i, h, d], 0.0) for i, h, d in units}
        prod = {u: _bdot(amat[u].astype(BF16), o_st[u]) for u in units}
        wmat = {u: amat[u] - prod[u] for u in units}
        for lvl in range(1, 5):
            o_st = {(i, h, d): stack_heads(jnp.where(off_blk[d][lvl], wmat[i, h, d], 0.0), lo_half) for i, h, d in units}
            prod = {u: _bdot(jnp.concatenate([tmat[u], wmat[u]], axis=0).astype(BF16), o_st[u]) for u in units}
            tmat = {u: tmat[u] - prod[u][0:CHUNK] for u in units}
            wmat = {u: wmat[u] - prod[u][CHUNK:] for u in units}
        o_st = {(i, h, d): stack_heads(jnp.where(off_blk[d][5], wmat[i, h, d], 0.0), lo_half) for i, h, d in units}
        prod = {u: _bdot(tmat[u].astype(BF16), o_st[u]) for u in units}
        tmat = {u: tmat[u] - prod[u] for u in units}
        x = {}
        for i, h, d in units:
            rhs = jnp.concatenate([vbeta[i, d][:, hp_sl[h]], kbeg[i, d][:, hp_sl[h]]], axis=1)
            x[i, h, d] = _bdot(tmat[i, h, d].astype(BF16), stack_heads(rhs, lo_half2))
        results = []
        for i in range(nc):
            out = []
            for d in range(2):
                u = lanes_cat([x[i, h, d][:, :LANES] for h in range(hps)])
                w = lanes_cat([x[i, h, d][:, LANES:] for h in range(hps)])
                qk = jnp.where(incl_w[d], lanes_cat([m1[i, h][0:CHUNK] for h in range(hps)]) * dec[i, d], 0.0)
                qg = qcs[i] * eg[i, d]
                glast = gc[i, d][CHUNK - 1:CHUNK, :] if d == 0 else gc[i, d][0:1, :]
                kmul = jnp.exp(glast - rrow[i, d])
                kd_bd = lanes_cat([kt_bd[i, h] * kmul[:, hp_sl[h]] for h in range(hps)])
                out.append((u,
                            jnp.concatenate([w, qg], axis=0).astype(BF16),
                            jnp.concatenate([qk, kd_bd], axis=0).astype(BF16),
                            jnp.broadcast_to(jnp.exp(glast), (8, wl))))
            results.append(out)
        return results

    group = min(a_chunks, n)

    def phase_a(gi, carry):
        cs = [gi * group + j for j in range(group)]
        loaded = [chunk_load(c) for c in cs]
        results = group_a(loaded)
        for c, res in zip(cs, results):
            for d in range(2):
                u_s[d, c], wq_s[d, c], lb_s[d, c], el_s[d, c] = res[d]
        return carry

    if n == group:
        phase_a(0, 0)
    else:
        lax.fori_loop(0, n // group, phase_a, 0)

    dh = [(d, h) for d in range(2) for h in range(hps)]
    if has_s0:
        zero = jnp.zeros((DN_DK, DN_DV), F32)

        def bd_state(d, h):
            top = jnp.concatenate([s0_ref[0, 0, d, 2 * h], zero], axis=1)
            bot = jnp.concatenate([zero, s0_ref[0, 0, d, 2 * h + 1]], axis=1)
            return jnp.concatenate([top, bot], axis=0)
        s_init = tuple(bd_state(d, h) for d, h in dh)
    else:
        s_init = tuple(jnp.zeros((LANES, LANES), F32) for _ in dh)

    def phase_b(s, states):
        cs = (s, n - 1 - s)
        loaded = [(wq_s[d, cs[d]], u_s[d, cs[d]], lb_s[d, cs[d]], el_s[d, cs[d], 0:1, :]) for d in range(2)]
        st = dict(zip(dh, states))
        ws = {(d, h): _bdot(loaded[d][0][:, hp_sl[h]], st[d, h].astype(BF16)) for d, h in dh}
        vn = {(d, h): loaded[d][1][:, hp_sl[h]] - ws[d, h][0:CHUNK] for d, h in dh}
        r2 = {(d, h): _bdot(loaded[d][2][:, hp_sl[h]], stack_heads(vn[d, h], lo_half)) for d, h in dh}
        new_states = tuple(st[d, h] * loaded[d][3][:, hp_sl[h]] + r2[d, h][CHUNK:] for d, h in dh)
        for d in range(2):
            odir_s[d, chunk_rows(cs[d]), :] = lanes_cat([ws[d, h][CHUNK:] + r2[d, h][0:CHUNK] for h in range(hps)])
        return new_states

    s_fin = dict(zip(dh, lax.fori_loop(0, n, phase_b, s_init)))

    for h in range(hps):
        o = odir_s[0, :, hp_sl[h]] + odir_s[1, :, hp_sl[h]]
        msq = head_sums(o * o) * (1.0 / DN_DV)
        o_ref[:, hp_sl[h]] = o * lax.rsqrt(msq + EPS) * cst_ref[2:3, :]

    if emit_state:
        for d, h in dh:
            st_ref[0, 0, d, 2 * h] = s_fin[d, h][0:DN_DK, 0:DN_DV]
            st_ref[0, 0, d, 2 * h + 1] = s_fin[d, h][DN_DK:, DN_DV:]


def _deltanet(p, conv_w8, cst, batch, seq, hps, a_chunks, s0=None, emit_state=False):
    n = seq // CHUNK
    wl = hps * LANES
    groups = N_PAIRS // hps
    has_s0 = s0 is not None

    def col(cbase):
        return pl.BlockSpec((seq, wl), lambda b, hg: (b, cbase // wl + hg))

    def cw(cbase):
        return pl.BlockSpec((8, wl), lambda b, hg: (0, cbase // wl + hg))

    st_spec = pl.BlockSpec((1, 1, 2, 2 * hps, DN_DK, DN_DV), lambda b, hg: (b, 0, 0, hg, 0, 0))
    in_specs = [col(COL_DQ), col(COL_DK), col(COL_DV),
                pl.BlockSpec((seq, LANES), lambda b, hg: (b, COL_BA // LANES)),
                cw(0), cw(512), cw(1024),
                pl.BlockSpec((8, LANES), lambda b, hg: (0, 0))]
    args = [p, p, p, p, conv_w8, conv_w8, conv_w8, cst]
    if has_s0:
        in_specs.append(st_spec)
        args.append(s0)
    out_specs = [pl.BlockSpec((seq, wl), lambda b, hg: (b, hg))]
    out_shape = [jax.ShapeDtypeStruct((batch * seq, DN_HEADS * DN_DV), F32)]
    if emit_state:
        out_specs.append(st_spec)
        out_shape.append(jax.ShapeDtypeStruct((batch, 1, 2, DN_HEADS, DN_DK, DN_DV), F32))
    scratch = [pltpu.VMEM((seq + 16, wl), F32),
               pltpu.VMEM((seq, wl), F32), pltpu.VMEM((seq, wl), F32), pltpu.VMEM((seq, wl), F32),
               pltpu.VMEM((2, seq, wl), F32), pltpu.VMEM((2, seq, wl), F32),
               pltpu.VMEM((2, n, CHUNK, wl), F32),
               pltpu.VMEM((2, n, 2 * CHUNK, wl), BF16),
               pltpu.VMEM((2, n, 3 * CHUNK, wl), BF16),
               pltpu.VMEM((2, n, 8, wl), F32),
               pltpu.VMEM((2, seq, wl), F32)]
    res = pl.pallas_call(
        functools.partial(_dn_kernel, seq=seq, hps=hps, a_chunks=a_chunks, has_s0=has_s0,
                          emit_state=emit_state),
        grid=(batch, groups),
        in_specs=in_specs,
        out_specs=out_specs,
        out_shape=out_shape,
        scratch_shapes=scratch,
        compiler_params=_params(2),
        name="deltanet_lat" if has_s0 else "deltanet_ctx",
    )(*args)
    return res


def _merge_kernel(x_ref, ona_ref, odn_ref, z_ref, gna_ref, gdn_ref, mod_ref, wpost_ref,
                  wao_ref, wdo_ref, wout_ref, o_ref):
    mod = mod_ref[0]
    odn = odn_ref[...] * _silu(z_ref[...])
    br_na = _bdot(ona_ref[...].astype(BF16), wao_ref[...])
    br_dn = _bdot(odn.astype(BF16), wdo_ref[...])
    m = _sigmoid(gna_ref[...]) * br_na + _sigmoid(gdn_ref[...]) * br_dn
    y = _bdot(m.astype(BF16), wout_ref[...])
    o_ref[...] = x_ref[...] + mod[2:3] * _rms_rows(y, wpost_ref[...])


def _merge(x2d, o_na, o_dn, p, mod3, mod_row_of_tile, wpost, w_ao, w_do, w_out):
    n_tok = x2d.shape[0]
    row = lambda i: (i, 0)
    const = lambda i: (0, 0)
    return pl.pallas_call(
        _merge_kernel,
        grid=(n_tok // TM,),
        in_specs=[pl.BlockSpec((TM, D_MODEL), row),
                  pl.BlockSpec((TM, NA_WIDTH), row),
                  pl.BlockSpec((TM, NA_WIDTH), row),
                  pl.BlockSpec((TM, 512), lambda i: (i, COL_Z // 512)),
                  pl.BlockSpec((TM, D_MODEL), lambda i: (i, COL_GNA // D_MODEL)),
                  pl.BlockSpec((TM, D_MODEL), lambda i: (i, COL_GDN // D_MODEL)),
                  pl.BlockSpec((1, N_MOD, D_MODEL), lambda i: (mod_row_of_tile(i), 0, 0)),
                  pl.BlockSpec((1, D_MODEL), const),
                  pl.BlockSpec((NA_WIDTH, D_MODEL), const),
                  pl.BlockSpec((NA_WIDTH, D_MODEL), const),
                  pl.BlockSpec((D_MODEL, D_MODEL), const)],
        out_specs=pl.BlockSpec((TM, D_MODEL), row),
        out_shape=jax.ShapeDtypeStruct((n_tok, D_MODEL), F32),
        compiler_params=_params(1),
        name="merge",
    )(x2d, o_na, o_dn, p, p, p, mod3, wpost, w_ao, w_do, w_out)


def _ffn_kernel(x_ref, mod_ref, wpre_ref, wpost_ref, w1_ref, w2_ref, o_ref):
    x = x_ref[...]
    mod = mod_ref[0]
    h = _rms_rows(x, wpre_ref[...]) * (1.0 + mod[4:5]) + mod[3:4]
    f1 = _bdot(h.astype(BF16), w1_ref[...])
    r = jnp.maximum(f1, 0.0)
    f = _bdot((r * r).astype(BF16), w2_ref[...])
    o_ref[...] = x + mod[5:6] * _rms_rows(f, wpost_ref[...])


def _ffn(x2d, mod3, mod_row_of_tile, wpre, wpost, w1, w2):
    n_tok = x2d.shape[0]
    row = lambda i: (i, 0)
    const = lambda i: (0, 0)
    return pl.pallas_call(
        _ffn_kernel,
        grid=(n_tok // TM,),
        in_specs=[pl.BlockSpec((TM, D_MODEL), row),
                  pl.BlockSpec((1, N_MOD, D_MODEL), lambda i: (mod_row_of_tile(i), 0, 0)),
                  pl.BlockSpec((1, D_MODEL), const),
                  pl.BlockSpec((1, D_MODEL), const),
                  pl.BlockSpec((D_MODEL, D_FF), const),
                  pl.BlockSpec((D_FF, D_MODEL), const)],
        out_specs=pl.BlockSpec((TM, D_MODEL), row),
        out_shape=jax.ShapeDtypeStruct((n_tok, D_MODEL), F32),
        compiler_params=_params(1),
        name="ffn",
    )(x2d, mod3, wpre, wpost, w1, w2)


def _pack_w_in(w):
    pad = jnp.zeros((w.shape[0], LANES - 32), w.dtype)
    packed = jnp.concatenate([w[:, 0:3072], w[:, 3616:5664], w[:, 3104:3616], w[:, 3072:3104], pad], axis=1)
    return packed.astype(BF16)


def kernel(x_prompt, x_sample, c, cache_na_k, cache_na_v, state_delta, c_ctx, w_ada, b_ada, norm_pre1, norm_post1, norm_pre2, norm_post2, w_in, conv_w, a_log, dt_bias, dn_norm, na_rpb, w_ao, w_do, w_out, w_ff1, w_ff2):
    batch, seq, _ = x_prompt.shape
    dec_batch, dec_seq, _ = x_sample.shape
    depth = w_in.shape[0]
    assert depth == 1 and seq == TM and dec_seq % TM == 0 and dec_batch < 8

    xp = x_prompt.reshape(batch * seq, D_MODEL)
    xs = x_sample.reshape(dec_batch * dec_seq, D_MODEL)
    l = 0

    cvecs = jnp.zeros((8, D_MODEL), F32).at[:dec_batch].set(c).at[dec_batch].set(c_ctx)
    mod3 = _adaln(cvecs, w_ada[l], b_ada[l]).reshape(8, N_MOD, D_MODEL)
    ctx_row = lambda i: dec_batch
    tiles_per_lat = dec_seq // TM
    lat_row = lambda i: i // tiles_per_lat

    w_in_p = _pack_w_in(w_in[l])
    w_ao_b = w_ao[l].astype(BF16)
    w_do_b = w_do[l].astype(BF16)
    w_out_b = w_out[l].astype(BF16)
    w1_b = w_ff1[l].astype(BF16)
    w2_b = w_ff2[l].astype(BF16)
    wpre1 = norm_pre1[l].reshape(1, D_MODEL)
    wpost1 = norm_post1[l].reshape(1, D_MODEL)
    wpre2 = norm_pre2[l].reshape(1, D_MODEL)
    wpost2 = norm_post2[l].reshape(1, D_MODEL)

    conv_w8 = jnp.zeros((8, conv_w.shape[2]), F32).at[:CONV_W].set(conv_w[l])
    cst = jnp.zeros((8, LANES), F32)
    cst = cst.at[0, 2 * DN_HEADS:4 * DN_HEADS].set(a_log[l].reshape(-1))
    cst = cst.at[1, 2 * DN_HEADS:4 * DN_HEADS].set(dt_bias[l].reshape(-1))
    cst = cst.at[2].set(jnp.tile(dn_norm[l], 2))

    p_ctx = _inproj(xp, mod3, ctx_row, wpre1, w_in_p)
    o_na_c, new_k, new_v = _ctx_attn(p_ctx, batch, seq)
    o_dn_c, new_s = _deltanet(p_ctx, conv_w8, cst, batch, seq, hps=4, a_chunks=4, emit_state=True)
    x1_c = _merge(xp, o_na_c, o_dn_c, p_ctx, mod3, ctx_row, wpost1, w_ao_b, w_do_b, w_out_b)
    y_c = _ffn(x1_c, mod3, ctx_row, wpre2, wpost2, w1_b, w2_b)

    p_lat = _inproj(xs, mod3, lat_row, wpre1, w_in_p, rope_tabs=_rope_tables(dec_seq))
    past = cache_na_k.shape[3]
    ck_tm = jnp.transpose(cache_na_k[:, l], (0, 2, 1, 3)).reshape(dec_batch, past, NA_WIDTH)
    cv_tm = jnp.transpose(cache_na_v[:, l], (0, 2, 1, 3)).reshape(dec_batch, past, NA_WIDTH)
    o_na_l = _lat_attn(p_lat, ck_tm, cv_tm, na_rpb[l], dec_batch, dec_seq)
    (o_dn_l,) = _deltanet(p_lat, conv_w8, cst, dec_batch, dec_seq, hps=2, a_chunks=4, s0=state_delta)
    x1_l = _merge(xs, o_na_l, o_dn_l, p_lat, mod3, lat_row, wpost1, w_ao_b, w_do_b, w_out_b)
    y_l = _ffn(x1_l, mod3, lat_row, wpre2, wpost2, w1_b, w2_b)

    return (y_c.reshape(batch, seq, D_MODEL), y_l.reshape(dec_batch, dec_seq, D_MODEL),
            new_k, new_v, new_s)
```

```python
import functools

import numpy as np
import jax
import jax.numpy as jnp
from jax import lax
from jax.experimental import pallas as pl
from jax.experimental.pallas import tpu as pltpu

F32 = jnp.float32
BF16 = jnp.bfloat16

D_MODEL = 1024
N_MOD = 6
GRID_W = 64
NA_HEADS = 8
HEAD_DIM = 64
NA_WIDTH = NA_HEADS * HEAD_DIM
WIN_R = 8
WIN_C = 16
ROPE_BASE = 10000.0
DN_HEADS = 8
DN_DK = 64
DN_DV = 64
DN_WIDTH = DN_HEADS * DN_DK
CONV_W = 5
CHUNK = 64
D_FF = 4 * D_MODEL
EPS = 1e-6
NEG_INF = -1e30

LANES = 128
SUBLANES = 8
N_PAIRS = NA_HEADS // 2
TM = 256
VMEM_LIMIT = 56 * 1024 * 1024

QKV_W = 3 * NA_WIDTH
DQKV_W = 3 * DN_WIDTH
GZ_W = 2 * D_MODEL + DN_WIDTH
COL_QKV, COL_DQKV, COL_GZ = 0, QKV_W, QKV_W + DQKV_W
COL_BA = COL_GZ + GZ_W
P_COLS = COL_BA + LANES


def _params(n_grid):
    return pltpu.CompilerParams(dimension_semantics=("arbitrary",) * n_grid,
                                vmem_limit_bytes=VMEM_LIMIT)


def _bdot(a, b):
    return jnp.dot(a, b, preferred_element_type=F32)


def _split3(a):
    hi = a.astype(BF16)
    r1 = a - hi.astype(F32)
    mid = r1.astype(BF16)
    lo = (r1 - mid.astype(F32)).astype(BF16)
    return hi, mid, lo


def _dot3_l(a, b_exact):
    hi, mid, lo = _split3(a)
    return _bdot(hi, b_exact) + _bdot(mid, b_exact) + _bdot(lo, b_exact)


def _dot2_l(a, b_exact):
    hi = a.astype(BF16)
    lo = (a - hi.astype(F32)).astype(BF16)
    return _bdot(hi, b_exact) + _bdot(lo, b_exact)


def _sigmoid(x):
    return 0.5 * jnp.tanh(0.5 * x) + 0.5


def _silu(x):
    return x * _sigmoid(x)


def _softplus(x):
    return jnp.maximum(x, 0.0) + jnp.log(1.0 + jnp.exp(-jnp.abs(x)))


def _rms_rows(x, w):
    ms = jnp.mean(x * x, axis=-1, keepdims=True)
    return x * lax.rsqrt(ms + EPS) * w


def _pair_block_ones():
    r = lax.broadcasted_iota(jnp.int32, (LANES, LANES), 0)
    c = lax.broadcasted_iota(jnp.int32, (LANES, LANES), 1)
    return jnp.where((r // 64) == (c // 64), 1.0, 0.0).astype(BF16)


def _adaln_kernel(c_ref, w_ref, b_ref, o_ref):
    c = c_ref[...]
    s = _silu(c).astype(BF16)
    o_ref[...] = _bdot(s, w_ref[...].astype(BF16)) + b_ref[...]


def _adaln(cvecs, w_ada, b_ada):
    tn = 1024
    n = w_ada.shape[1]
    return pl.pallas_call(
        _adaln_kernel,
        grid=(n // tn,),
        in_specs=[pl.BlockSpec((8, D_MODEL), lambda j: (0, 0)),
                  pl.BlockSpec((D_MODEL, tn), lambda j: (0, j)),
                  pl.BlockSpec((1, tn), lambda j: (0, j))],
        out_specs=pl.BlockSpec((8, tn), lambda j: (0, j)),
        out_shape=jax.ShapeDtypeStruct((8, n), F32),
        compiler_params=_params(1),
        name="adaln",
    )(cvecs, w_ada, b_ada.reshape(1, n))


def _inproj_kernel(*refs, latent, tiles_per_seq):
    it = iter(refs)
    x_ref, mod_ref, wpre_ref, w_ref, cw_ref = next(it), next(it), next(it), next(it), next(it)
    if latent:
        xprev_ref, xnext_ref, cos_ref, sin_ref = next(it), next(it), next(it), next(it)
    qkv_ref, dqkv_ref, gz_ref, ba_ref = next(it), next(it), next(it), next(it)
    if not latent:
        kc_ref, vc_ref = next(it), next(it)

    mod = mod_ref[0]

    def modulated(x):
        return (_rms_rows(x, wpre_ref[...]) * (1.0 + mod[1:2]) + mod[0:1]).astype(BF16)

    hb = modulated(x_ref[...])

    w_dn = w_ref[:, COL_DQKV:COL_DQKV + DQKV_W]
    halo_zero = jnp.zeros((SUBLANES, DQKV_W), F32)
    if latent:
        t = pl.program_id(0) % tiles_per_seq
        hb_ext = jnp.concatenate([modulated(xprev_ref[...]), hb, modulated(xnext_ref[...])], axis=0)
        pe = _bdot(hb_ext, w_dn)
        pe = jnp.concatenate([jnp.where(t > 0, pe[:SUBLANES], halo_zero), pe[SUBLANES:SUBLANES + TM],
                              jnp.where(t < tiles_per_seq - 1, pe[SUBLANES + TM:], halo_zero)], axis=0)
    else:
        pe = jnp.concatenate([halo_zero, _bdot(hb, w_dn), halo_zero], axis=0)
    acc = None
    for j in sorted(range(CONV_W), key=lambda s: abs(s - CONV_W // 2)):
        lo = SUBLANES - CONV_W // 2 + j
        term = pe[lo:lo + TM] * cw_ref[j:j + 1, :]
        acc = term if acc is None else acc + term
    y = _silu(acc)

    p = _bdot(hb, w_ref[:, COL_QKV:COL_QKV + QKV_W])
    if latent:
        qk = p[:, :2 * NA_WIDTH]
        lane = lax.broadcasted_iota(jnp.int32, qk.shape, 1)
        first = (lane % 32) < 16
        partner = jnp.where(first, pltpu.roll(qk, 2 * NA_WIDTH - 16, 1), pltpu.roll(qk, 16, 1))
        qkv_ref[:, :2 * NA_WIDTH] = (qk * cos_ref[...] + partner * sin_ref[...]).astype(BF16)
        qkv_ref[:, 2 * NA_WIDTH:] = p[:, 2 * NA_WIDTH:].astype(BF16)
    else:
        qkv_ref[...] = p.astype(BF16)
        for h in range(NA_HEADS):
            kc_ref[0, 0, h] = p[:, NA_WIDTH + h * HEAD_DIM:NA_WIDTH + (h + 1) * HEAD_DIM]
            vc_ref[0, 0, h] = p[:, 2 * NA_WIDTH + h * HEAD_DIM:2 * NA_WIDTH + (h + 1) * HEAD_DIM]

    half = GZ_W // 2
    for lo in (0, half):
        gz_ref[:, lo:lo + half] = _bdot(hb, w_ref[:, COL_GZ + lo:COL_GZ + lo + half]).astype(BF16)
    ba_ref[...] = _bdot(hb, w_ref[:, COL_BA:COL_BA + LANES])

    bd_ones = _pair_block_ones()
    for s in range(DQKV_W // LANES):
        ys = y[:, s * LANES:(s + 1) * LANES]
        if s < 2 * DN_WIDTH // LANES:
            ys = ys * lax.rsqrt(_dot2_l(ys * ys, bd_ones) + EPS)
        dqkv_ref[:, s * LANES:(s + 1) * LANES] = ys.astype(BF16)


def _inproj(x2d, mod3, mod_row_of_tile, wpre, w_packed, conv_w8, seq, batch, rope_tabs=None):
    n_tok = x2d.shape[0]
    latent = rope_tabs is not None
    tiles_per_seq = seq // TM
    rows8 = TM // SUBLANES
    n_blk8 = n_tok // SUBLANES
    in_specs = [pl.BlockSpec((TM, D_MODEL), lambda i: (i, 0)),
                pl.BlockSpec((1, N_MOD, D_MODEL), lambda i: (mod_row_of_tile(i), 0, 0)),
                pl.BlockSpec((1, D_MODEL), lambda i: (0, 0)),
                pl.BlockSpec((D_MODEL, P_COLS), lambda i: (0, 0)),
                pl.BlockSpec((8, DQKV_W), lambda i: (0, 0))]
    args = [x2d, mod3, wpre, w_packed, conv_w8]
    out_specs = [pl.BlockSpec((TM, QKV_W), lambda i: (i, 0)),
                 pl.BlockSpec((TM, DQKV_W), lambda i: (i, 0)),
                 pl.BlockSpec((TM, GZ_W), lambda i: (i, 0)),
                 pl.BlockSpec((TM, LANES), lambda i: (i, 0))]
    out_shape = [jax.ShapeDtypeStruct((n_tok, QKV_W), BF16),
                 jax.ShapeDtypeStruct((n_tok, DQKV_W), BF16),
                 jax.ShapeDtypeStruct((n_tok, GZ_W), BF16),
                 jax.ShapeDtypeStruct((n_tok, LANES), F32)]
    if latent:
        in_specs += [pl.BlockSpec((SUBLANES, D_MODEL), lambda i: (jnp.maximum(i * rows8 - 1, 0), 0)),
                     pl.BlockSpec((SUBLANES, D_MODEL), lambda i: (jnp.minimum((i + 1) * rows8, n_blk8 - 1), 0))]
        args += [x2d, x2d]
        for tab in rope_tabs:
            in_specs.append(pl.BlockSpec((TM, 2 * NA_WIDTH), lambda i: (i % tiles_per_seq, 0)))
            args.append(tab)
    else:
        assert seq == TM
        cache_spec = pl.BlockSpec((1, 1, NA_HEADS, seq, HEAD_DIM), lambda i: (i, 0, 0, 0, 0))
        cache_shape = jax.ShapeDtypeStruct((batch, 1, NA_HEADS, seq, HEAD_DIM), F32)
        out_specs += [cache_spec, cache_spec]
        out_shape += [cache_shape, cache_shape]
    return pl.pallas_call(
        functools.partial(_inproj_kernel, latent=latent, tiles_per_seq=tiles_per_seq),
        grid=(n_tok // TM,),
        in_specs=in_specs,
        out_specs=out_specs,
        out_shape=out_shape,
        compiler_params=_params(1),
        name="inproj_lat" if latent else "inproj_ctx",
    )(*args)


def _rope_tables(seq):
    half = HEAD_DIM // 2
    nf = half // 2
    pos = np.arange(seq)
    inv = (np.float32(ROPE_BASE) ** (-np.arange(nf, dtype=np.float32) / np.float32(nf))).astype(np.float32)
    ang_r = ((pos // GRID_W).astype(np.float32)[:, None] * inv).astype(np.float64)
    ang_c = ((pos % GRID_W).astype(np.float32)[:, None] * inv).astype(np.float64)

    def grp(ang):
        return (np.concatenate([np.cos(ang), np.cos(ang)], 1),
                np.concatenate([-np.sin(ang), np.sin(ang)], 1))

    cr, sr = grp(ang_r)
    cc, sc = grp(ang_c)
    cos_h = np.concatenate([cr, cc], 1)
    sin_h = np.concatenate([sr, sc], 1)
    reps = 2 * NA_HEADS
    return (jnp.asarray(np.tile(cos_h, (1, reps)), F32),
            jnp.asarray(np.tile(sin_h, (1, reps)), F32))


def _ctx_attn_kernel(q_ref, k_ref, v_ref, o_ref):
    lane = lax.broadcasted_iota(jnp.int32, (1, LANES), 1)
    scale = HEAD_DIM ** -0.5
    heads = [(hp, e) for hp in range(N_PAIRS) for e in range(2)]
    sls = [slice(hp * LANES, (hp + 1) * LANES) for hp in range(N_PAIRS)]
    q = [q_ref[:, sl] for sl in sls]
    k = [k_ref[:, sl] for sl in sls]
    v = [v_ref[:, sl] for sl in sls]
    m_e = (lane < 64, lane >= 64)
    zero = jnp.zeros((), BF16)
    s = {(hp, e): lax.dot_general(jnp.where(m_e[e], q[hp] * scale, zero), k[hp],
                                  (((1,), (1,)), ((), ())), preferred_element_type=F32) for hp, e in heads}
    p = {u: jnp.exp(s[u] - jnp.max(s[u], axis=-1, keepdims=True)) for u in heads}
    l = {u: jnp.sum(p[u], axis=-1, keepdims=True) for u in heads}
    o = {(hp, e): _bdot(p[hp, e].astype(BF16), jnp.where(m_e[e], v[hp], zero)) for hp, e in heads}
    for hp in range(N_PAIRS):
        o_ref[:, sls[hp]] = (o[hp, 0] / l[hp, 0] + o[hp, 1] / l[hp, 1]).astype(BF16)


def _ctx_attn(qkv, batch, seq):
    def col(cb):
        return pl.BlockSpec((seq, NA_WIDTH), lambda b: (b, cb))
    return pl.pallas_call(
        _ctx_attn_kernel,
        grid=(batch,),
        in_specs=[col(0), col(1), col(2)],
        out_specs=pl.BlockSpec((seq, NA_WIDTH), lambda b: (b, 0)),
        out_shape=jax.ShapeDtypeStruct((batch * seq, NA_WIDTH), BF16),
        compiler_params=_params(1),
        name="ctx_attn",
    )(qkv, qkv, qkv)


Q_HALF = 512
K_WIN = 768


def _lat_attn_kernel(rpb_ref, q_ref, k_ref, v_ref, ck_ref, cv_ref, o_ref, bias_ref, *, rows):
    hp = pl.program_id(0)
    b = pl.program_id(1)
    half = pl.program_id(2)
    kr_n = min(WIN_R, rows)
    rows_half = Q_HALF // GRID_W
    krows_win = K_WIN // GRID_W
    n_dr = 2 * WIN_R - 1
    n_dc = 2 * WIN_C - 1

    @pl.when(jnp.logical_and(b == 0, half == 0))
    def _build_bias():
        qc = lax.broadcasted_iota(jnp.int32, (GRID_W, LANES), 0)
        ln = lax.broadcasted_iota(jnp.int32, (GRID_W, LANES), 1)
        kc = ln % GRID_W
        cs = jnp.clip(qc - WIN_C // 2, 0, GRID_W - WIN_C)
        valid = jnp.logical_and(kc >= cs, kc < cs + WIN_C)
        dc = jnp.clip(kc - qc + (WIN_C - 1), 0, n_dc - 1)
        dcv = jnp.where(valid, dc, -1)
        neg = jnp.full((GRID_W, LANES), NEG_INF, F32)
        for e in range(2):
            blocks = []
            for a in range(n_dr):
                blk = neg
                for j in range(n_dc):
                    blk = jnp.where(dcv == j, rpb_ref[((2 * hp + e) * n_dr + a) * n_dc + j], blk)
                blocks.append(blk)
            for hf in range(2):
                for rl in range(rows_half):
                    r = hf * rows_half + rl
                    rs = min(max(r - kr_n // 2, 0), rows - kr_n)
                    for kp in range(krows_win // 2):
                        parts = []
                        for kl in (2 * kp, 2 * kp + 1):
                            kr = kl + hf * (rows - krows_win)
                            parts.append(blocks[kr - r + WIN_R - 1] if rs <= kr < rs + kr_n else neg)
                        blk = jnp.where(ln < GRID_W, parts[0], parts[1])
                        bias_ref[e, hf, rl * GRID_W:(rl + 1) * GRID_W, kp * LANES:(kp + 1) * LANES] = blk

    lane = lax.broadcasted_iota(jnp.int32, (1, LANES), 1)
    scale = HEAD_DIM ** -0.5
    zero = jnp.zeros((), BF16)
    k0 = pl.multiple_of(half * ((rows - krows_win) * GRID_W), GRID_W)
    kw = k_ref[pl.ds(k0, K_WIN), :]
    vw = v_ref[pl.ds(k0, K_WIN), :]
    ck = ck_ref[0].astype(BF16)
    cv = cv_ref[0].astype(BF16)
    SUB = 256
    m_e = (lane < 64, lane >= 64)
    ve = [jnp.where(m, vw, zero) for m in m_e]
    cve = [jnp.where(m, cv, zero) for m in m_e]
    units = [(qs, e) for qs in range(Q_HALF // SUB) for e in range(2)]
    nt = (((1,), (1,)), ((), ()))
    qe = {(qs, e): jnp.where(m_e[e], q_ref[qs * SUB:(qs + 1) * SUB, :] * scale, zero) for qs, e in units}
    s_loc = {(qs, e): lax.dot_general(qe[qs, e], kw, nt, preferred_element_type=F32)
             + bias_ref[e, half, qs * SUB:(qs + 1) * SUB, :] for qs, e in units}
    s_ctx = {u: lax.dot_general(qe[u], ck, nt, preferred_element_type=F32) for u in units}
    m = {u: jnp.maximum(jnp.max(s_loc[u], axis=-1, keepdims=True), jnp.max(s_ctx[u], axis=-1, keepdims=True))
         for u in units}
    p_loc = {u: jnp.exp(s_loc[u] - m[u]) for u in units}
    p_ctx = {u: jnp.exp(s_ctx[u] - m[u]) for u in units}
    l = {u: jnp.sum(p_loc[u], axis=-1, keepdims=True) + jnp.sum(p_ctx[u], axis=-1, keepdims=True) for u in units}
    o = {(qs, e): (_bdot(p_loc[qs, e].astype(BF16), ve[e]) + _bdot(p_ctx[qs, e].astype(BF16), cve[e])) / l[qs, e]
         for qs, e in units}
    for qs in range(Q_HALF // SUB):
        o_ref[qs * SUB:(qs + 1) * SUB, :] = (o[qs, 0] + o[qs, 1]).astype(BF16)


def _lat_attn(qkv, ck_tm, cv_tm, rpb, batch, seq):
    rows = seq // GRID_W
    assert rows * GRID_W == seq and rows - K_WIN // GRID_W == 4 and Q_HALF * 2 == seq
    halves = seq // Q_HALF
    past = ck_tm.shape[1]
    kernel = functools.partial(_lat_attn_kernel, rows=rows)
    return pl.pallas_call(
        kernel,
        grid=(N_PAIRS, batch, halves),
        in_specs=[pl.BlockSpec(memory_space=pltpu.SMEM),
                  pl.BlockSpec((Q_HALF, LANES), lambda hp, b, hf: (b * halves + hf, hp)),
                  pl.BlockSpec((seq, LANES), lambda hp, b, hf: (b, N_PAIRS + hp)),
                  pl.BlockSpec((seq, LANES), lambda hp, b, hf: (b, 2 * N_PAIRS + hp)),
                  pl.BlockSpec((1, past, LANES), lambda hp, b, hf: (b, 0, hp)),
                  pl.BlockSpec((1, past, LANES), lambda hp, b, hf: (b, 0, hp))],
        out_specs=pl.BlockSpec((Q_HALF, LANES), lambda hp, b, hf: (b * halves + hf, hp)),
        out_shape=jax.ShapeDtypeStruct((batch * seq, NA_WIDTH), BF16),
        scratch_shapes=[pltpu.VMEM((2, 2, Q_HALF, K_WIN), F32)],
        compiler_params=_params(3),
        name="lat_attn",
    )(rpb.reshape(-1), qkv, qkv, qkv, ck_tm, cv_tm)


def _dn_kernel(*refs, seq, hps, a_chunks, has_s0, emit_state):
    it = iter(refs)
    q_ref, k_ref, v_ref, ba_ref, cst_ref = next(it), next(it), next(it), next(it), next(it)
    s0_ref = next(it) if has_s0 else None
    o_ref = next(it)
    st_ref = next(it) if emit_state else None
    (beta_s, g_s, u_s, wq_s, lb_s, el_s, odir_s) = it

    hg = pl.program_id(1)
    wl = hps * LANES
    n = seq // CHUNK
    hp_sl = [slice(h * LANES, (h + 1) * LANES) for h in range(hps)]

    def lanes_cat(parts):
        return parts[0] if len(parts) == 1 else jnp.concatenate(parts, axis=1)

    r128 = lax.broadcasted_iota(jnp.int32, (LANES, LANES), 0)
    c128 = lax.broadcasted_iota(jnp.int32, (LANES, LANES), 1)
    bd_mask = (r128 // 64) == (c128 // 64)
    bd_ones = _pair_block_ones()

    ba = ba_ref[...]
    beta_all = _sigmoid(ba)
    g_all = -jnp.exp(cst_ref[0:1, :]) * _softplus(ba + cst_ref[1:2, :])
    src = lax.broadcasted_iota(jnp.int32, (LANES, wl), 0)
    dst_head = 2 * hps * hg + lax.broadcasted_iota(jnp.int32, (LANES, wl), 1) // 64
    for d in range(2):
        sel_b = jnp.where(src == d * DN_HEADS + dst_head, 1.0, 0.0).astype(BF16)
        sel_g = jnp.where(src == (2 + d) * DN_HEADS + dst_head, 1.0, 0.0).astype(BF16)
        beta_s[d] = _dot2_l(beta_all, sel_b)
        g_s[d] = _dot3_l(g_all, sel_g)

    ri = lax.broadcasted_iota(jnp.int32, (CHUNK, LANES), 0)
    lj = lax.broadcasted_iota(jnp.int32, (CHUNK, LANES), 1) % 64
    lo_half = lax.broadcasted_iota(jnp.int32, (CHUNK, LANES), 1) < 64
    lo_half2 = (lax.broadcasted_iota(jnp.int32, (CHUNK, 2 * LANES), 1) % LANES) < 64
    diag = ri == lj
    strict = (ri > lj, ri < lj)
    riw = lax.broadcasted_iota(jnp.int32, (CHUNK, wl), 0)
    ljw = lax.broadcasted_iota(jnp.int32, (CHUNK, wl), 1) % 64
    diag_w = riw == ljw
    incl_w = (riw >= ljw, riw <= ljw)
    r64 = lax.broadcasted_iota(jnp.int32, (CHUNK, CHUNK), 0)
    c64 = lax.broadcasted_iota(jnp.int32, (CHUNK, CHUNK), 1)
    tri = (jnp.where(c64 <= r64, 1.0, 0.0).astype(BF16), jnp.where(c64 >= r64, 1.0, 0.0).astype(BF16))

    off_blk = ([], [])
    for lvl in range(6):
        s = 1 << lvl
        same2 = (ri // (2 * s)) == (lj // (2 * s))
        off_blk[0].append(jnp.logical_and(same2, (ri // s) == (lj // s) + 1))
        off_blk[1].append(jnp.logical_and(same2, (lj // s) == (ri // s) + 1))

    def stack_heads(x, lo_mask):
        return jnp.concatenate([jnp.where(lo_mask, x, 0.0), jnp.where(lo_mask, 0.0, x)], axis=0).astype(BF16)

    def chunk_rows(c):
        return pl.ds(c * CHUNK if isinstance(c, int) else pl.multiple_of(c * CHUNK, CHUNK), CHUNK)

    def chunk_load(c):
        rows = chunk_rows(c)
        return (q_ref[rows, :].astype(F32), k_ref[rows, :].astype(F32), v_ref[rows, :].astype(F32),
                [beta_s[d, rows, :] for d in range(2)], [g_s[d, rows, :] for d in range(2)])

    def group_a(loaded):
        nc = len(loaded)
        cd = [(i, d) for i in range(nc) for d in range(2)]
        units = [(i, h, d) for i in range(nc) for h in range(hps) for d in range(2)]
        qcs = [ld[0] * (DN_DK ** -0.5) for ld in loaded]
        kcs = [ld[1] for ld in loaded]
        vcs = [ld[2] for ld in loaded]
        kt_bd = {}
        for i in range(nc):
            for h in range(hps):
                kc = kcs[i][:, hp_sl[h]]
                ktt = jnp.transpose(jnp.concatenate([kc, kc], axis=0))
                kt_bd[i, h] = jnp.where(bd_mask, ktt, 0.0)
        gsp = {(i, d): _split3(loaded[i][4][d]) for i, d in cd}
        gc = {}
        for part in range(3):
            for i, d in cd:
                t = _bdot(tri[d], gsp[i, d][part])
                gc[i, d] = t if part == 0 else gc[i, d] + t
        rrow = {u: jnp.sum(jnp.where(diag_w, gc[u], 0.0), axis=0, keepdims=True) for u in cd}
        dec = {u: jnp.exp(jnp.minimum(gc[u] - rrow[u], 0.0)) for u in cd}
        eg = {u: jnp.exp(gc[u]) for u in cd}
        kbeta = {(i, d): kcs[i] * loaded[i][3][d] for i, d in cd}
        vbeta = {(i, d): vcs[i] * loaded[i][3][d] for i, d in cd}
        kbeg = {u: kbeta[u] * eg[u] for u in cd}
        m1 = {(i, h): _bdot(jnp.concatenate([qcs[i][:, hp_sl[h]], kbeta[i, 0][:, hp_sl[h]],
                                             kbeta[i, 1][:, hp_sl[h]]], axis=0).astype(BF16),
                            kt_bd[i, h].astype(BF16))
              for i in range(nc) for h in range(hps)}
        amat = {(i, h, d): jnp.where(strict[d], m1[i, h][CHUNK * (1 + d):CHUNK * (2 + d)] * dec[i, d][:, hp_sl[h]], 0.0)
                for i, h, d in units}
        o_st = {(i, h, d): stack_heads(jnp.where(off_blk[d][0], amat[i, h, d], 0.0), lo_half) for i, h, d in units}
        tmat = {(i, h, d): jnp.where(diag, 1.0, 0.0) - jnp.where(off_blk[d][0], amat[i, h, d], 0.0) for i, h, d in units}
        prod = {u: _bdot(amat[u].astype(BF16), o_st[u]) for u in units}
        wmat = {u: amat[u] - prod[u] for u in units}
        for lvl in range(1, 5):
            o_st = {(i, h, d): stack_heads(jnp.where(off_blk[d][lvl], wmat[i, h, d], 0.0), lo_half) for i, h, d in units}
            prod = {u: _bdot(jnp.concatenate([tmat[u], wmat[u]], axis=0).astype(BF16), o_st[u]) for u in units}
            tmat = {u: tmat[u] - prod[u][0:CHUNK] for u in units}
            wmat = {u: wmat[u] - prod[u][CHUNK:] for u in units}
        o_st = {(i, h, d): stack_heads(jnp.where(off_blk[d][5], wmat[i, h, d], 0.0), lo_half) for i, h, d in units}
        prod = {u: _bdot(tmat[u].astype(BF16), o_st[u]) for u in units}
        tmat = {u: tmat[u] - prod[u] for u in units}
        x = {}
        for i, h, d in units:
            rhs = jnp.concatenate([vbeta[i, d][:, hp_sl[h]], kbeg[i, d][:, hp_sl[h]]], axis=1)
            x[i, h, d] = _bdot(tmat[i, h, d].astype(BF16), stack_heads(rhs, lo_half2))
        results = []
        for i in range(nc):
            out = []
            for d in range(2):
                u = lanes_cat([x[i, h, d][:, :LANES] for h in range(hps)])
                w = lanes_cat([x[i, h, d][:, LANES:] for h in range(hps)])
                qk = jnp.where(incl_w[d], lanes_cat([m1[i, h][0:CHUNK] for h in range(hps)]) * dec[i, d], 0.0)
                qg = qcs[i] * eg[i, d]
                glast = gc[i, d][CHUNK - 1:CHUNK, :] if d == 0 else gc[i, d][0:1, :]
                kmul = jnp.exp(glast - rrow[i, d])
                kd_bd = lanes_cat([kt_bd[i, h] * kmul[:, hp_sl[h]] for h in range(hps)])
                out.append((u,
                            jnp.concatenate([w, qg], axis=0).astype(BF16),
                            jnp.concatenate([qk, kd_bd], axis=0).astype(BF16),
                            jnp.broadcast_to(jnp.exp(glast), (8, wl))))
            results.append(out)
        return results

    group = min(a_chunks, n)

    def phase_a(gi, carry):
        cs = [gi * group + j for j in range(group)]
        loaded = [chunk_load(c) for c in cs]
        results = group_a(loaded)
        for c, res in zip(cs, results):
            for d in range(2):
                u_s[d, c], wq_s[d, c], lb_s[d, c], el_s[d, c] = res[d]
        return carry

    if n == group:
        phase_a(0, 0)
    else:
        lax.fori_loop(0, n // group, phase_a, 0)

    dh = [(d, h) for d in range(2) for h in range(hps)]
    if has_s0:
        zero = jnp.zeros((DN_DK, DN_DV), F32)

        def bd_state(d, h):
            top = jnp.concatenate([s0_ref[0, 0, d, 2 * h], zero], axis=1)
            bot = jnp.concatenate([zero, s0_ref[0, 0, d, 2 * h + 1]], axis=1)
            return jnp.concatenate([top, bot], axis=0)
        s_init = tuple(bd_state(d, h) for d, h in dh)
    else:
        s_init = tuple(jnp.zeros((LANES, LANES), F32) for _ in dh)

    def phase_b(s, states):
        cs = (s, n - 1 - s)
        loaded = [(wq_s[d, cs[d]], u_s[d, cs[d]], lb_s[d, cs[d]], el_s[d, cs[d], 0:1, :]) for d in range(2)]
        st = dict(zip(dh, states))
        ws = {(d, h): _bdot(loaded[d][0][:, hp_sl[h]], st[d, h].astype(BF16)) for d, h in dh}
        vn = {(d, h): loaded[d][1][:, hp_sl[h]] - ws[d, h][0:CHUNK] for d, h in dh}
        r2 = {(d, h): _bdot(loaded[d][2][:, hp_sl[h]], stack_heads(vn[d, h], lo_half)) for d, h in dh}
        new_states = tuple(st[d, h] * loaded[d][3][:, hp_sl[h]] + r2[d, h][CHUNK:] for d, h in dh)
        for d in range(2):
            odir_s[d, chunk_rows(cs[d]), :] = lanes_cat([ws[d, h][CHUNK:] + r2[d, h][0:CHUNK] for h in range(hps)])
        return new_states

    s_fin = dict(zip(dh, lax.fori_loop(0, n, phase_b, s_init)))

    for h in range(hps):
        o = odir_s[0, :, hp_sl[h]] + odir_s[1, :, hp_sl[h]]
        msq = _dot2_l(o * o, bd_ones) * (1.0 / DN_DV)
        o_ref[:, hp_sl[h]] = (o * lax.rsqrt(msq + EPS) * cst_ref[2:3, :]).astype(BF16)

    if emit_state:
        for d, h in dh:
            st_ref[0, 0, d, 2 * h] = s_fin[d, h][0:DN_DK, 0:DN_DV]
            st_ref[0, 0, d, 2 * h + 1] = s_fin[d, h][DN_DK:, DN_DV:]


def _deltanet(dqkv, ba, cst, batch, seq, hps, a_chunks, s0=None, emit_state=False):
    n = seq // CHUNK
    wl = hps * LANES
    groups = N_PAIRS // hps
    has_s0 = s0 is not None

    def col(cbase):
        return pl.BlockSpec((seq, wl), lambda b, hg: (b, cbase // wl + hg))

    st_spec = pl.BlockSpec((1, 1, 2, 2 * hps, DN_DK, DN_DV), lambda b, hg: (b, 0, 0, hg, 0, 0))
    in_specs = [col(0), col(DN_WIDTH), col(2 * DN_WIDTH),
                pl.BlockSpec((seq, LANES), lambda b, hg: (b, 0)),
                pl.BlockSpec((8, LANES), lambda b, hg: (0, 0))]
    args = [dqkv, dqkv, dqkv, ba, cst]
    if has_s0:
        in_specs.append(st_spec)
        args.append(s0)
    out_specs = [pl.BlockSpec((seq, wl), lambda b, hg: (b, hg))]
    out_shape = [jax.ShapeDtypeStruct((batch * seq, DN_HEADS * DN_DV), BF16)]
    if emit_state:
        out_specs.append(st_spec)
        out_shape.append(jax.ShapeDtypeStruct((batch, 1, 2, DN_HEADS, DN_DK, DN_DV), F32))
    scratch = [pltpu.VMEM((2, seq, wl), F32), pltpu.VMEM((2, seq, wl), F32),
               pltpu.VMEM((2, n, CHUNK, wl), F32),
               pltpu.VMEM((2, n, 2 * CHUNK, wl), BF16),
               pltpu.VMEM((2, n, 3 * CHUNK, wl), BF16),
               pltpu.VMEM((2, n, 8, wl), F32),
               pltpu.VMEM((2, seq, wl), F32)]
    res = pl.pallas_call(
        functools.partial(_dn_kernel, seq=seq, hps=hps, a_chunks=a_chunks, has_s0=has_s0,
                          emit_state=emit_state),
        grid=(batch, groups),
        in_specs=in_specs,
        out_specs=out_specs,
        out_shape=out_shape,
        scratch_shapes=scratch,
        compiler_params=_params(2),
        name="deltanet_lat" if has_s0 else "deltanet_ctx",
    )(*args)
    return res


def _merge_ffn_kernel(x_ref, ona_ref, odn_ref, z_ref, gna_ref, gdn_ref, mod_ref, wpost1_ref, wpre2_ref, wpost2_ref,
                      wao_ref, wdo_ref, wout_ref, w1_ref, w2_ref, o_ref):
    mod = mod_ref[0]
    odn = odn_ref[...].astype(F32) * _silu(z_ref[...].astype(F32))
    br_na = _bdot(ona_ref[...], wao_ref[...])
    br_dn = _bdot(odn.astype(BF16), wdo_ref[...])
    m = _sigmoid(gna_ref[...].astype(F32)) * br_na + _sigmoid(gdn_ref[...].astype(F32)) * br_dn
    y = _bdot(m.astype(BF16), wout_ref[...])
    x1 = x_ref[...] + mod[2:3] * _rms_rows(y, wpost1_ref[...])
    h = _rms_rows(x1, wpre2_ref[...]) * (1.0 + mod[4:5]) + mod[3:4]
    f1 = _bdot(h.astype(BF16), w1_ref[...])
    r = jnp.maximum(f1, 0.0)
    f = _bdot((r * r).astype(BF16), w2_ref[...])
    o_ref[...] = x1 + mod[5:6] * _rms_rows(f, wpost2_ref[...])


def _merge_ffn(x2d, o_na, o_dn, gz, mod3, mod_row_of_tile, wpost1, wpre2, wpost2, w_ao, w_do, w_out, w1, w2):
    n_tok = x2d.shape[0]
    row = lambda i: (i, 0)
    const = lambda i: (0, 0)

    def resident(shape):
        return pl.BlockSpec(shape, const, pipeline_mode=pl.Buffered(1))

    return pl.pallas_call(
        _merge_ffn_kernel,
        grid=(n_tok // TM,),
        in_specs=[pl.BlockSpec((TM, D_MODEL), row),
                  pl.BlockSpec((TM, NA_WIDTH), row),
                  pl.BlockSpec((TM, DN_WIDTH), row),
                  pl.BlockSpec((TM, DN_WIDTH), lambda i: (i, 2 * D_MODEL // DN_WIDTH)),
                  pl.BlockSpec((TM, D_MODEL), lambda i: (i, 0)),
                  pl.BlockSpec((TM, D_MODEL), lambda i: (i, 1)),
                  pl.BlockSpec((1, N_MOD, D_MODEL), lambda i: (mod_row_of_tile(i), 0, 0)),
                  pl.BlockSpec((1, D_MODEL), const),
                  pl.BlockSpec((1, D_MODEL), const),
                  pl.BlockSpec((1, D_MODEL), const),
                  resident((NA_WIDTH, D_MODEL)),
                  resident((DN_WIDTH, D_MODEL)),
                  resident((D_MODEL, D_MODEL)),
                  resident((D_MODEL, D_FF)),
                  resident((D_FF, D_MODEL))],
        out_specs=pl.BlockSpec((TM, D_MODEL), row),
        out_shape=jax.ShapeDtypeStruct((n_tok, D_MODEL), F32),
        compiler_params=_params(1),
        name="merge_ffn",
    )(x2d, o_na, o_dn, gz, gz, gz, mod3, wpost1, wpre2, wpost2, w_ao, w_do, w_out, w1, w2)


def _pack_w_in(w):
    pad = jnp.zeros((w.shape[0], LANES - 4 * DN_HEADS), w.dtype)
    packed = jnp.concatenate([w[:, 0:3072], w[:, 3616:5664], w[:, 3104:3616], w[:, 3072:3104], pad], axis=1)
    return packed.astype(BF16)


def kernel(x_prompt, x_sample, c, cache_na_k, cache_na_v, state_delta, c_ctx, w_ada, b_ada, norm_pre1, norm_post1, norm_pre2, norm_post2, w_in, conv_w, a_log, dt_bias, dn_norm, na_rpb, w_ao, w_do, w_out, w_ff1, w_ff2):
    batch, seq, _ = x_prompt.shape
    dec_batch, dec_seq, _ = x_sample.shape
    depth = w_in.shape[0]
    assert depth == 1 and seq == TM and dec_seq % TM == 0 and dec_batch < 8

    xp = x_prompt.reshape(batch * seq, D_MODEL)
    xs = x_sample.reshape(dec_batch * dec_seq, D_MODEL)
    l = 0

    cvecs = jnp.zeros((8, D_MODEL), F32).at[:dec_batch].set(c).at[dec_batch].set(c_ctx)
    mod3 = _adaln(cvecs, w_ada[l], b_ada[l]).reshape(8, N_MOD, D_MODEL)
    ctx_row = lambda i: dec_batch
    tiles_per_lat = dec_seq // TM
    lat_row = lambda i: i // tiles_per_lat

    w_in_p = _pack_w_in(w_in[l])
    w_ao_b = w_ao[l].astype(BF16)
    w_do_b = w_do[l].astype(BF16)
    w_out_b = w_out[l].astype(BF16)
    w1_b = w_ff1[l].astype(BF16)
    w2_b = w_ff2[l].astype(BF16)
    wpre1 = norm_pre1[l].reshape(1, D_MODEL)
    wpost1 = norm_post1[l].reshape(1, D_MODEL)
    wpre2 = norm_pre2[l].reshape(1, D_MODEL)
    wpost2 = norm_post2[l].reshape(1, D_MODEL)

    conv_w8 = jnp.zeros((8, conv_w.shape[2]), F32).at[:CONV_W].set(conv_w[l])
    cst = jnp.zeros((8, LANES), F32)
    cst = cst.at[0, 2 * DN_HEADS:4 * DN_HEADS].set(a_log[l].reshape(-1))
    cst = cst.at[1, 2 * DN_HEADS:4 * DN_HEADS].set(dt_bias[l].reshape(-1))
    cst = cst.at[2].set(jnp.tile(dn_norm[l], 2))

    qkv_c, dqkv_c, gz_c, ba_c, new_k, new_v = _inproj(xp, mod3, ctx_row, wpre1, w_in_p, conv_w8, seq, batch)
    o_na_c = _ctx_attn(qkv_c, batch, seq)
    o_dn_c, new_s = _deltanet(dqkv_c, ba_c, cst, batch, seq, hps=4, a_chunks=4, emit_state=True)
    y_c = _merge_ffn(xp, o_na_c, o_dn_c, gz_c, mod3, ctx_row, wpost1, wpre2, wpost2,
                     w_ao_b, w_do_b, w_out_b, w1_b, w2_b)

    qkv_l, dqkv_l, gz_l, ba_l = _inproj(xs, mod3, lat_row, wpre1, w_in_p, conv_w8, dec_seq, dec_batch,
                                        rope_tabs=_rope_tables(dec_seq))
    past = cache_na_k.shape[3]
    ck_tm = jnp.transpose(cache_na_k[:, l], (0, 2, 1, 3)).reshape(dec_batch, past, NA_WIDTH)
    cv_tm = jnp.transpose(cache_na_v[:, l], (0, 2, 1, 3)).reshape(dec_batch, past, NA_WIDTH)
    o_na_l = _lat_attn(qkv_l, ck_tm, cv_tm, na_rpb[l], dec_batch, dec_seq)
    (o_dn_l,) = _deltanet(dqkv_l, ba_l, cst, dec_batch, dec_seq, hps=4, a_chunks=2, s0=state_delta)
    y_l = _merge_ffn(xs, o_na_l, o_dn_l, gz_l, mod3, lat_row, wpost1, wpre2, wpost2,
                     w_ao_b, w_do_b, w_out_b, w1_b, w2_b)

    return (y_c.reshape(batch, seq, D_MODEL), y_l.reshape(dec_batch, dec_seq, D_MODEL),
            new_k, new_v, new_s)
```

```python
import functools

import numpy as np
import jax
import jax.numpy as jnp
from jax import lax
from jax.experimental import pallas as pl
from jax.experimental.pallas import tpu as pltpu

F32 = jnp.float32
BF16 = jnp.bfloat16

D_MODEL = 1024
N_MOD = 6
GRID_W = 64
NA_HEADS = 8
HEAD_DIM = 64
NA_WIDTH = NA_HEADS * HEAD_DIM
WIN_R = 8
WIN_C = 16
ROPE_BASE = 10000.0
DN_HEADS = 8
DN_DK = 64
DN_DV = 64
DN_WIDTH = DN_HEADS * DN_DK
CONV_W = 5
CHUNK = 64
D_FF = 4 * D_MODEL
EPS = 1e-6
NEG_INF = -1e30

LANES = 128
SUBLANES = 8
N_PAIRS = NA_HEADS // 2
TM = 256
VMEM_LIMIT = 56 * 1024 * 1024

QKV_W = 3 * NA_WIDTH
DQKV_W = 3 * DN_WIDTH
BA_W = 4 * DN_HEADS
GZ_W = DN_WIDTH + 2 * D_MODEL
COL_QKV, COL_DQKV, COL_TAIL = 0, QKV_W, QKV_W + DQKV_W
IN_COLS = COL_TAIL + BA_W + GZ_W
TAIL_W = -(-(BA_W + GZ_W) // LANES) * LANES
P_COLS = COL_TAIL + TAIL_W


def _params(n_grid):
    return pltpu.CompilerParams(dimension_semantics=("arbitrary",) * n_grid,
                                vmem_limit_bytes=VMEM_LIMIT)


def _bdot(a, b):
    return jnp.dot(a, b, preferred_element_type=F32)


def _split3(a):
    hi = a.astype(BF16)
    r1 = a - hi.astype(F32)
    mid = r1.astype(BF16)
    lo = (r1 - mid.astype(F32)).astype(BF16)
    return hi, mid, lo


def _dot3_l(a, b_exact):
    hi, mid, lo = _split3(a)
    return _bdot(hi, b_exact) + _bdot(mid, b_exact) + _bdot(lo, b_exact)


def _dot2_l(a, b_exact):
    hi = a.astype(BF16)
    lo = (a - hi.astype(F32)).astype(BF16)
    return _bdot(hi, b_exact) + _bdot(lo, b_exact)


def _sigmoid(x):
    return 0.5 * jnp.tanh(0.5 * x) + 0.5


def _silu(x):
    return x * _sigmoid(x)


def _softplus(x):
    return jnp.maximum(x, 0.0) + jnp.log(1.0 + jnp.exp(-jnp.abs(x)))


def _rms_rows(x, w):
    ms = jnp.mean(x * x, axis=-1, keepdims=True)
    return x * lax.rsqrt(ms + EPS) * w


def _pair_block_ones():
    r = lax.broadcasted_iota(jnp.int32, (LANES, LANES), 0)
    c = lax.broadcasted_iota(jnp.int32, (LANES, LANES), 1)
    return jnp.where((r // 64) == (c // 64), 1.0, 0.0).astype(BF16)


def _adaln_kernel(c_ref, w_ref, b_ref, o_ref):
    c = c_ref[...]
    s = _silu(c).astype(BF16)
    o_ref[...] = _bdot(s, w_ref[...].astype(BF16)) + b_ref[...]


def _adaln(cvecs, w_ada, b_ada):
    tn = 1024
    n = w_ada.shape[1]
    return pl.pallas_call(
        _adaln_kernel,
        grid=(n // tn,),
        in_specs=[pl.BlockSpec((8, D_MODEL), lambda j: (0, 0)),
                  pl.BlockSpec((D_MODEL, tn), lambda j: (0, j)),
                  pl.BlockSpec((1, tn), lambda j: (0, j))],
        out_specs=pl.BlockSpec((8, tn), lambda j: (0, j)),
        out_shape=jax.ShapeDtypeStruct((8, n), F32),
        compiler_params=_params(1),
        name="adaln",
    )(cvecs, w_ada, b_ada.reshape(1, n))


def _inproj_kernel(*refs, latent, tiles_per_seq):
    it = iter(refs)
    x_ref, mod_ref, wpre_ref, w_ref, cw_ref = next(it), next(it), next(it), next(it), next(it)
    if latent:
        xprev_ref, xnext_ref, cos_ref, sin_ref = next(it), next(it), next(it), next(it)
    qkv_ref, dqkv_ref, gz_ref, ba_ref = next(it), next(it), next(it), next(it)
    if not latent:
        kc_ref, vc_ref = next(it), next(it)

    mod = mod_ref[0]

    def modulated(x):
        return (_rms_rows(x, wpre_ref[...]) * (1.0 + mod[1:2]) + mod[0:1]).astype(BF16)

    hb = modulated(x_ref[...])

    w_dn = w_ref[:, COL_DQKV:COL_DQKV + DQKV_W]
    halo_zero = jnp.zeros((SUBLANES, DQKV_W), F32)
    if latent:
        t = pl.program_id(0) % tiles_per_seq
        hb_ext = jnp.concatenate([modulated(xprev_ref[...]), hb, modulated(xnext_ref[...])], axis=0)
        pe = _bdot(hb_ext, w_dn)
        pe = jnp.concatenate([jnp.where(t > 0, pe[:SUBLANES], halo_zero), pe[SUBLANES:SUBLANES + TM],
                              jnp.where(t < tiles_per_seq - 1, pe[SUBLANES + TM:], halo_zero)], axis=0)
    else:
        pe = jnp.concatenate([halo_zero, _bdot(hb, w_dn), halo_zero], axis=0)
    acc = None
    for j in sorted(range(CONV_W), key=lambda s: abs(s - CONV_W // 2)):
        lo = SUBLANES - CONV_W // 2 + j
        term = pe[lo:lo + TM] * cw_ref[j:j + 1, :]
        acc = term if acc is None else acc + term
    y = _silu(acc)

    p = _bdot(hb, w_ref[:, COL_QKV:COL_QKV + QKV_W])
    if latent:
        qk = p[:, :2 * NA_WIDTH]
        lane = lax.broadcasted_iota(jnp.int32, qk.shape, 1)
        first = (lane % 32) < 16
        partner = jnp.where(first, pltpu.roll(qk, 2 * NA_WIDTH - 16, 1), pltpu.roll(qk, 16, 1))
        qkv_ref[:, :2 * NA_WIDTH] = (qk * cos_ref[...] + partner * sin_ref[...]).astype(BF16)
        qkv_ref[:, 2 * NA_WIDTH:] = p[:, 2 * NA_WIDTH:].astype(BF16)
    else:
        qkv_ref[...] = p.astype(BF16)
        for h in range(NA_HEADS):
            kc_ref[0, 0, h] = p[:, NA_WIDTH + h * HEAD_DIM:NA_WIDTH + (h + 1) * HEAD_DIM]
            vc_ref[0, 0, h] = p[:, 2 * NA_WIDTH + h * HEAD_DIM:2 * NA_WIDTH + (h + 1) * HEAD_DIM]

    split = (TAIL_W // LANES // 2) * LANES
    tail = jnp.concatenate([_bdot(hb, w_ref[:, COL_TAIL:COL_TAIL + split]),
                            _bdot(hb, w_ref[:, COL_TAIL + split:COL_TAIL + TAIL_W])], axis=1)
    ba_ref[...] = tail[:, :LANES]
    gz_ref[...] = pltpu.roll(tail, TAIL_W - BA_W, 1)[:, :GZ_W].astype(BF16)

    bd_ones = _pair_block_ones()
    for s in range(DQKV_W // LANES):
        ys = y[:, s * LANES:(s + 1) * LANES]
        if s < 2 * DN_WIDTH // LANES:
            ys = ys * lax.rsqrt(_dot2_l(ys * ys, bd_ones) + EPS)
        dqkv_ref[:, s * LANES:(s + 1) * LANES] = ys.astype(BF16)


def _inproj(x2d, mod3, mod_row_of_tile, wpre, w_packed, conv_w8, seq, batch, rope_tabs=None):
    n_tok = x2d.shape[0]
    latent = rope_tabs is not None
    tiles_per_seq = seq // TM
    rows8 = TM // SUBLANES
    n_blk8 = n_tok // SUBLANES
    in_specs = [pl.BlockSpec((TM, D_MODEL), lambda i: (i, 0)),
                pl.BlockSpec((1, N_MOD, D_MODEL), lambda i: (mod_row_of_tile(i), 0, 0)),
                pl.BlockSpec((1, D_MODEL), lambda i: (0, 0)),
                pl.BlockSpec((D_MODEL, P_COLS), lambda i: (0, 0)),
                pl.BlockSpec((8, DQKV_W), lambda i: (0, 0))]
    args = [x2d, mod3, wpre, w_packed, conv_w8]
    out_specs = [pl.BlockSpec((TM, QKV_W), lambda i: (i, 0)),
                 pl.BlockSpec((TM, DQKV_W), lambda i: (i, 0)),
                 pl.BlockSpec((TM, GZ_W), lambda i: (i, 0)),
                 pl.BlockSpec((TM, LANES), lambda i: (i, 0))]
    out_shape = [jax.ShapeDtypeStruct((n_tok, QKV_W), BF16),
                 jax.ShapeDtypeStruct((n_tok, DQKV_W), BF16),
                 jax.ShapeDtypeStruct((n_tok, GZ_W), BF16),
                 jax.ShapeDtypeStruct((n_tok, LANES), F32)]
    if latent:
        in_specs += [pl.BlockSpec((SUBLANES, D_MODEL), lambda i: (jnp.maximum(i * rows8 - 1, 0), 0)),
                     pl.BlockSpec((SUBLANES, D_MODEL), lambda i: (jnp.minimum((i + 1) * rows8, n_blk8 - 1), 0))]
        args += [x2d, x2d]
        for tab in rope_tabs:
            in_specs.append(pl.BlockSpec((TM, 2 * NA_WIDTH), lambda i: (i % tiles_per_seq, 0)))
            args.append(tab)
    else:
        assert seq == TM
        cache_spec = pl.BlockSpec((1, 1, NA_HEADS, seq, HEAD_DIM), lambda i: (i, 0, 0, 0, 0))
        cache_shape = jax.ShapeDtypeStruct((batch, 1, NA_HEADS, seq, HEAD_DIM), F32)
        out_specs += [cache_spec, cache_spec]
        out_shape += [cache_shape, cache_shape]
    return pl.pallas_call(
        functools.partial(_inproj_kernel, latent=latent, tiles_per_seq=tiles_per_seq),
        grid=(n_tok // TM,),
        in_specs=in_specs,
        out_specs=out_specs,
        out_shape=out_shape,
        compiler_params=_params(1),
        name="inproj_lat" if latent else "inproj_ctx",
    )(*args)


def _rope_tables(seq):
    half = HEAD_DIM // 2
    nf = half // 2
    pos = np.arange(seq)
    inv = (np.float32(ROPE_BASE) ** (-np.arange(nf, dtype=np.float32) / np.float32(nf))).astype(np.float32)
    ang_r = ((pos // GRID_W).astype(np.float32)[:, None] * inv).astype(np.float64)
    ang_c = ((pos % GRID_W).astype(np.float32)[:, None] * inv).astype(np.float64)

    def grp(ang):
        return (np.concatenate([np.cos(ang), np.cos(ang)], 1),
                np.concatenate([-np.sin(ang), np.sin(ang)], 1))

    cr, sr = grp(ang_r)
    cc, sc = grp(ang_c)
    cos_h = np.concatenate([cr, cc], 1)
    sin_h = np.concatenate([sr, sc], 1)
    reps = 2 * NA_HEADS
    return (jnp.asarray(np.tile(cos_h, (1, reps)), F32),
            jnp.asarray(np.tile(sin_h, (1, reps)), F32))


def _ctx_attn_kernel(q_ref, k_ref, v_ref, o_ref):
    lane = lax.broadcasted_iota(jnp.int32, (1, LANES), 1)
    scale = HEAD_DIM ** -0.5
    heads = [(hp, e) for hp in range(N_PAIRS) for e in range(2)]
    sls = [slice(hp * LANES, (hp + 1) * LANES) for hp in range(N_PAIRS)]
    q = [q_ref[:, sl] for sl in sls]
    k = [k_ref[:, sl] for sl in sls]
    v = [v_ref[:, sl] for sl in sls]
    m_e = (lane < 64, lane >= 64)
    zero = jnp.zeros((), BF16)
    s = {(hp, e): lax.dot_general(jnp.where(m_e[e], q[hp] * scale, zero), k[hp],
                                  (((1,), (1,)), ((), ())), preferred_element_type=F32) for hp, e in heads}
    p = {u: jnp.exp(s[u] - jnp.max(s[u], axis=-1, keepdims=True)) for u in heads}
    l = {u: jnp.sum(p[u], axis=-1, keepdims=True) for u in heads}
    o = {(hp, e): _bdot(p[hp, e].astype(BF16), jnp.where(m_e[e], v[hp], zero)) for hp, e in heads}
    for hp in range(N_PAIRS):
        o_ref[:, sls[hp]] = (o[hp, 0] / l[hp, 0] + o[hp, 1] / l[hp, 1]).astype(BF16)


def _ctx_attn(qkv, batch, seq):
    def col(cb):
        return pl.BlockSpec((seq, NA_WIDTH), lambda b: (b, cb))
    return pl.pallas_call(
        _ctx_attn_kernel,
        grid=(batch,),
        in_specs=[col(0), col(1), col(2)],
        out_specs=pl.BlockSpec((seq, NA_WIDTH), lambda b: (b, 0)),
        out_shape=jax.ShapeDtypeStruct((batch * seq, NA_WIDTH), BF16),
        compiler_params=_params(1),
        name="ctx_attn",
    )(qkv, qkv, qkv)


Q_HALF = 512
K_WIN = 768


def _lat_attn_kernel(rpb_ref, q_ref, k_ref, v_ref, ck_ref, cv_ref, o_ref, bias_ref, *, rows):
    hp = pl.program_id(0)
    b = pl.program_id(1)
    half = pl.program_id(2)
    kr_n = min(WIN_R, rows)
    rows_half = Q_HALF // GRID_W
    krows_win = K_WIN // GRID_W
    n_dr = 2 * WIN_R - 1
    n_dc = 2 * WIN_C - 1

    @pl.when(jnp.logical_and(b == 0, half == 0))
    def _build_bias():
        qc = lax.broadcasted_iota(jnp.int32, (GRID_W, LANES), 0)
        ln = lax.broadcasted_iota(jnp.int32, (GRID_W, LANES), 1)
        kc = ln % GRID_W
        cs = jnp.clip(qc - WIN_C // 2, 0, GRID_W - WIN_C)
        valid = jnp.logical_and(kc >= cs, kc < cs + WIN_C)
        dc = jnp.clip(kc - qc + (WIN_C - 1), 0, n_dc - 1)
        dcv = jnp.where(valid, dc, -1)
        neg = jnp.full((GRID_W, LANES), NEG_INF, F32)
        for e in range(2):
            blocks = []
            for a in range(n_dr):
                blk = neg
                for j in range(n_dc):
                    blk = jnp.where(dcv == j, rpb_ref[((2 * hp + e) * n_dr + a) * n_dc + j], blk)
                blocks.append(blk)
            for hf in range(2):
                for rl in range(rows_half):
                    r = hf * rows_half + rl
                    rs = min(max(r - kr_n // 2, 0), rows - kr_n)
                    for kp in range(krows_win // 2):
                        parts = []
                        for kl in (2 * kp, 2 * kp + 1):
                            kr = kl + hf * (rows - krows_win)
                            parts.append(blocks[kr - r + WIN_R - 1] if rs <= kr < rs + kr_n else neg)
                        blk = jnp.where(ln < GRID_W, parts[0], parts[1])
                        bias_ref[e, hf, rl * GRID_W:(rl + 1) * GRID_W, kp * LANES:(kp + 1) * LANES] = blk

    lane = lax.broadcasted_iota(jnp.int32, (1, LANES), 1)
    scale = HEAD_DIM ** -0.5
    zero = jnp.zeros((), BF16)
    k0 = pl.multiple_of(half * ((rows - krows_win) * GRID_W), GRID_W)
    kw = k_ref[pl.ds(k0, K_WIN), :]
    vw = v_ref[pl.ds(k0, K_WIN), :]
    ck = ck_ref[0].astype(BF16)
    cv = cv_ref[0].astype(BF16)
    SUB = 256
    m_e = (lane < 64, lane >= 64)
    ve = [jnp.where(m, vw, zero) for m in m_e]
    cve = [jnp.where(m, cv, zero) for m in m_e]
    units = [(qs, e) for qs in range(Q_HALF // SUB) for e in range(2)]
    nt = (((1,), (1,)), ((), ()))
    qe = {(qs, e): jnp.where(m_e[e], q_ref[qs * SUB:(qs + 1) * SUB, :] * scale, zero) for qs, e in units}
    s_loc = {(qs, e): lax.dot_general(qe[qs, e], kw, nt, preferred_element_type=F32)
             + bias_ref[e, half, qs * SUB:(qs + 1) * SUB, :] for qs, e in units}
    s_ctx = {u: lax.dot_general(qe[u], ck, nt, preferred_element_type=F32) for u in units}
    m = {u: jnp.maximum(jnp.max(s_loc[u], axis=-1, keepdims=True), jnp.max(s_ctx[u], axis=-1, keepdims=True))
         for u in units}
    p_loc = {u: jnp.exp(s_loc[u] - m[u]) for u in units}
    p_ctx = {u: jnp.exp(s_ctx[u] - m[u]) for u in units}
    l = {u: jnp.sum(p_loc[u], axis=-1, keepdims=True) + jnp.sum(p_ctx[u], axis=-1, keepdims=True) for u in units}
    o = {(qs, e): (_bdot(p_loc[qs, e].astype(BF16), ve[e]) + _bdot(p_ctx[qs, e].astype(BF16), cve[e])) / l[qs, e]
         for qs, e in units}
    for qs in range(Q_HALF // SUB):
        o_ref[qs * SUB:(qs + 1) * SUB, :] = (o[qs, 0] + o[qs, 1]).astype(BF16)


def _lat_attn(qkv, ck_tm, cv_tm, rpb, batch, seq):
    rows = seq // GRID_W
    assert rows * GRID_W == seq and rows - K_WIN // GRID_W == 4 and Q_HALF * 2 == seq
    halves = seq // Q_HALF
    past = ck_tm.shape[1]
    kernel = functools.partial(_lat_attn_kernel, rows=rows)
    return pl.pallas_call(
        kernel,
        grid=(N_PAIRS, batch, halves),
        in_specs=[pl.BlockSpec(memory_space=pltpu.SMEM),
                  pl.BlockSpec((Q_HALF, LANES), lambda hp, b, hf: (b * halves + hf, hp)),
                  pl.BlockSpec((seq, LANES), lambda hp, b, hf: (b, N_PAIRS + hp)),
                  pl.BlockSpec((seq, LANES), lambda hp, b, hf: (b, 2 * N_PAIRS + hp)),
                  pl.BlockSpec((1, past, LANES), lambda hp, b, hf: (b, 0, hp)),
                  pl.BlockSpec((1, past, LANES), lambda hp, b, hf: (b, 0, hp))],
        out_specs=pl.BlockSpec((Q_HALF, LANES), lambda hp, b, hf: (b * halves + hf, hp)),
        out_shape=jax.ShapeDtypeStruct((batch * seq, NA_WIDTH), BF16),
        scratch_shapes=[pltpu.VMEM((2, 2, Q_HALF, K_WIN), F32)],
        compiler_params=_params(3),
        name="lat_attn",
    )(rpb.reshape(-1), qkv, qkv, qkv, ck_tm, cv_tm)


def _dn_kernel(*refs, seq, hps, a_chunks, has_s0, emit_state):
    it = iter(refs)
    q_ref, k_ref, v_ref, ba_ref, cst_ref = next(it), next(it), next(it), next(it), next(it)
    s0_ref = next(it) if has_s0 else None
    o_ref = next(it)
    st_ref = next(it) if emit_state else None
    (beta_s, g_s, u_s, wq_s, lb_s, el_s, odir_s) = it

    hg = pl.program_id(1)
    wl = hps * LANES
    n = seq // CHUNK
    hp_sl = [slice(h * LANES, (h + 1) * LANES) for h in range(hps)]

    def lanes_cat(parts):
        return parts[0] if len(parts) == 1 else jnp.concatenate(parts, axis=1)

    r128 = lax.broadcasted_iota(jnp.int32, (LANES, LANES), 0)
    c128 = lax.broadcasted_iota(jnp.int32, (LANES, LANES), 1)
    bd_mask = (r128 // 64) == (c128 // 64)
    bd_ones = _pair_block_ones()

    ba = ba_ref[...]
    beta_all = _sigmoid(ba)
    g_all = -jnp.exp(cst_ref[0:1, :]) * _softplus(ba + cst_ref[1:2, :])
    src = lax.broadcasted_iota(jnp.int32, (LANES, wl), 0)
    dst_head = 2 * hps * hg + lax.broadcasted_iota(jnp.int32, (LANES, wl), 1) // 64
    for d in range(2):
        sel_b = jnp.where(src == d * DN_HEADS + dst_head, 1.0, 0.0).astype(BF16)
        sel_g = jnp.where(src == (2 + d) * DN_HEADS + dst_head, 1.0, 0.0).astype(BF16)
        beta_s[d] = _dot2_l(beta_all, sel_b)
        g_s[d] = _dot3_l(g_all, sel_g)

    ri = lax.broadcasted_iota(jnp.int32, (CHUNK, LANES), 0)
    lj = lax.broadcasted_iota(jnp.int32, (CHUNK, LANES), 1) % 64
    lo_half = lax.broadcasted_iota(jnp.int32, (CHUNK, LANES), 1) < 64
    lo_half2 = (lax.broadcasted_iota(jnp.int32, (CHUNK, 2 * LANES), 1) % LANES) < 64
    diag = ri == lj
    strict = (ri > lj, ri < lj)
    riw = lax.broadcasted_iota(jnp.int32, (CHUNK, wl), 0)
    ljw = lax.broadcasted_iota(jnp.int32, (CHUNK, wl), 1) % 64
    diag_w = riw == ljw
    incl_w = (riw >= ljw, riw <= ljw)
    r64 = lax.broadcasted_iota(jnp.int32, (CHUNK, CHUNK), 0)
    c64 = lax.broadcasted_iota(jnp.int32, (CHUNK, CHUNK), 1)
    tri = (jnp.where(c64 <= r64, 1.0, 0.0).astype(BF16), jnp.where(c64 >= r64, 1.0, 0.0).astype(BF16))

    off_blk = ([], [])
    for lvl in range(6):
        s = 1 << lvl
        same2 = (ri // (2 * s)) == (lj // (2 * s))
        off_blk[0].append(jnp.logical_and(same2, (ri // s) == (lj // s) + 1))
        off_blk[1].append(jnp.logical_and(same2, (lj // s) == (ri // s) + 1))

    def stack_heads(x, lo_mask):
        return jnp.concatenate([jnp.where(lo_mask, x, 0.0), jnp.where(lo_mask, 0.0, x)], axis=0).astype(BF16)

    def chunk_rows(c):
        return pl.ds(c * CHUNK if isinstance(c, int) else pl.multiple_of(c * CHUNK, CHUNK), CHUNK)

    def chunk_load(c):
        rows = chunk_rows(c)
        return (q_ref[rows, :].astype(F32), k_ref[rows, :].astype(F32), v_ref[rows, :].astype(F32),
                [beta_s[d, rows, :] for d in range(2)], [g_s[d, rows, :] for d in range(2)])

    def group_a(loaded):
        nc = len(loaded)
        cd = [(i, d) for i in range(nc) for d in range(2)]
        units = [(i, h, d) for i in range(nc) for h in range(hps) for d in range(2)]
        qcs = [ld[0] * (DN_DK ** -0.5) for ld in loaded]
        kcs = [ld[1] for ld in loaded]
        vcs = [ld[2] for ld in loaded]
        kt_bd = {}
        for i in range(nc):
            for h in range(hps):
                kc = kcs[i][:, hp_sl[h]]
                ktt = jnp.transpose(jnp.concatenate([kc, kc], axis=0))
                kt_bd[i, h] = jnp.where(bd_mask, ktt, 0.0)
        gsp = {(i, d): _split3(loaded[i][4][d]) for i, d in cd}
        gc = {}
        for part in range(3):
            for i, d in cd:
                t = _bdot(tri[d], gsp[i, d][part])
                gc[i, d] = t if part == 0 else gc[i, d] + t
        rrow = {u: jnp.sum(jnp.where(diag_w, gc[u], 0.0), axis=0, keepdims=True) for u in cd}
        dec = {u: jnp.exp(jnp.minimum(gc[u] - rrow[u], 0.0)) for u in cd}
        eg = {u: jnp.exp(gc[u]) for u in cd}
        kbeta = {(i, d): kcs[i] * loaded[i][3][d] for i, d in cd}
        vbeta = {(i, d): vcs[i] * loaded[i][3][d] for i, d in cd}
        kbeg = {u: kbeta[u] * eg[u] for u in cd}
        m1 = {(i, h): _bdot(jnp.concatenate([qcs[i][:, hp_sl[h]], kbeta[i, 0][:, hp_sl[h]],
                                             kbeta[i, 1][:, hp_sl[h]]], axis=0).astype(BF16),
                            kt_bd[i, h].astype(BF16))
              for i in range(nc) for h in range(hps)}
        amat = {(i, h, d): jnp.where(strict[d], m1[i, h][CHUNK * (1 + d):CHUNK * (2 + d)] * dec[i, d][:, hp_sl[h]], 0.0)
                for i, h, d in units}
        o_st = {(i, h, d): stack_heads(jnp.where(off_blk[d][0], amat[i, h, d], 0.0), lo_half) for i, h, d in units}
        tmat = {(i, h, d): jnp.where(diag, 1.0, 0.0) - jnp.where(off_blk[d][0], amat[i, h, d], 0.0) for i, h, d in units}
        prod = {u: _bdot(amat[u].astype(BF16), o_st[u]) for u in units}
        wmat = {u: amat[u] - prod[u] for u in units}
        for lvl in range(1, 5):
            o_st = {(i, h, d): stack_heads(jnp.where(off_blk[d][lvl], wmat[i, h, d], 0.0), lo_half) for i, h, d in units}
            prod = {u: _bdot(jnp.concatenate([tmat[u], wmat[u]], axis=0).astype(BF16), o_st[u]) for u in units}
            tmat = {u: tmat[u] - prod[u][0:CHUNK] for u in units}
            wmat = {u: wmat[u] - prod[u][CHUNK:] for u in units}
        o_st = {(i, h, d): stack_heads(jnp.where(off_blk[d][5], wmat[i, h, d], 0.0), lo_half) for i, h, d in units}
        prod = {u: _bdot(tmat[u].astype(BF16), o_st[u]) for u in units}
        tmat = {u: tmat[u] - prod[u] for u in units}
        x = {}
        for i, h, d in units:
            rhs = jnp.concatenate([vbeta[i, d][:, hp_sl[h]], kbeg[i, d][:, hp_sl[h]]], axis=1)
            x[i, h, d] = _bdot(tmat[i, h, d].astype(BF16), stack_heads(rhs, lo_half2))
        results = []
        for i in range(nc):
            out = []
            for d in range(2):
                u = lanes_cat([x[i, h, d][:, :LANES] for h in range(hps)])
                w = lanes_cat([x[i, h, d][:, LANES:] for h in range(hps)])
                qk = jnp.where(incl_w[d], lanes_cat([m1[i, h][0:CHUNK] for h in range(hps)]) * dec[i, d], 0.0)
                qg = qcs[i] * eg[i, d]
                glast = gc[i, d][CHUNK - 1:CHUNK, :] if d == 0 else gc[i, d][0:1, :]
                kmul = jnp.exp(glast - rrow[i, d])
                kd_bd = lanes_cat([kt_bd[i, h] * kmul[:, hp_sl[h]] for h in range(hps)])
                out.append((u,
                            jnp.concatenate([w, qg], axis=0).astype(BF16),
                            jnp.concatenate([qk, kd_bd], axis=0).astype(BF16),
                            jnp.broadcast_to(jnp.exp(glast), (8, wl))))
            results.append(out)
        return results

    group = min(a_chunks, n)

    def phase_a(gi, carry):
        cs = [gi * group + j for j in range(group)]
        loaded = [chunk_load(c) for c in cs]
        results = group_a(loaded)
        for c, res in zip(cs, results):
            for d in range(2):
                u_s[d, c], wq_s[d, c], lb_s[d, c], el_s[d, c] = res[d]
        return carry

    if n == group:
        phase_a(0, 0)
    else:
        lax.fori_loop(0, n // group, phase_a, 0)

    dh = [(d, h) for d in range(2) for h in range(hps)]
    if has_s0:
        zero = jnp.zeros((DN_DK, DN_DV), F32)

        def bd_state(d, h):
            top = jnp.concatenate([s0_ref[0, 0, d, 2 * h], zero], axis=1)
            bot = jnp.concatenate([zero, s0_ref[0, 0, d, 2 * h + 1]], axis=1)
            return jnp.concatenate([top, bot], axis=0)
        s_init = tuple(bd_state(d, h) for d, h in dh)
    else:
        s_init = tuple(jnp.zeros((LANES, LANES), F32) for _ in dh)

    def phase_b(s, states):
        cs = (s, n - 1 - s)
        loaded = [(wq_s[d, cs[d]], u_s[d, cs[d]], lb_s[d, cs[d]], el_s[d, cs[d], 0:1, :]) for d in range(2)]
        st = dict(zip(dh, states))
        ws = {(d, h): _bdot(loaded[d][0][:, hp_sl[h]], st[d, h].astype(BF16)) for d, h in dh}
        vn = {(d, h): loaded[d][1][:, hp_sl[h]] - ws[d, h][0:CHUNK] for d, h in dh}
        r2 = {(d, h): _bdot(loaded[d][2][:, hp_sl[h]], stack_heads(vn[d, h], lo_half)) for d, h in dh}
        new_states = tuple(st[d, h] * loaded[d][3][:, hp_sl[h]] + r2[d, h][CHUNK:] for d, h in dh)
        for d in range(2):
            odir_s[d, chunk_rows(cs[d]), :] = lanes_cat([ws[d, h][CHUNK:] + r2[d, h][0:CHUNK] for h in range(hps)])
        return new_states

    s_fin = dict(zip(dh, lax.fori_loop(0, n, phase_b, s_init)))

    for h in range(hps):
        o = odir_s[0, :, hp_sl[h]] + odir_s[1, :, hp_sl[h]]
        msq = _dot2_l(o * o, bd_ones) * (1.0 / DN_DV)
        o_ref[:, hp_sl[h]] = (o * lax.rsqrt(msq + EPS) * cst_ref[2:3, :]).astype(BF16)

    if emit_state:
        for d, h in dh:
            st_ref[0, 0, d, 2 * h] = s_fin[d, h][0:DN_DK, 0:DN_DV]
            st_ref[0, 0, d, 2 * h + 1] = s_fin[d, h][DN_DK:, DN_DV:]


def _deltanet(dqkv, ba, cst, batch, seq, hps, a_chunks, s0=None, emit_state=False):
    n = seq // CHUNK
    wl = hps * LANES
    groups = N_PAIRS // hps
    has_s0 = s0 is not None

    def col(cbase):
        return pl.BlockSpec((seq, wl), lambda b, hg: (b, cbase // wl + hg))

    st_spec = pl.BlockSpec((1, 1, 2, 2 * hps, DN_DK, DN_DV), lambda b, hg: (b, 0, 0, hg, 0, 0))
    in_specs = [col(0), col(DN_WIDTH), col(2 * DN_WIDTH),
                pl.BlockSpec((seq, LANES), lambda b, hg: (b, 0)),
                pl.BlockSpec((8, LANES), lambda b, hg: (0, 0))]
    args = [dqkv, dqkv, dqkv, ba, cst]
    if has_s0:
        in_specs.append(st_spec)
        args.append(s0)
    out_specs = [pl.BlockSpec((seq, wl), lambda b, hg: (b, hg))]
    out_shape = [jax.ShapeDtypeStruct((batch * seq, DN_HEADS * DN_DV), BF16)]
    if emit_state:
        out_specs.append(st_spec)
        out_shape.append(jax.ShapeDtypeStruct((batch, 1, 2, DN_HEADS, DN_DK, DN_DV), F32))
    scratch = [pltpu.VMEM((2, seq, wl), F32), pltpu.VMEM((2, seq, wl), F32),
               pltpu.VMEM((2, n, CHUNK, wl), F32),
               pltpu.VMEM((2, n, 2 * CHUNK, wl), BF16),
               pltpu.VMEM((2, n, 3 * CHUNK, wl), BF16),
               pltpu.VMEM((2, n, 8, wl), F32),
               pltpu.VMEM((2, seq, wl), F32)]
    res = pl.pallas_call(
        functools.partial(_dn_kernel, seq=seq, hps=hps, a_chunks=a_chunks, has_s0=has_s0,
                          emit_state=emit_state),
        grid=(batch, groups),
        in_specs=in_specs,
        out_specs=out_specs,
        out_shape=out_shape,
        scratch_shapes=scratch,
        compiler_params=_params(2),
        name="deltanet_lat" if has_s0 else "deltanet_ctx",
    )(*args)
    return res


TM_F = 2 * TM


def _merge_ffn_kernel(x_ref, ona_ref, odn_ref, gz_ref, mod_ref, wpost1_ref, wpre2_ref, wpost2_ref,
                      wao_ref, wdo_ref, wout_ref, w1_ref, w2_ref, o_ref):
    mod = mod_ref[0]
    subs = [slice(s * TM, (s + 1) * TM) for s in range(TM_F // TM)]
    z_sl, gna_sl, gdn_sl = (slice(0, DN_WIDTH), slice(DN_WIDTH, DN_WIDTH + D_MODEL),
                            slice(DN_WIDTH + D_MODEL, GZ_W))
    odn = [odn_ref[sl, :].astype(F32) * _silu(gz_ref[sl, z_sl].astype(F32)) for sl in subs]
    br_na = [_bdot(ona_ref[sl, :], wao_ref[...]) for sl in subs]
    br_dn = [_bdot(t.astype(BF16), wdo_ref[...]) for t in odn]
    m = [_sigmoid(gz_ref[sl, gna_sl].astype(F32)) * a + _sigmoid(gz_ref[sl, gdn_sl].astype(F32)) * b
         for sl, a, b in zip(subs, br_na, br_dn)]
    y = [_bdot(t.astype(BF16), wout_ref[...]) for t in m]
    x1 = [x_ref[sl, :] + mod[2:3] * _rms_rows(t, wpost1_ref[...]) for sl, t in zip(subs, y)]
    h = [_rms_rows(t, wpre2_ref[...]) * (1.0 + mod[4:5]) + mod[3:4] for t in x1]
    f1 = [_bdot(t.astype(BF16), w1_ref[...]) for t in h]
    r = [jnp.maximum(t, 0.0) for t in f1]
    f = [_bdot((t * t).astype(BF16), w2_ref[...]) for t in r]
    for sl, a, b in zip(subs, x1, f):
        o_ref[sl, :] = a + mod[5:6] * _rms_rows(b, wpost2_ref[...])


def _merge_ffn(x2d, o_na, o_dn, gz, mod3, mod_row_of_tile, wpost1, wpre2, wpost2, w_ao, w_do, w_out, w1, w2):
    n_tok = x2d.shape[0]
    row = lambda i: (i, 0)
    const = lambda i: (0, 0)

    def resident(shape):
        return pl.BlockSpec(shape, const, pipeline_mode=pl.Buffered(1))

    return pl.pallas_call(
        _merge_ffn_kernel,
        grid=(n_tok // TM_F,),
        in_specs=[pl.BlockSpec((TM_F, D_MODEL), row),
                  pl.BlockSpec((TM_F, NA_WIDTH), row),
                  pl.BlockSpec((TM_F, DN_WIDTH), row),
                  pl.BlockSpec((TM_F, GZ_W), row),
                  pl.BlockSpec((1, N_MOD, D_MODEL), lambda i: (mod_row_of_tile(i), 0, 0)),
                  pl.BlockSpec((1, D_MODEL), const),
                  pl.BlockSpec((1, D_MODEL), const),
                  pl.BlockSpec((1, D_MODEL), const),
                  resident((NA_WIDTH, D_MODEL)),
                  resident((DN_WIDTH, D_MODEL)),
                  resident((D_MODEL, D_MODEL)),
                  resident((D_MODEL, D_FF)),
                  resident((D_FF, D_MODEL))],
        out_specs=pl.BlockSpec((TM_F, D_MODEL), row),
        out_shape=jax.ShapeDtypeStruct((n_tok, D_MODEL), F32),
        compiler_params=_params(1),
        name="merge_ffn",
    )(x2d, o_na, o_dn, gz, mod3, wpost1, wpre2, wpost2, w_ao, w_do, w_out, w1, w2)


def _pack_w_in(w):
    assert w.shape[1] == IN_COLS
    return jnp.pad(w.astype(BF16), ((0, 0), (0, P_COLS - IN_COLS)))


def kernel(x_prompt, x_sample, c, cache_na_k, cache_na_v, state_delta, c_ctx, w_ada, b_ada, norm_pre1, norm_post1, norm_pre2, norm_post2, w_in, conv_w, a_log, dt_bias, dn_norm, na_rpb, w_ao, w_do, w_out, w_ff1, w_ff2):
    batch, seq, _ = x_prompt.shape
    dec_batch, dec_seq, _ = x_sample.shape
    depth = w_in.shape[0]
    assert depth == 1 and seq == TM and dec_seq % TM_F == 0 and (batch * seq) % TM_F == 0 and dec_batch < 8

    xp = x_prompt.reshape(batch * seq, D_MODEL)
    xs = x_sample.reshape(dec_batch * dec_seq, D_MODEL)
    l = 0

    cvecs = jnp.zeros((8, D_MODEL), F32).at[:dec_batch].set(c).at[dec_batch].set(c_ctx)
    mod3 = _adaln(cvecs, w_ada[l], b_ada[l]).reshape(8, N_MOD, D_MODEL)
    ctx_row = lambda i: dec_batch
    tiles_per_lat = dec_seq // TM
    lat_row = lambda i: i // tiles_per_lat

    w_in_p = _pack_w_in(w_in[l])
    w_ao_b = w_ao[l].astype(BF16)
    w_do_b = w_do[l].astype(BF16)
    w_out_b = w_out[l].astype(BF16)
    w1_b = w_ff1[l].astype(BF16)
    w2_b = w_ff2[l].astype(BF16)
    wpre1 = norm_pre1[l].reshape(1, D_MODEL)
    wpost1 = norm_post1[l].reshape(1, D_MODEL)
    wpre2 = norm_pre2[l].reshape(1, D_MODEL)
    wpost2 = norm_post2[l].reshape(1, D_MODEL)

    conv_w8 = jnp.zeros((8, conv_w.shape[2]), F32).at[:CONV_W].set(conv_w[l])
    cst = jnp.zeros((8, LANES), F32)
    cst = cst.at[0, 2 * DN_HEADS:4 * DN_HEADS].set(a_log[l].reshape(-1))
    cst = cst.at[1, 2 * DN_HEADS:4 * DN_HEADS].set(dt_bias[l].reshape(-1))
    cst = cst.at[2].set(jnp.tile(dn_norm[l], 2))

    qkv_c, dqkv_c, gz_c, ba_c, new_k, new_v = _inproj(xp, mod3, ctx_row, wpre1, w_in_p, conv_w8, seq, batch)
    o_na_c = _ctx_attn(qkv_c, batch, seq)
    o_dn_c, new_s = _deltanet(dqkv_c, ba_c, cst, batch, seq, hps=4, a_chunks=4, emit_state=True)
    y_c = _merge_ffn(xp, o_na_c, o_dn_c, gz_c, mod3, ctx_row, wpost1, wpre2, wpost2,
                     w_ao_b, w_do_b, w_out_b, w1_b, w2_b)

    qkv_l, dqkv_l, gz_l, ba_l = _inproj(xs, mod3, lat_row, wpre1, w_in_p, conv_w8, dec_seq, dec_batch,
                                        rope_tabs=_rope_tables(dec_seq))
    past = cache_na_k.shape[3]
    ck_tm = jnp.transpose(cache_na_k[:, l], (0, 2, 1, 3)).reshape(dec_batch, past, NA_WIDTH)
    cv_tm = jnp.transpose(cache_na_v[:, l], (0, 2, 1, 3)).reshape(dec_batch, past, NA_WIDTH)
    o_na_l = _lat_attn(qkv_l, ck_tm, cv_tm, na_rpb[l], dec_batch, dec_seq)
    (o_dn_l,) = _deltanet(dqkv_l, ba_l, cst, dec_batch, dec_seq, hps=4, a_chunks=2, s0=state_delta)
    y_l = _merge_ffn(xs, o_na_l, o_dn_l, gz_l, mod3, lambda i: i // (dec_seq // TM_F), wpost1, wpre2, wpost2,
                     w_ao_b, w_do_b, w_out_b, w1_b, w2_b)

    return (y_c.reshape(batch, seq, D_MODEL), y_l.reshape(dec_batch, dec_seq, D_MODEL),
            new_k, new_v, new_s)
```

```python
import functools

import numpy as np
import jax
import jax.numpy as jnp
from jax import lax
from jax.experimental import pallas as pl
from jax.experimental.pallas import tpu as pltpu

F32 = jnp.float32
BF16 = jnp.bfloat16

D_MODEL = 1024
N_MOD = 6
GRID_W = 64
NA_HEADS = 8
HEAD_DIM = 64
NA_WIDTH = NA_HEADS * HEAD_DIM
WIN_R = 8
WIN_C = 16
ROPE_BASE = 10000.0
DN_HEADS = 8
DN_DK = 64
DN_DV = 64
DN_WIDTH = DN_HEADS * DN_DK
CONV_W = 5
CHUNK = 64
D_FF = 4 * D_MODEL
EPS = 1e-6
NEG_INF = -1e30

LANES = 128
SUBLANES = 8
N_PAIRS = NA_HEADS // 2
TM = 256
VMEM_LIMIT = 56 * 1024 * 1024

QKV_W = 3 * NA_WIDTH
DQKV_W = 3 * DN_WIDTH
BA_W = 4 * DN_HEADS
GZ_W = DN_WIDTH + 2 * D_MODEL
COL_QKV, COL_DQKV, COL_TAIL = 0, QKV_W, QKV_W + DQKV_W
IN_COLS = COL_TAIL + BA_W + GZ_W
TAIL_W = -(-(BA_W + GZ_W) // LANES) * LANES
P_COLS = COL_TAIL + TAIL_W


def _params(n_grid):
    return pltpu.CompilerParams(dimension_semantics=("arbitrary",) * n_grid,
                                vmem_limit_bytes=VMEM_LIMIT)


def _bdot(a, b):
    return jnp.dot(a, b, preferred_element_type=F32)


def _split3(a):
    hi = a.astype(BF16)
    r1 = a - hi.astype(F32)
    mid = r1.astype(BF16)
    lo = (r1 - mid.astype(F32)).astype(BF16)
    return hi, mid, lo


def _dot3_l(a, b_exact):
    hi, mid, lo = _split3(a)
    return _bdot(hi, b_exact) + _bdot(mid, b_exact) + _bdot(lo, b_exact)


def _dot2_l(a, b_exact):
    hi = a.astype(BF16)
    lo = (a - hi.astype(F32)).astype(BF16)
    return _bdot(hi, b_exact) + _bdot(lo, b_exact)


def _sigmoid(x):
    return 0.5 * jnp.tanh(0.5 * x) + 0.5


def _silu(x):
    return x * _sigmoid(x)


def _softplus(x):
    return jnp.maximum(x, 0.0) + jnp.log(1.0 + jnp.exp(-jnp.abs(x)))


def _rms_rows(x, w):
    ms = jnp.mean(x * x, axis=-1, keepdims=True)
    return x * lax.rsqrt(ms + EPS) * w


def _pair_block_ones():
    r = lax.broadcasted_iota(jnp.int32, (LANES, LANES), 0)
    c = lax.broadcasted_iota(jnp.int32, (LANES, LANES), 1)
    return jnp.where((r // 64) == (c // 64), 1.0, 0.0).astype(BF16)


def _adaln_kernel(c_ref, w_ref, b_ref, o_ref):
    c = c_ref[...]
    s = _silu(c).astype(BF16)
    o_ref[...] = _bdot(s, w_ref[...].astype(BF16)) + b_ref[...]


def _adaln(cvecs, w_ada, b_ada):
    tn = 1024
    n = w_ada.shape[1]
    return pl.pallas_call(
        _adaln_kernel,
        grid=(n // tn,),
        in_specs=[pl.BlockSpec((8, D_MODEL), lambda j: (0, 0)),
                  pl.BlockSpec((D_MODEL, tn), lambda j: (0, j)),
                  pl.BlockSpec((1, tn), lambda j: (0, j))],
        out_specs=pl.BlockSpec((8, tn), lambda j: (0, j)),
        out_shape=jax.ShapeDtypeStruct((8, n), F32),
        compiler_params=_params(1),
        name="adaln",
    )(cvecs, w_ada, b_ada.reshape(1, n))


def _inproj_kernel(*refs, latent, tiles_per_seq):
    it = iter(refs)
    x_ref, mod_ref, wpre_ref, w_ref, cw_ref = next(it), next(it), next(it), next(it), next(it)
    if latent:
        xprev_ref, xnext_ref, cos_ref, sin_ref = next(it), next(it), next(it), next(it)
    qkv_ref, dqkv_ref, gz_ref, ba_ref = next(it), next(it), next(it), next(it)
    if not latent:
        kc_ref, vc_ref = next(it), next(it)

    mod = mod_ref[0]

    def modulated(x):
        return (_rms_rows(x, wpre_ref[...]) * (1.0 + mod[1:2]) + mod[0:1]).astype(BF16)

    hb = modulated(x_ref[...])

    w_dn = w_ref[:, COL_DQKV:COL_DQKV + DQKV_W]
    halo_zero = jnp.zeros((SUBLANES, DQKV_W), F32)
    if latent:
        t = pl.program_id(0) % tiles_per_seq
        hb_ext = jnp.concatenate([modulated(xprev_ref[...]), hb, modulated(xnext_ref[...])], axis=0)
        pe = _bdot(hb_ext, w_dn)
        pe = jnp.concatenate([jnp.where(t > 0, pe[:SUBLANES], halo_zero), pe[SUBLANES:SUBLANES + TM],
                              jnp.where(t < tiles_per_seq - 1, pe[SUBLANES + TM:], halo_zero)], axis=0)
    else:
        pe = jnp.concatenate([halo_zero, _bdot(hb, w_dn), halo_zero], axis=0)
    acc = None
    for j in sorted(range(CONV_W), key=lambda s: abs(s - CONV_W // 2)):
        lo = SUBLANES - CONV_W // 2 + j
        term = pe[lo:lo + TM] * cw_ref[j:j + 1, :]
        acc = term if acc is None else acc + term
    y = _silu(acc)

    p = _bdot(hb, w_ref[:, COL_QKV:COL_QKV + QKV_W])
    if latent:
        qk = p[:, :2 * NA_WIDTH]
        lane = lax.broadcasted_iota(jnp.int32, qk.shape, 1)
        first = (lane % 32) < 16
        partner = jnp.where(first, pltpu.roll(qk, 2 * NA_WIDTH - 16, 1), pltpu.roll(qk, 16, 1))
        qkv_ref[:, :2 * NA_WIDTH] = (qk * cos_ref[...] + partner * sin_ref[...]).astype(BF16)
        qkv_ref[:, 2 * NA_WIDTH:] = p[:, 2 * NA_WIDTH:].astype(BF16)
    else:
        qkv_ref[...] = p.astype(BF16)
        for h in range(NA_HEADS):
            kc_ref[0, 0, h] = p[:, NA_WIDTH + h * HEAD_DIM:NA_WIDTH + (h + 1) * HEAD_DIM]
            vc_ref[0, 0, h] = p[:, 2 * NA_WIDTH + h * HEAD_DIM:2 * NA_WIDTH + (h + 1) * HEAD_DIM]

    split = (TAIL_W // LANES // 2) * LANES
    tail = jnp.concatenate([_bdot(hb, w_ref[:, COL_TAIL:COL_TAIL + split]),
                            _bdot(hb, w_ref[:, COL_TAIL + split:COL_TAIL + TAIL_W])], axis=1)
    ba_ref[...] = tail[:, :LANES]
    gz_ref[...] = pltpu.roll(tail, TAIL_W - BA_W, 1)[:, :GZ_W].astype(BF16)

    bd_ones = _pair_block_ones()
    for s in range(DQKV_W // LANES):
        ys = y[:, s * LANES:(s + 1) * LANES]
        if s < 2 * DN_WIDTH // LANES:
            ys = ys * lax.rsqrt(_dot2_l(ys * ys, bd_ones) + EPS)
        dqkv_ref[:, s * LANES:(s + 1) * LANES] = ys.astype(BF16)


def _inproj(x2d, mod3, mod_row_of_tile, wpre, w_packed, conv_w8, seq, batch, rope_tabs=None):
    n_tok = x2d.shape[0]
    latent = rope_tabs is not None
    tiles_per_seq = seq // TM
    rows8 = TM // SUBLANES
    n_blk8 = n_tok // SUBLANES
    in_specs = [pl.BlockSpec((TM, D_MODEL), lambda i: (i, 0)),
                pl.BlockSpec((1, N_MOD, D_MODEL), lambda i: (mod_row_of_tile(i), 0, 0)),
                pl.BlockSpec((1, D_MODEL), lambda i: (0, 0)),
                pl.BlockSpec((D_MODEL, P_COLS), lambda i: (0, 0)),
                pl.BlockSpec((8, DQKV_W), lambda i: (0, 0))]
    args = [x2d, mod3, wpre, w_packed, conv_w8]
    out_specs = [pl.BlockSpec((TM, QKV_W), lambda i: (i, 0)),
                 pl.BlockSpec((TM, DQKV_W), lambda i: (i, 0)),
                 pl.BlockSpec((TM, GZ_W), lambda i: (i, 0)),
                 pl.BlockSpec((TM, LANES), lambda i: (i, 0))]
    out_shape = [jax.ShapeDtypeStruct((n_tok, QKV_W), BF16),
                 jax.ShapeDtypeStruct((n_tok, DQKV_W), BF16),
                 jax.ShapeDtypeStruct((n_tok, GZ_W), BF16),
                 jax.ShapeDtypeStruct((n_tok, LANES), F32)]
    if latent:
        in_specs += [pl.BlockSpec((SUBLANES, D_MODEL), lambda i: (jnp.maximum(i * rows8 - 1, 0), 0)),
                     pl.BlockSpec((SUBLANES, D_MODEL), lambda i: (jnp.minimum((i + 1) * rows8, n_blk8 - 1), 0))]
        args += [x2d, x2d]
        for tab in rope_tabs:
            in_specs.append(pl.BlockSpec((TM, 2 * NA_WIDTH), lambda i: (i % tiles_per_seq, 0)))
            args.append(tab)
    else:
        assert seq == TM
        cache_spec = pl.BlockSpec((1, 1, NA_HEADS, seq, HEAD_DIM), lambda i: (i, 0, 0, 0, 0))
        cache_shape = jax.ShapeDtypeStruct((batch, 1, NA_HEADS, seq, HEAD_DIM), F32)
        out_specs += [cache_spec, cache_spec]
        out_shape += [cache_shape, cache_shape]
    return pl.pallas_call(
        functools.partial(_inproj_kernel, latent=latent, tiles_per_seq=tiles_per_seq),
        grid=(n_tok // TM,),
        in_specs=in_specs,
        out_specs=out_specs,
        out_shape=out_shape,
        compiler_params=_params(1),
        name="inproj_lat" if latent else "inproj_ctx",
    )(*args)


def _rope_tables(seq):
    half = HEAD_DIM // 2
    nf = half // 2
    pos = np.arange(seq)
    inv = (np.float32(ROPE_BASE) ** (-np.arange(nf, dtype=np.float32) / np.float32(nf))).astype(np.float32)
    ang_r = ((pos // GRID_W).astype(np.float32)[:, None] * inv).astype(np.float64)
    ang_c = ((pos % GRID_W).astype(np.float32)[:, None] * inv).astype(np.float64)

    def grp(ang):
        return (np.concatenate([np.cos(ang), np.cos(ang)], 1),
                np.concatenate([-np.sin(ang), np.sin(ang)], 1))

    cr, sr = grp(ang_r)
    cc, sc = grp(ang_c)
    cos_h = np.concatenate([cr, cc], 1)
    sin_h = np.concatenate([sr, sc], 1)
    reps = 2 * NA_HEADS
    return (jnp.asarray(np.tile(cos_h, (1, reps)), F32),
            jnp.asarray(np.tile(sin_h, (1, reps)), F32))


def _ctx_attn_kernel(q_ref, k_ref, v_ref, o_ref, *, seq, nb):
    lane = lax.broadcasted_iota(jnp.int32, (1, LANES), 1)
    scale = HEAD_DIM ** -0.5
    pairs = [(j, hp) for j in range(nb) for hp in range(N_PAIRS)]
    heads = [(j, hp, e) for j, hp in pairs for e in range(2)]
    rows = [slice(j * seq, (j + 1) * seq) for j in range(nb)]
    sls = [slice(hp * LANES, (hp + 1) * LANES) for hp in range(N_PAIRS)]
    q = {(j, hp): q_ref[rows[j], sls[hp]] for j, hp in pairs}
    k = {(j, hp): k_ref[rows[j], sls[hp]] for j, hp in pairs}
    v = {(j, hp): v_ref[rows[j], sls[hp]] for j, hp in pairs}
    m_e = (lane < 64, lane >= 64)
    zero = jnp.zeros((), BF16)
    s = {(j, hp, e): lax.dot_general(jnp.where(m_e[e], q[j, hp] * scale, zero), k[j, hp],
                                     (((1,), (1,)), ((), ())), preferred_element_type=F32) for j, hp, e in heads}
    p = {u: jnp.exp(s[u] - jnp.max(s[u], axis=-1, keepdims=True)) for u in heads}
    l = {u: jnp.sum(p[u], axis=-1, keepdims=True) for u in heads}
    o = {(j, hp, e): _bdot(p[j, hp, e].astype(BF16), jnp.where(m_e[e], v[j, hp], zero)) for j, hp, e in heads}
    for j, hp in pairs:
        o_ref[rows[j], sls[hp]] = (o[j, hp, 0] / l[j, hp, 0] + o[j, hp, 1] / l[j, hp, 1]).astype(BF16)


def _ctx_attn(qkv, batch, seq, nb):
    assert batch % nb == 0

    def col(cb):
        return pl.BlockSpec((nb * seq, NA_WIDTH), lambda b: (b, cb))
    return pl.pallas_call(
        functools.partial(_ctx_attn_kernel, seq=seq, nb=nb),
        grid=(batch // nb,),
        in_specs=[col(0), col(1), col(2)],
        out_specs=pl.BlockSpec((nb * seq, NA_WIDTH), lambda b: (b, 0)),
        out_shape=jax.ShapeDtypeStruct((batch * seq, NA_WIDTH), BF16),
        compiler_params=_params(1),
        name="ctx_attn",
    )(qkv, qkv, qkv)


Q_HALF = 512
K_WIN = 768


def _lat_attn_kernel(rpb_ref, q_ref, k_ref, v_ref, ck_ref, cv_ref, o_ref, bias_ref, *, rows):
    hp = pl.program_id(0)
    b = pl.program_id(1)
    half = pl.program_id(2)
    kr_n = min(WIN_R, rows)
    rows_half = Q_HALF // GRID_W
    krows_win = K_WIN // GRID_W
    n_dr = 2 * WIN_R - 1
    n_dc = 2 * WIN_C - 1

    @pl.when(jnp.logical_and(b == 0, half == 0))
    def _build_bias():
        qc = lax.broadcasted_iota(jnp.int32, (GRID_W, LANES), 0)
        ln = lax.broadcasted_iota(jnp.int32, (GRID_W, LANES), 1)
        kc = ln % GRID_W
        cs = jnp.clip(qc - WIN_C // 2, 0, GRID_W - WIN_C)
        valid = jnp.logical_and(kc >= cs, kc < cs + WIN_C)
        dc = jnp.clip(kc - qc + (WIN_C - 1), 0, n_dc - 1)
        dcv = jnp.where(valid, dc, -1)
        neg = jnp.full((GRID_W, LANES), NEG_INF, F32)
        for e in range(2):
            blocks = []
            for a in range(n_dr):
                blk = neg
                for j in range(n_dc):
                    blk = jnp.where(dcv == j, rpb_ref[((2 * hp + e) * n_dr + a) * n_dc + j], blk)
                blocks.append(blk)
            for hf in range(2):
                for rl in range(rows_half):
                    r = hf * rows_half + rl
                    rs = min(max(r - kr_n // 2, 0), rows - kr_n)
                    for kp in range(krows_win // 2):
                        parts = []
                        for kl in (2 * kp, 2 * kp + 1):
                            kr = kl + hf * (rows - krows_win)
                            parts.append(blocks[kr - r + WIN_R - 1] if rs <= kr < rs + kr_n else neg)
                        blk = jnp.where(ln < GRID_W, parts[0], parts[1])
                        bias_ref[e, hf, rl * GRID_W:(rl + 1) * GRID_W, kp * LANES:(kp + 1) * LANES] = blk

    lane = lax.broadcasted_iota(jnp.int32, (1, LANES), 1)
    scale = HEAD_DIM ** -0.5
    zero = jnp.zeros((), BF16)
    k0 = pl.multiple_of(half * ((rows - krows_win) * GRID_W), GRID_W)
    kw = k_ref[pl.ds(k0, K_WIN), :]
    vw = v_ref[pl.ds(k0, K_WIN), :]
    ck = ck_ref[0].astype(BF16)
    cv = cv_ref[0].astype(BF16)
    SUB = 256
    m_e = (lane < 64, lane >= 64)
    ve = [jnp.where(m, vw, zero) for m in m_e]
    cve = [jnp.where(m, cv, zero) for m in m_e]
    units = [(qs, e) for qs in range(Q_HALF // SUB) for e in range(2)]
    nt = (((1,), (1,)), ((), ()))
    qe = {(qs, e): jnp.where(m_e[e], q_ref[qs * SUB:(qs + 1) * SUB, :] * scale, zero) for qs, e in units}
    s_loc = {(qs, e): lax.dot_general(qe[qs, e], kw, nt, preferred_element_type=F32)
             + bias_ref[e, half, qs * SUB:(qs + 1) * SUB, :] for qs, e in units}
    s_ctx = {u: lax.dot_general(qe[u], ck, nt, preferred_element_type=F32) for u in units}
    m = {u: jnp.maximum(jnp.max(s_loc[u], axis=-1, keepdims=True), jnp.max(s_ctx[u], axis=-1, keepdims=True))
         for u in units}
    p_loc = {u: jnp.exp(s_loc[u] - m[u]) for u in units}
    p_ctx = {u: jnp.exp(s_ctx[u] - m[u]) for u in units}
    l = {u: jnp.sum(p_loc[u], axis=-1, keepdims=True) + jnp.sum(p_ctx[u], axis=-1, keepdims=True) for u in units}
    o = {(qs, e): (_bdot(p_loc[qs, e].astype(BF16), ve[e]) + _bdot(p_ctx[qs, e].astype(BF16), cve[e])) / l[qs, e]
         for qs, e in units}
    for qs in range(Q_HALF // SUB):
        o_ref[qs * SUB:(qs + 1) * SUB, :] = (o[qs, 0] + o[qs, 1]).astype(BF16)


def _lat_attn(qkv, ck_tm, cv_tm, rpb, batch, seq):
    rows = seq // GRID_W
    assert rows * GRID_W == seq and rows - K_WIN // GRID_W == 4 and Q_HALF * 2 == seq
    halves = seq // Q_HALF
    past = ck_tm.shape[1]
    kernel = functools.partial(_lat_attn_kernel, rows=rows)
    return pl.pallas_call(
        kernel,
        grid=(N_PAIRS, batch, halves),
        in_specs=[pl.BlockSpec(memory_space=pltpu.SMEM),
                  pl.BlockSpec((Q_HALF, LANES), lambda hp, b, hf: (b * halves + hf, hp)),
                  pl.BlockSpec((seq, LANES), lambda hp, b, hf: (b, N_PAIRS + hp)),
                  pl.BlockSpec((seq, LANES), lambda hp, b, hf: (b, 2 * N_PAIRS + hp)),
                  pl.BlockSpec((1, past, LANES), lambda hp, b, hf: (b, 0, hp)),
                  pl.BlockSpec((1, past, LANES), lambda hp, b, hf: (b, 0, hp))],
        out_specs=pl.BlockSpec((Q_HALF, LANES), lambda hp, b, hf: (b * halves + hf, hp)),
        out_shape=jax.ShapeDtypeStruct((batch * seq, NA_WIDTH), BF16),
        scratch_shapes=[pltpu.VMEM((2, 2, Q_HALF, K_WIN), F32)],
        compiler_params=_params(3),
        name="lat_attn",
    )(rpb.reshape(-1), qkv, qkv, qkv, ck_tm, cv_tm)


def _dn_kernel(*refs, seq, nb, hps, a_chunks, has_s0, emit_state):
    it = iter(refs)
    q_ref, k_ref, v_ref, ba_ref, cst_ref = next(it), next(it), next(it), next(it), next(it)
    s0_ref = next(it) if has_s0 else None
    o_ref = next(it)
    st_ref = next(it) if emit_state else None
    (beta_s, g_s, u_s, wq_s, lb_s, el_s, odir_s) = it

    hg = pl.program_id(1)
    wl = hps * LANES
    n = seq // CHUNK
    nt = nb * n
    hp_sl = [slice(h * LANES, (h + 1) * LANES) for h in range(hps)]

    def lanes_cat(parts):
        return parts[0] if len(parts) == 1 else jnp.concatenate(parts, axis=1)

    r128 = lax.broadcasted_iota(jnp.int32, (LANES, LANES), 0)
    c128 = lax.broadcasted_iota(jnp.int32, (LANES, LANES), 1)
    bd_mask = (r128 // 64) == (c128 // 64)
    bd_ones = _pair_block_ones()

    ba = ba_ref[...]
    beta_all = _sigmoid(ba)
    g_all = -jnp.exp(cst_ref[0:1, :]) * _softplus(ba + cst_ref[1:2, :])
    src = lax.broadcasted_iota(jnp.int32, (LANES, wl), 0)
    dst_head = 2 * hps * hg + lax.broadcasted_iota(jnp.int32, (LANES, wl), 1) // 64
    for d in range(2):
        sel_b = jnp.where(src == d * DN_HEADS + dst_head, 1.0, 0.0).astype(BF16)
        sel_g = jnp.where(src == (2 + d) * DN_HEADS + dst_head, 1.0, 0.0).astype(BF16)
        beta_s[d] = _dot2_l(beta_all, sel_b)
        g_s[d] = _dot3_l(g_all, sel_g)

    ri = lax.broadcasted_iota(jnp.int32, (CHUNK, LANES), 0)
    lj = lax.broadcasted_iota(jnp.int32, (CHUNK, LANES), 1) % 64
    lo_half = lax.broadcasted_iota(jnp.int32, (CHUNK, LANES), 1) < 64
    lo_half2 = (lax.broadcasted_iota(jnp.int32, (CHUNK, 2 * LANES), 1) % LANES) < 64
    diag = ri == lj
    strict = (ri > lj, ri < lj)
    riw = lax.broadcasted_iota(jnp.int32, (CHUNK, wl), 0)
    ljw = lax.broadcasted_iota(jnp.int32, (CHUNK, wl), 1) % 64
    diag_w = riw == ljw
    incl_w = (riw >= ljw, riw <= ljw)
    r64 = lax.broadcasted_iota(jnp.int32, (CHUNK, CHUNK), 0)
    c64 = lax.broadcasted_iota(jnp.int32, (CHUNK, CHUNK), 1)
    tri = (jnp.where(c64 <= r64, 1.0, 0.0).astype(BF16), jnp.where(c64 >= r64, 1.0, 0.0).astype(BF16))

    off_blk = ([], [])
    for lvl in range(6):
        s = 1 << lvl
        same2 = (ri // (2 * s)) == (lj // (2 * s))
        off_blk[0].append(jnp.logical_and(same2, (ri // s) == (lj // s) + 1))
        off_blk[1].append(jnp.logical_and(same2, (lj // s) == (ri // s) + 1))

    def stack_heads(x, lo_mask):
        return jnp.concatenate([jnp.where(lo_mask, x, 0.0), jnp.where(lo_mask, 0.0, x)], axis=0).astype(BF16)

    def chunk_rows(c):
        return pl.ds(c * CHUNK if isinstance(c, int) else pl.multiple_of(c * CHUNK, CHUNK), CHUNK)

    def chunk_load(c):
        rows = chunk_rows(c)
        return (q_ref[rows, :].astype(F32), k_ref[rows, :].astype(F32), v_ref[rows, :].astype(F32),
                [beta_s[d, rows, :] for d in range(2)], [g_s[d, rows, :] for d in range(2)])

    def group_a(loaded):
        nc = len(loaded)
        cd = [(i, d) for i in range(nc) for d in range(2)]
        units = [(i, h, d) for i in range(nc) for h in range(hps) for d in range(2)]
        qcs = [ld[0] * (DN_DK ** -0.5) for ld in loaded]
        kcs = [ld[1] for ld in loaded]
        vcs = [ld[2] for ld in loaded]
        kt_bd = {}
        for i in range(nc):
            for h in range(hps):
                kc = kcs[i][:, hp_sl[h]]
                ktt = jnp.transpose(jnp.concatenate([kc, kc], axis=0))
                kt_bd[i, h] = jnp.where(bd_mask, ktt, 0.0)
        gsp = {(i, d): _split3(loaded[i][4][d]) for i, d in cd}
        gc = {}
        for part in range(3):
            for i, d in cd:
                t = _bdot(tri[d], gsp[i, d][part])
                gc[i, d] = t if part == 0 else gc[i, d] + t
        rrow = {u: jnp.sum(jnp.where(diag_w, gc[u], 0.0), axis=0, keepdims=True) for u in cd}
        dec = {u: jnp.exp(jnp.minimum(gc[u] - rrow[u], 0.0)) for u in cd}
        eg = {u: jnp.exp(gc[u]) for u in cd}
        kbeta = {(i, d): kcs[i] * loaded[i][3][d] for i, d in cd}
        vbeta = {(i, d): vcs[i] * loaded[i][3][d] for i, d in cd}
        kbeg = {u: kbeta[u] * eg[u] for u in cd}
        m1 = {(i, h): _bdot(jnp.concatenate([qcs[i][:, hp_sl[h]], kbeta[i, 0][:, hp_sl[h]],
                                             kbeta[i, 1][:, hp_sl[h]]], axis=0).astype(BF16),
                            kt_bd[i, h].astype(BF16))
              for i in range(nc) for h in range(hps)}
        amat = {(i, h, d): jnp.where(strict[d], m1[i, h][CHUNK * (1 + d):CHUNK * (2 + d)] * dec[i, d][:, hp_sl[h]], 0.0)
                for i, h, d in units}
        o_st = {(i, h, d): stack_heads(jnp.where(off_blk[d][0], amat[i, h, d], 0.0), lo_half) for i, h, d in units}
        tmat = {(i, h, d): jnp.where(diag, 1.0, 0.0) - jnp.where(off_blk[d][0], amat[i, h, d], 0.0) for i, h, d in units}
        prod = {u: _bdot(amat[u].astype(BF16), o_st[u]) for u in units}
        wmat = {u: amat[u] - prod[u] for u in units}
        for lvl in range(1, 5):
            o_st = {(i, h, d): stack_heads(jnp.where(off_blk[d][lvl], wmat[i, h, d], 0.0), lo_half) for i, h, d in units}
            prod = {u: _bdot(jnp.concatenate([tmat[u], wmat[u]], axis=0).astype(BF16), o_st[u]) for u in units}
            tmat = {u: tmat[u] - prod[u][0:CHUNK] for u in units}
            wmat = {u: wmat[u] - prod[u][CHUNK:] for u in units}
        o_st = {(i, h, d): stack_heads(jnp.where(off_blk[d][5], wmat[i, h, d], 0.0), lo_half) for i, h, d in units}
        prod = {u: _bdot(tmat[u].astype(BF16), o_st[u]) for u in units}
        tmat = {u: tmat[u] - prod[u] for u in units}
        x = {}
        for i, h, d in units:
            rhs = jnp.concatenate([vbeta[i, d][:, hp_sl[h]], kbeg[i, d][:, hp_sl[h]]], axis=1)
            x[i, h, d] = _bdot(tmat[i, h, d].astype(BF16), stack_heads(rhs, lo_half2))
        results = []
        for i in range(nc):
            out = []
            for d in range(2):
                u = lanes_cat([x[i, h, d][:, :LANES] for h in range(hps)])
                w = lanes_cat([x[i, h, d][:, LANES:] for h in range(hps)])
                qk = jnp.where(incl_w[d], lanes_cat([m1[i, h][0:CHUNK] for h in range(hps)]) * dec[i, d], 0.0)
                qg = qcs[i] * eg[i, d]
                glast = gc[i, d][CHUNK - 1:CHUNK, :] if d == 0 else gc[i, d][0:1, :]
                kmul = jnp.exp(glast - rrow[i, d])
                kd_bd = lanes_cat([kt_bd[i, h] * kmul[:, hp_sl[h]] for h in range(hps)])
                out.append((u,
                            jnp.concatenate([w, qg], axis=0).astype(BF16),
                            jnp.concatenate([qk, kd_bd], axis=0).astype(BF16),
                            jnp.broadcast_to(jnp.exp(glast), (8, wl))))
            results.append(out)
        return results

    group = min(a_chunks, nt)

    def phase_a(gi, carry):
        cs = [gi * group + j for j in range(group)]
        loaded = [chunk_load(c) for c in cs]
        results = group_a(loaded)
        for c, res in zip(cs, results):
            for d in range(2):
                u_s[d, c], wq_s[d, c], lb_s[d, c], el_s[d, c] = res[d]
        return carry

    if nt == group:
        phase_a(0, 0)
    else:
        lax.fori_loop(0, nt // group, phase_a, 0)

    dh = [(j, d, h) for j in range(nb) for d in range(2) for h in range(hps)]
    if has_s0:
        zero = jnp.zeros((DN_DK, DN_DV), F32)

        def bd_state(j, d, h):
            top = jnp.concatenate([s0_ref[j, 0, d, 2 * h], zero], axis=1)
            bot = jnp.concatenate([zero, s0_ref[j, 0, d, 2 * h + 1]], axis=1)
            return jnp.concatenate([top, bot], axis=0)
        s_init = tuple(bd_state(j, d, h) for j, d, h in dh)
    else:
        s_init = tuple(jnp.zeros((LANES, LANES), F32) for _ in dh)

    def phase_b(s, states):
        jd = [(j, d) for j in range(nb) for d in range(2)]
        cs = {(j, d): j * n + (s if d == 0 else n - 1 - s) for j, d in jd}
        ld = {u: (wq_s[u[1], cs[u]], u_s[u[1], cs[u]], lb_s[u[1], cs[u]], el_s[u[1], cs[u], 0:1, :]) for u in jd}
        st = dict(zip(dh, states))
        ws = {(j, d, h): _bdot(ld[j, d][0][:, hp_sl[h]], st[j, d, h].astype(BF16)) for j, d, h in dh}
        vn = {(j, d, h): ld[j, d][1][:, hp_sl[h]] - ws[j, d, h][0:CHUNK] for j, d, h in dh}
        r2 = {(j, d, h): _bdot(ld[j, d][2][:, hp_sl[h]], stack_heads(vn[j, d, h], lo_half)) for j, d, h in dh}
        new_states = tuple(st[j, d, h] * ld[j, d][3][:, hp_sl[h]] + r2[j, d, h][CHUNK:] for j, d, h in dh)
        for j, d in jd:
            odir_s[d, chunk_rows(cs[j, d]), :] = lanes_cat([ws[j, d, h][CHUNK:] + r2[j, d, h][0:CHUNK]
                                                            for h in range(hps)])
        return new_states

    s_fin = dict(zip(dh, lax.fori_loop(0, n, phase_b, s_init)))

    for h in range(hps):
        o = odir_s[0, :, hp_sl[h]] + odir_s[1, :, hp_sl[h]]
        msq = _dot2_l(o * o, bd_ones) * (1.0 / DN_DV)
        o_ref[:, hp_sl[h]] = (o * lax.rsqrt(msq + EPS) * cst_ref[2:3, :]).astype(BF16)

    if emit_state:
        for j, d, h in dh:
            st_ref[j, 0, d, 2 * h] = s_fin[j, d, h][0:DN_DK, 0:DN_DV]
            st_ref[j, 0, d, 2 * h + 1] = s_fin[j, d, h][DN_DK:, DN_DV:]


def _deltanet(dqkv, ba, cst, batch, seq, nb, hps, a_chunks, s0=None, emit_state=False):
    assert batch % nb == 0
    rows = nb * seq
    nt = rows // CHUNK
    wl = hps * LANES
    groups = N_PAIRS // hps
    has_s0 = s0 is not None

    def col(cbase):
        return pl.BlockSpec((rows, wl), lambda b, hg: (b, cbase // wl + hg))

    st_spec = pl.BlockSpec((nb, 1, 2, 2 * hps, DN_DK, DN_DV), lambda b, hg: (b, 0, 0, hg, 0, 0))
    in_specs = [col(0), col(DN_WIDTH), col(2 * DN_WIDTH),
                pl.BlockSpec((rows, LANES), lambda b, hg: (b, 0)),
                pl.BlockSpec((8, LANES), lambda b, hg: (0, 0))]
    args = [dqkv, dqkv, dqkv, ba, cst]
    if has_s0:
        in_specs.append(st_spec)
        args.append(s0)
    out_specs = [pl.BlockSpec((rows, wl), lambda b, hg: (b, hg))]
    out_shape = [jax.ShapeDtypeStruct((batch * seq, DN_HEADS * DN_DV), BF16)]
    if emit_state:
        out_specs.append(st_spec)
        out_shape.append(jax.ShapeDtypeStruct((batch, 1, 2, DN_HEADS, DN_DK, DN_DV), F32))
    scratch = [pltpu.VMEM((2, rows, wl), F32), pltpu.VMEM((2, rows, wl), F32),
               pltpu.VMEM((2, nt, CHUNK, wl), F32),
               pltpu.VMEM((2, nt, 2 * CHUNK, wl), BF16),
               pltpu.VMEM((2, nt, 3 * CHUNK, wl), BF16),
               pltpu.VMEM((2, nt, 8, wl), F32),
               pltpu.VMEM((2, rows, wl), F32)]
    res = pl.pallas_call(
        functools.partial(_dn_kernel, seq=seq, nb=nb, hps=hps, a_chunks=a_chunks, has_s0=has_s0,
                          emit_state=emit_state),
        grid=(batch // nb, groups),
        in_specs=in_specs,
        out_specs=out_specs,
        out_shape=out_shape,
        scratch_shapes=scratch,
        compiler_params=_params(2),
        name="deltanet_lat" if has_s0 else "deltanet_ctx",
    )(*args)
    return res


TM_F = 2 * TM


def _merge_ffn_kernel(x_ref, ona_ref, odn_ref, gz_ref, mod_ref, wpost1_ref, wpre2_ref, wpost2_ref,
                      wao_ref, wdo_ref, wout_ref, w1_ref, w2_ref, o_ref):
    mod = mod_ref[0]
    subs = [slice(s * TM, (s + 1) * TM) for s in range(TM_F // TM)]
    z_sl, gna_sl, gdn_sl = (slice(0, DN_WIDTH), slice(DN_WIDTH, DN_WIDTH + D_MODEL),
                            slice(DN_WIDTH + D_MODEL, GZ_W))
    odn = [odn_ref[sl, :].astype(F32) * _silu(gz_ref[sl, z_sl].astype(F32)) for sl in subs]
    br_na = [_bdot(ona_ref[sl, :], wao_ref[...]) for sl in subs]
    br_dn = [_bdot(t.astype(BF16), wdo_ref[...]) for t in odn]
    m = [_sigmoid(gz_ref[sl, gna_sl].astype(F32)) * a + _sigmoid(gz_ref[sl, gdn_sl].astype(F32)) * b
         for sl, a, b in zip(subs, br_na, br_dn)]
    y = [_bdot(t.astype(BF16), wout_ref[...]) for t in m]
    x1 = [x_ref[sl, :] + mod[2:3] * _rms_rows(t, wpost1_ref[...]) for sl, t in zip(subs, y)]
    h = [_rms_rows(t, wpre2_ref[...]) * (1.0 + mod[4:5]) + mod[3:4] for t in x1]
    f1 = [_bdot(t.astype(BF16), w1_ref[...]) for t in h]
    r = [jnp.maximum(t, 0.0) for t in f1]
    f = [_bdot((t * t).astype(BF16), w2_ref[...]) for t in r]
    for sl, a, b in zip(subs, x1, f):
        o_ref[sl, :] = a + mod[5:6] * _rms_rows(b, wpost2_ref[...])


def _merge_ffn(x2d, o_na, o_dn, gz, mod3, mod_row_of_tile, wpost1, wpre2, wpost2, w_ao, w_do, w_out, w1, w2):
    n_tok = x2d.shape[0]
    row = lambda i: (i, 0)
    const = lambda i: (0, 0)

    def resident(shape):
        return pl.BlockSpec(shape, const, pipeline_mode=pl.Buffered(1))

    return pl.pallas_call(
        _merge_ffn_kernel,
        grid=(n_tok // TM_F,),
        in_specs=[pl.BlockSpec((TM_F, D_MODEL), row),
                  pl.BlockSpec((TM_F, NA_WIDTH), row),
                  pl.BlockSpec((TM_F, DN_WIDTH), row),
                  pl.BlockSpec((TM_F, GZ_W), row),
                  pl.BlockSpec((1, N_MOD, D_MODEL), lambda i: (mod_row_of_tile(i), 0, 0)),
                  pl.BlockSpec((1, D_MODEL), const),
                  pl.BlockSpec((1, D_MODEL), const),
                  pl.BlockSpec((1, D_MODEL), const),
                  resident((NA_WIDTH, D_MODEL)),
                  resident((DN_WIDTH, D_MODEL)),
                  resident((D_MODEL, D_MODEL)),
                  resident((D_MODEL, D_FF)),
                  resident((D_FF, D_MODEL))],
        out_specs=pl.BlockSpec((TM_F, D_MODEL), row),
        out_shape=jax.ShapeDtypeStruct((n_tok, D_MODEL), F32),
        compiler_params=_params(1),
        name="merge_ffn",
    )(x2d, o_na, o_dn, gz, mod3, wpost1, wpre2, wpost2, w_ao, w_do, w_out, w1, w2)


def _pack_w_in(w):
    assert w.shape[1] == IN_COLS
    return jnp.pad(w.astype(BF16), ((0, 0), (0, P_COLS - IN_COLS)))


def kernel(x_prompt, x_sample, c, cache_na_k, cache_na_v, state_delta, c_ctx, w_ada, b_ada, norm_pre1, norm_post1, norm_pre2, norm_post2, w_in, conv_w, a_log, dt_bias, dn_norm, na_rpb, w_ao, w_do, w_out, w_ff1, w_ff2):
    batch, seq, _ = x_prompt.shape
    dec_batch, dec_seq, _ = x_sample.shape
    depth = w_in.shape[0]
    assert depth == 1 and seq == TM and dec_seq % TM_F == 0 and (batch * seq) % TM_F == 0 and dec_batch < 8

    xp = x_prompt.reshape(batch * seq, D_MODEL)
    xs = x_sample.reshape(dec_batch * dec_seq, D_MODEL)
    l = 0

    cvecs = jnp.zeros((8, D_MODEL), F32).at[:dec_batch].set(c).at[dec_batch].set(c_ctx)
    mod3 = _adaln(cvecs, w_ada[l], b_ada[l]).reshape(8, N_MOD, D_MODEL)
    ctx_row = lambda i: dec_batch
    tiles_per_lat = dec_seq // TM
    lat_row = lambda i: i // tiles_per_lat

    w_in_p = _pack_w_in(w_in[l])
    w_ao_b = w_ao[l].astype(BF16)
    w_do_b = w_do[l].astype(BF16)
    w_out_b = w_out[l].astype(BF16)
    w1_b = w_ff1[l].astype(BF16)
    w2_b = w_ff2[l].astype(BF16)
    wpre1 = norm_pre1[l].reshape(1, D_MODEL)
    wpost1 = norm_post1[l].reshape(1, D_MODEL)
    wpre2 = norm_pre2[l].reshape(1, D_MODEL)
    wpost2 = norm_post2[l].reshape(1, D_MODEL)

    conv_w8 = jnp.zeros((8, conv_w.shape[2]), F32).at[:CONV_W].set(conv_w[l])
    cst = jnp.zeros((8, LANES), F32)
    cst = cst.at[0, 2 * DN_HEADS:4 * DN_HEADS].set(a_log[l].reshape(-1))
    cst = cst.at[1, 2 * DN_HEADS:4 * DN_HEADS].set(dt_bias[l].reshape(-1))
    cst = cst.at[2].set(jnp.tile(dn_norm[l], 2))

    qkv_c, dqkv_c, gz_c, ba_c, new_k, new_v = _inproj(xp, mod3, ctx_row, wpre1, w_in_p, conv_w8, seq, batch)
    o_na_c = _ctx_attn(qkv_c, batch, seq, nb=2)
    o_dn_c, new_s = _deltanet(dqkv_c, ba_c, cst, batch, seq, nb=2, hps=4, a_chunks=8, emit_state=True)
    y_c = _merge_ffn(xp, o_na_c, o_dn_c, gz_c, mod3, ctx_row, wpost1, wpre2, wpost2,
                     w_ao_b, w_do_b, w_out_b, w1_b, w2_b)

    qkv_l, dqkv_l, gz_l, ba_l = _inproj(xs, mod3, lat_row, wpre1, w_in_p, conv_w8, dec_seq, dec_batch,
                                        rope_tabs=_rope_tables(dec_seq))
    past = cache_na_k.shape[3]
    ck_tm = jnp.transpose(cache_na_k[:, l], (0, 2, 1, 3)).reshape(dec_batch, past, NA_WIDTH)
    cv_tm = jnp.transpose(cache_na_v[:, l], (0, 2, 1, 3)).reshape(dec_batch, past, NA_WIDTH)
    o_na_l = _lat_attn(qkv_l, ck_tm, cv_tm, na_rpb[l], dec_batch, dec_seq)
    (o_dn_l,) = _deltanet(dqkv_l, ba_l, cst, dec_batch, dec_seq, nb=1, hps=4, a_chunks=2, s0=state_delta)
    y_l = _merge_ffn(xs, o_na_l, o_dn_l, gz_l, mod3, lambda i: i // (dec_seq // TM_F), wpost1, wpre2, wpost2,
                     w_ao_b, w_do_b, w_out_b, w1_b, w2_b)

    return (y_c.reshape(batch, seq, D_MODEL), y_l.reshape(dec_batch, dec_seq, D_MODEL),
            new_k, new_v, new_s)
```

```python
import functools

import numpy as np
import jax
import jax.numpy as jnp
from jax import lax
from jax.experimental import pallas as pl
from jax.experimental.pallas import tpu as pltpu

F32 = jnp.float32
BF16 = jnp.bfloat16

D_MODEL = 1024
N_MOD = 6
GRID_W = 64
NA_HEADS = 8
HEAD_DIM = 64
NA_WIDTH = NA_HEADS * HEAD_DIM
WIN_R = 8
WIN_C = 16
ROPE_BASE = 10000.0
DN_HEADS = 8
DN_DK = 64
DN_DV = 64
DN_WIDTH = DN_HEADS * DN_DK
CONV_W = 5
CHUNK = 64
D_FF = 4 * D_MODEL
EPS = 1e-6
NEG_INF = -1e30

LANES = 128
SUBLANES = 8
N_PAIRS = NA_HEADS // 2
TM = 256
VMEM_LIMIT = 56 * 1024 * 1024

QKV_W = 3 * NA_WIDTH
DQKV_W = 3 * DN_WIDTH
BA_W = 4 * DN_HEADS
GZ_W = DN_WIDTH + 2 * D_MODEL
COL_QKV, COL_DQKV, COL_TAIL = 0, QKV_W, QKV_W + DQKV_W
IN_COLS = COL_TAIL + BA_W + GZ_W
MAIN_COLS = (IN_COLS // LANES) * LANES
TAIL_W = MAIN_COLS - COL_TAIL + LANES


def _params(n_grid):
    return pltpu.CompilerParams(dimension_semantics=("arbitrary",) * n_grid,
                                vmem_limit_bytes=VMEM_LIMIT)


def _bdot(a, b):
    return jnp.dot(a, b, preferred_element_type=F32)


def _split3(a):
    hi = a.astype(BF16)
    r1 = a - hi.astype(F32)
    mid = r1.astype(BF16)
    lo = (r1 - mid.astype(F32)).astype(BF16)
    return hi, mid, lo


def _dot3_l(a, b_exact):
    hi, mid, lo = _split3(a)
    return _bdot(hi, b_exact) + _bdot(mid, b_exact) + _bdot(lo, b_exact)


def _dot2_l(a, b_exact):
    hi = a.astype(BF16)
    lo = (a - hi.astype(F32)).astype(BF16)
    return _bdot(hi, b_exact) + _bdot(lo, b_exact)


def _sigmoid(x):
    return 0.5 * jnp.tanh(0.5 * x) + 0.5


def _silu(x):
    return x * _sigmoid(x)


def _softplus(x):
    return jnp.maximum(x, 0.0) + jnp.log(1.0 + jnp.exp(-jnp.abs(x)))


def _rms_rows(x, w):
    ms = jnp.mean(x * x, axis=-1, keepdims=True)
    return x * lax.rsqrt(ms + EPS) * w


def _pair_block_ones():
    r = lax.broadcasted_iota(jnp.int32, (LANES, LANES), 0)
    c = lax.broadcasted_iota(jnp.int32, (LANES, LANES), 1)
    return jnp.where((r // 64) == (c // 64), 1.0, 0.0).astype(BF16)


def _adaln_kernel(c_ref, w_ref, b_ref, o_ref):
    c = c_ref[...]
    s = _silu(c).astype(BF16)
    o_ref[...] = _bdot(s, w_ref[...].astype(BF16)) + b_ref[...]


def _adaln(cvecs, w_ada, b_ada):
    tn = 1024
    n = w_ada.shape[1]
    return pl.pallas_call(
        _adaln_kernel,
        grid=(n // tn,),
        in_specs=[pl.BlockSpec((8, D_MODEL), lambda j: (0, 0)),
                  pl.BlockSpec((D_MODEL, tn), lambda j: (0, j)),
                  pl.BlockSpec((1, tn), lambda j: (0, j))],
        out_specs=pl.BlockSpec((8, tn), lambda j: (0, j)),
        out_shape=jax.ShapeDtypeStruct((8, n), F32),
        compiler_params=_params(1),
        name="adaln",
    )(cvecs, w_ada, b_ada.reshape(1, n))


def _inproj_kernel(*refs, latent, tiles_per_seq):
    it = iter(refs)
    x_ref, mod_ref, wpre_ref, w_ref, wlast_ref, cw_ref = (next(it), next(it), next(it), next(it), next(it),
                                                          next(it))
    if latent:
        xprev_ref, xnext_ref, cos_ref, sin_ref = next(it), next(it), next(it), next(it)
    qkv_ref, dqkv_ref, gz_ref, ba_ref = next(it), next(it), next(it), next(it)
    if not latent:
        kc_ref, vc_ref = next(it), next(it)

    mod = mod_ref[0]

    def modulated(x):
        return (_rms_rows(x, wpre_ref[...]) * (1.0 + mod[1:2]) + mod[0:1]).astype(BF16)

    hb = modulated(x_ref[...])

    w_dn = w_ref[:, COL_DQKV:COL_DQKV + DQKV_W]
    halo_zero = jnp.zeros((SUBLANES, DQKV_W), F32)
    if latent:
        t = pl.program_id(0) % tiles_per_seq
        hb_ext = jnp.concatenate([modulated(xprev_ref[...]), hb, modulated(xnext_ref[...])], axis=0)
        pe = _bdot(hb_ext, w_dn)
        pe = jnp.concatenate([jnp.where(t > 0, pe[:SUBLANES], halo_zero), pe[SUBLANES:SUBLANES + TM],
                              jnp.where(t < tiles_per_seq - 1, pe[SUBLANES + TM:], halo_zero)], axis=0)
    else:
        pe = jnp.concatenate([halo_zero, _bdot(hb, w_dn), halo_zero], axis=0)
    acc = None
    for j in sorted(range(CONV_W), key=lambda s: abs(s - CONV_W // 2)):
        lo = SUBLANES - CONV_W // 2 + j
        term = pe[lo:lo + TM] * cw_ref[j:j + 1, :]
        acc = term if acc is None else acc + term
    y = _silu(acc)

    p = _bdot(hb, w_ref[:, COL_QKV:COL_QKV + QKV_W])
    if latent:
        qk = p[:, :2 * NA_WIDTH]
        lane = lax.broadcasted_iota(jnp.int32, qk.shape, 1)
        first = (lane % 32) < 16
        partner = jnp.where(first, pltpu.roll(qk, 2 * NA_WIDTH - 16, 1), pltpu.roll(qk, 16, 1))
        qkv_ref[:, :2 * NA_WIDTH] = (qk * cos_ref[...] + partner * sin_ref[...]).astype(BF16)
        qkv_ref[:, 2 * NA_WIDTH:] = p[:, 2 * NA_WIDTH:].astype(BF16)
    else:
        qkv_ref[...] = p.astype(BF16)
        for h in range(NA_HEADS):
            kc_ref[0, 0, h] = p[:, NA_WIDTH + h * HEAD_DIM:NA_WIDTH + (h + 1) * HEAD_DIM]
            vc_ref[0, 0, h] = p[:, 2 * NA_WIDTH + h * HEAD_DIM:2 * NA_WIDTH + (h + 1) * HEAD_DIM]

    split = COL_TAIL + ((MAIN_COLS - COL_TAIL) // LANES // 2) * LANES
    tail = jnp.concatenate([_bdot(hb, w_ref[:, COL_TAIL:split]), _bdot(hb, w_ref[:, split:MAIN_COLS]),
                            _bdot(hb, wlast_ref[...])], axis=1)
    ba_ref[...] = tail[:, :LANES]
    gz_ref[...] = pltpu.roll(tail, TAIL_W - BA_W, 1)[:, :GZ_W].astype(BF16)

    bd_ones = _pair_block_ones()
    for s in range(DQKV_W // LANES):
        ys = y[:, s * LANES:(s + 1) * LANES]
        if s < 2 * DN_WIDTH // LANES:
            ys = ys * lax.rsqrt(_dot2_l(ys * ys, bd_ones) + EPS)
        dqkv_ref[:, s * LANES:(s + 1) * LANES] = ys.astype(BF16)


def _inproj(x2d, mod3, mod_row_of_tile, wpre, w_packed, conv_w8, seq, batch, rope_tabs=None):
    n_tok = x2d.shape[0]
    latent = rope_tabs is not None
    tiles_per_seq = seq // TM
    rows8 = TM // SUBLANES
    n_blk8 = n_tok // SUBLANES
    in_specs = [pl.BlockSpec((TM, D_MODEL), lambda i: (i, 0)),
                pl.BlockSpec((1, N_MOD, D_MODEL), lambda i: (mod_row_of_tile(i), 0, 0)),
                pl.BlockSpec((1, D_MODEL), lambda i: (0, 0)),
                pl.BlockSpec((D_MODEL, MAIN_COLS), lambda i: (0, 0)),
                pl.BlockSpec((D_MODEL, LANES), lambda i: (0, 0)),
                pl.BlockSpec((8, DQKV_W), lambda i: (0, 0))]
    w_main, w_last = w_packed
    args = [x2d, mod3, wpre, w_main, w_last, conv_w8]
    out_specs = [pl.BlockSpec((TM, QKV_W), lambda i: (i, 0)),
                 pl.BlockSpec((TM, DQKV_W), lambda i: (i, 0)),
                 pl.BlockSpec((TM, GZ_W), lambda i: (i, 0)),
                 pl.BlockSpec((TM, LANES), lambda i: (i, 0))]
    out_shape = [jax.ShapeDtypeStruct((n_tok, QKV_W), BF16),
                 jax.ShapeDtypeStruct((n_tok, DQKV_W), BF16),
                 jax.ShapeDtypeStruct((n_tok, GZ_W), BF16),
                 jax.ShapeDtypeStruct((n_tok, LANES), F32)]
    if latent:
        in_specs += [pl.BlockSpec((SUBLANES, D_MODEL), lambda i: (jnp.maximum(i * rows8 - 1, 0), 0)),
                     pl.BlockSpec((SUBLANES, D_MODEL), lambda i: (jnp.minimum((i + 1) * rows8, n_blk8 - 1), 0))]
        args += [x2d, x2d]
        for tab in rope_tabs:
            in_specs.append(pl.BlockSpec((TM, 2 * NA_WIDTH), lambda i: (i % tiles_per_seq, 0)))
            args.append(tab)
    else:
        assert seq == TM
        cache_spec = pl.BlockSpec((1, 1, NA_HEADS, seq, HEAD_DIM), lambda i: (i, 0, 0, 0, 0))
        cache_shape = jax.ShapeDtypeStruct((batch, 1, NA_HEADS, seq, HEAD_DIM), F32)
        out_specs += [cache_spec, cache_spec]
        out_shape += [cache_shape, cache_shape]
    return pl.pallas_call(
        functools.partial(_inproj_kernel, latent=latent, tiles_per_seq=tiles_per_seq),
        grid=(n_tok // TM,),
        in_specs=in_specs,
        out_specs=out_specs,
        out_shape=out_shape,
        compiler_params=_params(1),
        name="inproj_lat" if latent else "inproj_ctx",
    )(*args)


def _rope_tables(seq):
    half = HEAD_DIM // 2
    nf = half // 2
    pos = np.arange(seq)
    inv = (np.float32(ROPE_BASE) ** (-np.arange(nf, dtype=np.float32) / np.float32(nf))).astype(np.float32)
    ang_r = ((pos // GRID_W).astype(np.float32)[:, None] * inv).astype(np.float64)
    ang_c = ((pos % GRID_W).astype(np.float32)[:, None] * inv).astype(np.float64)

    def grp(ang):
        return (np.concatenate([np.cos(ang), np.cos(ang)], 1),
                np.concatenate([-np.sin(ang), np.sin(ang)], 1))

    cr, sr = grp(ang_r)
    cc, sc = grp(ang_c)
    cos_h = np.concatenate([cr, cc], 1)
    sin_h = np.concatenate([sr, sc], 1)
    reps = 2 * NA_HEADS
    return (jnp.asarray(np.tile(cos_h, (1, reps)), F32),
            jnp.asarray(np.tile(sin_h, (1, reps)), F32))


def _ctx_attn_kernel(q_ref, k_ref, v_ref, o_ref, *, seq, nb):
    lane = lax.broadcasted_iota(jnp.int32, (1, LANES), 1)
    scale = HEAD_DIM ** -0.5
    pairs = [(j, hp) for j in range(nb) for hp in range(N_PAIRS)]
    heads = [(j, hp, e) for j, hp in pairs for e in range(2)]
    rows = [slice(j * seq, (j + 1) * seq) for j in range(nb)]
    sls = [slice(hp * LANES, (hp + 1) * LANES) for hp in range(N_PAIRS)]
    q = {(j, hp): q_ref[rows[j], sls[hp]] for j, hp in pairs}
    k = {(j, hp): k_ref[rows[j], sls[hp]] for j, hp in pairs}
    v = {(j, hp): v_ref[rows[j], sls[hp]] for j, hp in pairs}
    m_e = (lane < 64, lane >= 64)
    zero = jnp.zeros((), BF16)
    s = {(j, hp, e): lax.dot_general(jnp.where(m_e[e], q[j, hp] * scale, zero), k[j, hp],
                                     (((1,), (1,)), ((), ())), preferred_element_type=F32) for j, hp, e in heads}
    p = {u: jnp.exp(s[u] - jnp.max(s[u], axis=-1, keepdims=True)) for u in heads}
    l = {u: jnp.sum(p[u], axis=-1, keepdims=True) for u in heads}
    o = {(j, hp, e): _bdot(p[j, hp, e].astype(BF16), jnp.where(m_e[e], v[j, hp], zero)) for j, hp, e in heads}
    for j, hp in pairs:
        o_ref[rows[j], sls[hp]] = (o[j, hp, 0] / l[j, hp, 0] + o[j, hp, 1] / l[j, hp, 1]).astype(BF16)


def _ctx_attn(qkv, batch, seq, nb):
    assert batch % nb == 0

    def col(cb):
        return pl.BlockSpec((nb * seq, NA_WIDTH), lambda b: (b, cb))
    return pl.pallas_call(
        functools.partial(_ctx_attn_kernel, seq=seq, nb=nb),
        grid=(batch // nb,),
        in_specs=[col(0), col(1), col(2)],
        out_specs=pl.BlockSpec((nb * seq, NA_WIDTH), lambda b: (b, 0)),
        out_shape=jax.ShapeDtypeStruct((batch * seq, NA_WIDTH), BF16),
        compiler_params=_params(1),
        name="ctx_attn",
    )(qkv, qkv, qkv)


Q_HALF = 512
K_WIN = 768


def _lat_attn_kernel(rpb_ref, q_ref, k_ref, v_ref, ck_ref, cv_ref, o_ref, bias_ref, *, rows):
    hp = pl.program_id(0)
    b = pl.program_id(1)
    half = pl.program_id(2)
    kr_n = min(WIN_R, rows)
    rows_half = Q_HALF // GRID_W
    krows_win = K_WIN // GRID_W
    n_dr = 2 * WIN_R - 1
    n_dc = 2 * WIN_C - 1

    @pl.when(jnp.logical_and(b == 0, half == 0))
    def _build_bias():
        qc = lax.broadcasted_iota(jnp.int32, (GRID_W, LANES), 0)
        ln = lax.broadcasted_iota(jnp.int32, (GRID_W, LANES), 1)
        kc = ln % GRID_W
        cs = jnp.clip(qc - WIN_C // 2, 0, GRID_W - WIN_C)
        valid = jnp.logical_and(kc >= cs, kc < cs + WIN_C)
        dc = jnp.clip(kc - qc + (WIN_C - 1), 0, n_dc - 1)
        dcv = jnp.where(valid, dc, -1)
        neg = jnp.full((GRID_W, LANES), NEG_INF, F32)
        for e in range(2):
            blocks = []
            for a in range(n_dr):
                blk = neg
                for j in range(n_dc):
                    blk = jnp.where(dcv == j, rpb_ref[((2 * hp + e) * n_dr + a) * n_dc + j], blk)
                blocks.append(blk)
            for hf in range(2):
                for rl in range(rows_half):
                    r = hf * rows_half + rl
                    rs = min(max(r - kr_n // 2, 0), rows - kr_n)
                    for kp in range(krows_win // 2):
                        parts = []
                        for kl in (2 * kp, 2 * kp + 1):
                            kr = kl + hf * (rows - krows_win)
                            parts.append(blocks[kr - r + WIN_R - 1] if rs <= kr < rs + kr_n else neg)
                        blk = jnp.where(ln < GRID_W, parts[0], parts[1])
                        bias_ref[e, hf, rl * GRID_W:(rl + 1) * GRID_W, kp * LANES:(kp + 1) * LANES] = blk

    lane = lax.broadcasted_iota(jnp.int32, (1, LANES), 1)
    scale = HEAD_DIM ** -0.5
    zero = jnp.zeros((), BF16)
    k0 = pl.multiple_of(half * ((rows - krows_win) * GRID_W), GRID_W)
    kw = k_ref[pl.ds(k0, K_WIN), :]
    vw = v_ref[pl.ds(k0, K_WIN), :]
    ck = ck_ref[0].astype(BF16)
    cv = cv_ref[0].astype(BF16)
    SUB = 256
    m_e = (lane < 64, lane >= 64)
    ve = [jnp.where(m, vw, zero) for m in m_e]
    cve = [jnp.where(m, cv, zero) for m in m_e]
    units = [(qs, e) for qs in range(Q_HALF // SUB) for e in range(2)]
    nt = (((1,), (1,)), ((), ()))
    qe = {(qs, e): jnp.where(m_e[e], q_ref[qs * SUB:(qs + 1) * SUB, :] * scale, zero) for qs, e in units}
    s_loc = {(qs, e): lax.dot_general(qe[qs, e], kw, nt, preferred_element_type=F32)
             + bias_ref[e, half, qs * SUB:(qs + 1) * SUB, :] for qs, e in units}
    s_ctx = {u: lax.dot_general(qe[u], ck, nt, preferred_element_type=F32) for u in units}
    m = {u: jnp.maximum(jnp.max(s_loc[u], axis=-1, keepdims=True), jnp.max(s_ctx[u], axis=-1, keepdims=True))
         for u in units}
    p_loc = {u: jnp.exp(s_loc[u] - m[u]) for u in units}
    p_ctx = {u: jnp.exp(s_ctx[u] - m[u]) for u in units}
    l = {u: jnp.sum(p_loc[u], axis=-1, keepdims=True) + jnp.sum(p_ctx[u], axis=-1, keepdims=True) for u in units}
    o = {(qs, e): (_bdot(p_loc[qs, e].astype(BF16), ve[e]) + _bdot(p_ctx[qs, e].astype(BF16), cve[e])) / l[qs, e]
         for qs, e in units}
    for qs in range(Q_HALF // SUB):
        o_ref[qs * SUB:(qs + 1) * SUB, :] = (o[qs, 0] + o[qs, 1]).astype(BF16)


def _lat_attn(qkv, ck_tm, cv_tm, rpb, batch, seq):
    rows = seq // GRID_W
    assert rows * GRID_W == seq and rows - K_WIN // GRID_W == 4 and Q_HALF * 2 == seq
    halves = seq // Q_HALF
    past = ck_tm.shape[1]
    kernel = functools.partial(_lat_attn_kernel, rows=rows)
    return pl.pallas_call(
        kernel,
        grid=(N_PAIRS, batch, halves),
        in_specs=[pl.BlockSpec(memory_space=pltpu.SMEM),
                  pl.BlockSpec((Q_HALF, LANES), lambda hp, b, hf: (b * halves + hf, hp)),
                  pl.BlockSpec((seq, LANES), lambda hp, b, hf: (b, N_PAIRS + hp)),
                  pl.BlockSpec((seq, LANES), lambda hp, b, hf: (b, 2 * N_PAIRS + hp)),
                  pl.BlockSpec((1, past, LANES), lambda hp, b, hf: (b, 0, hp)),
                  pl.BlockSpec((1, past, LANES), lambda hp, b, hf: (b, 0, hp))],
        out_specs=pl.BlockSpec((Q_HALF, LANES), lambda hp, b, hf: (b * halves + hf, hp)),
        out_shape=jax.ShapeDtypeStruct((batch * seq, NA_WIDTH), BF16),
        scratch_shapes=[pltpu.VMEM((2, 2, Q_HALF, K_WIN), F32)],
        compiler_params=_params(3),
        name="lat_attn",
    )(rpb.reshape(-1), qkv, qkv, qkv, ck_tm, cv_tm)


def _dn_kernel(*refs, seq, nb, hps, a_chunks, has_s0, emit_state):
    it = iter(refs)
    q_ref, k_ref, v_ref, ba_ref, cst_ref = next(it), next(it), next(it), next(it), next(it)
    s0_ref = next(it) if has_s0 else None
    o_ref = next(it)
    st_ref = next(it) if emit_state else None
    (beta_s, g_s, u_s, wq_s, lb_s, el_s, odir_s) = it

    hg = pl.program_id(1)
    wl = hps * LANES
    n = seq // CHUNK
    nt = nb * n
    hp_sl = [slice(h * LANES, (h + 1) * LANES) for h in range(hps)]

    def lanes_cat(parts):
        return parts[0] if len(parts) == 1 else jnp.concatenate(parts, axis=1)

    r128 = lax.broadcasted_iota(jnp.int32, (LANES, LANES), 0)
    c128 = lax.broadcasted_iota(jnp.int32, (LANES, LANES), 1)
    bd_mask = (r128 // 64) == (c128 // 64)
    bd_ones = _pair_block_ones()

    ba = ba_ref[...]
    beta_all = _sigmoid(ba)
    g_all = -jnp.exp(cst_ref[0:1, :]) * _softplus(ba + cst_ref[1:2, :])
    src = lax.broadcasted_iota(jnp.int32, (LANES, wl), 0)
    dst_head = 2 * hps * hg + lax.broadcasted_iota(jnp.int32, (LANES, wl), 1) // 64
    for d in range(2):
        sel_b = jnp.where(src == d * DN_HEADS + dst_head, 1.0, 0.0).astype(BF16)
        sel_g = jnp.where(src == (2 + d) * DN_HEADS + dst_head, 1.0, 0.0).astype(BF16)
        beta_s[d] = _dot2_l(beta_all, sel_b)
        g_s[d] = _dot3_l(g_all, sel_g)

    ri = lax.broadcasted_iota(jnp.int32, (CHUNK, LANES), 0)
    lj = lax.broadcasted_iota(jnp.int32, (CHUNK, LANES), 1) % 64
    lo_half = lax.broadcasted_iota(jnp.int32, (CHUNK, LANES), 1) < 64
    lo_half2 = (lax.broadcasted_iota(jnp.int32, (CHUNK, 2 * LANES), 1) % LANES) < 64
    diag = ri == lj
    strict = (ri > lj, ri < lj)
    riw = lax.broadcasted_iota(jnp.int32, (CHUNK, wl), 0)
    ljw = lax.broadcasted_iota(jnp.int32, (CHUNK, wl), 1) % 64
    diag_w = riw == ljw
    incl_w = (riw >= ljw, riw <= ljw)
    r64 = lax.broadcasted_iota(jnp.int32, (CHUNK, CHUNK), 0)
    c64 = lax.broadcasted_iota(jnp.int32, (CHUNK, CHUNK), 1)
    tri = (jnp.where(c64 <= r64, 1.0, 0.0).astype(BF16), jnp.where(c64 >= r64, 1.0, 0.0).astype(BF16))

    off_blk = ([], [])
    for lvl in range(6):
        s = 1 << lvl
        same2 = (ri // (2 * s)) == (lj // (2 * s))
        off_blk[0].append(jnp.logical_and(same2, (ri // s) == (lj // s) + 1))
        off_blk[1].append(jnp.logical_and(same2, (lj // s) == (ri // s) + 1))

    def stack_heads(x, lo_mask):
        return jnp.concatenate([jnp.where(lo_mask, x, 0.0), jnp.where(lo_mask, 0.0, x)], axis=0).astype(BF16)

    def chunk_rows(c):
        return pl.ds(c * CHUNK if isinstance(c, int) else pl.multiple_of(c * CHUNK, CHUNK), CHUNK)

    def chunk_load(c):
        rows = chunk_rows(c)
        return (q_ref[rows, :].astype(F32), k_ref[rows, :].astype(F32), v_ref[rows, :].astype(F32),
                [beta_s[d, rows, :] for d in range(2)], [g_s[d, rows, :] for d in range(2)])

    def group_a(loaded):
        nc = len(loaded)
        cd = [(i, d) for i in range(nc) for d in range(2)]
        units = [(i, h, d) for i in range(nc) for h in range(hps) for d in range(2)]
        qcs = [ld[0] * (DN_DK ** -0.5) for ld in loaded]
        kcs = [ld[1] for ld in loaded]
        vcs = [ld[2] for ld in loaded]
        kt_bd = {}
        for i in range(nc):
            for h in range(hps):
                kc = kcs[i][:, hp_sl[h]]
                ktt = jnp.transpose(jnp.concatenate([kc, kc], axis=0))
                kt_bd[i, h] = jnp.where(bd_mask, ktt, 0.0)
        gsp = {(i, d): _split3(loaded[i][4][d]) for i, d in cd}
        gc = {}
        for part in range(3):
            for i, d in cd:
                t = _bdot(tri[d], gsp[i, d][part])
                gc[i, d] = t if part == 0 else gc[i, d] + t
        rrow = {u: jnp.sum(jnp.where(diag_w, gc[u], 0.0), axis=0, keepdims=True) for u in cd}
        dec = {u: jnp.exp(jnp.minimum(gc[u] - rrow[u], 0.0)) for u in cd}
        eg = {u: jnp.exp(gc[u]) for u in cd}
        kbeta = {(i, d): kcs[i] * loaded[i][3][d] for i, d in cd}
        vbeta = {(i, d): vcs[i] * loaded[i][3][d] for i, d in cd}
        kbeg = {u: kbeta[u] * eg[u] for u in cd}
        m1 = {(i, h): _bdot(jnp.concatenate([qcs[i][:, hp_sl[h]], kbeta[i, 0][:, hp_sl[h]],
                                             kbeta[i, 1][:, hp_sl[h]]], axis=0).astype(BF16),
                            kt_bd[i, h].astype(BF16))
              for i in range(nc) for h in range(hps)}
        amat = {(i, h, d): jnp.where(strict[d], m1[i, h][CHUNK * (1 + d):CHUNK * (2 + d)] * dec[i, d][:, hp_sl[h]], 0.0)
                for i, h, d in units}
        o_st = {(i, h, d): stack_heads(jnp.where(off_blk[d][0], amat[i, h, d], 0.0), lo_half) for i, h, d in units}
        tmat = {(i, h, d): jnp.where(diag, 1.0, 0.0) - jnp.where(off_blk[d][0], amat[i, h, d], 0.0) for i, h, d in units}
        prod = {u: _bdot(amat[u].astype(BF16), o_st[u]) for u in units}
        wmat = {u: amat[u] - prod[u] for u in units}
        for lvl in range(1, 5):
            o_st = {(i, h, d): stack_heads(jnp.where(off_blk[d][lvl], wmat[i, h, d], 0.0), lo_half) for i, h, d in units}
            prod = {u: _bdot(jnp.concatenate([tmat[u], wmat[u]], axis=0).astype(BF16), o_st[u]) for u in units}
            tmat = {u: tmat[u] - prod[u][0:CHUNK] for u in units}
            wmat = {u: wmat[u] - prod[u][CHUNK:] for u in units}
        o_st = {(i, h, d): stack_heads(jnp.where(off_blk[d][5], wmat[i, h, d], 0.0), lo_half) for i, h, d in units}
        prod = {u: _bdot(tmat[u].astype(BF16), o_st[u]) for u in units}
        tmat = {u: tmat[u] - prod[u] for u in units}
        x = {}
        for i, h, d in units:
            rhs = jnp.concatenate([vbeta[i, d][:, hp_sl[h]], kbeg[i, d][:, hp_sl[h]]], axis=1)
            x[i, h, d] = _bdot(tmat[i, h, d].astype(BF16), stack_heads(rhs, lo_half2))
        results = []
        for i in range(nc):
            out = []
            for d in range(2):
                u = lanes_cat([x[i, h, d][:, :LANES] for h in range(hps)])
                w = lanes_cat([x[i, h, d][:, LANES:] for h in range(hps)])
                qk = jnp.where(incl_w[d], lanes_cat([m1[i, h][0:CHUNK] for h in range(hps)]) * dec[i, d], 0.0)
                qg = qcs[i] * eg[i, d]
                glast = gc[i, d][CHUNK - 1:CHUNK, :] if d == 0 else gc[i, d][0:1, :]
                kmul = jnp.exp(glast - rrow[i, d])
                kd_bd = lanes_cat([kt_bd[i, h] * kmul[:, hp_sl[h]] for h in range(hps)])
                out.append((u,
                            jnp.concatenate([w, qg], axis=0).astype(BF16),
                            jnp.concatenate([qk, kd_bd], axis=0).astype(BF16),
                            jnp.broadcast_to(jnp.exp(glast), (8, wl))))
            results.append(out)
        return results

    group = min(a_chunks, nt)

    def phase_a(gi, carry):
        cs = [gi * group + j for j in range(group)]
        loaded = [chunk_load(c) for c in cs]
        results = group_a(loaded)
        for c, res in zip(cs, results):
            for d in range(2):
                u_s[d, c], wq_s[d, c], lb_s[d, c], el_s[d, c] = res[d]
        return carry

    if nt // group <= 2:
        for gi in range(nt // group):
            phase_a(gi, 0)
    else:
        lax.fori_loop(0, nt // group, phase_a, 0)

    dh = [(j, d, h) for j in range(nb) for d in range(2) for h in range(hps)]
    if has_s0:
        zero = jnp.zeros((DN_DK, DN_DV), F32)

        def bd_state(j, d, h):
            top = jnp.concatenate([s0_ref[j, 0, d, 2 * h], zero], axis=1)
            bot = jnp.concatenate([zero, s0_ref[j, 0, d, 2 * h + 1]], axis=1)
            return jnp.concatenate([top, bot], axis=0)
        s_init = tuple(bd_state(j, d, h) for j, d, h in dh)
    else:
        s_init = tuple(jnp.zeros((LANES, LANES), F32) for _ in dh)

    def phase_b(s, states):
        jd = [(j, d) for j in range(nb) for d in range(2)]
        cs = {(j, d): j * n + (s if d == 0 else n - 1 - s) for j, d in jd}
        ld = {u: (wq_s[u[1], cs[u]], u_s[u[1], cs[u]], lb_s[u[1], cs[u]], el_s[u[1], cs[u], 0:1, :]) for u in jd}
        st = dict(zip(dh, states))
        ws = {(j, d, h): _bdot(ld[j, d][0][:, hp_sl[h]], st[j, d, h].astype(BF16)) for j, d, h in dh}
        vn = {(j, d, h): ld[j, d][1][:, hp_sl[h]] - ws[j, d, h][0:CHUNK] for j, d, h in dh}
        r2 = {(j, d, h): _bdot(ld[j, d][2][:, hp_sl[h]], stack_heads(vn[j, d, h], lo_half)) for j, d, h in dh}
        new_states = tuple(st[j, d, h] * ld[j, d][3][:, hp_sl[h]] + r2[j, d, h][CHUNK:] for j, d, h in dh)
        for j, d in jd:
            odir_s[d, chunk_rows(cs[j, d]), :] = lanes_cat([ws[j, d, h][CHUNK:] + r2[j, d, h][0:CHUNK]
                                                            for h in range(hps)])
        return new_states

    s_fin = dict(zip(dh, lax.fori_loop(0, n, phase_b, s_init)))

    for h in range(hps):
        o = odir_s[0, :, hp_sl[h]] + odir_s[1, :, hp_sl[h]]
        msq = _dot2_l(o * o, bd_ones) * (1.0 / DN_DV)
        o_ref[:, hp_sl[h]] = (o * lax.rsqrt(msq + EPS) * cst_ref[2:3, :]).astype(BF16)

    if emit_state:
        for j, d, h in dh:
            st_ref[j, 0, d, 2 * h] = s_fin[j, d, h][0:DN_DK, 0:DN_DV]
            st_ref[j, 0, d, 2 * h + 1] = s_fin[j, d, h][DN_DK:, DN_DV:]


def _deltanet(dqkv, ba, cst, batch, seq, nb, hps, a_chunks, s0=None, emit_state=False):
    assert batch % nb == 0
    rows = nb * seq
    nt = rows // CHUNK
    wl = hps * LANES
    groups = N_PAIRS // hps
    has_s0 = s0 is not None

    def col(cbase):
        return pl.BlockSpec((rows, wl), lambda b, hg: (b, cbase // wl + hg))

    st_spec = pl.BlockSpec((nb, 1, 2, 2 * hps, DN_DK, DN_DV), lambda b, hg: (b, 0, 0, hg, 0, 0))
    in_specs = [col(0), col(DN_WIDTH), col(2 * DN_WIDTH),
                pl.BlockSpec((rows, LANES), lambda b, hg: (b, 0)),
                pl.BlockSpec((8, LANES), lambda b, hg: (0, 0))]
    args = [dqkv, dqkv, dqkv, ba, cst]
    if has_s0:
        in_specs.append(st_spec)
        args.append(s0)
    out_specs = [pl.BlockSpec((rows, wl), lambda b, hg: (b, hg))]
    out_shape = [jax.ShapeDtypeStruct((batch * seq, DN_HEADS * DN_DV), BF16)]
    if emit_state:
        out_specs.append(st_spec)
        out_shape.append(jax.ShapeDtypeStruct((batch, 1, 2, DN_HEADS, DN_DK, DN_DV), F32))
    scratch = [pltpu.VMEM((2, rows, wl), F32), pltpu.VMEM((2, rows, wl), F32),
               pltpu.VMEM((2, nt, CHUNK, wl), F32),
               pltpu.VMEM((2, nt, 2 * CHUNK, wl), BF16),
               pltpu.VMEM((2, nt, 3 * CHUNK, wl), BF16),
               pltpu.VMEM((2, nt, 8, wl), F32),
               pltpu.VMEM((2, rows, wl), F32)]
    res = pl.pallas_call(
        functools.partial(_dn_kernel, seq=seq, nb=nb, hps=hps, a_chunks=a_chunks, has_s0=has_s0,
                          emit_state=emit_state),
        grid=(batch // nb, groups),
        in_specs=in_specs,
        out_specs=out_specs,
        out_shape=out_shape,
        scratch_shapes=scratch,
        compiler_params=_params(2),
        name="deltanet_lat" if has_s0 else "deltanet_ctx",
    )(*args)
    return res


TM_F = 2 * TM


def _merge_ffn_kernel(x_ref, ona_ref, odn_ref, gz_ref, mod_ref, wpost1_ref, wpre2_ref, wpost2_ref,
                      wao_ref, wdo_ref, wout_ref, w1_ref, w2_ref, o_ref):
    mod = mod_ref[0]
    subs = [slice(s * TM, (s + 1) * TM) for s in range(TM_F // TM)]
    z_sl, gna_sl, gdn_sl = (slice(0, DN_WIDTH), slice(DN_WIDTH, DN_WIDTH + D_MODEL),
                            slice(DN_WIDTH + D_MODEL, GZ_W))
    odn = [odn_ref[sl, :].astype(F32) * _silu(gz_ref[sl, z_sl].astype(F32)) for sl in subs]
    br_na = [_bdot(ona_ref[sl, :], wao_ref[...]) for sl in subs]
    br_dn = [_bdot(t.astype(BF16), wdo_ref[...]) for t in odn]
    m = [_sigmoid(gz_ref[sl, gna_sl].astype(F32)) * a + _sigmoid(gz_ref[sl, gdn_sl].astype(F32)) * b
         for sl, a, b in zip(subs, br_na, br_dn)]
    y = [_bdot(t.astype(BF16), wout_ref[...]) for t in m]
    x1 = [x_ref[sl, :] + mod[2:3] * _rms_rows(t, wpost1_ref[...]) for sl, t in zip(subs, y)]
    h = [_rms_rows(t, wpre2_ref[...]) * (1.0 + mod[4:5]) + mod[3:4] for t in x1]
    f1 = [_bdot(t.astype(BF16), w1_ref[...]) for t in h]
    r = [jnp.maximum(t, 0.0) for t in f1]
    f = [_bdot((t * t).astype(BF16), w2_ref[...]) for t in r]
    for sl, a, b in zip(subs, x1, f):
        o_ref[sl, :] = a + mod[5:6] * _rms_rows(b, wpost2_ref[...])


def _merge_ffn(x2d, o_na, o_dn, gz, mod3, mod_row_of_tile, wpost1, wpre2, wpost2, w_ao, w_do, w_out, w1, w2):
    n_tok = x2d.shape[0]
    row = lambda i: (i, 0)
    const = lambda i: (0, 0)

    def resident(shape):
        return pl.BlockSpec(shape, const, pipeline_mode=pl.Buffered(1))

    return pl.pallas_call(
        _merge_ffn_kernel,
        grid=(n_tok // TM_F,),
        in_specs=[pl.BlockSpec((TM_F, D_MODEL), row),
                  pl.BlockSpec((TM_F, NA_WIDTH), row),
                  pl.BlockSpec((TM_F, DN_WIDTH), row),
                  pl.BlockSpec((TM_F, GZ_W), row),
                  pl.BlockSpec((1, N_MOD, D_MODEL), lambda i: (mod_row_of_tile(i), 0, 0)),
                  pl.BlockSpec((1, D_MODEL), const),
                  pl.BlockSpec((1, D_MODEL), const),
                  pl.BlockSpec((1, D_MODEL), const),
                  resident((NA_WIDTH, D_MODEL)),
                  resident((DN_WIDTH, D_MODEL)),
                  resident((D_MODEL, D_MODEL)),
                  resident((D_MODEL, D_FF)),
                  resident((D_FF, D_MODEL))],
        out_specs=pl.BlockSpec((TM_F, D_MODEL), row),
        out_shape=jax.ShapeDtypeStruct((n_tok, D_MODEL), F32),
        compiler_params=_params(1),
        name="merge_ffn",
    )(x2d, o_na, o_dn, gz, mod3, wpost1, wpre2, wpost2, w_ao, w_do, w_out, w1, w2)


def _pack_w_in(w):
    assert w.shape[1] == IN_COLS
    wb = w.astype(BF16)
    return wb, jnp.pad(wb[:, MAIN_COLS:], ((0, 0), (0, LANES - (IN_COLS - MAIN_COLS))))


def kernel(x_prompt, x_sample, c, cache_na_k, cache_na_v, state_delta, c_ctx, w_ada, b_ada, norm_pre1, norm_post1, norm_pre2, norm_post2, w_in, conv_w, a_log, dt_bias, dn_norm, na_rpb, w_ao, w_do, w_out, w_ff1, w_ff2):
    batch, seq, _ = x_prompt.shape
    dec_batch, dec_seq, _ = x_sample.shape
    depth = w_in.shape[0]
    assert depth == 1 and seq == TM and dec_seq % TM_F == 0 and (batch * seq) % TM_F == 0 and dec_batch < 8

    xp = x_prompt.reshape(batch * seq, D_MODEL)
    xs = x_sample.reshape(dec_batch * dec_seq, D_MODEL)
    l = 0

    cvecs = jnp.concatenate([c, c_ctx[None], jnp.zeros((8 - dec_batch - 1, D_MODEL), F32)], axis=0)
    mod3 = _adaln(cvecs, w_ada[l], b_ada[l]).reshape(8, N_MOD, D_MODEL)
    ctx_row = lambda i: dec_batch
    tiles_per_lat = dec_seq // TM
    lat_row = lambda i: i // tiles_per_lat

    w_in_p = _pack_w_in(w_in[l])
    w_ao_b = w_ao[l].astype(BF16)
    w_do_b = w_do[l].astype(BF16)
    w_out_b = w_out[l].astype(BF16)
    w1_b = w_ff1[l].astype(BF16)
    w2_b = w_ff2[l].astype(BF16)
    wpre1 = norm_pre1[l].reshape(1, D_MODEL)
    wpost1 = norm_post1[l].reshape(1, D_MODEL)
    wpre2 = norm_pre2[l].reshape(1, D_MODEL)
    wpost2 = norm_post2[l].reshape(1, D_MODEL)

    conv_w8 = jnp.pad(conv_w[l], ((0, 8 - CONV_W), (0, 0)))
    gate_rows = jnp.pad(jnp.stack([a_log[l].reshape(-1), dt_bias[l].reshape(-1)]),
                        ((0, 0), (2 * DN_HEADS, LANES - 4 * DN_HEADS)))
    cst = jnp.concatenate([gate_rows, jnp.tile(dn_norm[l], 2)[None], jnp.zeros((5, LANES), F32)], axis=0)

    qkv_c, dqkv_c, gz_c, ba_c, new_k, new_v = _inproj(xp, mod3, ctx_row, wpre1, w_in_p, conv_w8, seq, batch)
    o_na_c = _ctx_attn(qkv_c, batch, seq, nb=2)
    o_dn_c, new_s = _deltanet(dqkv_c, ba_c, cst, batch, seq, nb=2, hps=4, a_chunks=4, emit_state=True)
    y_c = _merge_ffn(xp, o_na_c, o_dn_c, gz_c, mod3, ctx_row, wpost1, wpre2, wpost2,
                     w_ao_b, w_do_b, w_out_b, w1_b, w2_b)

    qkv_l, dqkv_l, gz_l, ba_l = _inproj(xs, mod3, lat_row, wpre1, w_in_p, conv_w8, dec_seq, dec_batch,
                                        rope_tabs=_rope_tables(dec_seq))
    past = cache_na_k.shape[3]
    ck_tm = jnp.transpose(cache_na_k[:, l], (0, 2, 1, 3)).reshape(dec_batch, past, NA_WIDTH)
    cv_tm = jnp.transpose(cache_na_v[:, l], (0, 2, 1, 3)).reshape(dec_batch, past, NA_WIDTH)
    o_na_l = _lat_attn(qkv_l, ck_tm, cv_tm, na_rpb[l], dec_batch, dec_seq)
    (o_dn_l,) = _deltanet(dqkv_l, ba_l, cst, dec_batch, dec_seq, nb=1, hps=4, a_chunks=2, s0=state_delta)
    y_l = _merge_ffn(xs, o_na_l, o_dn_l, gz_l, mod3, lambda i: i // (dec_seq // TM_F), wpost1, wpre2, wpost2,
                     w_ao_b, w_do_b, w_out_b, w1_b, w2_b)

    return (y_c.reshape(batch, seq, D_MODEL), y_l.reshape(dec_batch, dec_seq, D_MODEL),
            new_k, new_v, new_s)
```

```python
import functools

import numpy as np
import jax
import jax.numpy as jnp
from jax import lax
from jax.experimental import pallas as pl
from jax.experimental.pallas import tpu as pltpu

F32 = jnp.float32
BF16 = jnp.bfloat16

D_MODEL = 1024
N_MOD = 6
GRID_W = 64
NA_HEADS = 8
HEAD_DIM = 64
NA_WIDTH = NA_HEADS * HEAD_DIM
WIN_R = 8
WIN_C = 16
ROPE_BASE = 10000.0
DN_HEADS = 8
DN_DK = 64
DN_DV = 64
DN_WIDTH = DN_HEADS * DN_DK
CONV_W = 5
CHUNK = 64
D_FF = 4 * D_MODEL
EPS = 1e-6
NEG_INF = -1e30

LANES = 128
SUBLANES = 8
N_PAIRS = NA_HEADS // 2
TM = 256
VMEM_LIMIT = 56 * 1024 * 1024

QKV_W = 3 * NA_WIDTH
DQKV_W = 3 * DN_WIDTH
BA_W = 4 * DN_HEADS
GZ_W = DN_WIDTH + 2 * D_MODEL
COL_QKV, COL_DQKV, COL_TAIL = 0, QKV_W, QKV_W + DQKV_W
IN_COLS = COL_TAIL + BA_W + GZ_W
MAIN_COLS = (IN_COLS // LANES) * LANES
TAIL_W = MAIN_COLS - COL_TAIL + LANES


def _params(n_grid):
    return pltpu.CompilerParams(dimension_semantics=("arbitrary",) * n_grid,
                                vmem_limit_bytes=VMEM_LIMIT)


def _bdot(a, b):
    return jnp.dot(a, b, preferred_element_type=F32)


def _split3(a):
    hi = a.astype(BF16)
    r1 = a - hi.astype(F32)
    mid = r1.astype(BF16)
    lo = (r1 - mid.astype(F32)).astype(BF16)
    return hi, mid, lo


def _dot2_l(a, b_exact):
    hi = a.astype(BF16)
    lo = (a - hi.astype(F32)).astype(BF16)
    return _bdot(jnp.concatenate([hi, lo], axis=1), jnp.concatenate([b_exact, b_exact], axis=0))


def _sigmoid(x):
    return 0.5 * jnp.tanh(0.5 * x) + 0.5


def _silu(x):
    return x * _sigmoid(x)


def _softplus(x):
    return jnp.maximum(x, 0.0) + jnp.log(1.0 + jnp.exp(-jnp.abs(x)))


def _rms_rows(x, w):
    ms = jnp.mean(x * x, axis=-1, keepdims=True)
    return x * lax.rsqrt(ms + EPS) * w


def _pair_block_ones():
    r = lax.broadcasted_iota(jnp.int32, (LANES, LANES), 0)
    c = lax.broadcasted_iota(jnp.int32, (LANES, LANES), 1)
    return jnp.where((r // 64) == (c // 64), 1.0, 0.0).astype(BF16)


def _adaln_kernel(c_ref, w_ref, b_ref, o_ref):
    c = c_ref[...]
    s = _silu(c).astype(BF16)
    o_ref[...] = _bdot(s, w_ref[...].astype(BF16)) + b_ref[...]


def _adaln(cvecs, w_ada, b_ada):
    tn = 1024
    n = w_ada.shape[1]
    return pl.pallas_call(
        _adaln_kernel,
        grid=(n // tn,),
        in_specs=[pl.BlockSpec((8, D_MODEL), lambda j: (0, 0)),
                  pl.BlockSpec((D_MODEL, tn), lambda j: (0, j)),
                  pl.BlockSpec((1, tn), lambda j: (0, j))],
        out_specs=pl.BlockSpec((8, tn), lambda j: (0, j)),
        out_shape=jax.ShapeDtypeStruct((8, n), F32),
        compiler_params=_params(1),
        name="adaln",
    )(cvecs, w_ada, b_ada.reshape(1, n))


def _inproj_kernel(*refs, latent, tiles_per_seq):
    it = iter(refs)
    x_ref, mod_ref, wpre_ref, w_ref, wlast_ref, cw_ref = (next(it), next(it), next(it), next(it), next(it),
                                                          next(it))
    if latent:
        xprev_ref, xnext_ref, cos_ref, sin_ref = next(it), next(it), next(it), next(it)
    qkv_ref, dqkv_ref, gz_ref, ba_ref = next(it), next(it), next(it), next(it)
    if not latent:
        kc_ref, vc_ref = next(it), next(it)

    mod = mod_ref[0]

    def modulated(x):
        return (_rms_rows(x, wpre_ref[...]) * (1.0 + mod[1:2]) + mod[0:1]).astype(BF16)

    hb = modulated(x_ref[...])

    w_dn = w_ref[:, COL_DQKV:COL_DQKV + DQKV_W]
    halo_zero = jnp.zeros((SUBLANES, DQKV_W), F32)
    if latent:
        t = pl.program_id(0) % tiles_per_seq
        hb_ext = jnp.concatenate([modulated(xprev_ref[...]), hb, modulated(xnext_ref[...])], axis=0)
        pe = _bdot(hb_ext, w_dn)
        pe = jnp.concatenate([jnp.where(t > 0, pe[:SUBLANES], halo_zero), pe[SUBLANES:SUBLANES + TM],
                              jnp.where(t < tiles_per_seq - 1, pe[SUBLANES + TM:], halo_zero)], axis=0)
    else:
        pe = jnp.concatenate([halo_zero, _bdot(hb, w_dn), halo_zero], axis=0)
    acc = None
    for j in sorted(range(CONV_W), key=lambda s: abs(s - CONV_W // 2)):
        lo = SUBLANES - CONV_W // 2 + j
        term = pe[lo:lo + TM] * cw_ref[j:j + 1, :]
        acc = term if acc is None else acc + term
    y = _silu(acc)

    p = _bdot(hb, w_ref[:, COL_QKV:COL_QKV + QKV_W])
    if latent:
        qk = p[:, :2 * NA_WIDTH]
        lane = lax.broadcasted_iota(jnp.int32, qk.shape, 1)
        first = (lane % 32) < 16
        partner = jnp.where(first, pltpu.roll(qk, 2 * NA_WIDTH - 16, 1), pltpu.roll(qk, 16, 1))
        qkv_ref[:, :2 * NA_WIDTH] = (qk * cos_ref[...] + partner * sin_ref[...]).astype(BF16)
        qkv_ref[:, 2 * NA_WIDTH:] = p[:, 2 * NA_WIDTH:].astype(BF16)
    else:
        qkv_ref[...] = p.astype(BF16)
        for h in range(NA_HEADS):
            kc_ref[0, 0, h] = p[:, NA_WIDTH + h * HEAD_DIM:NA_WIDTH + (h + 1) * HEAD_DIM]
            vc_ref[0, 0, h] = p[:, 2 * NA_WIDTH + h * HEAD_DIM:2 * NA_WIDTH + (h + 1) * HEAD_DIM]

    split = COL_TAIL + ((MAIN_COLS - COL_TAIL) // LANES // 2) * LANES
    tail = jnp.concatenate([_bdot(hb, w_ref[:, COL_TAIL:split]), _bdot(hb, w_ref[:, split:MAIN_COLS]),
                            _bdot(hb, wlast_ref[...])], axis=1)
    ba_ref[...] = tail[:, :LANES]
    gz_ref[...] = pltpu.roll(tail, TAIL_W - BA_W, 1)[:, :GZ_W].astype(BF16)

    bd_ones = _pair_block_ones()
    for s in range(DQKV_W // LANES):
        ys = y[:, s * LANES:(s + 1) * LANES]
        if s < 2 * DN_WIDTH // LANES:
            ys = ys * lax.rsqrt(_dot2_l(ys * ys, bd_ones) + EPS)
        dqkv_ref[:, s * LANES:(s + 1) * LANES] = ys.astype(BF16)


def _inproj(x2d, mod3, mod_row_of_tile, wpre, w_packed, conv_w8, seq, batch, rope_tabs=None):
    n_tok = x2d.shape[0]
    latent = rope_tabs is not None
    tiles_per_seq = seq // TM
    rows8 = TM // SUBLANES
    n_blk8 = n_tok // SUBLANES
    in_specs = [pl.BlockSpec((TM, D_MODEL), lambda i: (i, 0)),
                pl.BlockSpec((1, N_MOD, D_MODEL), lambda i: (mod_row_of_tile(i), 0, 0)),
                pl.BlockSpec((1, D_MODEL), lambda i: (0, 0)),
                pl.BlockSpec((D_MODEL, MAIN_COLS), lambda i: (0, 0)),
                pl.BlockSpec((D_MODEL, LANES), lambda i: (0, 0)),
                pl.BlockSpec((8, DQKV_W), lambda i: (0, 0))]
    w_main, w_last = w_packed
    args = [x2d, mod3, wpre, w_main, w_last, conv_w8]
    out_specs = [pl.BlockSpec((TM, QKV_W), lambda i: (i, 0)),
                 pl.BlockSpec((TM, DQKV_W), lambda i: (i, 0)),
                 pl.BlockSpec((TM, GZ_W), lambda i: (i, 0)),
                 pl.BlockSpec((TM, LANES), lambda i: (i, 0))]
    out_shape = [jax.ShapeDtypeStruct((n_tok, QKV_W), BF16),
                 jax.ShapeDtypeStruct((n_tok, DQKV_W), BF16),
                 jax.ShapeDtypeStruct((n_tok, GZ_W), BF16),
                 jax.ShapeDtypeStruct((n_tok, LANES), F32)]
    if latent:
        in_specs += [pl.BlockSpec((SUBLANES, D_MODEL), lambda i: (jnp.maximum(i * rows8 - 1, 0), 0)),
                     pl.BlockSpec((SUBLANES, D_MODEL), lambda i: (jnp.minimum((i + 1) * rows8, n_blk8 - 1), 0))]
        args += [x2d, x2d]
        for tab in rope_tabs:
            in_specs.append(pl.BlockSpec((TM, 2 * NA_WIDTH), lambda i: (i % tiles_per_seq, 0)))
            args.append(tab)
    else:
        assert seq == TM
        cache_spec = pl.BlockSpec((1, 1, NA_HEADS, seq, HEAD_DIM), lambda i: (i, 0, 0, 0, 0))
        cache_shape = jax.ShapeDtypeStruct((batch, 1, NA_HEADS, seq, HEAD_DIM), F32)
        out_specs += [cache_spec, cache_spec]
        out_shape += [cache_shape, cache_shape]
    return pl.pallas_call(
        functools.partial(_inproj_kernel, latent=latent, tiles_per_seq=tiles_per_seq),
        grid=(n_tok // TM,),
        in_specs=in_specs,
        out_specs=out_specs,
        out_shape=out_shape,
        compiler_params=_params(1),
        name="inproj_lat" if latent else "inproj_ctx",
    )(*args)


def _rope_tables(seq):
    half = HEAD_DIM // 2
    nf = half // 2
    pos = np.arange(seq)
    inv = (np.float32(ROPE_BASE) ** (-np.arange(nf, dtype=np.float32) / np.float32(nf))).astype(np.float32)
    ang_r = ((pos // GRID_W).astype(np.float32)[:, None] * inv).astype(np.float64)
    ang_c = ((pos % GRID_W).astype(np.float32)[:, None] * inv).astype(np.float64)

    def grp(ang):
        return (np.concatenate([np.cos(ang), np.cos(ang)], 1),
                np.concatenate([-np.sin(ang), np.sin(ang)], 1))

    cr, sr = grp(ang_r)
    cc, sc = grp(ang_c)
    cos_h = np.concatenate([cr, cc], 1)
    sin_h = np.concatenate([sr, sc], 1)
    reps = 2 * NA_HEADS
    return (jnp.asarray(np.tile(cos_h, (1, reps)), F32),
            jnp.asarray(np.tile(sin_h, (1, reps)), F32))


def _ctx_attn_kernel(q_ref, k_ref, v_ref, o_ref, *, seq, nb):
    lane = lax.broadcasted_iota(jnp.int32, (1, LANES), 1)
    scale = HEAD_DIM ** -0.5
    pairs = [(j, hp) for j in range(nb) for hp in range(N_PAIRS)]
    heads = [(j, hp, e) for j, hp in pairs for e in range(2)]
    rows = [slice(j * seq, (j + 1) * seq) for j in range(nb)]
    sls = [slice(hp * LANES, (hp + 1) * LANES) for hp in range(N_PAIRS)]
    q = {(j, hp): q_ref[rows[j], sls[hp]] for j, hp in pairs}
    k = {(j, hp): k_ref[rows[j], sls[hp]] for j, hp in pairs}
    v = {(j, hp): v_ref[rows[j], sls[hp]] for j, hp in pairs}
    m_e = (lane < 64, lane >= 64)
    zero = jnp.zeros((), BF16)
    s = {(j, hp, e): lax.dot_general(jnp.where(m_e[e], q[j, hp] * scale, zero), k[j, hp],
                                     (((1,), (1,)), ((), ())), preferred_element_type=F32) for j, hp, e in heads}
    p = {u: jnp.exp(s[u] - jnp.max(s[u], axis=-1, keepdims=True)) for u in heads}
    l = {u: jnp.sum(p[u], axis=-1, keepdims=True) for u in heads}
    o = {(j, hp, e): _bdot(p[j, hp, e].astype(BF16), jnp.where(m_e[e], v[j, hp], zero)) for j, hp, e in heads}
    for j, hp in pairs:
        o_ref[rows[j], sls[hp]] = (o[j, hp, 0] / l[j, hp, 0] + o[j, hp, 1] / l[j, hp, 1]).astype(BF16)


def _ctx_attn(qkv, batch, seq, nb):
    assert batch % nb == 0

    def col(cb):
        return pl.BlockSpec((nb * seq, NA_WIDTH), lambda b: (b, cb))
    return pl.pallas_call(
        functools.partial(_ctx_attn_kernel, seq=seq, nb=nb),
        grid=(batch // nb,),
        in_specs=[col(0), col(1), col(2)],
        out_specs=pl.BlockSpec((nb * seq, NA_WIDTH), lambda b: (b, 0)),
        out_shape=jax.ShapeDtypeStruct((batch * seq, NA_WIDTH), BF16),
        compiler_params=_params(1),
        name="ctx_attn",
    )(qkv, qkv, qkv)


Q_HALF = 512
K_WIN = 768
LAT_SUB = 256
LAT_GROUP = 1


def _lat_attn_kernel(rpb_ref, q_ref, k_ref, v_ref, ck_ref, cv_ref, o_ref, bias_ref, *, rows):
    hp = pl.program_id(0)
    b = pl.program_id(1)
    half = pl.program_id(2)
    kr_n = min(WIN_R, rows)
    rows_half = Q_HALF // GRID_W
    krows_win = K_WIN // GRID_W
    n_dr = 2 * WIN_R - 1
    n_dc = 2 * WIN_C - 1

    @pl.when(jnp.logical_and(b == 0, half == 0))
    def _build_bias():
        qc = lax.broadcasted_iota(jnp.int32, (GRID_W, LANES), 0)
        ln = lax.broadcasted_iota(jnp.int32, (GRID_W, LANES), 1)
        kc = ln % GRID_W
        cs = jnp.clip(qc - WIN_C // 2, 0, GRID_W - WIN_C)
        valid = jnp.logical_and(kc >= cs, kc < cs + WIN_C)
        dc = jnp.clip(kc - qc + (WIN_C - 1), 0, n_dc - 1)
        dcv = jnp.where(valid, dc, -1)
        neg = jnp.full((GRID_W, LANES), NEG_INF, F32)
        for e in range(2):
            blocks = []
            for a in range(n_dr):
                blk = neg
                for j in range(n_dc):
                    blk = jnp.where(dcv == j, rpb_ref[((2 * hp + e) * n_dr + a) * n_dc + j], blk)
                blocks.append(blk)
            for hf in range(2):
                for rl in range(rows_half):
                    r = hf * rows_half + rl
                    rs = min(max(r - kr_n // 2, 0), rows - kr_n)
                    for kp in range(krows_win // 2):
                        parts = []
                        for kl in (2 * kp, 2 * kp + 1):
                            kr = kl + hf * (rows - krows_win)
                            parts.append(blocks[kr - r + WIN_R - 1] if rs <= kr < rs + kr_n else neg)
                        blk = jnp.where(ln < GRID_W, parts[0], parts[1])
                        bias_ref[e, hf, rl * GRID_W:(rl + 1) * GRID_W, kp * LANES:(kp + 1) * LANES] = blk

    lane = lax.broadcasted_iota(jnp.int32, (1, LANES), 1)
    scale = HEAD_DIM ** -0.5
    zero = jnp.zeros((), BF16)
    k0 = pl.multiple_of(half * ((rows - krows_win) * GRID_W), GRID_W)
    kw = k_ref[pl.ds(k0, K_WIN), :]
    vw = v_ref[pl.ds(k0, K_WIN), :]
    ck = ck_ref[0].astype(BF16)
    cv = cv_ref[0].astype(BF16)
    m_e = (lane < 64, lane >= 64)
    ve = [jnp.where(m, vw, zero) for m in m_e]
    cve = [jnp.where(m, cv, zero) for m in m_e]
    nt = (((1,), (1,)), ((), ()))
    for g in range(Q_HALF // (LAT_SUB * LAT_GROUP)):
        units = [(qs, e) for qs in range(g * LAT_GROUP, (g + 1) * LAT_GROUP) for e in range(2)]
        rows_of = {qs: slice(qs * LAT_SUB, (qs + 1) * LAT_SUB) for qs, _ in units}
        qe = {(qs, e): jnp.where(m_e[e], q_ref[rows_of[qs], :] * scale, zero) for qs, e in units}
        s_loc = {(qs, e): lax.dot_general(qe[qs, e], kw, nt, preferred_element_type=F32)
                 + bias_ref[e, half, rows_of[qs], :] for qs, e in units}
        s_ctx = {u: lax.dot_general(qe[u], ck, nt, preferred_element_type=F32) for u in units}
        m = {u: jnp.maximum(jnp.max(s_loc[u], axis=-1, keepdims=True), jnp.max(s_ctx[u], axis=-1, keepdims=True))
             for u in units}
        p_loc = {u: jnp.exp(s_loc[u] - m[u]) for u in units}
        p_ctx = {u: jnp.exp(s_ctx[u] - m[u]) for u in units}
        l = {u: jnp.sum(p_loc[u], axis=-1, keepdims=True) + jnp.sum(p_ctx[u], axis=-1, keepdims=True)
             for u in units}
        o = {(qs, e): (_bdot(p_loc[qs, e].astype(BF16), ve[e]) + _bdot(p_ctx[qs, e].astype(BF16), cve[e]))
             / l[qs, e] for qs, e in units}
        for qs in rows_of:
            o_ref[rows_of[qs], :] = (o[qs, 0] + o[qs, 1]).astype(BF16)


def _lat_attn(qkv, ck_tm, cv_tm, rpb, batch, seq):
    rows = seq // GRID_W
    assert rows * GRID_W == seq and rows - K_WIN // GRID_W == 4 and Q_HALF * 2 == seq
    halves = seq // Q_HALF
    past = ck_tm.shape[1]
    kernel = functools.partial(_lat_attn_kernel, rows=rows)
    return pl.pallas_call(
        kernel,
        grid=(N_PAIRS, batch, halves),
        in_specs=[pl.BlockSpec(memory_space=pltpu.SMEM),
                  pl.BlockSpec((Q_HALF, LANES), lambda hp, b, hf: (b * halves + hf, hp)),
                  pl.BlockSpec((seq, LANES), lambda hp, b, hf: (b, N_PAIRS + hp)),
                  pl.BlockSpec((seq, LANES), lambda hp, b, hf: (b, 2 * N_PAIRS + hp)),
                  pl.BlockSpec((1, past, LANES), lambda hp, b, hf: (b, 0, hp)),
                  pl.BlockSpec((1, past, LANES), lambda hp, b, hf: (b, 0, hp))],
        out_specs=pl.BlockSpec((Q_HALF, LANES), lambda hp, b, hf: (b * halves + hf, hp)),
        out_shape=jax.ShapeDtypeStruct((batch * seq, NA_WIDTH), BF16),
        scratch_shapes=[pltpu.VMEM((2, 2, Q_HALF, K_WIN), F32)],
        compiler_params=_params(3),
        name="lat_attn",
    )(rpb.reshape(-1), qkv, qkv, qkv, ck_tm, cv_tm)


def _dn_kernel(*refs, seq, nb, hps, a_chunks, has_s0, emit_state):
    it = iter(refs)
    q_ref, k_ref, v_ref, ba_ref, cst_ref = next(it), next(it), next(it), next(it), next(it)
    s0_ref = next(it) if has_s0 else None
    o_ref = next(it)
    st_ref = next(it) if emit_state else None
    (beta_s, g_s, u_s, wq_s, lb_s, el_s, odir_s) = it

    hg = pl.program_id(1)
    wl = hps * LANES
    n = seq // CHUNK
    nt = nb * n
    hp_sl = [slice(h * LANES, (h + 1) * LANES) for h in range(hps)]

    def lanes_cat(parts):
        return parts[0] if len(parts) == 1 else jnp.concatenate(parts, axis=1)

    r128 = lax.broadcasted_iota(jnp.int32, (LANES, LANES), 0)
    c128 = lax.broadcasted_iota(jnp.int32, (LANES, LANES), 1)
    bd_mask = (r128 // 64) == (c128 // 64)
    bd_ones = _pair_block_ones()

    ba = ba_ref[...]
    lane_g = lax.broadcasted_iota(jnp.int32, (1, LANES), 1)
    gates = jnp.where(lane_g < 2 * DN_HEADS, _sigmoid(ba),
                      -jnp.exp(cst_ref[0:1, :]) * _softplus(ba + cst_ref[1:2, :]))
    hi, mid, lo = _split3(jnp.where(lane_g < BA_W, gates, 0.0))
    packed = (hi.astype(F32) + pltpu.roll(mid.astype(F32), BA_W, 1)
              + pltpu.roll(lo.astype(F32), 2 * BA_W, 1)).astype(BF16)
    src = lax.broadcasted_iota(jnp.int32, (LANES, 4 * wl), 0)
    col = lax.broadcasted_iota(jnp.int32, (LANES, 4 * wl), 1)
    want = DN_HEADS * (col // wl) + 2 * hps * hg + (col % wl) // 64
    sel = jnp.where(jnp.logical_and(src < 3 * BA_W, src % BA_W == want), 1.0, 0.0).astype(BF16)
    spread = _bdot(packed, sel)
    for d in range(2):
        beta_s[d] = spread[:, d * wl:(d + 1) * wl]
        g_s[d] = spread[:, (2 + d) * wl:(3 + d) * wl]

    ri = lax.broadcasted_iota(jnp.int32, (CHUNK, LANES), 0)
    lj = lax.broadcasted_iota(jnp.int32, (CHUNK, LANES), 1) % 64
    lo_half = lax.broadcasted_iota(jnp.int32, (CHUNK, LANES), 1) < 64
    lo_half2 = (lax.broadcasted_iota(jnp.int32, (CHUNK, 2 * LANES), 1) % LANES) < 64
    diag = ri == lj
    strict = (ri > lj, ri < lj)
    riw = lax.broadcasted_iota(jnp.int32, (CHUNK, wl), 0)
    ljw = lax.broadcasted_iota(jnp.int32, (CHUNK, wl), 1) % 64
    diag_w = riw == ljw
    incl_w = (riw >= ljw, riw <= ljw)
    r64 = lax.broadcasted_iota(jnp.int32, (CHUNK, 3 * CHUNK), 0)
    c64 = lax.broadcasted_iota(jnp.int32, (CHUNK, 3 * CHUNK), 1) % CHUNK
    tri3 = (jnp.where(c64 <= r64, 1.0, 0.0).astype(BF16), jnp.where(c64 >= r64, 1.0, 0.0).astype(BF16))

    off_blk = ([], [])
    for lvl in range(6):
        s = 1 << lvl
        same2 = (ri // (2 * s)) == (lj // (2 * s))
        off_blk[0].append(jnp.logical_and(same2, (ri // s) == (lj // s) + 1))
        off_blk[1].append(jnp.logical_and(same2, (lj // s) == (ri // s) + 1))

    def stack_heads(x, lo_mask):
        return jnp.concatenate([jnp.where(lo_mask, x, 0.0), jnp.where(lo_mask, 0.0, x)], axis=0).astype(BF16)

    def chunk_rows(c):
        return pl.ds(c * CHUNK if isinstance(c, int) else pl.multiple_of(c * CHUNK, CHUNK), CHUNK)

    def chunk_load(c):
        rows = chunk_rows(c)
        return (q_ref[rows, :].astype(F32), k_ref[rows, :].astype(F32), v_ref[rows, :].astype(F32),
                [beta_s[d, rows, :] for d in range(2)], [g_s[d, rows, :] for d in range(2)])

    def group_a(loaded):
        nc = len(loaded)
        cd = [(i, d) for i in range(nc) for d in range(2)]
        units = [(i, h, d) for i in range(nc) for h in range(hps) for d in range(2)]
        qcs = [ld[0] * (DN_DK ** -0.5) for ld in loaded]
        kcs = [ld[1] for ld in loaded]
        vcs = [ld[2] for ld in loaded]
        kt_bd = {}
        for i in range(nc):
            for h in range(hps):
                kc = kcs[i][:, hp_sl[h]]
                ktt = jnp.transpose(jnp.concatenate([kc, kc], axis=0))
                kt_bd[i, h] = jnp.where(bd_mask, ktt, 0.0)
        gc = {(i, d): _bdot(tri3[d], jnp.concatenate(_split3(loaded[i][4][d]), axis=0)) for i, d in cd}
        rrow = {u: jnp.sum(jnp.where(diag_w, gc[u], 0.0), axis=0, keepdims=True) for u in cd}
        dec = {u: jnp.exp(jnp.minimum(gc[u] - rrow[u], 0.0)) for u in cd}
        eg = {u: jnp.exp(gc[u]) for u in cd}
        kbeta = {(i, d): kcs[i] * loaded[i][3][d] for i, d in cd}
        vbeta = {(i, d): vcs[i] * loaded[i][3][d] for i, d in cd}
        kbeg = {u: kbeta[u] * eg[u] for u in cd}
        m1 = {(i, h): _bdot(jnp.concatenate([qcs[i][:, hp_sl[h]], kbeta[i, 0][:, hp_sl[h]],
                                             kbeta[i, 1][:, hp_sl[h]]], axis=0).astype(BF16),
                            kt_bd[i, h].astype(BF16))
              for i in range(nc) for h in range(hps)}
        amat = {(i, h, d): jnp.where(strict[d], m1[i, h][CHUNK * (1 + d):CHUNK * (2 + d)] * dec[i, d][:, hp_sl[h]], 0.0)
                for i, h, d in units}
        o_st = {(i, h, d): stack_heads(jnp.where(off_blk[d][0], amat[i, h, d], 0.0), lo_half) for i, h, d in units}
        tmat = {(i, h, d): jnp.where(diag, 1.0, 0.0) - jnp.where(off_blk[d][0], amat[i, h, d], 0.0) for i, h, d in units}
        prod = {u: _bdot(amat[u].astype(BF16), o_st[u]) for u in units}
        wmat = {u: amat[u] - prod[u] for u in units}
        for lvl in range(1, 5):
            o_st = {(i, h, d): stack_heads(jnp.where(off_blk[d][lvl], wmat[i, h, d], 0.0), lo_half) for i, h, d in units}
            prod = {u: _bdot(jnp.concatenate([tmat[u], wmat[u]], axis=0).astype(BF16), o_st[u]) for u in units}
            tmat = {u: tmat[u] - prod[u][0:CHUNK] for u in units}
            wmat = {u: wmat[u] - prod[u][CHUNK:] for u in units}
        o_st = {(i, h, d): stack_heads(jnp.where(off_blk[d][5], wmat[i, h, d], 0.0), lo_half) for i, h, d in units}
        prod = {u: _bdot(tmat[u].astype(BF16), o_st[u]) for u in units}
        tmat = {u: tmat[u] - prod[u] for u in units}
        x = {}
        for i, h, d in units:
            rhs = jnp.concatenate([vbeta[i, d][:, hp_sl[h]], kbeg[i, d][:, hp_sl[h]]], axis=1)
            x[i, h, d] = _bdot(tmat[i, h, d].astype(BF16), stack_heads(rhs, lo_half2))
        results = []
        for i in range(nc):
            out = []
            for d in range(2):
                u = lanes_cat([x[i, h, d][:, :LANES] for h in range(hps)])
                w = lanes_cat([x[i, h, d][:, LANES:] for h in range(hps)])
                qk = jnp.where(incl_w[d], lanes_cat([m1[i, h][0:CHUNK] for h in range(hps)]) * dec[i, d], 0.0)
                qg = qcs[i] * eg[i, d]
                glast = gc[i, d][CHUNK - 1:CHUNK, :] if d == 0 else gc[i, d][0:1, :]
                kmul = jnp.exp(glast - rrow[i, d])
                kd_bd = lanes_cat([kt_bd[i, h] * kmul[:, hp_sl[h]] for h in range(hps)])
                out.append((u,
                            jnp.concatenate([w, qg], axis=0).astype(BF16),
                            jnp.concatenate([qk, kd_bd], axis=0).astype(BF16),
                            jnp.broadcast_to(jnp.exp(glast), (8, wl))))
            results.append(out)
        return results

    group = min(a_chunks, nt)

    def phase_a(gi, carry):
        cs = [gi * group + j for j in range(group)]
        loaded = [chunk_load(c) for c in cs]
        results = group_a(loaded)
        for c, res in zip(cs, results):
            for d in range(2):
                u_s[d, c], wq_s[d, c], lb_s[d, c], el_s[d, c] = res[d]
        return carry

    if nt // group <= 2:
        for gi in range(nt // group):
            phase_a(gi, 0)
    else:
        lax.fori_loop(0, nt // group, phase_a, 0)

    dh = [(j, d, h) for j in range(nb) for d in range(2) for h in range(hps)]
    if has_s0:
        zero = jnp.zeros((DN_DK, DN_DV), F32)

        def bd_state(j, d, h):
            top = jnp.concatenate([s0_ref[j, 0, d, 2 * h], zero], axis=1)
            bot = jnp.concatenate([zero, s0_ref[j, 0, d, 2 * h + 1]], axis=1)
            return jnp.concatenate([top, bot], axis=0)
        s_init = tuple(bd_state(j, d, h) for j, d, h in dh)
    else:
        s_init = tuple(jnp.zeros((LANES, LANES), F32) for _ in dh)

    def phase_b(s, states):
        jd = [(j, d) for j in range(nb) for d in range(2)]
        cs = {(j, d): j * n + (s if d == 0 else n - 1 - s) for j, d in jd}
        ld = {u: (wq_s[u[1], cs[u]], u_s[u[1], cs[u]], lb_s[u[1], cs[u]], el_s[u[1], cs[u], 0:1, :]) for u in jd}
        st = dict(zip(dh, states))
        ws = {(j, d, h): _bdot(ld[j, d][0][:, hp_sl[h]], st[j, d, h].astype(BF16)) for j, d, h in dh}
        vn = {(j, d, h): ld[j, d][1][:, hp_sl[h]] - ws[j, d, h][0:CHUNK] for j, d, h in dh}
        r2 = {(j, d, h): _bdot(ld[j, d][2][:, hp_sl[h]], stack_heads(vn[j, d, h], lo_half)) for j, d, h in dh}
        new_states = tuple(st[j, d, h] * ld[j, d][3][:, hp_sl[h]] + r2[j, d, h][CHUNK:] for j, d, h in dh)
        for j, d in jd:
            odir_s[d, chunk_rows(cs[j, d]), :] = lanes_cat([ws[j, d, h][CHUNK:] + r2[j, d, h][0:CHUNK]
                                                            for h in range(hps)])
        return new_states

    s_fin = dict(zip(dh, lax.fori_loop(0, n, phase_b, s_init)))

    for h in range(hps):
        o = odir_s[0, :, hp_sl[h]] + odir_s[1, :, hp_sl[h]]
        msq = _dot2_l(o * o, bd_ones) * (1.0 / DN_DV)
        o_ref[:, hp_sl[h]] = (o * lax.rsqrt(msq + EPS) * cst_ref[2:3, :]).astype(BF16)

    if emit_state:
        for j, d, h in dh:
            st_ref[j, 0, d, 2 * h] = s_fin[j, d, h][0:DN_DK, 0:DN_DV]
            st_ref[j, 0, d, 2 * h + 1] = s_fin[j, d, h][DN_DK:, DN_DV:]


def _deltanet(dqkv, ba, cst, batch, seq, nb, hps, a_chunks, s0=None, emit_state=False):
    assert batch % nb == 0
    rows = nb * seq
    nt = rows // CHUNK
    wl = hps * LANES
    groups = N_PAIRS // hps
    has_s0 = s0 is not None

    def col(cbase):
        return pl.BlockSpec((rows, wl), lambda b, hg: (b, cbase // wl + hg))

    st_spec = pl.BlockSpec((nb, 1, 2, 2 * hps, DN_DK, DN_DV), lambda b, hg: (b, 0, 0, hg, 0, 0))
    in_specs = [col(0), col(DN_WIDTH), col(2 * DN_WIDTH),
                pl.BlockSpec((rows, LANES), lambda b, hg: (b, 0)),
                pl.BlockSpec((8, LANES), lambda b, hg: (0, 0))]
    args = [dqkv, dqkv, dqkv, ba, cst]
    if has_s0:
        in_specs.append(st_spec)
        args.append(s0)
    out_specs = [pl.BlockSpec((rows, wl), lambda b, hg: (b, hg))]
    out_shape = [jax.ShapeDtypeStruct((batch * seq, DN_HEADS * DN_DV), BF16)]
    if emit_state:
        out_specs.append(st_spec)
        out_shape.append(jax.ShapeDtypeStruct((batch, 1, 2, DN_HEADS, DN_DK, DN_DV), F32))
    scratch = [pltpu.VMEM((2, rows, wl), F32), pltpu.VMEM((2, rows, wl), F32),
               pltpu.VMEM((2, nt, CHUNK, wl), F32),
               pltpu.VMEM((2, nt, 2 * CHUNK, wl), BF16),
               pltpu.VMEM((2, nt, 3 * CHUNK, wl), BF16),
               pltpu.VMEM((2, nt, 8, wl), F32),
               pltpu.VMEM((2, rows, wl), F32)]
    res = pl.pallas_call(
        functools.partial(_dn_kernel, seq=seq, nb=nb, hps=hps, a_chunks=a_chunks, has_s0=has_s0,
                          emit_state=emit_state),
        grid=(batch // nb, groups),
        in_specs=in_specs,
        out_specs=out_specs,
        out_shape=out_shape,
        scratch_shapes=scratch,
        compiler_params=_params(2),
        name="deltanet_lat" if has_s0 else "deltanet_ctx",
    )(*args)
    return res


TM_F = 2 * TM


def _merge_ffn_kernel(x_ref, ona_ref, odn_ref, gz_ref, mod_ref, wpost1_ref, wpre2_ref, wpost2_ref,
                      wao_ref, wdo_ref, wout_ref, w1_ref, w2_ref, o_ref):
    mod = mod_ref[0]
    subs = [slice(s * TM, (s + 1) * TM) for s in range(TM_F // TM)]
    z_sl, gna_sl, gdn_sl = (slice(0, DN_WIDTH), slice(DN_WIDTH, DN_WIDTH + D_MODEL),
                            slice(DN_WIDTH + D_MODEL, GZ_W))
    odn = [odn_ref[sl, :].astype(F32) * _silu(gz_ref[sl, z_sl].astype(F32)) for sl in subs]
    br_na = [_bdot(ona_ref[sl, :], wao_ref[...]) for sl in subs]
    br_dn = [_bdot(t.astype(BF16), wdo_ref[...]) for t in odn]
    m = [_sigmoid(gz_ref[sl, gna_sl].astype(F32)) * a + _sigmoid(gz_ref[sl, gdn_sl].astype(F32)) * b
         for sl, a, b in zip(subs, br_na, br_dn)]
    y = [_bdot(t.astype(BF16), wout_ref[...]) for t in m]
    x1 = [x_ref[sl, :] + mod[2:3] * _rms_rows(t, wpost1_ref[...]) for sl, t in zip(subs, y)]
    h = [_rms_rows(t, wpre2_ref[...]) * (1.0 + mod[4:5]) + mod[3:4] for t in x1]
    f1 = [_bdot(t.astype(BF16), w1_ref[...]) for t in h]
    r = [jnp.maximum(t, 0.0) for t in f1]
    f = [_bdot((t * t).astype(BF16), w2_ref[...]) for t in r]
    for sl, a, b in zip(subs, x1, f):
        o_ref[sl, :] = a + mod[5:6] * _rms_rows(b, wpost2_ref[...])


def _merge_ffn(x2d, o_na, o_dn, gz, mod3, mod_row_of_tile, wpost1, wpre2, wpost2, w_ao, w_do, w_out, w1, w2):
    n_tok = x2d.shape[0]
    row = lambda i: (i, 0)
    const = lambda i: (0, 0)

    def resident(shape):
        return pl.BlockSpec(shape, const, pipeline_mode=pl.Buffered(1))

    return pl.pallas_call(
        _merge_ffn_kernel,
        grid=(n_tok // TM_F,),
        in_specs=[pl.BlockSpec((TM_F, D_MODEL), row),
                  pl.BlockSpec((TM_F, NA_WIDTH), row),
                  pl.BlockSpec((TM_F, DN_WIDTH), row),
                  pl.BlockSpec((TM_F, GZ_W), row),
                  pl.BlockSpec((1, N_MOD, D_MODEL), lambda i: (mod_row_of_tile(i), 0, 0)),
                  pl.BlockSpec((1, D_MODEL), const),
                  pl.BlockSpec((1, D_MODEL), const),
                  pl.BlockSpec((1, D_MODEL), const),
                  resident((NA_WIDTH, D_MODEL)),
                  resident((DN_WIDTH, D_MODEL)),
                  resident((D_MODEL, D_MODEL)),
                  resident((D_MODEL, D_FF)),
                  resident((D_FF, D_MODEL))],
        out_specs=pl.BlockSpec((TM_F, D_MODEL), row),
        out_shape=jax.ShapeDtypeStruct((n_tok, D_MODEL), F32),
        compiler_params=_params(1),
        name="merge_ffn",
    )(x2d, o_na, o_dn, gz, mod3, wpost1, wpre2, wpost2, w_ao, w_do, w_out, w1, w2)


def _pack_w_in(w):
    assert w.shape[1] == IN_COLS
    wb = w.astype(BF16)
    return wb, jnp.pad(wb[:, MAIN_COLS:], ((0, 0), (0, LANES - (IN_COLS - MAIN_COLS))))


def kernel(x_prompt, x_sample, c, cache_na_k, cache_na_v, state_delta, c_ctx, w_ada, b_ada, norm_pre1, norm_post1, norm_pre2, norm_post2, w_in, conv_w, a_log, dt_bias, dn_norm, na_rpb, w_ao, w_do, w_out, w_ff1, w_ff2):
    batch, seq, _ = x_prompt.shape
    dec_batch, dec_seq, _ = x_sample.shape
    depth = w_in.shape[0]
    assert depth == 1 and seq == TM and dec_seq % TM_F == 0 and (batch * seq) % TM_F == 0 and dec_batch < 8

    xp = x_prompt.reshape(batch * seq, D_MODEL)
    xs = x_sample.reshape(dec_batch * dec_seq, D_MODEL)
    l = 0

    cvecs = jnp.concatenate([c, c_ctx[None], jnp.zeros((8 - dec_batch - 1, D_MODEL), F32)], axis=0)
    mod3 = _adaln(cvecs, w_ada[l], b_ada[l]).reshape(8, N_MOD, D_MODEL)
    ctx_row = lambda i: dec_batch
    tiles_per_lat = dec_seq // TM
    lat_row = lambda i: i // tiles_per_lat

    w_in_p = _pack_w_in(w_in[l])
    w_ao_b = w_ao[l].astype(BF16)
    w_do_b = w_do[l].astype(BF16)
    w_out_b = w_out[l].astype(BF16)
    w1_b = w_ff1[l].astype(BF16)
    w2_b = w_ff2[l].astype(BF16)
    wpre1 = norm_pre1[l].reshape(1, D_MODEL)
    wpost1 = norm_post1[l].reshape(1, D_MODEL)
    wpre2 = norm_pre2[l].reshape(1, D_MODEL)
    wpost2 = norm_post2[l].reshape(1, D_MODEL)

    conv_w8 = jnp.pad(conv_w[l], ((0, 8 - CONV_W), (0, 0)))
    gate_rows = jnp.pad(jnp.stack([a_log[l].reshape(-1), dt_bias[l].reshape(-1)]),
                        ((0, 0), (2 * DN_HEADS, LANES - 4 * DN_HEADS)))
    cst = jnp.concatenate([gate_rows, jnp.tile(dn_norm[l], 2)[None], jnp.zeros((5, LANES), F32)], axis=0)

    qkv_c, dqkv_c, gz_c, ba_c, new_k, new_v = _inproj(xp, mod3, ctx_row, wpre1, w_in_p, conv_w8, seq, batch)
    o_na_c = _ctx_attn(qkv_c, batch, seq, nb=2)
    o_dn_c, new_s = _deltanet(dqkv_c, ba_c, cst, batch, seq, nb=2, hps=4, a_chunks=4, emit_state=True)
    y_c = _merge_ffn(xp, o_na_c, o_dn_c, gz_c, mod3, ctx_row, wpost1, wpre2, wpost2,
                     w_ao_b, w_do_b, w_out_b, w1_b, w2_b)

    qkv_l, dqkv_l, gz_l, ba_l = _inproj(xs, mod3, lat_row, wpre1, w_in_p, conv_w8, dec_seq, dec_batch,
                                        rope_tabs=_rope_tables(dec_seq))
    past = cache_na_k.shape[3]
    ck_tm = jnp.transpose(cache_na_k[:, l], (0, 2, 1, 3)).reshape(dec_batch, past, NA_WIDTH)
    cv_tm = jnp.transpose(cache_na_v[:, l], (0, 2, 1, 3)).reshape(dec_batch, past, NA_WIDTH)
    o_na_l = _lat_attn(qkv_l, ck_tm, cv_tm, na_rpb[l], dec_batch, dec_seq)
    (o_dn_l,) = _deltanet(dqkv_l, ba_l, cst, dec_batch, dec_seq, nb=1, hps=4, a_chunks=2, s0=state_delta)
    y_l = _merge_ffn(xs, o_na_l, o_dn_l, gz_l, mod3, lambda i: i // (dec_seq // TM_F), wpost1, wpre2, wpost2,
                     w_ao_b, w_do_b, w_out_b, w1_b, w2_b)

    return (y_c.reshape(batch, seq, D_MODEL), y_l.reshape(dec_batch, dec_seq, D_MODEL),
            new_k, new_v, new_s)
```

```python
import functools

import numpy as np
import jax
import jax.numpy as jnp
from jax import lax
from jax.experimental import pallas as pl
from jax.experimental.pallas import tpu as pltpu

F32 = jnp.float32
BF16 = jnp.bfloat16

D_MODEL = 1024
N_MOD = 6
GRID_W = 64
NA_HEADS = 8
HEAD_DIM = 64
NA_WIDTH = NA_HEADS * HEAD_DIM
WIN_R = 8
WIN_C = 16
ROPE_BASE = 10000.0
DN_HEADS = 8
DN_DK = 64
DN_DV = 64
DN_WIDTH = DN_HEADS * DN_DK
CONV_W = 5
CHUNK = 64
D_FF = 4 * D_MODEL
EPS = 1e-6
NEG_INF = -1e30

LANES = 128
SUBLANES = 8
N_PAIRS = NA_HEADS // 2
TM = 256
TMI = 2 * TM
VMEM_LIMIT = 56 * 1024 * 1024

QKV_W = 3 * NA_WIDTH
DQKV_W = 3 * DN_WIDTH
BA_W = 4 * DN_HEADS
GZ_W = DN_WIDTH + 2 * D_MODEL
COL_QKV, COL_DQKV, COL_TAIL = 0, QKV_W, QKV_W + DQKV_W
IN_COLS = COL_TAIL + BA_W + GZ_W
MAIN_COLS = (IN_COLS // LANES) * LANES
TAIL_W = MAIN_COLS - COL_TAIL + LANES


def _params(n_grid):
    return pltpu.CompilerParams(dimension_semantics=("arbitrary",) * n_grid,
                                vmem_limit_bytes=VMEM_LIMIT)


def _bdot(a, b):
    return jnp.dot(a, b, preferred_element_type=F32)


def _split3(a):
    hi = a.astype(BF16)
    r1 = a - hi.astype(F32)
    mid = r1.astype(BF16)
    lo = (r1 - mid.astype(F32)).astype(BF16)
    return hi, mid, lo


def _dot2_l(a, b_exact):
    hi = a.astype(BF16)
    lo = (a - hi.astype(F32)).astype(BF16)
    return _bdot(jnp.concatenate([hi, lo], axis=1), jnp.concatenate([b_exact, b_exact], axis=0))


def _sigmoid(x):
    return 0.5 * jnp.tanh(0.5 * x) + 0.5


def _silu(x):
    return x * _sigmoid(x)


def _softplus(x):
    return jnp.maximum(x, 0.0) + jnp.log(1.0 + jnp.exp(-jnp.abs(x)))


def _rms_rows(x, w):
    ms = jnp.mean(x * x, axis=-1, keepdims=True)
    return x * lax.rsqrt(ms + EPS) * w


def _pair_block_ones():
    r = lax.broadcasted_iota(jnp.int32, (LANES, LANES), 0)
    c = lax.broadcasted_iota(jnp.int32, (LANES, LANES), 1)
    return jnp.where((r // 64) == (c // 64), 1.0, 0.0).astype(BF16)


def _adaln_kernel(c_ref, w_ref, b_ref, o_ref):
    c = c_ref[...]
    s = _silu(c).astype(BF16)
    o_ref[...] = _bdot(s, w_ref[...].astype(BF16)) + b_ref[...]


def _adaln(cvecs, w_ada, b_ada):
    tn = 1024
    n = w_ada.shape[1]
    return pl.pallas_call(
        _adaln_kernel,
        grid=(n // tn,),
        in_specs=[pl.BlockSpec((8, D_MODEL), lambda j: (0, 0)),
                  pl.BlockSpec((D_MODEL, tn), lambda j: (0, j)),
                  pl.BlockSpec((1, tn), lambda j: (0, j))],
        out_specs=pl.BlockSpec((8, tn), lambda j: (0, j)),
        out_shape=jax.ShapeDtypeStruct((8, n), F32),
        compiler_params=_params(1),
        name="adaln",
    )(cvecs, w_ada, b_ada.reshape(1, n))


def _inproj_kernel(*refs, latent, tiles_per_seq):
    it = iter(refs)
    x_ref, mod_ref, wpre_ref, w_ref, wlast_ref, cw_ref = (next(it), next(it), next(it), next(it), next(it),
                                                          next(it))
    if latent:
        xprev_ref, xnext_ref, cos_ref, sin_ref = next(it), next(it), next(it), next(it)
    qkv_ref, dqkv_ref, gz_ref, ba_ref = next(it), next(it), next(it), next(it)
    if not latent:
        kc_ref, vc_ref = next(it), next(it)

    mod = mod_ref[0]
    n_sub = TMI // TM
    bd_ones = _pair_block_ones()

    def modulated(x):
        return (_rms_rows(x, wpre_ref[...]) * (1.0 + mod[1:2]) + mod[0:1]).astype(BF16)

    for sub in range(n_sub):
        rs = slice(sub * TM, (sub + 1) * TM)
        hb = modulated(x_ref[rs, :])

        w_dn = w_ref[:, COL_DQKV:COL_DQKV + DQKV_W]
        halo_zero = jnp.zeros((SUBLANES, DQKV_W), F32)
        if latent:
            t = (pl.program_id(0) * n_sub + sub) % tiles_per_seq
            prev = x_ref[sub * TM - SUBLANES:sub * TM, :] if sub > 0 else xprev_ref[...]
            nxt = x_ref[(sub + 1) * TM:(sub + 1) * TM + SUBLANES, :] if sub < n_sub - 1 else xnext_ref[...]
            hb_ext = jnp.concatenate([modulated(prev), hb, modulated(nxt)], axis=0)
            pe = _bdot(hb_ext, w_dn)
            pe = jnp.concatenate([jnp.where(t > 0, pe[:SUBLANES], halo_zero), pe[SUBLANES:SUBLANES + TM],
                                  jnp.where(t < tiles_per_seq - 1, pe[SUBLANES + TM:], halo_zero)], axis=0)
        else:
            pe = jnp.concatenate([halo_zero, _bdot(hb, w_dn), halo_zero], axis=0)
        acc = None
        for j in sorted(range(CONV_W), key=lambda s: abs(s - CONV_W // 2)):
            lo = SUBLANES - CONV_W // 2 + j
            term = pe[lo:lo + TM] * cw_ref[j:j + 1, :]
            acc = term if acc is None else acc + term
        y = _silu(acc)

        p = _bdot(hb, w_ref[:, COL_QKV:COL_QKV + QKV_W])
        if latent:
            qk = p[:, :2 * NA_WIDTH]
            lane = lax.broadcasted_iota(jnp.int32, qk.shape, 1)
            first = (lane % 32) < 16
            partner = jnp.where(first, pltpu.roll(qk, 2 * NA_WIDTH - 16, 1), pltpu.roll(qk, 16, 1))
            qkv_ref[rs, :2 * NA_WIDTH] = (qk * cos_ref[rs, :] + partner * sin_ref[rs, :]).astype(BF16)
            qkv_ref[rs, 2 * NA_WIDTH:] = p[:, 2 * NA_WIDTH:].astype(BF16)
        else:
            qkv_ref[rs, :] = p.astype(BF16)
            for h in range(NA_HEADS):
                kc_ref[sub, 0, h] = p[:, NA_WIDTH + h * HEAD_DIM:NA_WIDTH + (h + 1) * HEAD_DIM]
                vc_ref[sub, 0, h] = p[:, 2 * NA_WIDTH + h * HEAD_DIM:2 * NA_WIDTH + (h + 1) * HEAD_DIM]

        split = COL_TAIL + ((MAIN_COLS - COL_TAIL) // LANES // 2) * LANES
        tail = jnp.concatenate([_bdot(hb, w_ref[:, COL_TAIL:split]), _bdot(hb, w_ref[:, split:MAIN_COLS]),
                                _bdot(hb, wlast_ref[...])], axis=1)
        ba_ref[rs, :] = tail[:, :LANES]
        gz_ref[rs, :] = pltpu.roll(tail, TAIL_W - BA_W, 1)[:, :GZ_W].astype(BF16)

        for s in range(DQKV_W // LANES):
            ys = y[:, s * LANES:(s + 1) * LANES]
            if s < 2 * DN_WIDTH // LANES:
                ys = ys * lax.rsqrt(_dot2_l(ys * ys, bd_ones) + EPS)
            dqkv_ref[rs, s * LANES:(s + 1) * LANES] = ys.astype(BF16)


def _inproj(x2d, mod3, mod_row_of_tile, wpre, w_packed, conv_w8, seq, batch, rope_tabs=None):
    n_tok = x2d.shape[0]
    latent = rope_tabs is not None
    assert n_tok % TMI == 0 and seq % TM == 0 and (seq % TMI == 0 or TMI % seq == 0)
    rows8 = TMI // SUBLANES
    n_blk8 = n_tok // SUBLANES

    def resident(shape):
        return pl.BlockSpec(shape, lambda i: (0, 0), pipeline_mode=pl.Buffered(1))

    in_specs = [pl.BlockSpec((TMI, D_MODEL), lambda i: (i, 0)),
                pl.BlockSpec((1, N_MOD, D_MODEL), lambda i: (mod_row_of_tile(i), 0, 0)),
                pl.BlockSpec((1, D_MODEL), lambda i: (0, 0)),
                resident((D_MODEL, MAIN_COLS)),
                resident((D_MODEL, LANES)),
                pl.BlockSpec((8, DQKV_W), lambda i: (0, 0))]
    w_main, w_last = w_packed
    args = [x2d, mod3, wpre, w_main, w_last, conv_w8]
    out_specs = [pl.BlockSpec((TMI, QKV_W), lambda i: (i, 0)),
                 pl.BlockSpec((TMI, DQKV_W), lambda i: (i, 0)),
                 pl.BlockSpec((TMI, GZ_W), lambda i: (i, 0)),
                 pl.BlockSpec((TMI, LANES), lambda i: (i, 0))]
    out_shape = [jax.ShapeDtypeStruct((n_tok, QKV_W), BF16),
                 jax.ShapeDtypeStruct((n_tok, DQKV_W), BF16),
                 jax.ShapeDtypeStruct((n_tok, GZ_W), BF16),
                 jax.ShapeDtypeStruct((n_tok, LANES), F32)]
    if latent:
        steps_per_seq = seq // TMI
        in_specs += [pl.BlockSpec((SUBLANES, D_MODEL), lambda i: (jnp.maximum(i * rows8 - 1, 0), 0)),
                     pl.BlockSpec((SUBLANES, D_MODEL), lambda i: (jnp.minimum((i + 1) * rows8, n_blk8 - 1), 0))]
        args += [x2d, x2d]
        for tab in rope_tabs:
            in_specs.append(pl.BlockSpec((TMI, 2 * NA_WIDTH), lambda i: (i % steps_per_seq, 0)))
            args.append(tab)
    else:
        assert seq == TM
        seqs = TMI // seq
        cache_spec = pl.BlockSpec((seqs, 1, NA_HEADS, seq, HEAD_DIM), lambda i: (i, 0, 0, 0, 0))
        cache_shape = jax.ShapeDtypeStruct((batch, 1, NA_HEADS, seq, HEAD_DIM), F32)
        out_specs += [cache_spec, cache_spec]
        out_shape += [cache_shape, cache_shape]
    return pl.pallas_call(
        functools.partial(_inproj_kernel, latent=latent, tiles_per_seq=seq // TM),
        grid=(n_tok // TMI,),
        in_specs=in_specs,
        out_specs=out_specs,
        out_shape=out_shape,
        compiler_params=_params(1),
        name="inproj_lat" if latent else "inproj_ctx",
    )(*args)


def _rope_tables(seq):
    half = HEAD_DIM // 2
    nf = half // 2
    pos = np.arange(seq)
    inv = (np.float32(ROPE_BASE) ** (-np.arange(nf, dtype=np.float32) / np.float32(nf))).astype(np.float32)
    ang_r = ((pos // GRID_W).astype(np.float32)[:, None] * inv).astype(np.float64)
    ang_c = ((pos % GRID_W).astype(np.float32)[:, None] * inv).astype(np.float64)

    def grp(ang):
        return (np.concatenate([np.cos(ang), np.cos(ang)], 1),
                np.concatenate([-np.sin(ang), np.sin(ang)], 1))

    cr, sr = grp(ang_r)
    cc, sc = grp(ang_c)
    cos_h = np.concatenate([cr, cc], 1)
    sin_h = np.concatenate([sr, sc], 1)
    reps = 2 * NA_HEADS
    return (jnp.asarray(np.tile(cos_h, (1, reps)), F32),
            jnp.asarray(np.tile(sin_h, (1, reps)), F32))


def _ctx_attn_kernel(q_ref, k_ref, v_ref, o_ref, *, seq, nb):
    lane = lax.broadcasted_iota(jnp.int32, (1, LANES), 1)
    scale = HEAD_DIM ** -0.5
    pairs = [(j, hp) for j in range(nb) for hp in range(N_PAIRS)]
    heads = [(j, hp, e) for j, hp in pairs for e in range(2)]
    rows = [slice(j * seq, (j + 1) * seq) for j in range(nb)]
    sls = [slice(hp * LANES, (hp + 1) * LANES) for hp in range(N_PAIRS)]
    q = {(j, hp): q_ref[rows[j], sls[hp]] for j, hp in pairs}
    k = {(j, hp): k_ref[rows[j], sls[hp]] for j, hp in pairs}
    v = {(j, hp): v_ref[rows[j], sls[hp]] for j, hp in pairs}
    m_e = (lane < 64, lane >= 64)
    zero = jnp.zeros((), BF16)
    s = {(j, hp, e): lax.dot_general(jnp.where(m_e[e], q[j, hp] * scale, zero), k[j, hp],
                                     (((1,), (1,)), ((), ())), preferred_element_type=F32) for j, hp, e in heads}
    p = {u: jnp.exp(s[u] - jnp.max(s[u], axis=-1, keepdims=True)) for u in heads}
    l = {u: jnp.sum(p[u], axis=-1, keepdims=True) for u in heads}
    o = {(j, hp, e): _bdot(p[j, hp, e].astype(BF16), jnp.where(m_e[e], v[j, hp], zero)) for j, hp, e in heads}
    for j, hp in pairs:
        o_ref[rows[j], sls[hp]] = (o[j, hp, 0] / l[j, hp, 0] + o[j, hp, 1] / l[j, hp, 1]).astype(BF16)


def _ctx_attn(qkv, batch, seq, nb):
    assert batch % nb == 0

    def col(cb):
        return pl.BlockSpec((nb * seq, NA_WIDTH), lambda b: (b, cb))
    return pl.pallas_call(
        functools.partial(_ctx_attn_kernel, seq=seq, nb=nb),
        grid=(batch // nb,),
        in_specs=[col(0), col(1), col(2)],
        out_specs=pl.BlockSpec((nb * seq, NA_WIDTH), lambda b: (b, 0)),
        out_shape=jax.ShapeDtypeStruct((batch * seq, NA_WIDTH), BF16),
        compiler_params=_params(1),
        name="ctx_attn",
    )(qkv, qkv, qkv)


Q_HALF = 512
K_WIN = 768
LAT_SUB = 256
LAT_GROUP = 1


def _lat_attn_kernel(rpb_ref, q_ref, k_ref, v_ref, ck_ref, cv_ref, o_ref, bias_ref, *, rows):
    hp = pl.program_id(0)
    b = pl.program_id(1)
    half = pl.program_id(2)
    kr_n = min(WIN_R, rows)
    rows_half = Q_HALF // GRID_W
    krows_win = K_WIN // GRID_W
    n_dr = 2 * WIN_R - 1
    n_dc = 2 * WIN_C - 1

    @pl.when(jnp.logical_and(b == 0, half == 0))
    def _build_bias():
        qc = lax.broadcasted_iota(jnp.int32, (GRID_W, LANES), 0)
        ln = lax.broadcasted_iota(jnp.int32, (GRID_W, LANES), 1)
        kc = ln % GRID_W
        cs = jnp.clip(qc - WIN_C // 2, 0, GRID_W - WIN_C)
        valid = jnp.logical_and(kc >= cs, kc < cs + WIN_C)
        dc = jnp.clip(kc - qc + (WIN_C - 1), 0, n_dc - 1)
        dcv = jnp.where(valid, dc, -1)
        neg = jnp.full((GRID_W, LANES), NEG_INF, F32)
        for e in range(2):
            blocks = []
            for a in range(n_dr):
                blk = neg
                for j in range(n_dc):
                    blk = jnp.where(dcv == j, rpb_ref[((2 * hp + e) * n_dr + a) * n_dc + j], blk)
                blocks.append(blk)
            for hf in range(2):
                for rl in range(rows_half):
                    r = hf * rows_half + rl
                    rs = min(max(r - kr_n // 2, 0), rows - kr_n)
                    for kp in range(krows_win // 2):
                        parts = []
                        for kl in (2 * kp, 2 * kp + 1):
                            kr = kl + hf * (rows - krows_win)
                            parts.append(blocks[kr - r + WIN_R - 1] if rs <= kr < rs + kr_n else neg)
                        blk = jnp.where(ln < GRID_W, parts[0], parts[1])
                        bias_ref[e, hf, rl * GRID_W:(rl + 1) * GRID_W, kp * LANES:(kp + 1) * LANES] = blk

    lane = lax.broadcasted_iota(jnp.int32, (1, LANES), 1)
    scale = HEAD_DIM ** -0.5
    zero = jnp.zeros((), BF16)
    k0 = pl.multiple_of(half * ((rows - krows_win) * GRID_W), GRID_W)
    kw = k_ref[pl.ds(k0, K_WIN), :]
    vw = v_ref[pl.ds(k0, K_WIN), :]
    ck = ck_ref[0].astype(BF16)
    cv = cv_ref[0].astype(BF16)
    m_e = (lane < 64, lane >= 64)
    ve = [jnp.where(m, vw, zero) for m in m_e]
    cve = [jnp.where(m, cv, zero) for m in m_e]
    nt = (((1,), (1,)), ((), ()))
    for g in range(Q_HALF // (LAT_SUB * LAT_GROUP)):
        units = [(qs, e) for qs in range(g * LAT_GROUP, (g + 1) * LAT_GROUP) for e in range(2)]
        rows_of = {qs: slice(qs * LAT_SUB, (qs + 1) * LAT_SUB) for qs, _ in units}
        qe = {(qs, e): jnp.where(m_e[e], q_ref[rows_of[qs], :] * scale, zero) for qs, e in units}
        s_loc = {(qs, e): lax.dot_general(qe[qs, e], kw, nt, preferred_element_type=F32)
                 + bias_ref[e, half, rows_of[qs], :] for qs, e in units}
        s_ctx = {u: lax.dot_general(qe[u], ck, nt, preferred_element_type=F32) for u in units}
        m = {u: jnp.maximum(jnp.max(s_loc[u], axis=-1, keepdims=True), jnp.max(s_ctx[u], axis=-1, keepdims=True))
             for u in units}
        p_loc = {u: jnp.exp(s_loc[u] - m[u]) for u in units}
        p_ctx = {u: jnp.exp(s_ctx[u] - m[u]) for u in units}
        l = {u: jnp.sum(p_loc[u], axis=-1, keepdims=True) + jnp.sum(p_ctx[u], axis=-1, keepdims=True)
             for u in units}
        o = {(qs, e): (_bdot(p_loc[qs, e].astype(BF16), ve[e]) + _bdot(p_ctx[qs, e].astype(BF16), cve[e]))
             / l[qs, e] for qs, e in units}
        for qs in rows_of:
            o_ref[rows_of[qs], :] = (o[qs, 0] + o[qs, 1]).astype(BF16)


def _lat_attn(qkv, ck_tm, cv_tm, rpb, batch, seq):
    rows = seq // GRID_W
    assert rows * GRID_W == seq and rows - K_WIN // GRID_W == 4 and Q_HALF * 2 == seq
    halves = seq // Q_HALF
    past = ck_tm.shape[1]
    kernel = functools.partial(_lat_attn_kernel, rows=rows)
    return pl.pallas_call(
        kernel,
        grid=(N_PAIRS, batch, halves),
        in_specs=[pl.BlockSpec(memory_space=pltpu.SMEM),
                  pl.BlockSpec((Q_HALF, LANES), lambda hp, b, hf: (b * halves + hf, hp)),
                  pl.BlockSpec((seq, LANES), lambda hp, b, hf: (b, N_PAIRS + hp)),
                  pl.BlockSpec((seq, LANES), lambda hp, b, hf: (b, 2 * N_PAIRS + hp)),
                  pl.BlockSpec((1, past, LANES), lambda hp, b, hf: (b, 0, hp)),
                  pl.BlockSpec((1, past, LANES), lambda hp, b, hf: (b, 0, hp))],
        out_specs=pl.BlockSpec((Q_HALF, LANES), lambda hp, b, hf: (b * halves + hf, hp)),
        out_shape=jax.ShapeDtypeStruct((batch * seq, NA_WIDTH), BF16),
        scratch_shapes=[pltpu.VMEM((2, 2, Q_HALF, K_WIN), F32)],
        compiler_params=_params(3),
        name="lat_attn",
    )(rpb.reshape(-1), qkv, qkv, qkv, ck_tm, cv_tm)


def _dn_kernel(*refs, seq, nb, hps, a_chunks, has_s0, emit_state):
    it = iter(refs)
    q_ref, k_ref, v_ref, ba_ref, cst_ref = next(it), next(it), next(it), next(it), next(it)
    s0_ref = next(it) if has_s0 else None
    o_ref = next(it)
    st_ref = next(it) if emit_state else None
    (beta_s, g_s, u_s, wq_s, lb_s, el_s, odir_s) = it

    hg = pl.program_id(1)
    wl = hps * LANES
    n = seq // CHUNK
    nt = nb * n
    hp_sl = [slice(h * LANES, (h + 1) * LANES) for h in range(hps)]

    def lanes_cat(parts):
        return parts[0] if len(parts) == 1 else jnp.concatenate(parts, axis=1)

    r128 = lax.broadcasted_iota(jnp.int32, (LANES, LANES), 0)
    c128 = lax.broadcasted_iota(jnp.int32, (LANES, LANES), 1)
    bd_mask = (r128 // 64) == (c128 // 64)
    bd_ones = _pair_block_ones()

    ba = ba_ref[...]
    lane_g = lax.broadcasted_iota(jnp.int32, (1, LANES), 1)
    gates = jnp.where(lane_g < 2 * DN_HEADS, _sigmoid(ba),
                      -jnp.exp(cst_ref[0:1, :]) * _softplus(ba + cst_ref[1:2, :]))
    hi, mid, lo = _split3(jnp.where(lane_g < BA_W, gates, 0.0))
    packed = (hi.astype(F32) + pltpu.roll(mid.astype(F32), BA_W, 1)
              + pltpu.roll(lo.astype(F32), 2 * BA_W, 1)).astype(BF16)
    src = lax.broadcasted_iota(jnp.int32, (LANES, 4 * wl), 0)
    col = lax.broadcasted_iota(jnp.int32, (LANES, 4 * wl), 1)
    want = DN_HEADS * (col // wl) + 2 * hps * hg + (col % wl) // 64
    sel = jnp.where(jnp.logical_and(src < 3 * BA_W, src % BA_W == want), 1.0, 0.0).astype(BF16)
    spread = _bdot(packed, sel)
    for d in range(2):
        beta_s[d] = spread[:, d * wl:(d + 1) * wl]
        g_s[d] = spread[:, (2 + d) * wl:(3 + d) * wl]

    ri = lax.broadcasted_iota(jnp.int32, (CHUNK, LANES), 0)
    lj = lax.broadcasted_iota(jnp.int32, (CHUNK, LANES), 1) % 64
    lo_half = lax.broadcasted_iota(jnp.int32, (CHUNK, LANES), 1) < 64
    lo_half2 = (lax.broadcasted_iota(jnp.int32, (CHUNK, 2 * LANES), 1) % LANES) < 64
    diag = ri == lj
    strict = (ri > lj, ri < lj)
    riw = lax.broadcasted_iota(jnp.int32, (CHUNK, wl), 0)
    ljw = lax.broadcasted_iota(jnp.int32, (CHUNK, wl), 1) % 64
    diag_w = riw == ljw
    incl_w = (riw >= ljw, riw <= ljw)
    r64 = lax.broadcasted_iota(jnp.int32, (CHUNK, 3 * CHUNK), 0)
    c64 = lax.broadcasted_iota(jnp.int32, (CHUNK, 3 * CHUNK), 1) % CHUNK
    tri3 = (jnp.where(c64 <= r64, 1.0, 0.0).astype(BF16), jnp.where(c64 >= r64, 1.0, 0.0).astype(BF16))

    off_blk = ([], [])
    for lvl in range(6):
        s = 1 << lvl
        same2 = (ri // (2 * s)) == (lj // (2 * s))
        off_blk[0].append(jnp.logical_and(same2, (ri // s) == (lj // s) + 1))
        off_blk[1].append(jnp.logical_and(same2, (lj // s) == (ri // s) + 1))

    def stack_heads(x, lo_mask):
        return jnp.concatenate([jnp.where(lo_mask, x, 0.0), jnp.where(lo_mask, 0.0, x)], axis=0).astype(BF16)

    def chunk_rows(c):
        return pl.ds(c * CHUNK if isinstance(c, int) else pl.multiple_of(c * CHUNK, CHUNK), CHUNK)

    def chunk_load(c):
        rows = chunk_rows(c)
        return (q_ref[rows, :].astype(F32), k_ref[rows, :].astype(F32), v_ref[rows, :].astype(F32),
                [beta_s[d, rows, :] for d in range(2)], [g_s[d, rows, :] for d in range(2)])

    def group_a(loaded):
        nc = len(loaded)
        cd = [(i, d) for i in range(nc) for d in range(2)]
        units = [(i, h, d) for i in range(nc) for h in range(hps) for d in range(2)]
        qcs = [ld[0] * (DN_DK ** -0.5) for ld in loaded]
        kcs = [ld[1] for ld in loaded]
        vcs = [ld[2] for ld in loaded]
        kt_bd = {}
        for i in range(nc):
            for h in range(hps):
                kc = kcs[i][:, hp_sl[h]]
                ktt = jnp.transpose(jnp.concatenate([kc, kc], axis=0))
                kt_bd[i, h] = jnp.where(bd_mask, ktt, 0.0)
        gc = {(i, d): _bdot(tri3[d], jnp.concatenate(_split3(loaded[i][4][d]), axis=0)) for i, d in cd}
        rrow = {u: jnp.sum(jnp.where(diag_w, gc[u], 0.0), axis=0, keepdims=True) for u in cd}
        dec = {u: jnp.exp(jnp.minimum(gc[u] - rrow[u], 0.0)) for u in cd}
        eg = {u: jnp.exp(gc[u]) for u in cd}
        kbeta = {(i, d): kcs[i] * loaded[i][3][d] for i, d in cd}
        vbeta = {(i, d): vcs[i] * loaded[i][3][d] for i, d in cd}
        kbeg = {u: kbeta[u] * eg[u] for u in cd}
        m1 = {(i, h): _bdot(jnp.concatenate([qcs[i][:, hp_sl[h]], kbeta[i, 0][:, hp_sl[h]],
                                             kbeta[i, 1][:, hp_sl[h]]], axis=0).astype(BF16),
                            kt_bd[i, h].astype(BF16))
              for i in range(nc) for h in range(hps)}
        amat = {(i, h, d): jnp.where(strict[d], m1[i, h][CHUNK * (1 + d):CHUNK * (2 + d)] * dec[i, d][:, hp_sl[h]], 0.0)
                for i, h, d in units}
        o_st = {(i, h, d): stack_heads(jnp.where(off_blk[d][0], amat[i, h, d], 0.0), lo_half) for i, h, d in units}
        tmat = {(i, h, d): jnp.where(diag, 1.0, 0.0) - jnp.where(off_blk[d][0], amat[i, h, d], 0.0) for i, h, d in units}
        prod = {u: _bdot(amat[u].astype(BF16), o_st[u]) for u in units}
        wmat = {u: amat[u] - prod[u] for u in units}
        for lvl in range(1, 5):
            o_st = {(i, h, d): stack_heads(jnp.where(off_blk[d][lvl], wmat[i, h, d], 0.0), lo_half) for i, h, d in units}
            prod = {u: _bdot(jnp.concatenate([tmat[u], wmat[u]], axis=0).astype(BF16), o_st[u]) for u in units}
            tmat = {u: tmat[u] - prod[u][0:CHUNK] for u in units}
            wmat = {u: wmat[u] - prod[u][CHUNK:] for u in units}
        o_st = {(i, h, d): stack_heads(jnp.where(off_blk[d][5], wmat[i, h, d], 0.0), lo_half) for i, h, d in units}
        prod = {u: _bdot(tmat[u].astype(BF16), o_st[u]) for u in units}
        tmat = {u: tmat[u] - prod[u] for u in units}
        x = {}
        for i, h, d in units:
            rhs = jnp.concatenate([vbeta[i, d][:, hp_sl[h]], kbeg[i, d][:, hp_sl[h]]], axis=1)
            x[i, h, d] = _bdot(tmat[i, h, d].astype(BF16), stack_heads(rhs, lo_half2))
        results = []
        for i in range(nc):
            out = []
            for d in range(2):
                u = lanes_cat([x[i, h, d][:, :LANES] for h in range(hps)])
                w = lanes_cat([x[i, h, d][:, LANES:] for h in range(hps)])
                qk = jnp.where(incl_w[d], lanes_cat([m1[i, h][0:CHUNK] for h in range(hps)]) * dec[i, d], 0.0)
                qg = qcs[i] * eg[i, d]
                glast = gc[i, d][CHUNK - 1:CHUNK, :] if d == 0 else gc[i, d][0:1, :]
                kmul = jnp.exp(glast - rrow[i, d])
                kd_bd = lanes_cat([kt_bd[i, h] * kmul[:, hp_sl[h]] for h in range(hps)])
                out.append((u,
                            jnp.concatenate([w, qg], axis=0).astype(BF16),
                            jnp.concatenate([qk, kd_bd], axis=0).astype(BF16),
                            jnp.broadcast_to(jnp.exp(glast), (8, wl))))
            results.append(out)
        return results

    group = min(a_chunks, nt)

    def phase_a(gi, carry):
        cs = [gi * group + j for j in range(group)]
        loaded = [chunk_load(c) for c in cs]
        results = group_a(loaded)
        for c, res in zip(cs, results):
            for d in range(2):
                u_s[d, c], wq_s[d, c], lb_s[d, c], el_s[d, c] = res[d]
        return carry

    if nt // group <= 2:
        for gi in range(nt // group):
            phase_a(gi, 0)
    else:
        lax.fori_loop(0, nt // group, phase_a, 0)

    dh = [(j, d, h) for j in range(nb) for d in range(2) for h in range(hps)]
    if has_s0:
        zero = jnp.zeros((DN_DK, DN_DV), F32)

        def bd_state(j, d, h):
            top = jnp.concatenate([s0_ref[j, 0, d, 2 * h], zero], axis=1)
            bot = jnp.concatenate([zero, s0_ref[j, 0, d, 2 * h + 1]], axis=1)
            return jnp.concatenate([top, bot], axis=0)
        s_init = tuple(bd_state(j, d, h) for j, d, h in dh)
    else:
        s_init = tuple(jnp.zeros((LANES, LANES), F32) for _ in dh)

    def phase_b(s, states):
        jd = [(j, d) for j in range(nb) for d in range(2)]
        cs = {(j, d): j * n + (s if d == 0 else n - 1 - s) for j, d in jd}
        ld = {u: (wq_s[u[1], cs[u]], u_s[u[1], cs[u]], lb_s[u[1], cs[u]], el_s[u[1], cs[u], 0:1, :]) for u in jd}
        st = dict(zip(dh, states))
        ws = {(j, d, h): _bdot(ld[j, d][0][:, hp_sl[h]], st[j, d, h].astype(BF16)) for j, d, h in dh}
        vn = {(j, d, h): ld[j, d][1][:, hp_sl[h]] - ws[j, d, h][0:CHUNK] for j, d, h in dh}
        r2 = {(j, d, h): _bdot(ld[j, d][2][:, hp_sl[h]], stack_heads(vn[j, d, h], lo_half)) for j, d, h in dh}
        new_states = tuple(st[j, d, h] * ld[j, d][3][:, hp_sl[h]] + r2[j, d, h][CHUNK:] for j, d, h in dh)
        for j, d in jd:
            odir_s[d, chunk_rows(cs[j, d]), :] = lanes_cat([ws[j, d, h][CHUNK:] + r2[j, d, h][0:CHUNK]
                                                            for h in range(hps)])
        return new_states

    s_fin = dict(zip(dh, lax.fori_loop(0, n, phase_b, s_init)))

    for h in range(hps):
        o = odir_s[0, :, hp_sl[h]] + odir_s[1, :, hp_sl[h]]
        msq = _dot2_l(o * o, bd_ones) * (1.0 / DN_DV)
        o_ref[:, hp_sl[h]] = (o * lax.rsqrt(msq + EPS) * cst_ref[2:3, :]).astype(BF16)

    if emit_state:
        for j, d, h in dh:
            st_ref[j, 0, d, 2 * h] = s_fin[j, d, h][0:DN_DK, 0:DN_DV]
            st_ref[j, 0, d, 2 * h + 1] = s_fin[j, d, h][DN_DK:, DN_DV:]


def _deltanet(dqkv, ba, cst, batch, seq, nb, hps, a_chunks, s0=None, emit_state=False):
    assert batch % nb == 0
    rows = nb * seq
    nt = rows // CHUNK
    wl = hps * LANES
    groups = N_PAIRS // hps
    has_s0 = s0 is not None

    def col(cbase):
        return pl.BlockSpec((rows, wl), lambda b, hg: (b, cbase // wl + hg))

    st_spec = pl.BlockSpec((nb, 1, 2, 2 * hps, DN_DK, DN_DV), lambda b, hg: (b, 0, 0, hg, 0, 0))
    in_specs = [col(0), col(DN_WIDTH), col(2 * DN_WIDTH),
                pl.BlockSpec((rows, LANES), lambda b, hg: (b, 0)),
                pl.BlockSpec((8, LANES), lambda b, hg: (0, 0))]
    args = [dqkv, dqkv, dqkv, ba, cst]
    if has_s0:
        in_specs.append(st_spec)
        args.append(s0)
    out_specs = [pl.BlockSpec((rows, wl), lambda b, hg: (b, hg))]
    out_shape = [jax.ShapeDtypeStruct((batch * seq, DN_HEADS * DN_DV), BF16)]
    if emit_state:
        out_specs.append(st_spec)
        out_shape.append(jax.ShapeDtypeStruct((batch, 1, 2, DN_HEADS, DN_DK, DN_DV), F32))
    scratch = [pltpu.VMEM((2, rows, wl), F32), pltpu.VMEM((2, rows, wl), F32),
               pltpu.VMEM((2, nt, CHUNK, wl), F32),
               pltpu.VMEM((2, nt, 2 * CHUNK, wl), BF16),
               pltpu.VMEM((2, nt, 3 * CHUNK, wl), BF16),
               pltpu.VMEM((2, nt, 8, wl), F32),
               pltpu.VMEM((2, rows, wl), F32)]
    res = pl.pallas_call(
        functools.partial(_dn_kernel, seq=seq, nb=nb, hps=hps, a_chunks=a_chunks, has_s0=has_s0,
                          emit_state=emit_state),
        grid=(batch // nb, groups),
        in_specs=in_specs,
        out_specs=out_specs,
        out_shape=out_shape,
        scratch_shapes=scratch,
        compiler_params=_params(2),
        name="deltanet_lat" if has_s0 else "deltanet_ctx",
    )(*args)
    return res


TM_F = 2 * TM


def _merge_ffn_kernel(x_ref, ona_ref, odn_ref, gz_ref, mod_ref, wpost1_ref, wpre2_ref, wpost2_ref,
                      wao_ref, wdo_ref, wout_ref, w1_ref, w2_ref, o_ref):
    mod = mod_ref[0]
    subs = [slice(s * TM, (s + 1) * TM) for s in range(TM_F // TM)]
    z_sl, gna_sl, gdn_sl = (slice(0, DN_WIDTH), slice(DN_WIDTH, DN_WIDTH + D_MODEL),
                            slice(DN_WIDTH + D_MODEL, GZ_W))
    odn = [odn_ref[sl, :].astype(F32) * _silu(gz_ref[sl, z_sl].astype(F32)) for sl in subs]
    br_na = [_bdot(ona_ref[sl, :], wao_ref[...]) for sl in subs]
    br_dn = [_bdot(t.astype(BF16), wdo_ref[...]) for t in odn]
    m = [_sigmoid(gz_ref[sl, gna_sl].astype(F32)) * a + _sigmoid(gz_ref[sl, gdn_sl].astype(F32)) * b
         for sl, a, b in zip(subs, br_na, br_dn)]
    y = [_bdot(t.astype(BF16), wout_ref[...]) for t in m]
    x1 = [x_ref[sl, :] + mod[2:3] * _rms_rows(t, wpost1_ref[...]) for sl, t in zip(subs, y)]
    h = [_rms_rows(t, wpre2_ref[...]) * (1.0 + mod[4:5]) + mod[3:4] for t in x1]
    f1 = [_bdot(t.astype(BF16), w1_ref[...]) for t in h]
    r = [jnp.maximum(t, 0.0) for t in f1]
    f = [_bdot((t * t).astype(BF16), w2_ref[...]) for t in r]
    for sl, a, b in zip(subs, x1, f):
        o_ref[sl, :] = a + mod[5:6] * _rms_rows(b, wpost2_ref[...])


def _merge_ffn(x2d, o_na, o_dn, gz, mod3, mod_row_of_tile, wpost1, wpre2, wpost2, w_ao, w_do, w_out, w1, w2):
    n_tok = x2d.shape[0]
    row = lambda i: (i, 0)
    const = lambda i: (0, 0)

    def resident(shape):
        return pl.BlockSpec(shape, const, pipeline_mode=pl.Buffered(1))

    return pl.pallas_call(
        _merge_ffn_kernel,
        grid=(n_tok // TM_F,),
        in_specs=[pl.BlockSpec((TM_F, D_MODEL), row),
                  pl.BlockSpec((TM_F, NA_WIDTH), row),
                  pl.BlockSpec((TM_F, DN_WIDTH), row),
                  pl.BlockSpec((TM_F, GZ_W), row),
                  pl.BlockSpec((1, N_MOD, D_MODEL), lambda i: (mod_row_of_tile(i), 0, 0)),
                  pl.BlockSpec((1, D_MODEL), const),
                  pl.BlockSpec((1, D_MODEL), const),
                  pl.BlockSpec((1, D_MODEL), const),
                  resident((NA_WIDTH, D_MODEL)),
                  resident((DN_WIDTH, D_MODEL)),
                  resident((D_MODEL, D_MODEL)),
                  resident((D_MODEL, D_FF)),
                  resident((D_FF, D_MODEL))],
        out_specs=pl.BlockSpec((TM_F, D_MODEL), row),
        out_shape=jax.ShapeDtypeStruct((n_tok, D_MODEL), F32),
        compiler_params=_params(1),
        name="merge_ffn",
    )(x2d, o_na, o_dn, gz, mod3, wpost1, wpre2, wpost2, w_ao, w_do, w_out, w1, w2)


def _pack_w_in(w):
    assert w.shape[1] == IN_COLS
    wb = w.astype(BF16)
    return wb, jnp.pad(wb[:, MAIN_COLS:], ((0, 0), (0, LANES - (IN_COLS - MAIN_COLS))))


def kernel(x_prompt, x_sample, c, cache_na_k, cache_na_v, state_delta, c_ctx, w_ada, b_ada, norm_pre1, norm_post1, norm_pre2, norm_post2, w_in, conv_w, a_log, dt_bias, dn_norm, na_rpb, w_ao, w_do, w_out, w_ff1, w_ff2):
    batch, seq, _ = x_prompt.shape
    dec_batch, dec_seq, _ = x_sample.shape
    depth = w_in.shape[0]
    assert depth == 1 and seq == TM and dec_seq % TM_F == 0 and (batch * seq) % TM_F == 0 and dec_batch < 8

    xp = x_prompt.reshape(batch * seq, D_MODEL)
    xs = x_sample.reshape(dec_batch * dec_seq, D_MODEL)
    l = 0

    cvecs = jnp.concatenate([c, c_ctx[None], jnp.zeros((8 - dec_batch - 1, D_MODEL), F32)], axis=0)
    mod3 = _adaln(cvecs, w_ada[l], b_ada[l]).reshape(8, N_MOD, D_MODEL)
    ctx_row = lambda i: dec_batch
    lat_row = lambda i: i // (dec_seq // TMI)

    w_in_p = _pack_w_in(w_in[l])
    w_ao_b = w_ao[l].astype(BF16)
    w_do_b = w_do[l].astype(BF16)
    w_out_b = w_out[l].astype(BF16)
    w1_b = w_ff1[l].astype(BF16)
    w2_b = w_ff2[l].astype(BF16)
    wpre1 = norm_pre1[l].reshape(1, D_MODEL)
    wpost1 = norm_post1[l].reshape(1, D_MODEL)
    wpre2 = norm_pre2[l].reshape(1, D_MODEL)
    wpost2 = norm_post2[l].reshape(1, D_MODEL)

    conv_w8 = jnp.pad(conv_w[l], ((0, 8 - CONV_W), (0, 0)))
    gate_rows = jnp.pad(jnp.stack([a_log[l].reshape(-1), dt_bias[l].reshape(-1)]),
                        ((0, 0), (2 * DN_HEADS, LANES - 4 * DN_HEADS)))
    cst = jnp.concatenate([gate_rows, jnp.tile(dn_norm[l], 2)[None], jnp.zeros((5, LANES), F32)], axis=0)

    qkv_c, dqkv_c, gz_c, ba_c, new_k, new_v = _inproj(xp, mod3, ctx_row, wpre1, w_in_p, conv_w8, seq, batch)
    o_na_c = _ctx_attn(qkv_c, batch, seq, nb=2)
    o_dn_c, new_s = _deltanet(dqkv_c, ba_c, cst, batch, seq, nb=2, hps=4, a_chunks=4, emit_state=True)
    y_c = _merge_ffn(xp, o_na_c, o_dn_c, gz_c, mod3, ctx_row, wpost1, wpre2, wpost2,
                     w_ao_b, w_do_b, w_out_b, w1_b, w2_b)

    qkv_l, dqkv_l, gz_l, ba_l = _inproj(xs, mod3, lat_row, wpre1, w_in_p, conv_w8, dec_seq, dec_batch,
                                        rope_tabs=_rope_tables(dec_seq))
    past = cache_na_k.shape[3]
    ck_tm = jnp.transpose(cache_na_k[:, l], (0, 2, 1, 3)).reshape(dec_batch, past, NA_WIDTH)
    cv_tm = jnp.transpose(cache_na_v[:, l], (0, 2, 1, 3)).reshape(dec_batch, past, NA_WIDTH)
    o_na_l = _lat_attn(qkv_l, ck_tm, cv_tm, na_rpb[l], dec_batch, dec_seq)
    (o_dn_l,) = _deltanet(dqkv_l, ba_l, cst, dec_batch, dec_seq, nb=1, hps=4, a_chunks=2, s0=state_delta)
    y_l = _merge_ffn(xs, o_na_l, o_dn_l, gz_l, mod3, lambda i: i // (dec_seq // TM_F), wpost1, wpre2, wpost2,
                     w_ao_b, w_do_b, w_out_b, w1_b, w2_b)

    return (y_c.reshape(batch, seq, D_MODEL), y_l.reshape(dec_batch, dec_seq, D_MODEL),
            new_k, new_v, new_s)
```

```python
import functools

import numpy as np
import jax
import jax.numpy as jnp
from jax import lax
from jax.experimental import pallas as pl
from jax.experimental.pallas import tpu as pltpu

F32 = jnp.float32
BF16 = jnp.bfloat16

D_MODEL = 1024
N_MOD = 6
GRID_W = 64
NA_HEADS = 8
HEAD_DIM = 64
NA_WIDTH = NA_HEADS * HEAD_DIM
WIN_R = 8
WIN_C = 16
ROPE_BASE = 10000.0
DN_HEADS = 8
DN_DK = 64
DN_DV = 64
DN_WIDTH = DN_HEADS * DN_DK
CONV_W = 5
CHUNK = 64
D_FF = 4 * D_MODEL
EPS = 1e-6
NEG_INF = -1e30

LANES = 128
SUBLANES = 8
HALF = LANES // 2
assert HEAD_DIM == HALF and DN_DK == HALF and DN_DV == HALF and CHUNK == HALF
N_PAIRS = NA_HEADS // 2
TM = 256
TMI = 2 * TM
VMEM_LIMIT = 56 * 1024 * 1024

QKV_W = 3 * NA_WIDTH
DQKV_W = 3 * DN_WIDTH
BA_W = 4 * DN_HEADS
GZ_W = DN_WIDTH + 2 * D_MODEL
COL_QKV, COL_DQKV, COL_TAIL = 0, QKV_W, QKV_W + DQKV_W
IN_COLS = COL_TAIL + BA_W + GZ_W
MAIN_COLS = (IN_COLS // LANES) * LANES
TAIL_W = MAIN_COLS - COL_TAIL + LANES


def _params(n_grid):
    return pltpu.CompilerParams(dimension_semantics=("arbitrary",) * n_grid,
                                vmem_limit_bytes=VMEM_LIMIT)


def _bdot(a, b):
    return jnp.dot(a, b, preferred_element_type=F32)


def _split3(a):
    hi = a.astype(BF16)
    r1 = a - hi.astype(F32)
    mid = r1.astype(BF16)
    lo = (r1 - mid.astype(F32)).astype(BF16)
    return hi, mid, lo


def _dot2_l(a, b_exact):
    hi = a.astype(BF16)
    lo = (a - hi.astype(F32)).astype(BF16)
    return _bdot(jnp.concatenate([hi, lo], axis=1), jnp.concatenate([b_exact, b_exact], axis=0))


def _sigmoid(x):
    return 0.5 * jnp.tanh(0.5 * x) + 0.5


def _silu(x):
    return x * _sigmoid(x)


def _softplus(x):
    return jnp.maximum(x, 0.0) + jnp.log(1.0 + jnp.exp(-jnp.abs(x)))


def _rms_rows(x, w):
    ms = jnp.mean(x * x, axis=-1, keepdims=True)
    return x * lax.rsqrt(ms + EPS) * w


def _pair_block_ones():
    r = lax.broadcasted_iota(jnp.int32, (LANES, LANES), 0)
    c = lax.broadcasted_iota(jnp.int32, (LANES, LANES), 1)
    return jnp.where((r // HALF) == (c // HALF), 1.0, 0.0).astype(BF16)


def _adaln_kernel(c_ref, w_ref, b_ref, o_ref):
    c = c_ref[...]
    s = _silu(c).astype(BF16)
    o_ref[...] = _bdot(s, w_ref[...].astype(BF16)) + b_ref[...]


def _adaln(cvecs, w_ada, b_ada):
    tn = 1024
    n = w_ada.shape[1]
    return pl.pallas_call(
        _adaln_kernel,
        grid=(n // tn,),
        in_specs=[pl.BlockSpec((SUBLANES, D_MODEL), lambda j: (0, 0)),
                  pl.BlockSpec((D_MODEL, tn), lambda j: (0, j)),
                  pl.BlockSpec((1, tn), lambda j: (0, j))],
        out_specs=pl.BlockSpec((SUBLANES, tn), lambda j: (0, j)),
        out_shape=jax.ShapeDtypeStruct((SUBLANES, n), F32),
        compiler_params=_params(1),
        name="adaln",
    )(cvecs, w_ada, b_ada.reshape(1, n))


def _inproj_kernel(*refs, latent, tiles_per_seq):
    it = iter(refs)
    x_ref, mod_ref, wpre_ref, w_ref, wlast_ref, cw_ref = (next(it), next(it), next(it), next(it), next(it),
                                                          next(it))
    if latent:
        xprev_ref, xnext_ref, cos_ref, sin_ref = next(it), next(it), next(it), next(it)
    qkv_ref, dqkv_ref, gz_ref, ba_ref = next(it), next(it), next(it), next(it)
    if not latent:
        kc_ref, vc_ref = next(it), next(it)

    mod = mod_ref[0]
    n_sub = TMI // TM
    bd_ones = _pair_block_ones()

    def modulated(x):
        return (_rms_rows(x, wpre_ref[...]) * (1.0 + mod[1:2]) + mod[0:1]).astype(BF16)

    for sub in range(n_sub):
        rs = slice(sub * TM, (sub + 1) * TM)
        hb = modulated(x_ref[rs, :])

        w_dn = w_ref[:, COL_DQKV:COL_DQKV + DQKV_W]
        halo_zero = jnp.zeros((SUBLANES, DQKV_W), F32)
        if latent:
            t = (pl.program_id(0) * n_sub + sub) % tiles_per_seq
            prev = x_ref[sub * TM - SUBLANES:sub * TM, :] if sub > 0 else xprev_ref[...]
            nxt = x_ref[(sub + 1) * TM:(sub + 1) * TM + SUBLANES, :] if sub < n_sub - 1 else xnext_ref[...]
            hb_ext = jnp.concatenate([modulated(prev), hb, modulated(nxt)], axis=0)
            pe = _bdot(hb_ext, w_dn)
            pe = jnp.concatenate([jnp.where(t > 0, pe[:SUBLANES], halo_zero), pe[SUBLANES:SUBLANES + TM],
                                  jnp.where(t < tiles_per_seq - 1, pe[SUBLANES + TM:], halo_zero)], axis=0)
        else:
            pe = jnp.concatenate([halo_zero, _bdot(hb, w_dn), halo_zero], axis=0)
        acc = None
        for j in sorted(range(CONV_W), key=lambda s: abs(s - CONV_W // 2)):
            lo = SUBLANES - CONV_W // 2 + j
            term = pe[lo:lo + TM] * cw_ref[j:j + 1, :]
            acc = term if acc is None else acc + term
        y = _silu(acc)

        p = _bdot(hb, w_ref[:, COL_QKV:COL_QKV + QKV_W])
        if latent:
            qk = p[:, :2 * NA_WIDTH]
            lane = lax.broadcasted_iota(jnp.int32, qk.shape, 1)
            nf = HEAD_DIM // 4
            first = (lane % (2 * nf)) < nf
            partner = jnp.where(first, pltpu.roll(qk, 2 * NA_WIDTH - nf, 1), pltpu.roll(qk, nf, 1))
            qkv_ref[rs, :2 * NA_WIDTH] = (qk * cos_ref[rs, :] + partner * sin_ref[rs, :]).astype(BF16)
            qkv_ref[rs, 2 * NA_WIDTH:] = p[:, 2 * NA_WIDTH:].astype(BF16)
        else:
            qkv_ref[rs, :] = p.astype(BF16)
            for h in range(NA_HEADS):
                kc_ref[sub, 0, h] = p[:, NA_WIDTH + h * HEAD_DIM:NA_WIDTH + (h + 1) * HEAD_DIM]
                vc_ref[sub, 0, h] = p[:, 2 * NA_WIDTH + h * HEAD_DIM:2 * NA_WIDTH + (h + 1) * HEAD_DIM]

        split = COL_TAIL + ((MAIN_COLS - COL_TAIL) // LANES // 2) * LANES
        tail = jnp.concatenate([_bdot(hb, w_ref[:, COL_TAIL:split]), _bdot(hb, w_ref[:, split:MAIN_COLS]),
                                _bdot(hb, wlast_ref[...])], axis=1)
        ba_ref[rs, :] = tail[:, :LANES]
        gz_ref[rs, :] = pltpu.roll(tail, TAIL_W - BA_W, 1)[:, :GZ_W].astype(BF16)

        for s in range(DQKV_W // LANES):
            ys = y[:, s * LANES:(s + 1) * LANES]
            if s < 2 * DN_WIDTH // LANES:
                ys = ys * lax.rsqrt(_dot2_l(ys * ys, bd_ones) + EPS)
            dqkv_ref[rs, s * LANES:(s + 1) * LANES] = ys.astype(BF16)


def _inproj(x2d, mod3, mod_row_of_tile, wpre, w_packed, conv_w8, seq, batch, rope_tabs=None):
    n_tok = x2d.shape[0]
    latent = rope_tabs is not None
    assert n_tok % TMI == 0 and seq % TM == 0 and (seq % TMI == 0 or TMI % seq == 0)
    rows8 = TMI // SUBLANES
    n_blk8 = n_tok // SUBLANES

    def resident(shape):
        return pl.BlockSpec(shape, lambda i: (0, 0), pipeline_mode=pl.Buffered(1))

    in_specs = [pl.BlockSpec((TMI, D_MODEL), lambda i: (i, 0)),
                pl.BlockSpec((1, N_MOD, D_MODEL), lambda i: (mod_row_of_tile(i), 0, 0)),
                pl.BlockSpec((1, D_MODEL), lambda i: (0, 0)),
                resident((D_MODEL, MAIN_COLS)),
                resident((D_MODEL, LANES)),
                pl.BlockSpec((SUBLANES, DQKV_W), lambda i: (0, 0))]
    w_main, w_last = w_packed
    args = [x2d, mod3, wpre, w_main, w_last, conv_w8]
    out_specs = [pl.BlockSpec((TMI, QKV_W), lambda i: (i, 0)),
                 pl.BlockSpec((TMI, DQKV_W), lambda i: (i, 0)),
                 pl.BlockSpec((TMI, GZ_W), lambda i: (i, 0)),
                 pl.BlockSpec((TMI, LANES), lambda i: (i, 0))]
    out_shape = [jax.ShapeDtypeStruct((n_tok, QKV_W), BF16),
                 jax.ShapeDtypeStruct((n_tok, DQKV_W), BF16),
                 jax.ShapeDtypeStruct((n_tok, GZ_W), BF16),
                 jax.ShapeDtypeStruct((n_tok, LANES), F32)]
    if latent:
        steps_per_seq = seq // TMI
        in_specs += [pl.BlockSpec((SUBLANES, D_MODEL), lambda i: (jnp.maximum(i * rows8 - 1, 0), 0)),
                     pl.BlockSpec((SUBLANES, D_MODEL), lambda i: (jnp.minimum((i + 1) * rows8, n_blk8 - 1), 0))]
        args += [x2d, x2d]
        for tab in rope_tabs:
            in_specs.append(pl.BlockSpec((TMI, 2 * NA_WIDTH), lambda i: (i % steps_per_seq, 0)))
            args.append(tab)
    else:
        assert seq == TM
        seqs = TMI // seq
        cache_spec = pl.BlockSpec((seqs, 1, NA_HEADS, seq, HEAD_DIM), lambda i: (i, 0, 0, 0, 0))
        cache_shape = jax.ShapeDtypeStruct((batch, 1, NA_HEADS, seq, HEAD_DIM), F32)
        out_specs += [cache_spec, cache_spec]
        out_shape += [cache_shape, cache_shape]
    return pl.pallas_call(
        functools.partial(_inproj_kernel, latent=latent, tiles_per_seq=seq // TM),
        grid=(n_tok // TMI,),
        in_specs=in_specs,
        out_specs=out_specs,
        out_shape=out_shape,
        compiler_params=_params(1),
        name="inproj_lat" if latent else "inproj_ctx",
    )(*args)


def _rope_tables(seq):
    half = HEAD_DIM // 2
    nf = half // 2
    pos = np.arange(seq)
    inv = (np.float32(ROPE_BASE) ** (-np.arange(nf, dtype=np.float32) / np.float32(nf))).astype(np.float32)
    ang_r = ((pos // GRID_W).astype(np.float32)[:, None] * inv).astype(np.float64)
    ang_c = ((pos % GRID_W).astype(np.float32)[:, None] * inv).astype(np.float64)

    def grp(ang):
        return (np.concatenate([np.cos(ang), np.cos(ang)], 1),
                np.concatenate([-np.sin(ang), np.sin(ang)], 1))

    cr, sr = grp(ang_r)
    cc, sc = grp(ang_c)
    cos_h = np.concatenate([cr, cc], 1)
    sin_h = np.concatenate([sr, sc], 1)
    reps = 2 * NA_HEADS
    return (jnp.asarray(np.tile(cos_h, (1, reps)), F32),
            jnp.asarray(np.tile(sin_h, (1, reps)), F32))


def _ctx_attn_kernel(q_ref, k_ref, v_ref, o_ref, *, seq, nb):
    lane = lax.broadcasted_iota(jnp.int32, (1, LANES), 1)
    scale = HEAD_DIM ** -0.5
    pairs = [(j, hp) for j in range(nb) for hp in range(N_PAIRS)]
    heads = [(j, hp, e) for j, hp in pairs for e in range(2)]
    rows = [slice(j * seq, (j + 1) * seq) for j in range(nb)]
    sls = [slice(hp * LANES, (hp + 1) * LANES) for hp in range(N_PAIRS)]
    q = {(j, hp): q_ref[rows[j], sls[hp]] for j, hp in pairs}
    k = {(j, hp): k_ref[rows[j], sls[hp]] for j, hp in pairs}
    v = {(j, hp): v_ref[rows[j], sls[hp]] for j, hp in pairs}
    m_e = (lane < HALF, lane >= HALF)
    zero = jnp.zeros((), BF16)
    s = {(j, hp, e): lax.dot_general(jnp.where(m_e[e], q[j, hp] * scale, zero), k[j, hp],
                                     (((1,), (1,)), ((), ())), preferred_element_type=F32) for j, hp, e in heads}
    p = {u: jnp.exp(s[u] - jnp.max(s[u], axis=-1, keepdims=True)) for u in heads}
    l = {u: jnp.sum(p[u], axis=-1, keepdims=True) for u in heads}
    o = {(j, hp, e): _bdot(p[j, hp, e].astype(BF16), jnp.where(m_e[e], v[j, hp], zero)) for j, hp, e in heads}
    for j, hp in pairs:
        o_ref[rows[j], sls[hp]] = (o[j, hp, 0] / l[j, hp, 0] + o[j, hp, 1] / l[j, hp, 1]).astype(BF16)


def _ctx_attn(qkv, batch, seq, nb):
    assert batch % nb == 0

    def col(cb):
        return pl.BlockSpec((nb * seq, NA_WIDTH), lambda b: (b, cb))
    return pl.pallas_call(
        functools.partial(_ctx_attn_kernel, seq=seq, nb=nb),
        grid=(batch // nb,),
        in_specs=[col(0), col(1), col(2)],
        out_specs=pl.BlockSpec((nb * seq, NA_WIDTH), lambda b: (b, 0)),
        out_shape=jax.ShapeDtypeStruct((batch * seq, NA_WIDTH), BF16),
        compiler_params=_params(1),
        name="ctx_attn",
    )(qkv, qkv, qkv)


Q_HALF = 512
K_WIN = 768
LAT_SUB = 256
LAT_GROUP = 1


def _lat_attn_kernel(rpb_ref, q_ref, k_ref, v_ref, ck_ref, cv_ref, o_ref, bias_ref, *, rows):
    hp = pl.program_id(0)
    b = pl.program_id(1)
    half = pl.program_id(2)
    kr_n = min(WIN_R, rows)
    rows_half = Q_HALF // GRID_W
    krows_win = K_WIN // GRID_W
    n_dr = 2 * WIN_R - 1
    n_dc = 2 * WIN_C - 1

    @pl.when(jnp.logical_and(b == 0, half == 0))
    def _build_bias():
        qc = lax.broadcasted_iota(jnp.int32, (GRID_W, LANES), 0)
        ln = lax.broadcasted_iota(jnp.int32, (GRID_W, LANES), 1)
        kc = ln % GRID_W
        cs = jnp.clip(qc - WIN_C // 2, 0, GRID_W - WIN_C)
        valid = jnp.logical_and(kc >= cs, kc < cs + WIN_C)
        dc = jnp.clip(kc - qc + (WIN_C - 1), 0, n_dc - 1)
        dcv = jnp.where(valid, dc, -1)
        neg = jnp.full((GRID_W, LANES), NEG_INF, F32)
        for e in range(2):
            blocks = []
            for a in range(n_dr):
                blk = neg
                for j in range(n_dc):
                    blk = jnp.where(dcv == j, rpb_ref[((2 * hp + e) * n_dr + a) * n_dc + j], blk)
                blocks.append(blk)
            for hf in range(2):
                for rl in range(rows_half):
                    r = hf * rows_half + rl
                    rs = min(max(r - kr_n // 2, 0), rows - kr_n)
                    for kp in range(krows_win // 2):
                        parts = []
                        for kl in (2 * kp, 2 * kp + 1):
                            kr = kl + hf * (rows - krows_win)
                            parts.append(blocks[kr - r + WIN_R - 1] if rs <= kr < rs + kr_n else neg)
                        blk = jnp.where(ln < GRID_W, parts[0], parts[1])
                        bias_ref[e, hf, rl * GRID_W:(rl + 1) * GRID_W, kp * LANES:(kp + 1) * LANES] = blk

    lane = lax.broadcasted_iota(jnp.int32, (1, LANES), 1)
    scale = HEAD_DIM ** -0.5
    zero = jnp.zeros((), BF16)
    k0 = pl.multiple_of(half * ((rows - krows_win) * GRID_W), GRID_W)
    kw = k_ref[pl.ds(k0, K_WIN), :]
    vw = v_ref[pl.ds(k0, K_WIN), :]
    ck = ck_ref[0].astype(BF16)
    cv = cv_ref[0].astype(BF16)
    m_e = (lane < HALF, lane >= HALF)
    ve = [jnp.where(m, vw, zero) for m in m_e]
    cve = [jnp.where(m, cv, zero) for m in m_e]
    nt = (((1,), (1,)), ((), ()))
    for g in range(Q_HALF // (LAT_SUB * LAT_GROUP)):
        units = [(qs, e) for qs in range(g * LAT_GROUP, (g + 1) * LAT_GROUP) for e in range(2)]
        rows_of = {qs: slice(qs * LAT_SUB, (qs + 1) * LAT_SUB) for qs, _ in units}
        qe = {(qs, e): jnp.where(m_e[e], q_ref[rows_of[qs], :] * scale, zero) for qs, e in units}
        s_loc = {(qs, e): lax.dot_general(qe[qs, e], kw, nt, preferred_element_type=F32)
                 + bias_ref[e, half, rows_of[qs], :] for qs, e in units}
        s_ctx = {u: lax.dot_general(qe[u], ck, nt, preferred_element_type=F32) for u in units}
        m = {u: jnp.maximum(jnp.max(s_loc[u], axis=-1, keepdims=True), jnp.max(s_ctx[u], axis=-1, keepdims=True))
             for u in units}
        p_loc = {u: jnp.exp(s_loc[u] - m[u]) for u in units}
        p_ctx = {u: jnp.exp(s_ctx[u] - m[u]) for u in units}
        l = {u: jnp.sum(p_loc[u], axis=-1, keepdims=True) + jnp.sum(p_ctx[u], axis=-1, keepdims=True)
             for u in units}
        o = {(qs, e): (_bdot(p_loc[qs, e].astype(BF16), ve[e]) + _bdot(p_ctx[qs, e].astype(BF16), cve[e]))
             / l[qs, e] for qs, e in units}
        for qs in rows_of:
            o_ref[rows_of[qs], :] = (o[qs, 0] + o[qs, 1]).astype(BF16)


def _lat_attn(qkv, ck_tm, cv_tm, rpb, batch, seq):
    rows = seq // GRID_W
    assert rows * GRID_W == seq and rows - K_WIN // GRID_W == 4 and Q_HALF * 2 == seq
    halves = seq // Q_HALF
    past = ck_tm.shape[1]
    kernel = functools.partial(_lat_attn_kernel, rows=rows)
    return pl.pallas_call(
        kernel,
        grid=(N_PAIRS, batch, halves),
        in_specs=[pl.BlockSpec(memory_space=pltpu.SMEM),
                  pl.BlockSpec((Q_HALF, LANES), lambda hp, b, hf: (b * halves + hf, hp)),
                  pl.BlockSpec((seq, LANES), lambda hp, b, hf: (b, N_PAIRS + hp)),
                  pl.BlockSpec((seq, LANES), lambda hp, b, hf: (b, 2 * N_PAIRS + hp)),
                  pl.BlockSpec((1, past, LANES), lambda hp, b, hf: (b, 0, hp)),
                  pl.BlockSpec((1, past, LANES), lambda hp, b, hf: (b, 0, hp))],
        out_specs=pl.BlockSpec((Q_HALF, LANES), lambda hp, b, hf: (b * halves + hf, hp)),
        out_shape=jax.ShapeDtypeStruct((batch * seq, NA_WIDTH), BF16),
        scratch_shapes=[pltpu.VMEM((2, 2, Q_HALF, K_WIN), F32)],
        compiler_params=_params(3),
        name="lat_attn",
    )(rpb.reshape(-1), qkv, qkv, qkv, ck_tm, cv_tm)


def _dn_kernel(*refs, seq, nb, hps, a_chunks, has_s0, emit_state):
    it = iter(refs)
    q_ref, k_ref, v_ref, ba_ref, cst_ref = next(it), next(it), next(it), next(it), next(it)
    s0_ref = next(it) if has_s0 else None
    o_ref = next(it)
    st_ref = next(it) if emit_state else None
    (beta_s, g_s, u_s, wq_s, lb_s, el_s, odir_s) = it

    hg = pl.program_id(1)
    wl = hps * LANES
    n = seq // CHUNK
    nt = nb * n
    hp_sl = [slice(h * LANES, (h + 1) * LANES) for h in range(hps)]

    def lanes_cat(parts):
        return parts[0] if len(parts) == 1 else jnp.concatenate(parts, axis=1)

    r128 = lax.broadcasted_iota(jnp.int32, (LANES, LANES), 0)
    c128 = lax.broadcasted_iota(jnp.int32, (LANES, LANES), 1)
    bd_mask = (r128 // HALF) == (c128 // HALF)
    bd_ones = _pair_block_ones()

    ba = ba_ref[...]
    lane_g = lax.broadcasted_iota(jnp.int32, (1, LANES), 1)
    gates = jnp.where(lane_g < 2 * DN_HEADS, _sigmoid(ba),
                      -jnp.exp(cst_ref[0:1, :]) * _softplus(ba + cst_ref[1:2, :]))
    hi, mid, lo = _split3(jnp.where(lane_g < BA_W, gates, 0.0))
    packed = (hi.astype(F32) + pltpu.roll(mid.astype(F32), BA_W, 1)
              + pltpu.roll(lo.astype(F32), 2 * BA_W, 1)).astype(BF16)
    src = lax.broadcasted_iota(jnp.int32, (LANES, 4 * wl), 0)
    col = lax.broadcasted_iota(jnp.int32, (LANES, 4 * wl), 1)
    want = DN_HEADS * (col // wl) + 2 * hps * hg + (col % wl) // HALF
    sel = jnp.where(jnp.logical_and(src < 3 * BA_W, src % BA_W == want), 1.0, 0.0).astype(BF16)
    spread = _bdot(packed, sel)
    for d in range(2):
        beta_s[d] = spread[:, d * wl:(d + 1) * wl]
        g_s[d] = spread[:, (2 + d) * wl:(3 + d) * wl]

    ri = lax.broadcasted_iota(jnp.int32, (CHUNK, LANES), 0)
    lj = lax.broadcasted_iota(jnp.int32, (CHUNK, LANES), 1) % HALF
    lo_half = lax.broadcasted_iota(jnp.int32, (CHUNK, LANES), 1) < HALF
    lo_half2 = (lax.broadcasted_iota(jnp.int32, (CHUNK, 2 * LANES), 1) % LANES) < HALF
    diag = ri == lj
    strict = (ri > lj, ri < lj)
    riw = lax.broadcasted_iota(jnp.int32, (CHUNK, wl), 0)
    ljw = lax.broadcasted_iota(jnp.int32, (CHUNK, wl), 1) % HALF
    diag_w = riw == ljw
    incl_w = (riw >= ljw, riw <= ljw)
    r64 = lax.broadcasted_iota(jnp.int32, (CHUNK, 3 * CHUNK), 0)
    c64 = lax.broadcasted_iota(jnp.int32, (CHUNK, 3 * CHUNK), 1) % CHUNK
    tri3 = (jnp.where(c64 <= r64, 1.0, 0.0).astype(BF16), jnp.where(c64 >= r64, 1.0, 0.0).astype(BF16))

    off_blk = ([], [])
    n_lvl = CHUNK.bit_length() - 1
    for lvl in range(n_lvl):
        s = 1 << lvl
        same2 = (ri // (2 * s)) == (lj // (2 * s))
        off_blk[0].append(jnp.logical_and(same2, (ri // s) == (lj // s) + 1))
        off_blk[1].append(jnp.logical_and(same2, (lj // s) == (ri // s) + 1))

    def stack_heads(x, lo_mask):
        return jnp.concatenate([jnp.where(lo_mask, x, 0.0), jnp.where(lo_mask, 0.0, x)], axis=0).astype(BF16)

    def chunk_rows(c):
        return pl.ds(c * CHUNK if isinstance(c, int) else pl.multiple_of(c * CHUNK, CHUNK), CHUNK)

    def chunk_load(c):
        rows = chunk_rows(c)
        return (q_ref[rows, :].astype(F32), k_ref[rows, :].astype(F32), v_ref[rows, :].astype(F32),
                [beta_s[d, rows, :] for d in range(2)], [g_s[d, rows, :] for d in range(2)])

    def group_a(loaded):
        nc = len(loaded)
        cd = [(i, d) for i in range(nc) for d in range(2)]
        units = [(i, h, d) for i in range(nc) for h in range(hps) for d in range(2)]
        qcs = [ld[0] * (DN_DK ** -0.5) for ld in loaded]
        kcs = [ld[1] for ld in loaded]
        vcs = [ld[2] for ld in loaded]
        kt_bd = {}
        for i in range(nc):
            for h in range(hps):
                kc = kcs[i][:, hp_sl[h]]
                ktt = jnp.transpose(jnp.concatenate([kc, kc], axis=0))
                kt_bd[i, h] = jnp.where(bd_mask, ktt, 0.0)
        gc = {(i, d): _bdot(tri3[d], jnp.concatenate(_split3(loaded[i][4][d]), axis=0)) for i, d in cd}
        rrow = {u: jnp.sum(jnp.where(diag_w, gc[u], 0.0), axis=0, keepdims=True) for u in cd}
        dec = {u: jnp.exp(jnp.minimum(gc[u] - rrow[u], 0.0)) for u in cd}
        eg = {u: jnp.exp(gc[u]) for u in cd}
        kbeta = {(i, d): kcs[i] * loaded[i][3][d] for i, d in cd}
        vbeta = {(i, d): vcs[i] * loaded[i][3][d] for i, d in cd}
        kbeg = {u: kbeta[u] * eg[u] for u in cd}
        m1 = {(i, h): _bdot(jnp.concatenate([qcs[i][:, hp_sl[h]], kbeta[i, 0][:, hp_sl[h]],
                                             kbeta[i, 1][:, hp_sl[h]]], axis=0).astype(BF16),
                            kt_bd[i, h].astype(BF16))
              for i in range(nc) for h in range(hps)}
        amat = {(i, h, d): jnp.where(strict[d], m1[i, h][CHUNK * (1 + d):CHUNK * (2 + d)] * dec[i, d][:, hp_sl[h]], 0.0)
                for i, h, d in units}
        o_st = {(i, h, d): stack_heads(jnp.where(off_blk[d][0], amat[i, h, d], 0.0), lo_half) for i, h, d in units}
        tmat = {(i, h, d): jnp.where(diag, 1.0, 0.0) - jnp.where(off_blk[d][0], amat[i, h, d], 0.0) for i, h, d in units}
        prod = {u: _bdot(amat[u].astype(BF16), o_st[u]) for u in units}
        wmat = {u: amat[u] - prod[u] for u in units}
        for lvl in range(1, n_lvl - 1):
            o_st = {(i, h, d): stack_heads(jnp.where(off_blk[d][lvl], wmat[i, h, d], 0.0), lo_half) for i, h, d in units}
            prod = {u: _bdot(jnp.concatenate([tmat[u], wmat[u]], axis=0).astype(BF16), o_st[u]) for u in units}
            tmat = {u: tmat[u] - prod[u][0:CHUNK] for u in units}
            wmat = {u: wmat[u] - prod[u][CHUNK:] for u in units}
        o_st = {(i, h, d): stack_heads(jnp.where(off_blk[d][n_lvl - 1], wmat[i, h, d], 0.0), lo_half)
                for i, h, d in units}
        prod = {u: _bdot(tmat[u].astype(BF16), o_st[u]) for u in units}
        tmat = {u: tmat[u] - prod[u] for u in units}
        x = {}
        for i, h, d in units:
            rhs = jnp.concatenate([vbeta[i, d][:, hp_sl[h]], kbeg[i, d][:, hp_sl[h]]], axis=1)
            x[i, h, d] = _bdot(tmat[i, h, d].astype(BF16), stack_heads(rhs, lo_half2))
        results = []
        for i in range(nc):
            out = []
            for d in range(2):
                u = lanes_cat([x[i, h, d][:, :LANES] for h in range(hps)])
                w = lanes_cat([x[i, h, d][:, LANES:] for h in range(hps)])
                qk = jnp.where(incl_w[d], lanes_cat([m1[i, h][0:CHUNK] for h in range(hps)]) * dec[i, d], 0.0)
                qg = qcs[i] * eg[i, d]
                glast = gc[i, d][CHUNK - 1:CHUNK, :] if d == 0 else gc[i, d][0:1, :]
                kmul = jnp.exp(glast - rrow[i, d])
                kd_bd = lanes_cat([kt_bd[i, h] * kmul[:, hp_sl[h]] for h in range(hps)])
                out.append((u,
                            jnp.concatenate([w, qg], axis=0).astype(BF16),
                            jnp.concatenate([qk, kd_bd], axis=0).astype(BF16),
                            jnp.broadcast_to(jnp.exp(glast), (SUBLANES, wl))))
            results.append(out)
        return results

    group = min(a_chunks, nt)

    def phase_a(gi, carry):
        cs = [gi * group + j for j in range(group)]
        loaded = [chunk_load(c) for c in cs]
        results = group_a(loaded)
        for c, res in zip(cs, results):
            for d in range(2):
                u_s[d, c], wq_s[d, c], lb_s[d, c], el_s[d, c] = res[d]
        return carry

    if nt // group <= 2:
        for gi in range(nt // group):
            phase_a(gi, 0)
    else:
        lax.fori_loop(0, nt // group, phase_a, 0)

    dh = [(j, d, h) for j in range(nb) for d in range(2) for h in range(hps)]
    if has_s0:
        zero = jnp.zeros((DN_DK, DN_DV), F32)

        def bd_state(j, d, h):
            top = jnp.concatenate([s0_ref[j, 0, d, 2 * h], zero], axis=1)
            bot = jnp.concatenate([zero, s0_ref[j, 0, d, 2 * h + 1]], axis=1)
            return jnp.concatenate([top, bot], axis=0)
        s_init = tuple(bd_state(j, d, h) for j, d, h in dh)
    else:
        s_init = tuple(jnp.zeros((LANES, LANES), F32) for _ in dh)

    def phase_b(s, states):
        jd = [(j, d) for j in range(nb) for d in range(2)]
        cs = {(j, d): j * n + (s if d == 0 else n - 1 - s) for j, d in jd}
        ld = {u: (wq_s[u[1], cs[u]], u_s[u[1], cs[u]], lb_s[u[1], cs[u]], el_s[u[1], cs[u], 0:1, :]) for u in jd}
        st = dict(zip(dh, states))
        ws = {(j, d, h): _bdot(ld[j, d][0][:, hp_sl[h]], st[j, d, h].astype(BF16)) for j, d, h in dh}
        vn = {(j, d, h): ld[j, d][1][:, hp_sl[h]] - ws[j, d, h][0:CHUNK] for j, d, h in dh}
        r2 = {(j, d, h): _bdot(ld[j, d][2][:, hp_sl[h]], stack_heads(vn[j, d, h], lo_half)) for j, d, h in dh}
        new_states = tuple(st[j, d, h] * ld[j, d][3][:, hp_sl[h]] + r2[j, d, h][CHUNK:] for j, d, h in dh)
        for j, d in jd:
            odir_s[d, chunk_rows(cs[j, d]), :] = lanes_cat([ws[j, d, h][CHUNK:] + r2[j, d, h][0:CHUNK]
                                                            for h in range(hps)])
        return new_states

    s_fin = dict(zip(dh, lax.fori_loop(0, n, phase_b, s_init)))

    for h in range(hps):
        o = odir_s[0, :, hp_sl[h]] + odir_s[1, :, hp_sl[h]]
        msq = _dot2_l(o * o, bd_ones) * (1.0 / DN_DV)
        o_ref[:, hp_sl[h]] = (o * lax.rsqrt(msq + EPS) * cst_ref[2:3, :]).astype(BF16)

    if emit_state:
        for j, d, h in dh:
            st_ref[j, 0, d, 2 * h] = s_fin[j, d, h][0:DN_DK, 0:DN_DV]
            st_ref[j, 0, d, 2 * h + 1] = s_fin[j, d, h][DN_DK:, DN_DV:]


def _deltanet(dqkv, ba, cst, batch, seq, nb, hps, a_chunks, s0=None, emit_state=False):
    assert batch % nb == 0
    rows = nb * seq
    nt = rows // CHUNK
    wl = hps * LANES
    groups = N_PAIRS // hps
    has_s0 = s0 is not None

    def col(cbase):
        return pl.BlockSpec((rows, wl), lambda b, hg: (b, cbase // wl + hg))

    st_spec = pl.BlockSpec((nb, 1, 2, 2 * hps, DN_DK, DN_DV), lambda b, hg: (b, 0, 0, hg, 0, 0))
    in_specs = [col(0), col(DN_WIDTH), col(2 * DN_WIDTH),
                pl.BlockSpec((rows, LANES), lambda b, hg: (b, 0)),
                pl.BlockSpec((SUBLANES, LANES), lambda b, hg: (0, 0))]
    args = [dqkv, dqkv, dqkv, ba, cst]
    if has_s0:
        in_specs.append(st_spec)
        args.append(s0)
    out_specs = [pl.BlockSpec((rows, wl), lambda b, hg: (b, hg))]
    out_shape = [jax.ShapeDtypeStruct((batch * seq, DN_HEADS * DN_DV), BF16)]
    if emit_state:
        out_specs.append(st_spec)
        out_shape.append(jax.ShapeDtypeStruct((batch, 1, 2, DN_HEADS, DN_DK, DN_DV), F32))
    scratch = [pltpu.VMEM((2, rows, wl), F32), pltpu.VMEM((2, rows, wl), F32),
               pltpu.VMEM((2, nt, CHUNK, wl), F32),
               pltpu.VMEM((2, nt, 2 * CHUNK, wl), BF16),
               pltpu.VMEM((2, nt, 3 * CHUNK, wl), BF16),
               pltpu.VMEM((2, nt, SUBLANES, wl), F32),
               pltpu.VMEM((2, rows, wl), F32)]
    res = pl.pallas_call(
        functools.partial(_dn_kernel, seq=seq, nb=nb, hps=hps, a_chunks=a_chunks, has_s0=has_s0,
                          emit_state=emit_state),
        grid=(batch // nb, groups),
        in_specs=in_specs,
        out_specs=out_specs,
        out_shape=out_shape,
        scratch_shapes=scratch,
        compiler_params=_params(2),
        name="deltanet_lat" if has_s0 else "deltanet_ctx",
    )(*args)
    return res


TM_F = 2 * TM


def _merge_ffn_kernel(x_ref, ona_ref, odn_ref, gz_ref, mod_ref, wpost1_ref, wpre2_ref, wpost2_ref,
                      wao_ref, wdo_ref, wout_ref, w1_ref, w2_ref, o_ref):
    mod = mod_ref[0]
    subs = [slice(s * TM, (s + 1) * TM) for s in range(TM_F // TM)]
    z_sl, gna_sl, gdn_sl = (slice(0, DN_WIDTH), slice(DN_WIDTH, DN_WIDTH + D_MODEL),
                            slice(DN_WIDTH + D_MODEL, GZ_W))
    odn = [odn_ref[sl, :].astype(F32) * _silu(gz_ref[sl, z_sl].astype(F32)) for sl in subs]
    br_na = [_bdot(ona_ref[sl, :], wao_ref[...]) for sl in subs]
    br_dn = [_bdot(t.astype(BF16), wdo_ref[...]) for t in odn]
    m = [_sigmoid(gz_ref[sl, gna_sl].astype(F32)) * a + _sigmoid(gz_ref[sl, gdn_sl].astype(F32)) * b
         for sl, a, b in zip(subs, br_na, br_dn)]
    y = [_bdot(t.astype(BF16), wout_ref[...]) for t in m]
    x1 = [x_ref[sl, :] + mod[2:3] * _rms_rows(t, wpost1_ref[...]) for sl, t in zip(subs, y)]
    h = [_rms_rows(t, wpre2_ref[...]) * (1.0 + mod[4:5]) + mod[3:4] for t in x1]
    f1 = [_bdot(t.astype(BF16), w1_ref[...]) for t in h]
    r = [jnp.maximum(t, 0.0) for t in f1]
    f = [_bdot((t * t).astype(BF16), w2_ref[...]) for t in r]
    for sl, a, b in zip(subs, x1, f):
        o_ref[sl, :] = a + mod[5:6] * _rms_rows(b, wpost2_ref[...])


def _merge_ffn(x2d, o_na, o_dn, gz, mod3, mod_row_of_tile, wpost1, wpre2, wpost2, w_ao, w_do, w_out, w1, w2):
    n_tok = x2d.shape[0]
    row = lambda i: (i, 0)
    const = lambda i: (0, 0)

    def resident(shape):
        return pl.BlockSpec(shape, const, pipeline_mode=pl.Buffered(1))

    return pl.pallas_call(
        _merge_ffn_kernel,
        grid=(n_tok // TM_F,),
        in_specs=[pl.BlockSpec((TM_F, D_MODEL), row),
                  pl.BlockSpec((TM_F, NA_WIDTH), row),
                  pl.BlockSpec((TM_F, DN_WIDTH), row),
                  pl.BlockSpec((TM_F, GZ_W), row),
                  pl.BlockSpec((1, N_MOD, D_MODEL), lambda i: (mod_row_of_tile(i), 0, 0)),
                  pl.BlockSpec((1, D_MODEL), const),
                  pl.BlockSpec((1, D_MODEL), const),
                  pl.BlockSpec((1, D_MODEL), const),
                  resident((NA_WIDTH, D_MODEL)),
                  resident((DN_WIDTH, D_MODEL)),
                  resident((D_MODEL, D_MODEL)),
                  resident((D_MODEL, D_FF)),
                  resident((D_FF, D_MODEL))],
        out_specs=pl.BlockSpec((TM_F, D_MODEL), row),
        out_shape=jax.ShapeDtypeStruct((n_tok, D_MODEL), F32),
        compiler_params=_params(1),
        name="merge_ffn",
    )(x2d, o_na, o_dn, gz, mod3, wpost1, wpre2, wpost2, w_ao, w_do, w_out, w1, w2)


def _pack_w_in(w):
    assert w.shape[1] == IN_COLS
    wb = w.astype(BF16)
    return wb, jnp.pad(wb[:, MAIN_COLS:], ((0, 0), (0, LANES - (IN_COLS - MAIN_COLS))))


def kernel(x_prompt, x_sample, c, cache_na_k, cache_na_v, state_delta, c_ctx, w_ada, b_ada, norm_pre1, norm_post1, norm_pre2, norm_post2, w_in, conv_w, a_log, dt_bias, dn_norm, na_rpb, w_ao, w_do, w_out, w_ff1, w_ff2):
    batch, seq, _ = x_prompt.shape
    dec_batch, dec_seq, _ = x_sample.shape
    depth = w_in.shape[0]
    assert depth == 1 and seq == TM and dec_seq % TM_F == 0 and (batch * seq) % TM_F == 0 and dec_batch < SUBLANES

    xp = x_prompt.reshape(batch * seq, D_MODEL)
    xs = x_sample.reshape(dec_batch * dec_seq, D_MODEL)
    l = 0

    cvecs = jnp.concatenate([c, c_ctx[None], jnp.zeros((SUBLANES - dec_batch - 1, D_MODEL), F32)], axis=0)
    mod3 = _adaln(cvecs, w_ada[l], b_ada[l]).reshape(SUBLANES, N_MOD, D_MODEL)
    ctx_row = lambda i: dec_batch
    lat_row = lambda i: i // (dec_seq // TMI)

    w_in_p = _pack_w_in(w_in[l])
    w_ao_b = w_ao[l].astype(BF16)
    w_do_b = w_do[l].astype(BF16)
    w_out_b = w_out[l].astype(BF16)
    w1_b = w_ff1[l].astype(BF16)
    w2_b = w_ff2[l].astype(BF16)
    wpre1 = norm_pre1[l].reshape(1, D_MODEL)
    wpost1 = norm_post1[l].reshape(1, D_MODEL)
    wpre2 = norm_pre2[l].reshape(1, D_MODEL)
    wpost2 = norm_post2[l].reshape(1, D_MODEL)

    conv_w8 = jnp.pad(conv_w[l], ((0, SUBLANES - CONV_W), (0, 0)))
    gate_rows = jnp.pad(jnp.stack([a_log[l].reshape(-1), dt_bias[l].reshape(-1)]),
                        ((0, 0), (2 * DN_HEADS, LANES - 4 * DN_HEADS)))
    cst = jnp.concatenate([gate_rows, jnp.tile(dn_norm[l], 2)[None],
                           jnp.zeros((SUBLANES - 3, LANES), F32)], axis=0)

    qkv_c, dqkv_c, gz_c, ba_c, new_k, new_v = _inproj(xp, mod3, ctx_row, wpre1, w_in_p, conv_w8, seq, batch)
    o_na_c = _ctx_attn(qkv_c, batch, seq, nb=2)
    o_dn_c, new_s = _deltanet(dqkv_c, ba_c, cst, batch, seq, nb=2, hps=4, a_chunks=4, emit_state=True)
    y_c = _merge_ffn(xp, o_na_c, o_dn_c, gz_c, mod3, ctx_row, wpost1, wpre2, wpost2,
                     w_ao_b, w_do_b, w_out_b, w1_b, w2_b)

    qkv_l, dqkv_l, gz_l, ba_l = _inproj(xs, mod3, lat_row, wpre1, w_in_p, conv_w8, dec_seq, dec_batch,
                                        rope_tabs=_rope_tables(dec_seq))
    past = cache_na_k.shape[3]
    ck_tm = jnp.transpose(cache_na_k[:, l], (0, 2, 1, 3)).reshape(dec_batch, past, NA_WIDTH)
    cv_tm = jnp.transpose(cache_na_v[:, l], (0, 2, 1, 3)).reshape(dec_batch, past, NA_WIDTH)
    o_na_l = _lat_attn(qkv_l, ck_tm, cv_tm, na_rpb[l], dec_batch, dec_seq)
    (o_dn_l,) = _deltanet(dqkv_l, ba_l, cst, dec_batch, dec_seq, nb=1, hps=4, a_chunks=2, s0=state_delta)
    y_l = _merge_ffn(xs, o_na_l, o_dn_l, gz_l, mod3, lambda i: i // (dec_seq // TM_F), wpost1, wpre2, wpost2,
                     w_ao_b, w_do_b, w_out_b, w1_b, w2_b)

    return (y_c.reshape(batch, seq, D_MODEL), y_l.reshape(dec_batch, dec_seq, D_MODEL),
            new_k, new_v, new_s)
```

```python
import functools

import numpy as np
import jax
import jax.numpy as jnp
from jax import lax
from jax.experimental import pallas as pl
from jax.experimental.pallas import tpu as pltpu

F32 = jnp.float32
BF16 = jnp.bfloat16

D_MODEL = 1024
N_MOD = 6
GRID_W = 64
NA_HEADS = 8
HEAD_DIM = 64
NA_WIDTH = NA_HEADS * HEAD_DIM
WIN_R = 8
WIN_C = 16
ROPE_BASE = 10000.0
DN_HEADS = 8
DN_DK = 64
DN_DV = 64
DN_WIDTH = DN_HEADS * DN_DK
CONV_W = 5
CHUNK = 64
D_FF = 4 * D_MODEL
EPS = 1e-6
NEG_INF = -1e30

LANES = 128
SUBLANES = 8
HALF = LANES // 2
assert HEAD_DIM == HALF and DN_DK == HALF and DN_DV == HALF and CHUNK == HALF
N_PAIRS = NA_HEADS // 2
TM = 256
TMI = 2 * TM
VMEM_LIMIT = 56 * 1024 * 1024

QKV_W = 3 * NA_WIDTH
DQKV_W = 3 * DN_WIDTH
BA_W = 4 * DN_HEADS
GZ_W = DN_WIDTH + 2 * D_MODEL
COL_QKV, COL_DQKV, COL_TAIL = 0, QKV_W, QKV_W + DQKV_W
IN_COLS = COL_TAIL + BA_W + GZ_W
MAIN_COLS = (IN_COLS // LANES) * LANES
TAIL_W = MAIN_COLS - COL_TAIL + LANES


def _params(n_grid):
    return pltpu.CompilerParams(dimension_semantics=("arbitrary",) * n_grid,
                                vmem_limit_bytes=VMEM_LIMIT)


def _bdot(a, b):
    return jnp.dot(a, b, preferred_element_type=F32)


def _split3(a):
    hi = a.astype(BF16)
    r1 = a - hi.astype(F32)
    mid = r1.astype(BF16)
    lo = (r1 - mid.astype(F32)).astype(BF16)
    return hi, mid, lo


def _dot2_l(a, b_exact):
    hi = a.astype(BF16)
    lo = (a - hi.astype(F32)).astype(BF16)
    return _bdot(jnp.concatenate([hi, lo], axis=1), jnp.concatenate([b_exact, b_exact], axis=0))


def _sigmoid(x):
    return 0.5 * jnp.tanh(0.5 * x) + 0.5


def _silu(x):
    return x * _sigmoid(x)


def _softplus(x):
    return jnp.maximum(x, 0.0) + jnp.log(1.0 + jnp.exp(-jnp.abs(x)))


def _rms_rows(x, w):
    ms = jnp.mean(x * x, axis=-1, keepdims=True)
    return x * lax.rsqrt(ms + EPS) * w


def _pair_block_ones():
    r = lax.broadcasted_iota(jnp.int32, (LANES, LANES), 0)
    c = lax.broadcasted_iota(jnp.int32, (LANES, LANES), 1)
    return jnp.where((r // HALF) == (c // HALF), 1.0, 0.0).astype(BF16)


def _adaln_kernel(c_ref, w_ref, b_ref, o_ref):
    c = c_ref[...]
    s = _silu(c).astype(BF16)
    o_ref[...] = _bdot(s, w_ref[...].astype(BF16)) + b_ref[...]


def _adaln(cvecs, w_ada, b_ada):
    tn = 1024
    n = w_ada.shape[1]
    return pl.pallas_call(
        _adaln_kernel,
        grid=(n // tn,),
        in_specs=[pl.BlockSpec((SUBLANES, D_MODEL), lambda j: (0, 0)),
                  pl.BlockSpec((D_MODEL, tn), lambda j: (0, j)),
                  pl.BlockSpec((1, tn), lambda j: (0, j))],
        out_specs=pl.BlockSpec((SUBLANES, tn), lambda j: (0, j)),
        out_shape=jax.ShapeDtypeStruct((SUBLANES, n), F32),
        compiler_params=_params(1),
        name="adaln",
    )(cvecs, w_ada, b_ada.reshape(1, n))


def _inproj_kernel(*refs, latent, tiles_per_seq):
    it = iter(refs)
    x_ref, mod_ref, wpre_ref, w_ref, wlast_ref, cw_ref = (next(it), next(it), next(it), next(it), next(it),
                                                          next(it))
    if latent:
        xprev_ref, xnext_ref, cos_ref, sin_ref = next(it), next(it), next(it), next(it)
    qkv_ref, dqkv_ref, gz_ref, ba_ref = next(it), next(it), next(it), next(it)
    if not latent:
        kc_ref, vc_ref = next(it), next(it)

    mod = mod_ref[0]
    n_sub = TMI // TM
    bd_ones = _pair_block_ones()

    def modulated(x):
        return (_rms_rows(x, wpre_ref[...]) * (1.0 + mod[1:2]) + mod[0:1]).astype(BF16)

    for sub in range(n_sub):
        rs = slice(sub * TM, (sub + 1) * TM)
        hb = modulated(x_ref[rs, :])

        w_dn = w_ref[:, COL_DQKV:COL_DQKV + DQKV_W]
        halo_zero = jnp.zeros((SUBLANES, DQKV_W), F32)
        if latent:
            t = (pl.program_id(0) * n_sub + sub) % tiles_per_seq
            prev = x_ref[sub * TM - SUBLANES:sub * TM, :] if sub > 0 else xprev_ref[...]
            nxt = x_ref[(sub + 1) * TM:(sub + 1) * TM + SUBLANES, :] if sub < n_sub - 1 else xnext_ref[...]
            hb_ext = jnp.concatenate([modulated(prev), hb, modulated(nxt)], axis=0)
            pe = _bdot(hb_ext, w_dn)
            pe = jnp.concatenate([jnp.where(t > 0, pe[:SUBLANES], halo_zero), pe[SUBLANES:SUBLANES + TM],
                                  jnp.where(t < tiles_per_seq - 1, pe[SUBLANES + TM:], halo_zero)], axis=0)
        else:
            pe = jnp.concatenate([halo_zero, _bdot(hb, w_dn), halo_zero], axis=0)
        acc = None
        for j in sorted(range(CONV_W), key=lambda s: abs(s - CONV_W // 2)):
            lo = SUBLANES - CONV_W // 2 + j
            term = pe[lo:lo + TM] * cw_ref[j:j + 1, :]
            acc = term if acc is None else acc + term
        y = _silu(acc)

        p = _bdot(hb, w_ref[:, COL_QKV:COL_QKV + QKV_W])
        if latent:
            qk = p[:, :2 * NA_WIDTH]
            lane = lax.broadcasted_iota(jnp.int32, qk.shape, 1)
            nf = HEAD_DIM // 4
            first = (lane % (2 * nf)) < nf
            partner = jnp.where(first, pltpu.roll(qk, 2 * NA_WIDTH - nf, 1), pltpu.roll(qk, nf, 1))
            qkv_ref[rs, :2 * NA_WIDTH] = (qk * cos_ref[rs, :] + partner * sin_ref[rs, :]).astype(BF16)
            qkv_ref[rs, 2 * NA_WIDTH:] = p[:, 2 * NA_WIDTH:].astype(BF16)
        else:
            qkv_ref[rs, :] = p.astype(BF16)
            for h in range(NA_HEADS):
                kc_ref[sub, 0, h] = p[:, NA_WIDTH + h * HEAD_DIM:NA_WIDTH + (h + 1) * HEAD_DIM]
                vc_ref[sub, 0, h] = p[:, 2 * NA_WIDTH + h * HEAD_DIM:2 * NA_WIDTH + (h + 1) * HEAD_DIM]

        split = COL_TAIL + ((MAIN_COLS - COL_TAIL) // LANES // 2) * LANES
        tail = jnp.concatenate([_bdot(hb, w_ref[:, COL_TAIL:split]), _bdot(hb, w_ref[:, split:MAIN_COLS]),
                                _bdot(hb, wlast_ref[...])], axis=1)
        ba_ref[rs, :] = tail[:, :LANES]
        gz_ref[rs, :] = pltpu.roll(tail, TAIL_W - BA_W, 1)[:, :GZ_W].astype(BF16)

        for s in range(DQKV_W // LANES):
            ys = y[:, s * LANES:(s + 1) * LANES]
            if s < 2 * DN_WIDTH // LANES:
                ys = ys * lax.rsqrt(_dot2_l(ys * ys, bd_ones) + EPS)
            dqkv_ref[rs, s * LANES:(s + 1) * LANES] = ys.astype(BF16)


def _inproj(x2d, mod3, mod_row_of_tile, wpre, w_packed, conv_w8, seq, batch, rope_tabs=None):
    n_tok = x2d.shape[0]
    latent = rope_tabs is not None
    assert n_tok % TMI == 0 and seq % TM == 0 and (seq % TMI == 0 or TMI % seq == 0)
    rows8 = TMI // SUBLANES
    n_blk8 = n_tok // SUBLANES

    def resident(shape):
        return pl.BlockSpec(shape, lambda i: (0, 0), pipeline_mode=pl.Buffered(1))

    in_specs = [pl.BlockSpec((TMI, D_MODEL), lambda i: (i, 0)),
                pl.BlockSpec((1, N_MOD, D_MODEL), lambda i: (mod_row_of_tile(i), 0, 0)),
                pl.BlockSpec((1, D_MODEL), lambda i: (0, 0)),
                resident((D_MODEL, MAIN_COLS)),
                resident((D_MODEL, LANES)),
                pl.BlockSpec((SUBLANES, DQKV_W), lambda i: (0, 0))]
    w_main, w_last = w_packed
    args = [x2d, mod3, wpre, w_main, w_last, conv_w8]
    out_specs = [pl.BlockSpec((TMI, QKV_W), lambda i: (i, 0)),
                 pl.BlockSpec((TMI, DQKV_W), lambda i: (i, 0)),
                 pl.BlockSpec((TMI, GZ_W), lambda i: (i, 0)),
                 pl.BlockSpec((TMI, LANES), lambda i: (i, 0))]
    out_shape = [jax.ShapeDtypeStruct((n_tok, QKV_W), BF16),
                 jax.ShapeDtypeStruct((n_tok, DQKV_W), BF16),
                 jax.ShapeDtypeStruct((n_tok, GZ_W), BF16),
                 jax.ShapeDtypeStruct((n_tok, LANES), F32)]
    if latent:
        steps_per_seq = seq // TMI
        in_specs += [pl.BlockSpec((SUBLANES, D_MODEL), lambda i: (jnp.maximum(i * rows8 - 1, 0), 0)),
                     pl.BlockSpec((SUBLANES, D_MODEL), lambda i: (jnp.minimum((i + 1) * rows8, n_blk8 - 1), 0))]
        args += [x2d, x2d]
        for tab in rope_tabs:
            in_specs.append(pl.BlockSpec((TMI, 2 * NA_WIDTH), lambda i: (i % steps_per_seq, 0)))
            args.append(tab)
    else:
        assert seq == TM
        seqs = TMI // seq
        cache_spec = pl.BlockSpec((seqs, 1, NA_HEADS, seq, HEAD_DIM), lambda i: (i, 0, 0, 0, 0))
        cache_shape = jax.ShapeDtypeStruct((batch, 1, NA_HEADS, seq, HEAD_DIM), F32)
        out_specs += [cache_spec, cache_spec]
        out_shape += [cache_shape, cache_shape]
    return pl.pallas_call(
        functools.partial(_inproj_kernel, latent=latent, tiles_per_seq=seq // TM),
        grid=(n_tok // TMI,),
        in_specs=in_specs,
        out_specs=out_specs,
        out_shape=out_shape,
        compiler_params=_params(1),
        name="inproj_lat" if latent else "inproj_ctx",
    )(*args)


def _rope_tables(seq):
    half = HEAD_DIM // 2
    nf = half // 2
    pos = np.arange(seq)
    inv = (np.float32(ROPE_BASE) ** (-np.arange(nf, dtype=np.float32) / np.float32(nf))).astype(np.float32)
    ang_r = ((pos // GRID_W).astype(np.float32)[:, None] * inv).astype(np.float64)
    ang_c = ((pos % GRID_W).astype(np.float32)[:, None] * inv).astype(np.float64)

    def grp(ang):
        return (np.concatenate([np.cos(ang), np.cos(ang)], 1),
                np.concatenate([-np.sin(ang), np.sin(ang)], 1))

    cr, sr = grp(ang_r)
    cc, sc = grp(ang_c)
    cos_h = np.concatenate([cr, cc], 1)
    sin_h = np.concatenate([sr, sc], 1)
    reps = 2 * NA_HEADS
    return (jnp.asarray(np.tile(cos_h, (1, reps)), F32),
            jnp.asarray(np.tile(sin_h, (1, reps)), F32))


def _ctx_attn_kernel(q_ref, k_ref, v_ref, o_ref, *, seq, nb):
    lane = lax.broadcasted_iota(jnp.int32, (1, LANES), 1)
    scale = HEAD_DIM ** -0.5
    pairs = [(j, hp) for j in range(nb) for hp in range(N_PAIRS)]
    heads = [(j, hp, e) for j, hp in pairs for e in range(2)]
    rows = [slice(j * seq, (j + 1) * seq) for j in range(nb)]
    sls = [slice(hp * LANES, (hp + 1) * LANES) for hp in range(N_PAIRS)]
    q = {(j, hp): q_ref[rows[j], sls[hp]] for j, hp in pairs}
    k = {(j, hp): k_ref[rows[j], sls[hp]] for j, hp in pairs}
    v = {(j, hp): v_ref[rows[j], sls[hp]] for j, hp in pairs}
    m_e = (lane < HALF, lane >= HALF)
    zero = jnp.zeros((), BF16)
    s = {(j, hp, e): lax.dot_general(jnp.where(m_e[e], q[j, hp] * scale, zero), k[j, hp],
                                     (((1,), (1,)), ((), ())), preferred_element_type=F32) for j, hp, e in heads}
    p = {u: jnp.exp(s[u] - jnp.max(s[u], axis=-1, keepdims=True)) for u in heads}
    l = {u: jnp.sum(p[u], axis=-1, keepdims=True) for u in heads}
    o = {(j, hp, e): _bdot(p[j, hp, e].astype(BF16), jnp.where(m_e[e], v[j, hp], zero)) for j, hp, e in heads}
    for j, hp in pairs:
        o_ref[rows[j], sls[hp]] = (o[j, hp, 0] / l[j, hp, 0] + o[j, hp, 1] / l[j, hp, 1]).astype(BF16)


def _ctx_attn(qkv, batch, seq, nb):
    assert batch % nb == 0

    def col(cb):
        return pl.BlockSpec((nb * seq, NA_WIDTH), lambda b: (b, cb))
    return pl.pallas_call(
        functools.partial(_ctx_attn_kernel, seq=seq, nb=nb),
        grid=(batch // nb,),
        in_specs=[col(0), col(1), col(2)],
        out_specs=pl.BlockSpec((nb * seq, NA_WIDTH), lambda b: (b, 0)),
        out_shape=jax.ShapeDtypeStruct((batch * seq, NA_WIDTH), BF16),
        compiler_params=_params(1),
        name="ctx_attn",
    )(qkv, qkv, qkv)


Q_HALF = 512
K_WIN = 768
LAT_SUB = 256
LAT_GROUP = 1


def _lat_attn_kernel(rpb_ref, q_ref, k_ref, v_ref, ck_ref, cv_ref, o_ref, bias_ref, *, rows):
    hp = pl.program_id(0)
    b = pl.program_id(1)
    half = pl.program_id(2)
    kr_n = min(WIN_R, rows)
    rows_half = Q_HALF // GRID_W
    krows_win = K_WIN // GRID_W
    n_dr = 2 * WIN_R - 1
    n_dc = 2 * WIN_C - 1

    @pl.when(jnp.logical_and(b == 0, half == 0))
    def _build_bias():
        qc = lax.broadcasted_iota(jnp.int32, (GRID_W, LANES), 0)
        ln = lax.broadcasted_iota(jnp.int32, (GRID_W, LANES), 1)
        kc = ln % GRID_W
        cs = jnp.clip(qc - WIN_C // 2, 0, GRID_W - WIN_C)
        valid = jnp.logical_and(kc >= cs, kc < cs + WIN_C)
        dc = jnp.clip(kc - qc + (WIN_C - 1), 0, n_dc - 1)
        dcv = jnp.where(valid, dc, -1)
        neg = jnp.full((GRID_W, LANES), NEG_INF, F32)
        for e in range(2):
            blocks = []
            for a in range(n_dr):
                blk = neg
                for j in range(n_dc):
                    blk = jnp.where(dcv == j, rpb_ref[((2 * hp + e) * n_dr + a) * n_dc + j], blk)
                blocks.append(blk)
            for hf in range(2):
                for rl in range(rows_half):
                    r = hf * rows_half + rl
                    rs = min(max(r - kr_n // 2, 0), rows - kr_n)
                    for kp in range(krows_win // 2):
                        parts = []
                        for kl in (2 * kp, 2 * kp + 1):
                            kr = kl + hf * (rows - krows_win)
                            parts.append(blocks[kr - r + WIN_R - 1] if rs <= kr < rs + kr_n else neg)
                        blk = jnp.where(ln < GRID_W, parts[0], parts[1])
                        bias_ref[e, hf, rl * GRID_W:(rl + 1) * GRID_W, kp * LANES:(kp + 1) * LANES] = blk

    lane = lax.broadcasted_iota(jnp.int32, (1, LANES), 1)
    scale = HEAD_DIM ** -0.5
    zero = jnp.zeros((), BF16)
    k0 = pl.multiple_of(half * ((rows - krows_win) * GRID_W), GRID_W)
    kw = k_ref[pl.ds(k0, K_WIN), :]
    vw = v_ref[pl.ds(k0, K_WIN), :]
    ck = ck_ref[0].astype(BF16)
    cv = cv_ref[0].astype(BF16)
    m_e = (lane < HALF, lane >= HALF)
    ve = [jnp.where(m, vw, zero) for m in m_e]
    cve = [jnp.where(m, cv, zero) for m in m_e]
    nt = (((1,), (1,)), ((), ()))
    for g in range(Q_HALF // (LAT_SUB * LAT_GROUP)):
        units = [(qs, e) for qs in range(g * LAT_GROUP, (g + 1) * LAT_GROUP) for e in range(2)]
        rows_of = {qs: slice(qs * LAT_SUB, (qs + 1) * LAT_SUB) for qs, _ in units}
        qe = {(qs, e): jnp.where(m_e[e], q_ref[rows_of[qs], :] * scale, zero) for qs, e in units}
        s_loc = {(qs, e): lax.dot_general(qe[qs, e], kw, nt, preferred_element_type=F32)
                 + bias_ref[e, half, rows_of[qs], :] for qs, e in units}
        s_ctx = {u: lax.dot_general(qe[u], ck, nt, preferred_element_type=F32) for u in units}
        m = {u: jnp.maximum(jnp.max(s_loc[u], axis=-1, keepdims=True), jnp.max(s_ctx[u], axis=-1, keepdims=True))
             for u in units}
        p_loc = {u: jnp.exp(s_loc[u] - m[u]) for u in units}
        p_ctx = {u: jnp.exp(s_ctx[u] - m[u]) for u in units}
        l = {u: jnp.sum(p_loc[u], axis=-1, keepdims=True) + jnp.sum(p_ctx[u], axis=-1, keepdims=True)
             for u in units}
        o = {(qs, e): (_bdot(p_loc[qs, e].astype(BF16), ve[e]) + _bdot(p_ctx[qs, e].astype(BF16), cve[e]))
             / l[qs, e] for qs, e in units}
        for qs in rows_of:
            o_ref[rows_of[qs], :] = (o[qs, 0] + o[qs, 1]).astype(BF16)


def _lat_attn(qkv, ck_tm, cv_tm, rpb, batch, seq):
    rows = seq // GRID_W
    assert rows * GRID_W == seq and rows - K_WIN // GRID_W == 4 and Q_HALF * 2 == seq
    halves = seq // Q_HALF
    past = ck_tm.shape[1]
    kernel = functools.partial(_lat_attn_kernel, rows=rows)
    return pl.pallas_call(
        kernel,
        grid=(N_PAIRS, batch, halves),
        in_specs=[pl.BlockSpec(memory_space=pltpu.SMEM),
                  pl.BlockSpec((Q_HALF, LANES), lambda hp, b, hf: (b * halves + hf, hp)),
                  pl.BlockSpec((seq, LANES), lambda hp, b, hf: (b, N_PAIRS + hp)),
                  pl.BlockSpec((seq, LANES), lambda hp, b, hf: (b, 2 * N_PAIRS + hp)),
                  pl.BlockSpec((1, past, LANES), lambda hp, b, hf: (b, 0, hp)),
                  pl.BlockSpec((1, past, LANES), lambda hp, b, hf: (b, 0, hp))],
        out_specs=pl.BlockSpec((Q_HALF, LANES), lambda hp, b, hf: (b * halves + hf, hp)),
        out_shape=jax.ShapeDtypeStruct((batch * seq, NA_WIDTH), BF16),
        scratch_shapes=[pltpu.VMEM((2, 2, Q_HALF, K_WIN), F32)],
        compiler_params=_params(3),
        name="lat_attn",
    )(rpb.reshape(-1), qkv, qkv, qkv, ck_tm, cv_tm)


def _dn_kernel(*refs, seq, nb, hps, a_chunks, has_s0, emit_state):
    it = iter(refs)
    q_ref, k_ref, v_ref, ba_ref, cst_ref = next(it), next(it), next(it), next(it), next(it)
    s0_ref = next(it) if has_s0 else None
    o_ref = next(it)
    st_ref = next(it) if emit_state else None
    (beta_s, g_s, u_s, wq_s, lb_s, el_s, odir_s) = it

    hg = pl.program_id(1)
    wl = hps * LANES
    n = seq // CHUNK
    nt = nb * n
    hp_sl = [slice(h * LANES, (h + 1) * LANES) for h in range(hps)]

    def lanes_cat(parts):
        return parts[0] if len(parts) == 1 else jnp.concatenate(parts, axis=1)

    r128 = lax.broadcasted_iota(jnp.int32, (LANES, LANES), 0)
    c128 = lax.broadcasted_iota(jnp.int32, (LANES, LANES), 1)
    bd_mask = (r128 // HALF) == (c128 // HALF)
    bd_ones = _pair_block_ones()

    ba = ba_ref[...]
    lane_g = lax.broadcasted_iota(jnp.int32, (1, LANES), 1)
    gates = jnp.where(lane_g < 2 * DN_HEADS, _sigmoid(ba),
                      -jnp.exp(cst_ref[0:1, :]) * _softplus(ba + cst_ref[1:2, :]))
    hi, mid, lo = _split3(jnp.where(lane_g < BA_W, gates, 0.0))
    packed = (hi.astype(F32) + pltpu.roll(mid.astype(F32), BA_W, 1)
              + pltpu.roll(lo.astype(F32), 2 * BA_W, 1)).astype(BF16)
    src = lax.broadcasted_iota(jnp.int32, (LANES, 4 * wl), 0)
    col = lax.broadcasted_iota(jnp.int32, (LANES, 4 * wl), 1)
    want = DN_HEADS * (col // wl) + 2 * hps * hg + (col % wl) // HALF
    sel = jnp.where(jnp.logical_and(src < 3 * BA_W, src % BA_W == want), 1.0, 0.0).astype(BF16)
    spread = _bdot(packed, sel)
    for d in range(2):
        beta_s[d] = spread[:, d * wl:(d + 1) * wl]
        g_s[d] = spread[:, (2 + d) * wl:(3 + d) * wl]

    ri = lax.broadcasted_iota(jnp.int32, (CHUNK, LANES), 0)
    lj = lax.broadcasted_iota(jnp.int32, (CHUNK, LANES), 1) % HALF
    lo_half = lax.broadcasted_iota(jnp.int32, (CHUNK, LANES), 1) < HALF
    lo_half2 = (lax.broadcasted_iota(jnp.int32, (CHUNK, 2 * LANES), 1) % LANES) < HALF
    diag = ri == lj
    strict = (ri > lj, ri < lj)
    riw = lax.broadcasted_iota(jnp.int32, (CHUNK, wl), 0)
    ljw = lax.broadcasted_iota(jnp.int32, (CHUNK, wl), 1) % HALF
    diag_w = riw == ljw
    incl_w = (riw >= ljw, riw <= ljw)
    r64 = lax.broadcasted_iota(jnp.int32, (CHUNK, 3 * CHUNK), 0)
    c64 = lax.broadcasted_iota(jnp.int32, (CHUNK, 3 * CHUNK), 1) % CHUNK
    tri3 = (jnp.where(c64 <= r64, 1.0, 0.0).astype(BF16), jnp.where(c64 >= r64, 1.0, 0.0).astype(BF16))

    off_blk = ([], [])
    n_lvl = CHUNK.bit_length() - 1
    for lvl in range(n_lvl):
        s = 1 << lvl
        same2 = (ri // (2 * s)) == (lj // (2 * s))
        off_blk[0].append(jnp.logical_and(same2, (ri // s) == (lj // s) + 1))
        off_blk[1].append(jnp.logical_and(same2, (lj // s) == (ri // s) + 1))

    def stack_heads(x, lo_mask):
        return jnp.concatenate([jnp.where(lo_mask, x, 0.0), jnp.where(lo_mask, 0.0, x)], axis=0).astype(BF16)

    def chunk_rows(c):
        return pl.ds(c * CHUNK if isinstance(c, int) else pl.multiple_of(c * CHUNK, CHUNK), CHUNK)

    def chunk_load(c):
        rows = chunk_rows(c)
        return (q_ref[rows, :].astype(F32), k_ref[rows, :].astype(F32), v_ref[rows, :].astype(F32),
                [beta_s[d, rows, :] for d in range(2)], [g_s[d, rows, :] for d in range(2)])

    def group_a(loaded):
        nc = len(loaded)
        cd = [(i, d) for i in range(nc) for d in range(2)]
        units = [(i, h, d) for i in range(nc) for h in range(hps) for d in range(2)]
        qcs = [ld[0] * (DN_DK ** -0.5) for ld in loaded]
        kcs = [ld[1] for ld in loaded]
        vcs = [ld[2] for ld in loaded]
        kt_bd = {}
        for i in range(nc):
            for h in range(hps):
                kc = kcs[i][:, hp_sl[h]]
                ktt = jnp.transpose(jnp.concatenate([kc, kc], axis=0))
                kt_bd[i, h] = jnp.where(bd_mask, ktt, 0.0)
        gc = {(i, d): _bdot(tri3[d], jnp.concatenate(_split3(loaded[i][4][d]), axis=0)) for i, d in cd}
        rrow = {u: jnp.sum(jnp.where(diag_w, gc[u], 0.0), axis=0, keepdims=True) for u in cd}
        dec = {u: jnp.exp(jnp.minimum(gc[u] - rrow[u], 0.0)) for u in cd}
        eg = {u: jnp.exp(gc[u]) for u in cd}
        kbeta = {(i, d): kcs[i] * loaded[i][3][d] for i, d in cd}
        vbeta = {(i, d): vcs[i] * loaded[i][3][d] for i, d in cd}
        kbeg = {u: kbeta[u] * eg[u] for u in cd}
        m1 = {(i, h): _bdot(jnp.concatenate([qcs[i][:, hp_sl[h]], kbeta[i, 0][:, hp_sl[h]],
                                             kbeta[i, 1][:, hp_sl[h]]], axis=0).astype(BF16),
                            kt_bd[i, h].astype(BF16))
              for i in range(nc) for h in range(hps)}
        amat = {(i, h, d): jnp.where(strict[d], m1[i, h][CHUNK * (1 + d):CHUNK * (2 + d)] * dec[i, d][:, hp_sl[h]], 0.0)
                for i, h, d in units}
        o_st = {(i, h, d): stack_heads(jnp.where(off_blk[d][0], amat[i, h, d], 0.0), lo_half) for i, h, d in units}
        tmat = {(i, h, d): jnp.where(diag, 1.0, 0.0) - jnp.where(off_blk[d][0], amat[i, h, d], 0.0) for i, h, d in units}
        prod = {u: _bdot(amat[u].astype(BF16), o_st[u]) for u in units}
        wmat = {u: amat[u] - prod[u] for u in units}
        for lvl in range(1, n_lvl - 1):
            o_st = {(i, h, d): stack_heads(jnp.where(off_blk[d][lvl], wmat[i, h, d], 0.0), lo_half) for i, h, d in units}
            prod = {u: _bdot(jnp.concatenate([tmat[u], wmat[u]], axis=0).astype(BF16), o_st[u]) for u in units}
            tmat = {u: tmat[u] - prod[u][0:CHUNK] for u in units}
            wmat = {u: wmat[u] - prod[u][CHUNK:] for u in units}
        o_st = {(i, h, d): stack_heads(jnp.where(off_blk[d][n_lvl - 1], wmat[i, h, d], 0.0), lo_half)
                for i, h, d in units}
        prod = {u: _bdot(tmat[u].astype(BF16), o_st[u]) for u in units}
        tmat = {u: tmat[u] - prod[u] for u in units}
        x = {}
        for i, h, d in units:
            rhs = jnp.concatenate([vbeta[i, d][:, hp_sl[h]], kbeg[i, d][:, hp_sl[h]]], axis=1)
            x[i, h, d] = _bdot(tmat[i, h, d].astype(BF16), stack_heads(rhs, lo_half2))
        results = []
        for i in range(nc):
            out = []
            for d in range(2):
                u = lanes_cat([x[i, h, d][:, :LANES] for h in range(hps)])
                w = lanes_cat([x[i, h, d][:, LANES:] for h in range(hps)])
                qk = jnp.where(incl_w[d], lanes_cat([m1[i, h][0:CHUNK] for h in range(hps)]) * dec[i, d], 0.0)
                qg = qcs[i] * eg[i, d]
                glast = gc[i, d][CHUNK - 1:CHUNK, :] if d == 0 else gc[i, d][0:1, :]
                kmul = jnp.exp(glast - rrow[i, d])
                kd_bd = lanes_cat([kt_bd[i, h] * kmul[:, hp_sl[h]] for h in range(hps)])
                out.append((u,
                            jnp.concatenate([w, qg], axis=0).astype(BF16),
                            jnp.concatenate([qk, kd_bd], axis=0).astype(BF16),
                            jnp.broadcast_to(jnp.exp(glast), (SUBLANES, wl))))
            results.append(out)
        return results

    group = min(a_chunks, nt)

    def phase_a(gi, carry):
        cs = [gi * group + j for j in range(group)]
        loaded = [chunk_load(c) for c in cs]
        results = group_a(loaded)
        for c, res in zip(cs, results):
            for d in range(2):
                u_s[d, c], wq_s[d, c], lb_s[d, c], el_s[d, c] = res[d]
        return carry

    if nt // group <= 2:
        for gi in range(nt // group):
            phase_a(gi, 0)
    else:
        lax.fori_loop(0, nt // group, phase_a, 0)

    dh = [(j, d, h) for j in range(nb) for d in range(2) for h in range(hps)]
    if has_s0:
        zero = jnp.zeros((DN_DK, DN_DV), F32)

        def bd_state(j, d, h):
            top = jnp.concatenate([s0_ref[j, 0, d, 2 * h], zero], axis=1)
            bot = jnp.concatenate([zero, s0_ref[j, 0, d, 2 * h + 1]], axis=1)
            return jnp.concatenate([top, bot], axis=0)
        s_init = tuple(bd_state(j, d, h) for j, d, h in dh)
    else:
        s_init = tuple(jnp.zeros((LANES, LANES), F32) for _ in dh)

    def phase_b(s, states):
        jd = [(j, d) for j in range(nb) for d in range(2)]
        cs = {(j, d): j * n + (s if d == 0 else n - 1 - s) for j, d in jd}
        ld = {u: (wq_s[u[1], cs[u]], u_s[u[1], cs[u]], lb_s[u[1], cs[u]], el_s[u[1], cs[u], 0:1, :]) for u in jd}
        st = dict(zip(dh, states))
        ws = {(j, d, h): _bdot(ld[j, d][0][:, hp_sl[h]], st[j, d, h].astype(BF16)) for j, d, h in dh}
        vn = {(j, d, h): ld[j, d][1][:, hp_sl[h]] - ws[j, d, h][0:CHUNK] for j, d, h in dh}
        r2 = {(j, d, h): _bdot(ld[j, d][2][:, hp_sl[h]], stack_heads(vn[j, d, h], lo_half)) for j, d, h in dh}
        new_states = tuple(st[j, d, h] * ld[j, d][3][:, hp_sl[h]] + r2[j, d, h][CHUNK:] for j, d, h in dh)
        for j, d in jd:
            odir_s[d, chunk_rows(cs[j, d]), :] = lanes_cat([ws[j, d, h][CHUNK:] + r2[j, d, h][0:CHUNK]
                                                            for h in range(hps)])
        return new_states

    s_fin = dict(zip(dh, lax.fori_loop(0, n, phase_b, s_init)))

    for h in range(hps):
        o = odir_s[0, :, hp_sl[h]] + odir_s[1, :, hp_sl[h]]
        msq = _dot2_l(o * o, bd_ones) * (1.0 / DN_DV)
        o_ref[:, hp_sl[h]] = (o * lax.rsqrt(msq + EPS) * cst_ref[2:3, :]).astype(BF16)

    if emit_state:
        for j, d, h in dh:
            st_ref[j, 0, d, 2 * h] = s_fin[j, d, h][0:DN_DK, 0:DN_DV]
            st_ref[j, 0, d, 2 * h + 1] = s_fin[j, d, h][DN_DK:, DN_DV:]


def _deltanet(dqkv, ba, cst, batch, seq, nb, hps, a_chunks, s0=None, emit_state=False):
    assert batch % nb == 0
    rows = nb * seq
    nt = rows // CHUNK
    wl = hps * LANES
    groups = N_PAIRS // hps
    has_s0 = s0 is not None

    def col(cbase):
        return pl.BlockSpec((rows, wl), lambda b, hg: (b, cbase // wl + hg))

    st_spec = pl.BlockSpec((nb, 1, 2, 2 * hps, DN_DK, DN_DV), lambda b, hg: (b, 0, 0, hg, 0, 0))
    in_specs = [col(0), col(DN_WIDTH), col(2 * DN_WIDTH),
                pl.BlockSpec((rows, LANES), lambda b, hg: (b, 0)),
                pl.BlockSpec((SUBLANES, LANES), lambda b, hg: (0, 0))]
    args = [dqkv, dqkv, dqkv, ba, cst]
    if has_s0:
        in_specs.append(st_spec)
        args.append(s0)
    out_specs = [pl.BlockSpec((rows, wl), lambda b, hg: (b, hg))]
    out_shape = [jax.ShapeDtypeStruct((batch * seq, DN_HEADS * DN_DV), BF16)]
    if emit_state:
        out_specs.append(st_spec)
        out_shape.append(jax.ShapeDtypeStruct((batch, 1, 2, DN_HEADS, DN_DK, DN_DV), F32))
    scratch = [pltpu.VMEM((2, rows, wl), F32), pltpu.VMEM((2, rows, wl), F32),
               pltpu.VMEM((2, nt, CHUNK, wl), F32),
               pltpu.VMEM((2, nt, 2 * CHUNK, wl), BF16),
               pltpu.VMEM((2, nt, 3 * CHUNK, wl), BF16),
               pltpu.VMEM((2, nt, SUBLANES, wl), F32),
               pltpu.VMEM((2, rows, wl), F32)]
    res = pl.pallas_call(
        functools.partial(_dn_kernel, seq=seq, nb=nb, hps=hps, a_chunks=a_chunks, has_s0=has_s0,
                          emit_state=emit_state),
        grid=(batch // nb, groups),
        in_specs=in_specs,
        out_specs=out_specs,
        out_shape=out_shape,
        scratch_shapes=scratch,
        compiler_params=_params(2),
        name="deltanet_lat" if has_s0 else "deltanet_ctx",
    )(*args)
    return res


TM_F = 2 * TM


def _merge_ffn_kernel(x_ref, ona_ref, odn_ref, gz_ref, mod_ref, wpost1_ref, wpre2_ref, wpost2_ref,
                      wao_ref, wdo_ref, wout_ref, w1_ref, w2_ref, o_ref):
    mod = mod_ref[0]
    subs = [slice(s * TM, (s + 1) * TM) for s in range(TM_F // TM)]
    z_sl, gna_sl, gdn_sl = (slice(0, DN_WIDTH), slice(DN_WIDTH, DN_WIDTH + D_MODEL),
                            slice(DN_WIDTH + D_MODEL, GZ_W))
    odn = [odn_ref[sl, :].astype(F32) * _silu(gz_ref[sl, z_sl].astype(F32)) for sl in subs]
    br_na = [_bdot(ona_ref[sl, :], wao_ref[...]) for sl in subs]
    br_dn = [_bdot(t.astype(BF16), wdo_ref[...]) for t in odn]
    m = [_sigmoid(gz_ref[sl, gna_sl].astype(F32)) * a + _sigmoid(gz_ref[sl, gdn_sl].astype(F32)) * b
         for sl, a, b in zip(subs, br_na, br_dn)]
    y = [_bdot(t.astype(BF16), wout_ref[...]) for t in m]
    x1 = [x_ref[sl, :] + mod[2:3] * _rms_rows(t, wpost1_ref[...]) for sl, t in zip(subs, y)]
    h = [_rms_rows(t, wpre2_ref[...]) * (1.0 + mod[4:5]) + mod[3:4] for t in x1]
    f1 = [_bdot(t.astype(BF16), w1_ref[...]) for t in h]
    r = [jnp.maximum(t, 0.0) for t in f1]
    f = [_bdot((t * t).astype(BF16), w2_ref[...]) for t in r]
    for sl, a, b in zip(subs, x1, f):
        o_ref[sl, :] = a + mod[5:6] * _rms_rows(b, wpost2_ref[...])


def _merge_ffn(x2d, o_na, o_dn, gz, mod3, mod_row_of_tile, wpost1, wpre2, wpost2, w_ao, w_do, w_out, w1, w2):
    n_tok = x2d.shape[0]
    row = lambda i: (i, 0)
    const = lambda i: (0, 0)

    def resident(shape):
        return pl.BlockSpec(shape, const, pipeline_mode=pl.Buffered(1))

    return pl.pallas_call(
        _merge_ffn_kernel,
        grid=(n_tok // TM_F,),
        in_specs=[pl.BlockSpec((TM_F, D_MODEL), row),
                  pl.BlockSpec((TM_F, NA_WIDTH), row),
                  pl.BlockSpec((TM_F, DN_WIDTH), row),
                  pl.BlockSpec((TM_F, GZ_W), row),
                  pl.BlockSpec((1, N_MOD, D_MODEL), lambda i: (mod_row_of_tile(i), 0, 0)),
                  pl.BlockSpec((1, D_MODEL), const),
                  pl.BlockSpec((1, D_MODEL), const),
                  pl.BlockSpec((1, D_MODEL), const),
                  resident((NA_WIDTH, D_MODEL)),
                  resident((DN_WIDTH, D_MODEL)),
                  resident((D_MODEL, D_MODEL)),
                  resident((D_MODEL, D_FF)),
                  resident((D_FF, D_MODEL))],
        out_specs=pl.BlockSpec((TM_F, D_MODEL), row),
        out_shape=jax.ShapeDtypeStruct((n_tok, D_MODEL), F32),
        compiler_params=_params(1),
        name="merge_ffn",
    )(x2d, o_na, o_dn, gz, mod3, wpost1, wpre2, wpost2, w_ao, w_do, w_out, w1, w2)


def _cast_kernel(x_ref, o_ref):
    o_ref[...] = x_ref[...].astype(BF16)


def _cast_bf16(w):
    rows, cols = w.shape
    assert rows % TM == 0
    return pl.pallas_call(
        _cast_kernel,
        grid=(rows // TM,),
        in_specs=[pl.BlockSpec((TM, cols), lambda i: (i, 0))],
        out_specs=pl.BlockSpec((TM, cols), lambda i: (i, 0)),
        out_shape=jax.ShapeDtypeStruct((rows, cols), BF16),
        compiler_params=_params(1),
        name="cast_bf16",
    )(w)


def _pack_w_in(w):
    assert w.shape[1] == IN_COLS
    wb = _cast_bf16(w)
    return wb, jnp.pad(wb[:, MAIN_COLS:], ((0, 0), (0, LANES - (IN_COLS - MAIN_COLS))))


def kernel(x_prompt, x_sample, c, cache_na_k, cache_na_v, state_delta, c_ctx, w_ada, b_ada, norm_pre1, norm_post1, norm_pre2, norm_post2, w_in, conv_w, a_log, dt_bias, dn_norm, na_rpb, w_ao, w_do, w_out, w_ff1, w_ff2):
    batch, seq, _ = x_prompt.shape
    dec_batch, dec_seq, _ = x_sample.shape
    depth = w_in.shape[0]
    assert depth == 1 and seq == TM and dec_seq % TM_F == 0 and (batch * seq) % TM_F == 0 and dec_batch < SUBLANES

    xp = x_prompt.reshape(batch * seq, D_MODEL)
    xs = x_sample.reshape(dec_batch * dec_seq, D_MODEL)
    l = 0

    cvecs = jnp.concatenate([c, c_ctx[None], jnp.zeros((SUBLANES - dec_batch - 1, D_MODEL), F32)], axis=0)
    mod3 = _adaln(cvecs, w_ada[l], b_ada[l]).reshape(SUBLANES, N_MOD, D_MODEL)
    ctx_row = lambda i: dec_batch
    lat_row = lambda i: i // (dec_seq // TMI)

    w_in_p = _pack_w_in(w_in[l])
    w_ao_b = w_ao[l].astype(BF16)
    w_do_b = w_do[l].astype(BF16)
    w_out_b = w_out[l].astype(BF16)
    w1_b = w_ff1[l].astype(BF16)
    w2_b = _cast_bf16(w_ff2[l])
    wpre1 = norm_pre1[l].reshape(1, D_MODEL)
    wpost1 = norm_post1[l].reshape(1, D_MODEL)
    wpre2 = norm_pre2[l].reshape(1, D_MODEL)
    wpost2 = norm_post2[l].reshape(1, D_MODEL)

    conv_w8 = jnp.pad(conv_w[l], ((0, SUBLANES - CONV_W), (0, 0)))
    gate_rows = jnp.pad(jnp.stack([a_log[l].reshape(-1), dt_bias[l].reshape(-1)]),
                        ((0, 0), (2 * DN_HEADS, LANES - 4 * DN_HEADS)))
    cst = jnp.concatenate([gate_rows, jnp.tile(dn_norm[l], 2)[None],
                           jnp.zeros((SUBLANES - 3, LANES), F32)], axis=0)

    qkv_c, dqkv_c, gz_c, ba_c, new_k, new_v = _inproj(xp, mod3, ctx_row, wpre1, w_in_p, conv_w8, seq, batch)
    o_na_c = _ctx_attn(qkv_c, batch, seq, nb=2)
    o_dn_c, new_s = _deltanet(dqkv_c, ba_c, cst, batch, seq, nb=2, hps=4, a_chunks=4, emit_state=True)
    y_c = _merge_ffn(xp, o_na_c, o_dn_c, gz_c, mod3, ctx_row, wpost1, wpre2, wpost2,
                     w_ao_b, w_do_b, w_out_b, w1_b, w2_b)

    qkv_l, dqkv_l, gz_l, ba_l = _inproj(xs, mod3, lat_row, wpre1, w_in_p, conv_w8, dec_seq, dec_batch,
                                        rope_tabs=_rope_tables(dec_seq))
    past = cache_na_k.shape[3]
    ck_tm = jnp.transpose(cache_na_k[:, l], (0, 2, 1, 3)).reshape(dec_batch, past, NA_WIDTH)
    cv_tm = jnp.transpose(cache_na_v[:, l], (0, 2, 1, 3)).reshape(dec_batch, past, NA_WIDTH)
    o_na_l = _lat_attn(qkv_l, ck_tm, cv_tm, na_rpb[l], dec_batch, dec_seq)
    (o_dn_l,) = _deltanet(dqkv_l, ba_l, cst, dec_batch, dec_seq, nb=1, hps=4, a_chunks=2, s0=state_delta)
    y_l = _merge_ffn(xs, o_na_l, o_dn_l, gz_l, mod3, lambda i: i // (dec_seq // TM_F), wpost1, wpre2, wpost2,
                     w_ao_b, w_do_b, w_out_b, w1_b, w2_b)

    return (y_c.reshape(batch, seq, D_MODEL), y_l.reshape(dec_batch, dec_seq, D_MODEL),
            new_k, new_v, new_s)
```

```python
import functools

import numpy as np
import jax
import jax.numpy as jnp
from jax import lax
from jax.experimental import pallas as pl
from jax.experimental.pallas import tpu as pltpu

F32 = jnp.float32
BF16 = jnp.bfloat16

D_MODEL = 1024
N_MOD = 6
GRID_W = 64
NA_HEADS = 8
HEAD_DIM = 64
NA_WIDTH = NA_HEADS * HEAD_DIM
WIN_R = 8
WIN_C = 16
ROPE_BASE = 10000.0
DN_HEADS = 8
DN_DK = 64
DN_DV = 64
DN_WIDTH = DN_HEADS * DN_DK
CONV_W = 5
CHUNK = 64
D_FF = 4 * D_MODEL
EPS = 1e-6
NEG_INF = -1e30

LANES = 128
SUBLANES = 8
HALF = LANES // 2
assert HEAD_DIM == HALF and DN_DK == HALF and DN_DV == HALF and CHUNK == HALF
N_PAIRS = NA_HEADS // 2
TM = 256
TMI = 2 * TM
VMEM_LIMIT = 56 * 1024 * 1024

QKV_W = 3 * NA_WIDTH
DQKV_W = 3 * DN_WIDTH
BA_W = 4 * DN_HEADS
GZ_W = DN_WIDTH + 2 * D_MODEL
COL_QKV, COL_DQKV, COL_TAIL = 0, QKV_W, QKV_W + DQKV_W
IN_COLS = COL_TAIL + BA_W + GZ_W
MAIN_COLS = (IN_COLS // LANES) * LANES
TAIL_W = MAIN_COLS - COL_TAIL + LANES


def _params(n_grid):
    return pltpu.CompilerParams(dimension_semantics=("arbitrary",) * n_grid,
                                vmem_limit_bytes=VMEM_LIMIT)


def _bdot(a, b):
    return jnp.dot(a, b, preferred_element_type=F32)


def _split3(a):
    hi = a.astype(BF16)
    r1 = a - hi.astype(F32)
    mid = r1.astype(BF16)
    lo = (r1 - mid.astype(F32)).astype(BF16)
    return hi, mid, lo


def _dot2_l(a, b_exact):
    hi = a.astype(BF16)
    lo = (a - hi.astype(F32)).astype(BF16)
    return _bdot(jnp.concatenate([hi, lo], axis=1), jnp.concatenate([b_exact, b_exact], axis=0))


def _sigmoid(x):
    return 0.5 * jnp.tanh(0.5 * x) + 0.5


def _silu(x):
    return x * _sigmoid(x)


def _softplus(x):
    return jnp.maximum(x, 0.0) + jnp.log(1.0 + jnp.exp(-jnp.abs(x)))


def _rms_rows(x, w):
    ms = jnp.mean(x * x, axis=-1, keepdims=True)
    return x * lax.rsqrt(ms + EPS) * w


def _pair_block_ones():
    r = lax.broadcasted_iota(jnp.int32, (LANES, LANES), 0)
    c = lax.broadcasted_iota(jnp.int32, (LANES, LANES), 1)
    return jnp.where((r // HALF) == (c // HALF), 1.0, 0.0).astype(BF16)


def _adaln_kernel(c_ref, w_ref, b_ref, o_ref):
    c = c_ref[...]
    s = _silu(c).astype(BF16)
    o_ref[...] = _bdot(s, w_ref[...].astype(BF16)) + b_ref[...]


def _adaln(cvecs, w_ada, b_ada):
    tn = 1024
    n = w_ada.shape[1]
    return pl.pallas_call(
        _adaln_kernel,
        grid=(n // tn,),
        in_specs=[pl.BlockSpec((SUBLANES, D_MODEL), lambda j: (0, 0)),
                  pl.BlockSpec((D_MODEL, tn), lambda j: (0, j)),
                  pl.BlockSpec((1, tn), lambda j: (0, j))],
        out_specs=pl.BlockSpec((SUBLANES, tn), lambda j: (0, j)),
        out_shape=jax.ShapeDtypeStruct((SUBLANES, n), F32),
        compiler_params=_params(1),
        name="adaln",
    )(cvecs, w_ada, b_ada.reshape(1, n))


def _inproj_kernel(*refs, latent, tiles_per_seq):
    it = iter(refs)
    x_ref, mod_ref, wpre_ref, w_ref, wlast_ref, cw_ref = (next(it), next(it), next(it), next(it), next(it),
                                                          next(it))
    if latent:
        xprev_ref, xnext_ref, cos_ref, sin_ref = next(it), next(it), next(it), next(it)
    qkv_ref, dqkv_ref, gz_ref, ba_ref = next(it), next(it), next(it), next(it)
    if not latent:
        kc_ref, vc_ref = next(it), next(it)

    mod = mod_ref[0]
    n_sub = TMI // TM
    bd_ones = _pair_block_ones()

    def modulated(x):
        return (_rms_rows(x, wpre_ref[...]) * (1.0 + mod[1:2]) + mod[0:1]).astype(BF16)

    for sub in range(n_sub):
        rs = slice(sub * TM, (sub + 1) * TM)
        hb = modulated(x_ref[rs, :])

        w_dn = w_ref[:, COL_DQKV:COL_DQKV + DQKV_W]
        halo_zero = jnp.zeros((SUBLANES, DQKV_W), F32)
        if latent:
            t = (pl.program_id(0) * n_sub + sub) % tiles_per_seq
            prev = x_ref[sub * TM - SUBLANES:sub * TM, :] if sub > 0 else xprev_ref[...]
            nxt = x_ref[(sub + 1) * TM:(sub + 1) * TM + SUBLANES, :] if sub < n_sub - 1 else xnext_ref[...]
            hb_ext = jnp.concatenate([modulated(prev), hb, modulated(nxt)], axis=0)
            pe = _bdot(hb_ext, w_dn)
            pe = jnp.concatenate([jnp.where(t > 0, pe[:SUBLANES], halo_zero), pe[SUBLANES:SUBLANES + TM],
                                  jnp.where(t < tiles_per_seq - 1, pe[SUBLANES + TM:], halo_zero)], axis=0)
        else:
            pe = jnp.concatenate([halo_zero, _bdot(hb, w_dn), halo_zero], axis=0)
        acc = None
        for j in sorted(range(CONV_W), key=lambda s: abs(s - CONV_W // 2)):
            lo = SUBLANES - CONV_W // 2 + j
            term = pe[lo:lo + TM] * cw_ref[j:j + 1, :]
            acc = term if acc is None else acc + term
        y = _silu(acc)

        p = _bdot(hb, w_ref[:, COL_QKV:COL_QKV + QKV_W])
        if latent:
            qk = p[:, :2 * NA_WIDTH]
            lane = lax.broadcasted_iota(jnp.int32, qk.shape, 1)
            nf = HEAD_DIM // 4
            first = (lane % (2 * nf)) < nf
            partner = jnp.where(first, pltpu.roll(qk, 2 * NA_WIDTH - nf, 1), pltpu.roll(qk, nf, 1))
            qkv_ref[rs, :2 * NA_WIDTH] = (qk * cos_ref[rs, :] + partner * sin_ref[rs, :]).astype(BF16)
            qkv_ref[rs, 2 * NA_WIDTH:] = p[:, 2 * NA_WIDTH:].astype(BF16)
        else:
            qkv_ref[rs, :] = p.astype(BF16)
            for h in range(NA_HEADS):
                kc_ref[sub, 0, h] = p[:, NA_WIDTH + h * HEAD_DIM:NA_WIDTH + (h + 1) * HEAD_DIM]
                vc_ref[sub, 0, h] = p[:, 2 * NA_WIDTH + h * HEAD_DIM:2 * NA_WIDTH + (h + 1) * HEAD_DIM]

        split = COL_TAIL + ((MAIN_COLS - COL_TAIL) // LANES // 2) * LANES
        tail = jnp.concatenate([_bdot(hb, w_ref[:, COL_TAIL:split]), _bdot(hb, w_ref[:, split:MAIN_COLS]),
                                _bdot(hb, wlast_ref[...])], axis=1)
        ba_ref[rs, :] = tail[:, :LANES]
        gz_ref[rs, :] = pltpu.roll(tail, TAIL_W - BA_W, 1)[:, :GZ_W].astype(BF16)

        for s in range(DQKV_W // LANES):
            ys = y[:, s * LANES:(s + 1) * LANES]
            if s < 2 * DN_WIDTH // LANES:
                ys = ys * lax.rsqrt(_dot2_l(ys * ys, bd_ones) + EPS)
            dqkv_ref[rs, s * LANES:(s + 1) * LANES] = ys.astype(BF16)


def _inproj(x2d, mod3, mod_row_of_tile, wpre, w_packed, conv_w8, seq, batch, rope_tabs=None):
    n_tok = x2d.shape[0]
    latent = rope_tabs is not None
    assert n_tok % TMI == 0 and seq % TM == 0 and (seq % TMI == 0 or TMI % seq == 0)
    rows8 = TMI // SUBLANES
    n_blk8 = n_tok // SUBLANES

    def resident(shape):
        return pl.BlockSpec(shape, lambda i: (0, 0), pipeline_mode=pl.Buffered(1))

    in_specs = [pl.BlockSpec((TMI, D_MODEL), lambda i: (i, 0)),
                pl.BlockSpec((1, N_MOD, D_MODEL), lambda i: (mod_row_of_tile(i), 0, 0)),
                pl.BlockSpec((1, D_MODEL), lambda i: (0, 0)),
                resident((D_MODEL, MAIN_COLS)),
                resident((D_MODEL, LANES)),
                pl.BlockSpec((SUBLANES, DQKV_W), lambda i: (0, 0))]
    w_main, w_last = w_packed
    args = [x2d, mod3, wpre, w_main, w_last, conv_w8]
    out_specs = [pl.BlockSpec((TMI, QKV_W), lambda i: (i, 0)),
                 pl.BlockSpec((TMI, DQKV_W), lambda i: (i, 0)),
                 pl.BlockSpec((TMI, GZ_W), lambda i: (i, 0)),
                 pl.BlockSpec((TMI, LANES), lambda i: (i, 0))]
    out_shape = [jax.ShapeDtypeStruct((n_tok, QKV_W), BF16),
                 jax.ShapeDtypeStruct((n_tok, DQKV_W), BF16),
                 jax.ShapeDtypeStruct((n_tok, GZ_W), BF16),
                 jax.ShapeDtypeStruct((n_tok, LANES), F32)]
    if latent:
        steps_per_seq = seq // TMI
        in_specs += [pl.BlockSpec((SUBLANES, D_MODEL), lambda i: (jnp.maximum(i * rows8 - 1, 0), 0)),
                     pl.BlockSpec((SUBLANES, D_MODEL), lambda i: (jnp.minimum((i + 1) * rows8, n_blk8 - 1), 0))]
        args += [x2d, x2d]
        for tab in rope_tabs:
            in_specs.append(pl.BlockSpec((TMI, 2 * NA_WIDTH), lambda i: (i % steps_per_seq, 0)))
            args.append(tab)
    else:
        assert seq == TM
        seqs = TMI // seq
        cache_spec = pl.BlockSpec((seqs, 1, NA_HEADS, seq, HEAD_DIM), lambda i: (i, 0, 0, 0, 0))
        cache_shape = jax.ShapeDtypeStruct((batch, 1, NA_HEADS, seq, HEAD_DIM), F32)
        out_specs += [cache_spec, cache_spec]
        out_shape += [cache_shape, cache_shape]
    return pl.pallas_call(
        functools.partial(_inproj_kernel, latent=latent, tiles_per_seq=seq // TM),
        grid=(n_tok // TMI,),
        in_specs=in_specs,
        out_specs=out_specs,
        out_shape=out_shape,
        compiler_params=_params(1),
        name="inproj_lat" if latent else "inproj_ctx",
    )(*args)


def _rope_tables(seq):
    half = HEAD_DIM // 2
    nf = half // 2
    pos = np.arange(seq)
    inv = (np.float32(ROPE_BASE) ** (-np.arange(nf, dtype=np.float32) / np.float32(nf))).astype(np.float32)
    ang_r = ((pos // GRID_W).astype(np.float32)[:, None] * inv).astype(np.float64)
    ang_c = ((pos % GRID_W).astype(np.float32)[:, None] * inv).astype(np.float64)

    def grp(ang):
        return (np.concatenate([np.cos(ang), np.cos(ang)], 1),
                np.concatenate([-np.sin(ang), np.sin(ang)], 1))

    cr, sr = grp(ang_r)
    cc, sc = grp(ang_c)
    cos_h = np.concatenate([cr, cc], 1)
    sin_h = np.concatenate([sr, sc], 1)
    reps = 2 * NA_HEADS
    return (jnp.asarray(np.tile(cos_h, (1, reps)), F32),
            jnp.asarray(np.tile(sin_h, (1, reps)), F32))


def _ctx_attn_kernel(q_ref, k_ref, v_ref, o_ref, *, seq, nb):
    lane = lax.broadcasted_iota(jnp.int32, (1, LANES), 1)
    scale = HEAD_DIM ** -0.5
    pairs = [(j, hp) for j in range(nb) for hp in range(N_PAIRS)]
    heads = [(j, hp, e) for j, hp in pairs for e in range(2)]
    rows = [slice(j * seq, (j + 1) * seq) for j in range(nb)]
    sls = [slice(hp * LANES, (hp + 1) * LANES) for hp in range(N_PAIRS)]
    q = {(j, hp): q_ref[rows[j], sls[hp]] for j, hp in pairs}
    k = {(j, hp): k_ref[rows[j], sls[hp]] for j, hp in pairs}
    v = {(j, hp): v_ref[rows[j], sls[hp]] for j, hp in pairs}
    m_e = (lane < HALF, lane >= HALF)
    zero = jnp.zeros((), BF16)
    s = {(j, hp, e): lax.dot_general(jnp.where(m_e[e], q[j, hp] * scale, zero), k[j, hp],
                                     (((1,), (1,)), ((), ())), preferred_element_type=F32) for j, hp, e in heads}
    p = {u: jnp.exp(s[u] - jnp.max(s[u], axis=-1, keepdims=True)) for u in heads}
    l = {u: jnp.sum(p[u], axis=-1, keepdims=True) for u in heads}
    o = {(j, hp, e): _bdot(p[j, hp, e].astype(BF16), jnp.where(m_e[e], v[j, hp], zero)) for j, hp, e in heads}
    for j, hp in pairs:
        o_ref[rows[j], sls[hp]] = (o[j, hp, 0] / l[j, hp, 0] + o[j, hp, 1] / l[j, hp, 1]).astype(BF16)


def _ctx_attn(qkv, batch, seq, nb):
    assert batch % nb == 0

    def col(cb):
        return pl.BlockSpec((nb * seq, NA_WIDTH), lambda b: (b, cb))
    return pl.pallas_call(
        functools.partial(_ctx_attn_kernel, seq=seq, nb=nb),
        grid=(batch // nb,),
        in_specs=[col(0), col(1), col(2)],
        out_specs=pl.BlockSpec((nb * seq, NA_WIDTH), lambda b: (b, 0)),
        out_shape=jax.ShapeDtypeStruct((batch * seq, NA_WIDTH), BF16),
        compiler_params=_params(1),
        name="ctx_attn",
    )(qkv, qkv, qkv)


Q_ROWS = 4


def _lat_windows(rows):
    kr_n = min(WIN_R, rows)
    out = []
    for qb in range(rows // Q_ROWS):
        starts = [min(max(r - kr_n // 2, 0), rows - kr_n) for r in range(qb * Q_ROWS, (qb + 1) * Q_ROWS)]
        k_lo, k_hi = min(starts), max(starts) + kr_n
        if (k_hi - k_lo) % 2:
            if k_hi < rows:
                k_hi += 1
            else:
                k_lo -= 1
        out.append((k_lo, k_hi - k_lo))
    return out


def _lat_attn_kernel(rpb_ref, q_ref, k_ref, v_ref, ck_ref, cv_ref, o_ref, bias_ref, *, rows):
    hp = pl.program_id(0)
    b = pl.program_id(1)
    kr_n = min(WIN_R, rows)
    windows = _lat_windows(rows)
    n_dr = 2 * WIN_R - 1
    n_dc = 2 * WIN_C - 1
    qn = Q_ROWS * GRID_W

    @pl.when(b == 0)
    def _build_bias():
        qc = lax.broadcasted_iota(jnp.int32, (GRID_W, LANES), 0)
        ln = lax.broadcasted_iota(jnp.int32, (GRID_W, LANES), 1)
        kc = ln % GRID_W
        cs = jnp.clip(qc - WIN_C // 2, 0, GRID_W - WIN_C)
        valid = jnp.logical_and(kc >= cs, kc < cs + WIN_C)
        dc = jnp.clip(kc - qc + (WIN_C - 1), 0, n_dc - 1)
        dcv = jnp.where(valid, dc, -1)
        neg = jnp.full((GRID_W, LANES), NEG_INF, F32)
        for e in range(2):
            blocks = []
            for a in range(n_dr):
                blk = neg
                for j in range(n_dc):
                    blk = jnp.where(dcv == j, rpb_ref[((2 * hp + e) * n_dr + a) * n_dc + j], blk)
                blocks.append(blk)
            for qb, (k_lo, nk) in enumerate(windows):
                for rl in range(Q_ROWS):
                    r = qb * Q_ROWS + rl
                    rs = min(max(r - kr_n // 2, 0), rows - kr_n)
                    for kp in range(nk // 2):
                        parts = []
                        for kr in (k_lo + 2 * kp, k_lo + 2 * kp + 1):
                            parts.append(blocks[kr - r + WIN_R - 1] if rs <= kr < rs + kr_n else neg)
                        blk = jnp.where(ln < GRID_W, parts[0], parts[1])
                        bias_ref[e, qb, rl * GRID_W:(rl + 1) * GRID_W, kp * LANES:(kp + 1) * LANES] = blk

    lane = lax.broadcasted_iota(jnp.int32, (1, LANES), 1)
    scale = HEAD_DIM ** -0.5
    zero = jnp.zeros((), BF16)
    ck = ck_ref[0].astype(BF16)
    cv = cv_ref[0].astype(BF16)
    m_e = (lane < HALF, lane >= HALF)
    cve = [jnp.where(m, cv, zero) for m in m_e]
    nt = (((1,), (1,)), ((), ()))
    for qb, (k_lo, nk) in enumerate(windows):
        qrows = slice(qb * qn, (qb + 1) * qn)
        krows = slice(k_lo * GRID_W, (k_lo + nk) * GRID_W)
        kw = k_ref[krows, :]
        vw = v_ref[krows, :]
        qe = [jnp.where(m, q_ref[qrows, :] * scale, zero) for m in m_e]
        s_loc = [lax.dot_general(qe[e], kw, nt, preferred_element_type=F32) + bias_ref[e, qb, :, :nk * GRID_W]
                 for e in range(2)]
        s_ctx = [lax.dot_general(qe[e], ck, nt, preferred_element_type=F32) for e in range(2)]
        m = [jnp.maximum(jnp.max(s_loc[e], axis=-1, keepdims=True), jnp.max(s_ctx[e], axis=-1, keepdims=True))
             for e in range(2)]
        p_loc = [jnp.exp(s_loc[e] - m[e]) for e in range(2)]
        p_ctx = [jnp.exp(s_ctx[e] - m[e]) for e in range(2)]
        l = [jnp.sum(p_loc[e], axis=-1, keepdims=True) + jnp.sum(p_ctx[e], axis=-1, keepdims=True)
             for e in range(2)]
        o = [(_bdot(p_loc[e].astype(BF16), jnp.where(m_e[e], vw, zero)) + _bdot(p_ctx[e].astype(BF16), cve[e]))
             / l[e] for e in range(2)]
        o_ref[qrows, :] = (o[0] + o[1]).astype(BF16)


def _lat_attn(qkv, ck_tm, cv_tm, rpb, batch, seq):
    rows = seq // GRID_W
    assert rows * GRID_W == seq and rows % Q_ROWS == 0
    k_max = max(nk for _, nk in _lat_windows(rows)) * GRID_W
    past = ck_tm.shape[1]
    kernel = functools.partial(_lat_attn_kernel, rows=rows)
    return pl.pallas_call(
        kernel,
        grid=(N_PAIRS, batch),
        in_specs=[pl.BlockSpec(memory_space=pltpu.SMEM),
                  pl.BlockSpec((seq, LANES), lambda hp, b: (b, hp)),
                  pl.BlockSpec((seq, LANES), lambda hp, b: (b, N_PAIRS + hp)),
                  pl.BlockSpec((seq, LANES), lambda hp, b: (b, 2 * N_PAIRS + hp)),
                  pl.BlockSpec((1, past, LANES), lambda hp, b: (b, 0, hp)),
                  pl.BlockSpec((1, past, LANES), lambda hp, b: (b, 0, hp))],
        out_specs=pl.BlockSpec((seq, LANES), lambda hp, b: (b, hp)),
        out_shape=jax.ShapeDtypeStruct((batch * seq, NA_WIDTH), BF16),
        scratch_shapes=[pltpu.VMEM((2, rows // Q_ROWS, Q_ROWS * GRID_W, k_max), F32)],
        compiler_params=_params(2),
        name="lat_attn",
    )(rpb.reshape(-1), qkv, qkv, qkv, ck_tm, cv_tm)


def _dn_kernel(*refs, seq, nb, hps, a_chunks, has_s0, emit_state):
    it = iter(refs)
    q_ref, k_ref, v_ref, ba_ref, cst_ref = next(it), next(it), next(it), next(it), next(it)
    s0_ref = next(it) if has_s0 else None
    o_ref = next(it)
    st_ref = next(it) if emit_state else None
    (beta_s, g_s, u_s, wq_s, lb_s, el_s, odir_s) = it

    hg = pl.program_id(1)
    wl = hps * LANES
    n = seq // CHUNK
    nt = nb * n
    hp_sl = [slice(h * LANES, (h + 1) * LANES) for h in range(hps)]

    def lanes_cat(parts):
        return parts[0] if len(parts) == 1 else jnp.concatenate(parts, axis=1)

    r128 = lax.broadcasted_iota(jnp.int32, (LANES, LANES), 0)
    c128 = lax.broadcasted_iota(jnp.int32, (LANES, LANES), 1)
    bd_mask = (r128 // HALF) == (c128 // HALF)
    bd_ones = _pair_block_ones()

    ba = ba_ref[...]
    lane_g = lax.broadcasted_iota(jnp.int32, (1, LANES), 1)
    gates = jnp.where(lane_g < 2 * DN_HEADS, _sigmoid(ba),
                      -jnp.exp(cst_ref[0:1, :]) * _softplus(ba + cst_ref[1:2, :]))
    hi, mid, lo = _split3(jnp.where(lane_g < BA_W, gates, 0.0))
    packed = (hi.astype(F32) + pltpu.roll(mid.astype(F32), BA_W, 1)
              + pltpu.roll(lo.astype(F32), 2 * BA_W, 1)).astype(BF16)
    src = lax.broadcasted_iota(jnp.int32, (LANES, 4 * wl), 0)
    col = lax.broadcasted_iota(jnp.int32, (LANES, 4 * wl), 1)
    want = DN_HEADS * (col // wl) + 2 * hps * hg + (col % wl) // HALF
    sel = jnp.where(jnp.logical_and(src < 3 * BA_W, src % BA_W == want), 1.0, 0.0).astype(BF16)
    spread = _bdot(packed, sel)
    for d in range(2):
        beta_s[d] = spread[:, d * wl:(d + 1) * wl]
        g_s[d] = spread[:, (2 + d) * wl:(3 + d) * wl]

    ri = lax.broadcasted_iota(jnp.int32, (CHUNK, LANES), 0)
    lj = lax.broadcasted_iota(jnp.int32, (CHUNK, LANES), 1) % HALF
    lo_half = lax.broadcasted_iota(jnp.int32, (CHUNK, LANES), 1) < HALF
    lo_half2 = (lax.broadcasted_iota(jnp.int32, (CHUNK, 2 * LANES), 1) % LANES) < HALF
    diag = ri == lj
    strict = (ri > lj, ri < lj)
    riw = lax.broadcasted_iota(jnp.int32, (CHUNK, wl), 0)
    ljw = lax.broadcasted_iota(jnp.int32, (CHUNK, wl), 1) % HALF
    diag_w = riw == ljw
    incl_w = (riw >= ljw, riw <= ljw)
    r64 = lax.broadcasted_iota(jnp.int32, (CHUNK, 3 * CHUNK), 0)
    c64 = lax.broadcasted_iota(jnp.int32, (CHUNK, 3 * CHUNK), 1) % CHUNK
    tri3 = (jnp.where(c64 <= r64, 1.0, 0.0).astype(BF16), jnp.where(c64 >= r64, 1.0, 0.0).astype(BF16))

    off_blk = ([], [])
    n_lvl = CHUNK.bit_length() - 1
    for lvl in range(n_lvl):
        s = 1 << lvl
        same2 = (ri // (2 * s)) == (lj // (2 * s))
        off_blk[0].append(jnp.logical_and(same2, (ri // s) == (lj // s) + 1))
        off_blk[1].append(jnp.logical_and(same2, (lj // s) == (ri // s) + 1))

    def stack_heads(x, lo_mask):
        return jnp.concatenate([jnp.where(lo_mask, x, 0.0), jnp.where(lo_mask, 0.0, x)], axis=0).astype(BF16)

    def chunk_rows(c):
        return pl.ds(c * CHUNK if isinstance(c, int) else pl.multiple_of(c * CHUNK, CHUNK), CHUNK)

    def chunk_load(c):
        rows = chunk_rows(c)
        return (q_ref[rows, :].astype(F32), k_ref[rows, :].astype(F32), v_ref[rows, :].astype(F32),
                [beta_s[d, rows, :] for d in range(2)], [g_s[d, rows, :] for d in range(2)])

    def group_a(loaded):
        nc = len(loaded)
        cd = [(i, d) for i in range(nc) for d in range(2)]
        units = [(i, h, d) for i in range(nc) for h in range(hps) for d in range(2)]
        qcs = [ld[0] * (DN_DK ** -0.5) for ld in loaded]
        kcs = [ld[1] for ld in loaded]
        vcs = [ld[2] for ld in loaded]
        kt_bd = {}
        for i in range(nc):
            for h in range(hps):
                kc = kcs[i][:, hp_sl[h]]
                ktt = jnp.transpose(jnp.concatenate([kc, kc], axis=0))
                kt_bd[i, h] = jnp.where(bd_mask, ktt, 0.0)
        gc = {(i, d): _bdot(tri3[d], jnp.concatenate(_split3(loaded[i][4][d]), axis=0)) for i, d in cd}
        rrow = {u: jnp.sum(jnp.where(diag_w, gc[u], 0.0), axis=0, keepdims=True) for u in cd}
        dec = {u: jnp.exp(jnp.minimum(gc[u] - rrow[u], 0.0)) for u in cd}
        eg = {u: jnp.exp(gc[u]) for u in cd}
        kbeta = {(i, d): kcs[i] * loaded[i][3][d] for i, d in cd}
        vbeta = {(i, d): vcs[i] * loaded[i][3][d] for i, d in cd}
        kbeg = {u: kbeta[u] * eg[u] for u in cd}
        m1 = {(i, h): _bdot(jnp.concatenate([qcs[i][:, hp_sl[h]], kbeta[i, 0][:, hp_sl[h]],
                                             kbeta[i, 1][:, hp_sl[h]]], axis=0).astype(BF16),
                            kt_bd[i, h].astype(BF16))
              for i in range(nc) for h in range(hps)}
        amat = {(i, h, d): jnp.where(strict[d], m1[i, h][CHUNK * (1 + d):CHUNK * (2 + d)] * dec[i, d][:, hp_sl[h]], 0.0)
                for i, h, d in units}
        o_st = {(i, h, d): stack_heads(jnp.where(off_blk[d][0], amat[i, h, d], 0.0), lo_half) for i, h, d in units}
        tmat = {(i, h, d): jnp.where(diag, 1.0, 0.0) - jnp.where(off_blk[d][0], amat[i, h, d], 0.0) for i, h, d in units}
        prod = {u: _bdot(amat[u].astype(BF16), o_st[u]) for u in units}
        wmat = {u: amat[u] - prod[u] for u in units}
        for lvl in range(1, n_lvl - 1):
            o_st = {(i, h, d): stack_heads(jnp.where(off_blk[d][lvl], wmat[i, h, d], 0.0), lo_half) for i, h, d in units}
            prod = {u: _bdot(jnp.concatenate([tmat[u], wmat[u]], axis=0).astype(BF16), o_st[u]) for u in units}
            tmat = {u: tmat[u] - prod[u][0:CHUNK] for u in units}
            wmat = {u: wmat[u] - prod[u][CHUNK:] for u in units}
        o_st = {(i, h, d): stack_heads(jnp.where(off_blk[d][n_lvl - 1], wmat[i, h, d], 0.0), lo_half)
                for i, h, d in units}
        prod = {u: _bdot(tmat[u].astype(BF16), o_st[u]) for u in units}
        tmat = {u: tmat[u] - prod[u] for u in units}
        x = {}
        for i, h, d in units:
            rhs = jnp.concatenate([vbeta[i, d][:, hp_sl[h]], kbeg[i, d][:, hp_sl[h]]], axis=1)
            x[i, h, d] = _bdot(tmat[i, h, d].astype(BF16), stack_heads(rhs, lo_half2))
        results = []
        for i in range(nc):
            out = []
            for d in range(2):
                u = lanes_cat([x[i, h, d][:, :LANES] for h in range(hps)])
                w = lanes_cat([x[i, h, d][:, LANES:] for h in range(hps)])
                qk = jnp.where(incl_w[d], lanes_cat([m1[i, h][0:CHUNK] for h in range(hps)]) * dec[i, d], 0.0)
                qg = qcs[i] * eg[i, d]
                glast = gc[i, d][CHUNK - 1:CHUNK, :] if d == 0 else gc[i, d][0:1, :]
                kmul = jnp.exp(glast - rrow[i, d])
                kd_bd = lanes_cat([kt_bd[i, h] * kmul[:, hp_sl[h]] for h in range(hps)])
                out.append((u,
                            jnp.concatenate([w, qg], axis=0).astype(BF16),
                            jnp.concatenate([qk, kd_bd], axis=0).astype(BF16),
                            jnp.broadcast_to(jnp.exp(glast), (SUBLANES, wl))))
            results.append(out)
        return results

    group = min(a_chunks, nt)

    def phase_a(gi, carry):
        cs = [gi * group + j for j in range(group)]
        loaded = [chunk_load(c) for c in cs]
        results = group_a(loaded)
        for c, res in zip(cs, results):
            for d in range(2):
                u_s[d, c], wq_s[d, c], lb_s[d, c], el_s[d, c] = res[d]
        return carry

    if nt // group <= 2:
        for gi in range(nt // group):
            phase_a(gi, 0)
    else:
        lax.fori_loop(0, nt // group, phase_a, 0)

    dh = [(j, d, h) for j in range(nb) for d in range(2) for h in range(hps)]
    if has_s0:
        zero = jnp.zeros((DN_DK, DN_DV), F32)

        def bd_state(j, d, h):
            top = jnp.concatenate([s0_ref[j, 0, d, 2 * h], zero], axis=1)
            bot = jnp.concatenate([zero, s0_ref[j, 0, d, 2 * h + 1]], axis=1)
            return jnp.concatenate([top, bot], axis=0)
        s_init = tuple(bd_state(j, d, h) for j, d, h in dh)
    else:
        s_init = tuple(jnp.zeros((LANES, LANES), F32) for _ in dh)

    def phase_b(s, states):
        jd = [(j, d) for j in range(nb) for d in range(2)]
        cs = {(j, d): j * n + (s if d == 0 else n - 1 - s) for j, d in jd}
        ld = {u: (wq_s[u[1], cs[u]], u_s[u[1], cs[u]], lb_s[u[1], cs[u]], el_s[u[1], cs[u], 0:1, :]) for u in jd}
        st = dict(zip(dh, states))
        ws = {(j, d, h): _bdot(ld[j, d][0][:, hp_sl[h]], st[j, d, h].astype(BF16)) for j, d, h in dh}
        vn = {(j, d, h): ld[j, d][1][:, hp_sl[h]] - ws[j, d, h][0:CHUNK] for j, d, h in dh}
        r2 = {(j, d, h): _bdot(ld[j, d][2][:, hp_sl[h]], stack_heads(vn[j, d, h], lo_half)) for j, d, h in dh}
        new_states = tuple(st[j, d, h] * ld[j, d][3][:, hp_sl[h]] + r2[j, d, h][CHUNK:] for j, d, h in dh)
        for j, d in jd:
            odir_s[d, chunk_rows(cs[j, d]), :] = lanes_cat([ws[j, d, h][CHUNK:] + r2[j, d, h][0:CHUNK]
                                                            for h in range(hps)])
        return new_states

    s_fin = dict(zip(dh, lax.fori_loop(0, n, phase_b, s_init)))

    for h in range(hps):
        o = odir_s[0, :, hp_sl[h]] + odir_s[1, :, hp_sl[h]]
        msq = _dot2_l(o * o, bd_ones) * (1.0 / DN_DV)
        o_ref[:, hp_sl[h]] = (o * lax.rsqrt(msq + EPS) * cst_ref[2:3, :]).astype(BF16)

    if emit_state:
        for j, d, h in dh:
            st_ref[j, 0, d, 2 * h] = s_fin[j, d, h][0:DN_DK, 0:DN_DV]
            st_ref[j, 0, d, 2 * h + 1] = s_fin[j, d, h][DN_DK:, DN_DV:]


def _deltanet(dqkv, ba, cst, batch, seq, nb, hps, a_chunks, s0=None, emit_state=False):
    assert batch % nb == 0
    rows = nb * seq
    nt = rows // CHUNK
    wl = hps * LANES
    groups = N_PAIRS // hps
    has_s0 = s0 is not None

    def col(cbase):
        return pl.BlockSpec((rows, wl), lambda b, hg: (b, cbase // wl + hg))

    st_spec = pl.BlockSpec((nb, 1, 2, 2 * hps, DN_DK, DN_DV), lambda b, hg: (b, 0, 0, hg, 0, 0))
    in_specs = [col(0), col(DN_WIDTH), col(2 * DN_WIDTH),
                pl.BlockSpec((rows, LANES), lambda b, hg: (b, 0)),
                pl.BlockSpec((SUBLANES, LANES), lambda b, hg: (0, 0))]
    args = [dqkv, dqkv, dqkv, ba, cst]
    if has_s0:
        in_specs.append(st_spec)
        args.append(s0)
    out_specs = [pl.BlockSpec((rows, wl), lambda b, hg: (b, hg))]
    out_shape = [jax.ShapeDtypeStruct((batch * seq, DN_HEADS * DN_DV), BF16)]
    if emit_state:
        out_specs.append(st_spec)
        out_shape.append(jax.ShapeDtypeStruct((batch, 1, 2, DN_HEADS, DN_DK, DN_DV), F32))
    scratch = [pltpu.VMEM((2, rows, wl), F32), pltpu.VMEM((2, rows, wl), F32),
               pltpu.VMEM((2, nt, CHUNK, wl), F32),
               pltpu.VMEM((2, nt, 2 * CHUNK, wl), BF16),
               pltpu.VMEM((2, nt, 3 * CHUNK, wl), BF16),
               pltpu.VMEM((2, nt, SUBLANES, wl), F32),
               pltpu.VMEM((2, rows, wl), F32)]
    res = pl.pallas_call(
        functools.partial(_dn_kernel, seq=seq, nb=nb, hps=hps, a_chunks=a_chunks, has_s0=has_s0,
                          emit_state=emit_state),
        grid=(batch // nb, groups),
        in_specs=in_specs,
        out_specs=out_specs,
        out_shape=out_shape,
        scratch_shapes=scratch,
        compiler_params=_params(2),
        name="deltanet_lat" if has_s0 else "deltanet_ctx",
    )(*args)
    return res


TM_F = 2 * TM


def _merge_ffn_kernel(x_ref, ona_ref, odn_ref, gz_ref, mod_ref, wpost1_ref, wpre2_ref, wpost2_ref,
                      wao_ref, wdo_ref, wout_ref, w1_ref, w2_ref, o_ref):
    mod = mod_ref[0]
    subs = [slice(s * TM, (s + 1) * TM) for s in range(TM_F // TM)]
    z_sl, gna_sl, gdn_sl = (slice(0, DN_WIDTH), slice(DN_WIDTH, DN_WIDTH + D_MODEL),
                            slice(DN_WIDTH + D_MODEL, GZ_W))
    odn = [odn_ref[sl, :].astype(F32) * _silu(gz_ref[sl, z_sl].astype(F32)) for sl in subs]
    br_na = [_bdot(ona_ref[sl, :], wao_ref[...]) for sl in subs]
    br_dn = [_bdot(t.astype(BF16), wdo_ref[...]) for t in odn]
    m = [_sigmoid(gz_ref[sl, gna_sl].astype(F32)) * a + _sigmoid(gz_ref[sl, gdn_sl].astype(F32)) * b
         for sl, a, b in zip(subs, br_na, br_dn)]
    y = [_bdot(t.astype(BF16), wout_ref[...]) for t in m]
    x1 = [x_ref[sl, :] + mod[2:3] * _rms_rows(t, wpost1_ref[...]) for sl, t in zip(subs, y)]
    h = [_rms_rows(t, wpre2_ref[...]) * (1.0 + mod[4:5]) + mod[3:4] for t in x1]
    f1 = [_bdot(t.astype(BF16), w1_ref[...]) for t in h]
    r = [jnp.maximum(t, 0.0) for t in f1]
    f = [_bdot((t * t).astype(BF16), w2_ref[...]) for t in r]
    for sl, a, b in zip(subs, x1, f):
        o_ref[sl, :] = a + mod[5:6] * _rms_rows(b, wpost2_ref[...])


def _merge_ffn(x2d, o_na, o_dn, gz, mod3, mod_row_of_tile, wpost1, wpre2, wpost2, w_ao, w_do, w_out, w1, w2):
    n_tok = x2d.shape[0]
    row = lambda i: (i, 0)
    const = lambda i: (0, 0)

    def resident(shape):
        return pl.BlockSpec(shape, const, pipeline_mode=pl.Buffered(1))

    return pl.pallas_call(
        _merge_ffn_kernel,
        grid=(n_tok // TM_F,),
        in_specs=[pl.BlockSpec((TM_F, D_MODEL), row),
                  pl.BlockSpec((TM_F, NA_WIDTH), row),
                  pl.BlockSpec((TM_F, DN_WIDTH), row),
                  pl.BlockSpec((TM_F, GZ_W), row),
                  pl.BlockSpec((1, N_MOD, D_MODEL), lambda i: (mod_row_of_tile(i), 0, 0)),
                  pl.BlockSpec((1, D_MODEL), const),
                  pl.BlockSpec((1, D_MODEL), const),
                  pl.BlockSpec((1, D_MODEL), const),
                  resident((NA_WIDTH, D_MODEL)),
                  resident((DN_WIDTH, D_MODEL)),
                  resident((D_MODEL, D_MODEL)),
                  resident((D_MODEL, D_FF)),
                  resident((D_FF, D_MODEL))],
        out_specs=pl.BlockSpec((TM_F, D_MODEL), row),
        out_shape=jax.ShapeDtypeStruct((n_tok, D_MODEL), F32),
        compiler_params=_params(1),
        name="merge_ffn",
    )(x2d, o_na, o_dn, gz, mod3, wpost1, wpre2, wpost2, w_ao, w_do, w_out, w1, w2)


def _pack_w_in(w):
    assert w.shape[1] == IN_COLS
    wb = w.astype(BF16)
    return wb, jnp.pad(wb[:, MAIN_COLS:], ((0, 0), (0, LANES - (IN_COLS - MAIN_COLS))))


def kernel(x_prompt, x_sample, c, cache_na_k, cache_na_v, state_delta, c_ctx, w_ada, b_ada, norm_pre1, norm_post1, norm_pre2, norm_post2, w_in, conv_w, a_log, dt_bias, dn_norm, na_rpb, w_ao, w_do, w_out, w_ff1, w_ff2):
    batch, seq, _ = x_prompt.shape
    dec_batch, dec_seq, _ = x_sample.shape
    depth = w_in.shape[0]
    assert depth == 1 and seq == TM and dec_seq % TM_F == 0 and (batch * seq) % TM_F == 0 and dec_batch < SUBLANES

    xp = x_prompt.reshape(batch * seq, D_MODEL)
    xs = x_sample.reshape(dec_batch * dec_seq, D_MODEL)
    l = 0

    cvecs = jnp.concatenate([c, c_ctx[None], jnp.zeros((SUBLANES - dec_batch - 1, D_MODEL), F32)], axis=0)
    mod3 = _adaln(cvecs, w_ada[l], b_ada[l]).reshape(SUBLANES, N_MOD, D_MODEL)
    ctx_row = lambda i: dec_batch
    lat_row = lambda i: i // (dec_seq // TMI)

    w_in_p = _pack_w_in(w_in[l])
    w_ao_b = w_ao[l].astype(BF16)
    w_do_b = w_do[l].astype(BF16)
    w_out_b = w_out[l].astype(BF16)
    w1_b = w_ff1[l].astype(BF16)
    w2_b = w_ff2[l].astype(BF16)
    wpre1 = norm_pre1[l].reshape(1, D_MODEL)
    wpost1 = norm_post1[l].reshape(1, D_MODEL)
    wpre2 = norm_pre2[l].reshape(1, D_MODEL)
    wpost2 = norm_post2[l].reshape(1, D_MODEL)

    conv_w8 = jnp.pad(conv_w[l], ((0, SUBLANES - CONV_W), (0, 0)))
    gate_rows = jnp.pad(jnp.stack([a_log[l].reshape(-1), dt_bias[l].reshape(-1)]),
                        ((0, 0), (2 * DN_HEADS, LANES - 4 * DN_HEADS)))
    cst = jnp.concatenate([gate_rows, jnp.tile(dn_norm[l], 2)[None],
                           jnp.zeros((SUBLANES - 3, LANES), F32)], axis=0)

    qkv_c, dqkv_c, gz_c, ba_c, new_k, new_v = _inproj(xp, mod3, ctx_row, wpre1, w_in_p, conv_w8, seq, batch)
    o_na_c = _ctx_attn(qkv_c, batch, seq, nb=2)
    o_dn_c, new_s = _deltanet(dqkv_c, ba_c, cst, batch, seq, nb=2, hps=4, a_chunks=4, emit_state=True)
    y_c = _merge_ffn(xp, o_na_c, o_dn_c, gz_c, mod3, ctx_row, wpost1, wpre2, wpost2,
                     w_ao_b, w_do_b, w_out_b, w1_b, w2_b)

    qkv_l, dqkv_l, gz_l, ba_l = _inproj(xs, mod3, lat_row, wpre1, w_in_p, conv_w8, dec_seq, dec_batch,
                                        rope_tabs=_rope_tables(dec_seq))
    past = cache_na_k.shape[3]
    ck_tm = jnp.transpose(cache_na_k[:, l], (0, 2, 1, 3)).reshape(dec_batch, past, NA_WIDTH)
    cv_tm = jnp.transpose(cache_na_v[:, l], (0, 2, 1, 3)).reshape(dec_batch, past, NA_WIDTH)
    o_na_l = _lat_attn(qkv_l, ck_tm, cv_tm, na_rpb[l], dec_batch, dec_seq)
    (o_dn_l,) = _deltanet(dqkv_l, ba_l, cst, dec_batch, dec_seq, nb=1, hps=4, a_chunks=2, s0=state_delta)
    y_l = _merge_ffn(xs, o_na_l, o_dn_l, gz_l, mod3, lambda i: i // (dec_seq // TM_F), wpost1, wpre2, wpost2,
                     w_ao_b, w_do_b, w_out_b, w1_b, w2_b)

    return (y_c.reshape(batch, seq, D_MODEL), y_l.reshape(dec_batch, dec_seq, D_MODEL),
            new_k, new_v, new_s)
```

```python
import functools

import numpy as np
import jax
import jax.numpy as jnp
from jax import lax
from jax.experimental import pallas as pl
from jax.experimental.pallas import tpu as pltpu

F32 = jnp.float32
BF16 = jnp.bfloat16

D_MODEL = 1024
N_MOD = 6
GRID_W = 64
NA_HEADS = 8
HEAD_DIM = 64
NA_WIDTH = NA_HEADS * HEAD_DIM
WIN_R = 8
WIN_C = 16
ROPE_BASE = 10000.0
DN_HEADS = 8
DN_DK = 64
DN_DV = 64
DN_WIDTH = DN_HEADS * DN_DK
CONV_W = 5
CHUNK = 64
D_FF = 4 * D_MODEL
EPS = 1e-6
NEG_INF = -1e30

LANES = 128
SUBLANES = 8
HALF = LANES // 2
assert HEAD_DIM == HALF and DN_DK == HALF and DN_DV == HALF and CHUNK == HALF
N_PAIRS = NA_HEADS // 2
TM = 256
TMI = 2 * TM
VMEM_LIMIT = 56 * 1024 * 1024

QKV_W = 3 * NA_WIDTH
DQKV_W = 3 * DN_WIDTH
BA_W = 4 * DN_HEADS
GZ_W = DN_WIDTH + 2 * D_MODEL
COL_QKV, COL_DQKV, COL_TAIL = 0, QKV_W, QKV_W + DQKV_W
IN_COLS = COL_TAIL + BA_W + GZ_W
MAIN_COLS = (IN_COLS // LANES) * LANES
TAIL_W = MAIN_COLS - COL_TAIL + LANES


def _params(n_grid):
    return pltpu.CompilerParams(dimension_semantics=("arbitrary",) * n_grid,
                                vmem_limit_bytes=VMEM_LIMIT)


def _bdot(a, b):
    return jnp.dot(a, b, preferred_element_type=F32)


def _split3(a):
    hi = a.astype(BF16)
    r1 = a - hi.astype(F32)
    mid = r1.astype(BF16)
    lo = (r1 - mid.astype(F32)).astype(BF16)
    return hi, mid, lo


def _dot2_l(a, b_exact):
    hi = a.astype(BF16)
    lo = (a - hi.astype(F32)).astype(BF16)
    return _bdot(jnp.concatenate([hi, lo], axis=1), jnp.concatenate([b_exact, b_exact], axis=0))


def _sigmoid(x):
    return 0.5 * jnp.tanh(0.5 * x) + 0.5


def _silu(x):
    return x * _sigmoid(x)


def _softplus(x):
    return jnp.maximum(x, 0.0) + jnp.log(1.0 + jnp.exp(-jnp.abs(x)))


def _rms_rows(x, w):
    ms = jnp.mean(x * x, axis=-1, keepdims=True)
    return x * lax.rsqrt(ms + EPS) * w


def _pair_block_ones():
    r = lax.broadcasted_iota(jnp.int32, (LANES, LANES), 0)
    c = lax.broadcasted_iota(jnp.int32, (LANES, LANES), 1)
    return jnp.where((r // HALF) == (c // HALF), 1.0, 0.0).astype(BF16)


def _adaln_kernel(c_ref, w_ref, b_ref, o_ref):
    c = c_ref[...]
    s = _silu(c).astype(BF16)
    o_ref[...] = _bdot(s, w_ref[...].astype(BF16)) + b_ref[...]


def _adaln(cvecs, w_ada, b_ada):
    tn = 1024
    n = w_ada.shape[1]
    return pl.pallas_call(
        _adaln_kernel,
        grid=(n // tn,),
        in_specs=[pl.BlockSpec((SUBLANES, D_MODEL), lambda j: (0, 0)),
                  pl.BlockSpec((D_MODEL, tn), lambda j: (0, j)),
                  pl.BlockSpec((1, tn), lambda j: (0, j))],
        out_specs=pl.BlockSpec((SUBLANES, tn), lambda j: (0, j)),
        out_shape=jax.ShapeDtypeStruct((SUBLANES, n), F32),
        compiler_params=_params(1),
        name="adaln",
    )(cvecs, w_ada, b_ada.reshape(1, n))


def _inproj_kernel(*refs, latent, tiles_per_seq):
    it = iter(refs)
    x_ref, mod_ref, wpre_ref, w_ref, wlast_ref, cw_ref = (next(it), next(it), next(it), next(it), next(it),
                                                          next(it))
    if latent:
        xprev_ref, xnext_ref, cos_ref, sin_ref = next(it), next(it), next(it), next(it)
    qkv_ref, dqkv_ref, gz_ref, ba_ref = next(it), next(it), next(it), next(it)
    if not latent:
        kc_ref, vc_ref = next(it), next(it)

    mod = mod_ref[0]
    n_sub = TMI // TM
    bd_ones = _pair_block_ones()

    def modulated(x):
        return (_rms_rows(x, wpre_ref[...]) * (1.0 + mod[1:2]) + mod[0:1]).astype(BF16)

    for sub in range(n_sub):
        rs = slice(sub * TM, (sub + 1) * TM)
        hb = modulated(x_ref[rs, :])

        w_dn = w_ref[:, COL_DQKV:COL_DQKV + DQKV_W]
        halo_zero = jnp.zeros((SUBLANES, DQKV_W), F32)
        if latent:
            t = (pl.program_id(0) * n_sub + sub) % tiles_per_seq
            prev = x_ref[sub * TM - SUBLANES:sub * TM, :] if sub > 0 else xprev_ref[...]
            nxt = x_ref[(sub + 1) * TM:(sub + 1) * TM + SUBLANES, :] if sub < n_sub - 1 else xnext_ref[...]
            hb_ext = jnp.concatenate([modulated(prev), hb, modulated(nxt)], axis=0)
            pe = _bdot(hb_ext, w_dn)
            pe = jnp.concatenate([jnp.where(t > 0, pe[:SUBLANES], halo_zero), pe[SUBLANES:SUBLANES + TM],
                                  jnp.where(t < tiles_per_seq - 1, pe[SUBLANES + TM:], halo_zero)], axis=0)
        else:
            pe = jnp.concatenate([halo_zero, _bdot(hb, w_dn), halo_zero], axis=0)
        acc = None
        for j in sorted(range(CONV_W), key=lambda s: abs(s - CONV_W // 2)):
            lo = SUBLANES - CONV_W // 2 + j
            term = pe[lo:lo + TM] * cw_ref[j:j + 1, :]
            acc = term if acc is None else acc + term
        y = _silu(acc)

        p = _bdot(hb, w_ref[:, COL_QKV:COL_QKV + QKV_W])
        if latent:
            qk = p[:, :2 * NA_WIDTH]
            lane = lax.broadcasted_iota(jnp.int32, qk.shape, 1)
            nf = HEAD_DIM // 4
            first = (lane % (2 * nf)) < nf
            partner = jnp.where(first, pltpu.roll(qk, 2 * NA_WIDTH - nf, 1), pltpu.roll(qk, nf, 1))
            qkv_ref[rs, :2 * NA_WIDTH] = (qk * cos_ref[rs, :] + partner * sin_ref[rs, :]).astype(BF16)
            qkv_ref[rs, 2 * NA_WIDTH:] = p[:, 2 * NA_WIDTH:].astype(BF16)
        else:
            qkv_ref[rs, :] = p.astype(BF16)
            for h in range(NA_HEADS):
                kc_ref[sub, 0, h] = p[:, NA_WIDTH + h * HEAD_DIM:NA_WIDTH + (h + 1) * HEAD_DIM]
                vc_ref[sub, 0, h] = p[:, 2 * NA_WIDTH + h * HEAD_DIM:2 * NA_WIDTH + (h + 1) * HEAD_DIM]

        split = COL_TAIL + ((MAIN_COLS - COL_TAIL) // LANES // 2) * LANES
        tail = jnp.concatenate([_bdot(hb, w_ref[:, COL_TAIL:split]), _bdot(hb, w_ref[:, split:MAIN_COLS]),
                                _bdot(hb, wlast_ref[...])], axis=1)
        ba_ref[rs, :] = tail[:, :LANES]
        gz_ref[rs, :] = pltpu.roll(tail, TAIL_W - BA_W, 1)[:, :GZ_W].astype(BF16)

        for s in range(DQKV_W // LANES):
            ys = y[:, s * LANES:(s + 1) * LANES]
            if s < 2 * DN_WIDTH // LANES:
                ys = ys * lax.rsqrt(_dot2_l(ys * ys, bd_ones) + EPS)
            dqkv_ref[rs, s * LANES:(s + 1) * LANES] = ys.astype(BF16)


def _inproj(x2d, mod3, mod_row_of_tile, wpre, w_packed, conv_w8, seq, batch, rope_tabs=None):
    n_tok = x2d.shape[0]
    latent = rope_tabs is not None
    assert n_tok % TMI == 0 and seq % TM == 0 and (seq % TMI == 0 or TMI % seq == 0)
    rows8 = TMI // SUBLANES
    n_blk8 = n_tok // SUBLANES

    def resident(shape):
        return pl.BlockSpec(shape, lambda i: (0, 0), pipeline_mode=pl.Buffered(1))

    in_specs = [pl.BlockSpec((TMI, D_MODEL), lambda i: (i, 0)),
                pl.BlockSpec((1, N_MOD, D_MODEL), lambda i: (mod_row_of_tile(i), 0, 0)),
                pl.BlockSpec((1, D_MODEL), lambda i: (0, 0)),
                resident((D_MODEL, MAIN_COLS)),
                resident((D_MODEL, LANES)),
                pl.BlockSpec((SUBLANES, DQKV_W), lambda i: (0, 0))]
    w_main, w_last = w_packed
    args = [x2d, mod3, wpre, w_main, w_last, conv_w8]
    out_specs = [pl.BlockSpec((TMI, QKV_W), lambda i: (i, 0)),
                 pl.BlockSpec((TMI, DQKV_W), lambda i: (i, 0)),
                 pl.BlockSpec((TMI, GZ_W), lambda i: (i, 0)),
                 pl.BlockSpec((TMI, LANES), lambda i: (i, 0))]
    out_shape = [jax.ShapeDtypeStruct((n_tok, QKV_W), BF16),
                 jax.ShapeDtypeStruct((n_tok, DQKV_W), BF16),
                 jax.ShapeDtypeStruct((n_tok, GZ_W), BF16),
                 jax.ShapeDtypeStruct((n_tok, LANES), F32)]
    if latent:
        steps_per_seq = seq // TMI
        in_specs += [pl.BlockSpec((SUBLANES, D_MODEL), lambda i: (jnp.maximum(i * rows8 - 1, 0), 0)),
                     pl.BlockSpec((SUBLANES, D_MODEL), lambda i: (jnp.minimum((i + 1) * rows8, n_blk8 - 1), 0))]
        args += [x2d, x2d]
        for tab in rope_tabs:
            in_specs.append(pl.BlockSpec((TMI, 2 * NA_WIDTH), lambda i: (i % steps_per_seq, 0)))
            args.append(tab)
    else:
        assert seq == TM
        seqs = TMI // seq
        cache_spec = pl.BlockSpec((seqs, 1, NA_HEADS, seq, HEAD_DIM), lambda i: (i, 0, 0, 0, 0))
        cache_shape = jax.ShapeDtypeStruct((batch, 1, NA_HEADS, seq, HEAD_DIM), F32)
        out_specs += [cache_spec, cache_spec]
        out_shape += [cache_shape, cache_shape]
    return pl.pallas_call(
        functools.partial(_inproj_kernel, latent=latent, tiles_per_seq=seq // TM),
        grid=(n_tok // TMI,),
        in_specs=in_specs,
        out_specs=out_specs,
        out_shape=out_shape,
        compiler_params=_params(1),
        name="inproj_lat" if latent else "inproj_ctx",
    )(*args)


def _rope_tables(seq):
    half = HEAD_DIM // 2
    nf = half // 2
    pos = np.arange(seq)
    inv = (np.float32(ROPE_BASE) ** (-np.arange(nf, dtype=np.float32) / np.float32(nf))).astype(np.float32)
    ang_r = ((pos // GRID_W).astype(np.float32)[:, None] * inv).astype(np.float64)
    ang_c = ((pos % GRID_W).astype(np.float32)[:, None] * inv).astype(np.float64)

    def grp(ang):
        return (np.concatenate([np.cos(ang), np.cos(ang)], 1),
                np.concatenate([-np.sin(ang), np.sin(ang)], 1))

    cr, sr = grp(ang_r)
    cc, sc = grp(ang_c)
    cos_h = np.concatenate([cr, cc], 1)
    sin_h = np.concatenate([sr, sc], 1)
    reps = 2 * NA_HEADS
    return (jnp.asarray(np.tile(cos_h, (1, reps)), F32),
            jnp.asarray(np.tile(sin_h, (1, reps)), F32))


def _ctx_attn_kernel(q_ref, k_ref, v_ref, o_ref, *, seq, nb):
    lane = lax.broadcasted_iota(jnp.int32, (1, LANES), 1)
    scale = HEAD_DIM ** -0.5
    pairs = [(j, hp) for j in range(nb) for hp in range(N_PAIRS)]
    heads = [(j, hp, e) for j, hp in pairs for e in range(2)]
    rows = [slice(j * seq, (j + 1) * seq) for j in range(nb)]
    sls = [slice(hp * LANES, (hp + 1) * LANES) for hp in range(N_PAIRS)]
    q = {(j, hp): q_ref[rows[j], sls[hp]] for j, hp in pairs}
    k = {(j, hp): k_ref[rows[j], sls[hp]] for j, hp in pairs}
    v = {(j, hp): v_ref[rows[j], sls[hp]] for j, hp in pairs}
    m_e = (lane < HALF, lane >= HALF)
    zero = jnp.zeros((), BF16)
    s = {(j, hp, e): lax.dot_general(jnp.where(m_e[e], q[j, hp] * scale, zero), k[j, hp],
                                     (((1,), (1,)), ((), ())), preferred_element_type=F32) for j, hp, e in heads}
    p = {u: jnp.exp(s[u] - jnp.max(s[u], axis=-1, keepdims=True)) for u in heads}
    l = {u: jnp.sum(p[u], axis=-1, keepdims=True) for u in heads}
    o = {(j, hp, e): _bdot(p[j, hp, e].astype(BF16), jnp.where(m_e[e], v[j, hp], zero)) for j, hp, e in heads}
    for j, hp in pairs:
        o_ref[rows[j], sls[hp]] = (o[j, hp, 0] / l[j, hp, 0] + o[j, hp, 1] / l[j, hp, 1]).astype(BF16)


def _ctx_attn(qkv, batch, seq, nb):
    assert batch % nb == 0

    def col(cb):
        return pl.BlockSpec((nb * seq, NA_WIDTH), lambda b: (b, cb))
    return pl.pallas_call(
        functools.partial(_ctx_attn_kernel, seq=seq, nb=nb),
        grid=(batch // nb,),
        in_specs=[col(0), col(1), col(2)],
        out_specs=pl.BlockSpec((nb * seq, NA_WIDTH), lambda b: (b, 0)),
        out_shape=jax.ShapeDtypeStruct((batch * seq, NA_WIDTH), BF16),
        compiler_params=_params(1),
        name="ctx_attn",
    )(qkv, qkv, qkv)


Q_ROWS = 4
KEY_TILE = 256


def _lat_windows(rows):
    kr_n = min(WIN_R, rows)
    out = []
    for qb in range(rows // Q_ROWS):
        starts = [min(max(r - kr_n // 2, 0), rows - kr_n) for r in range(qb * Q_ROWS, (qb + 1) * Q_ROWS)]
        k_lo, k_hi = min(starts), max(starts) + kr_n
        if (k_hi - k_lo) % 2:
            if k_hi < rows:
                k_hi += 1
            else:
                k_lo -= 1
        out.append((k_lo, k_hi - k_lo))
    return out


def _lat_attn_kernel(rpb_ref, q_ref, k_ref, v_ref, ck_ref, cv_ref, o_ref, bias_ref, *, rows):
    hp = pl.program_id(0)
    b = pl.program_id(1)
    kr_n = min(WIN_R, rows)
    windows = _lat_windows(rows)
    n_dr = 2 * WIN_R - 1
    n_dc = 2 * WIN_C - 1
    qn = Q_ROWS * GRID_W

    @pl.when(b == 0)
    def _build_bias():
        qc = lax.broadcasted_iota(jnp.int32, (GRID_W, LANES), 0)
        ln = lax.broadcasted_iota(jnp.int32, (GRID_W, LANES), 1)
        kc = ln % GRID_W
        cs = jnp.clip(qc - WIN_C // 2, 0, GRID_W - WIN_C)
        valid = jnp.logical_and(kc >= cs, kc < cs + WIN_C)
        dc = jnp.clip(kc - qc + (WIN_C - 1), 0, n_dc - 1)
        dcv = jnp.where(valid, dc, -1)
        neg = jnp.full((GRID_W, LANES), NEG_INF, F32)
        for e in range(2):
            blocks = []
            for a in range(n_dr):
                blk = neg
                for j in range(n_dc):
                    blk = jnp.where(dcv == j, rpb_ref[((2 * hp + e) * n_dr + a) * n_dc + j], blk)
                blocks.append(blk)
            for qb, (k_lo, nk) in enumerate(windows):
                for rl in range(Q_ROWS):
                    r = qb * Q_ROWS + rl
                    rs = min(max(r - kr_n // 2, 0), rows - kr_n)
                    for kp in range(nk // 2):
                        parts = []
                        for kr in (k_lo + 2 * kp, k_lo + 2 * kp + 1):
                            parts.append(blocks[kr - r + WIN_R - 1] if rs <= kr < rs + kr_n else neg)
                        blk = jnp.where(ln < GRID_W, parts[0], parts[1])
                        bias_ref[e, qb, rl * GRID_W:(rl + 1) * GRID_W, kp * LANES:(kp + 1) * LANES] = blk

    lane = lax.broadcasted_iota(jnp.int32, (1, LANES), 1)
    scale = HEAD_DIM ** -0.5
    zero = jnp.zeros((), BF16)
    ck = ck_ref[0].astype(BF16)
    cv = cv_ref[0].astype(BF16)
    m_e = (lane < HALF, lane >= HALF)
    cve = [jnp.where(m, cv, zero) for m in m_e]
    nt = (((1,), (1,)), ((), ()))
    units = [(qb, e) for qb in range(len(windows)) for e in range(2)]
    qrows = [slice(qb * qn, (qb + 1) * qn) for qb in range(len(windows))]
    qe = {(qb, e): jnp.where(m_e[e], q_ref[qrows[qb], :] * scale, zero) for qb, e in units}
    s = {u: lax.dot_general(qe[u], ck, nt, preferred_element_type=F32) for u in units}
    m_run = {u: jnp.max(s[u], axis=-1, keepdims=True) for u in units}
    p = {u: jnp.exp(s[u] - m_run[u]) for u in units}
    l_run = {u: jnp.sum(p[u], axis=-1, keepdims=True) for u in units}
    acc = {(qb, e): _bdot(p[qb, e].astype(BF16), cve[e]) for qb, e in units}
    for kt in range(max(nk for _, nk in windows) * GRID_W // KEY_TILE):
        live = [(qb, e) for qb, e in units if (kt + 1) * KEY_TILE <= windows[qb][1] * GRID_W]
        kt_rows = {qb: slice(windows[qb][0] * GRID_W + kt * KEY_TILE, windows[qb][0] * GRID_W + (kt + 1) * KEY_TILE)
                   for qb, _ in live}
        s = {(qb, e): lax.dot_general(qe[qb, e], k_ref[kt_rows[qb], :], nt, preferred_element_type=F32)
             + bias_ref[e, qb, :, kt * KEY_TILE:(kt + 1) * KEY_TILE] for qb, e in live}
        m_new = {u: jnp.maximum(m_run[u], jnp.max(s[u], axis=-1, keepdims=True)) for u in live}
        alpha = {u: jnp.exp(m_run[u] - m_new[u]) for u in live}
        p = {u: jnp.exp(s[u] - m_new[u]) for u in live}
        for qb, e in live:
            l_run[qb, e] = alpha[qb, e] * l_run[qb, e] + jnp.sum(p[qb, e], axis=-1, keepdims=True)
            acc[qb, e] = alpha[qb, e] * acc[qb, e] + _bdot(p[qb, e].astype(BF16),
                                                           jnp.where(m_e[e], v_ref[kt_rows[qb], :], zero))
            m_run[qb, e] = m_new[qb, e]
    for qb in range(len(windows)):
        o_ref[qrows[qb], :] = (acc[qb, 0] / l_run[qb, 0] + acc[qb, 1] / l_run[qb, 1]).astype(BF16)


def _lat_attn(qkv, ck_tm, cv_tm, rpb, batch, seq):
    rows = seq // GRID_W
    assert rows * GRID_W == seq and rows % Q_ROWS == 0
    k_max = max(nk for _, nk in _lat_windows(rows)) * GRID_W
    past = ck_tm.shape[1]
    kernel = functools.partial(_lat_attn_kernel, rows=rows)
    return pl.pallas_call(
        kernel,
        grid=(N_PAIRS, batch),
        in_specs=[pl.BlockSpec(memory_space=pltpu.SMEM),
                  pl.BlockSpec((seq, LANES), lambda hp, b: (b, hp)),
                  pl.BlockSpec((seq, LANES), lambda hp, b: (b, N_PAIRS + hp)),
                  pl.BlockSpec((seq, LANES), lambda hp, b: (b, 2 * N_PAIRS + hp)),
                  pl.BlockSpec((1, past, LANES), lambda hp, b: (b, 0, hp)),
                  pl.BlockSpec((1, past, LANES), lambda hp, b: (b, 0, hp))],
        out_specs=pl.BlockSpec((seq, LANES), lambda hp, b: (b, hp)),
        out_shape=jax.ShapeDtypeStruct((batch * seq, NA_WIDTH), BF16),
        scratch_shapes=[pltpu.VMEM((2, rows // Q_ROWS, Q_ROWS * GRID_W, k_max), F32)],
        compiler_params=_params(2),
        name="lat_attn",
    )(rpb.reshape(-1), qkv, qkv, qkv, ck_tm, cv_tm)


def _dn_kernel(*refs, seq, nb, hps, a_chunks, has_s0, emit_state):
    it = iter(refs)
    q_ref, k_ref, v_ref, ba_ref, cst_ref = next(it), next(it), next(it), next(it), next(it)
    s0_ref = next(it) if has_s0 else None
    o_ref = next(it)
    st_ref = next(it) if emit_state else None
    (beta_s, g_s, u_s, wq_s, lb_s, el_s, odir_s) = it

    hg = pl.program_id(1)
    wl = hps * LANES
    n = seq // CHUNK
    nt = nb * n
    hp_sl = [slice(h * LANES, (h + 1) * LANES) for h in range(hps)]

    def lanes_cat(parts):
        return parts[0] if len(parts) == 1 else jnp.concatenate(parts, axis=1)

    r128 = lax.broadcasted_iota(jnp.int32, (LANES, LANES), 0)
    c128 = lax.broadcasted_iota(jnp.int32, (LANES, LANES), 1)
    bd_mask = (r128 // HALF) == (c128 // HALF)
    bd_ones = _pair_block_ones()

    ba = ba_ref[...]
    lane_g = lax.broadcasted_iota(jnp.int32, (1, LANES), 1)
    gates = jnp.where(lane_g < 2 * DN_HEADS, _sigmoid(ba),
                      -jnp.exp(cst_ref[0:1, :]) * _softplus(ba + cst_ref[1:2, :]))
    hi, mid, lo = _split3(jnp.where(lane_g < BA_W, gates, 0.0))
    packed = (hi.astype(F32) + pltpu.roll(mid.astype(F32), BA_W, 1)
              + pltpu.roll(lo.astype(F32), 2 * BA_W, 1)).astype(BF16)
    src = lax.broadcasted_iota(jnp.int32, (LANES, 4 * wl), 0)
    col = lax.broadcasted_iota(jnp.int32, (LANES, 4 * wl), 1)
    want = DN_HEADS * (col // wl) + 2 * hps * hg + (col % wl) // HALF
    sel = jnp.where(jnp.logical_and(src < 3 * BA_W, src % BA_W == want), 1.0, 0.0).astype(BF16)
    spread = _bdot(packed, sel)
    for d in range(2):
        beta_s[d] = spread[:, d * wl:(d + 1) * wl]
        g_s[d] = spread[:, (2 + d) * wl:(3 + d) * wl]

    ri = lax.broadcasted_iota(jnp.int32, (CHUNK, LANES), 0)
    lj = lax.broadcasted_iota(jnp.int32, (CHUNK, LANES), 1) % HALF
    lo_half = lax.broadcasted_iota(jnp.int32, (CHUNK, LANES), 1) < HALF
    lo_half2 = (lax.broadcasted_iota(jnp.int32, (CHUNK, 2 * LANES), 1) % LANES) < HALF
    diag = ri == lj
    strict = (ri > lj, ri < lj)
    riw = lax.broadcasted_iota(jnp.int32, (CHUNK, wl), 0)
    ljw = lax.broadcasted_iota(jnp.int32, (CHUNK, wl), 1) % HALF
    diag_w = riw == ljw
    incl_w = (riw >= ljw, riw <= ljw)
    r64 = lax.broadcasted_iota(jnp.int32, (CHUNK, 3 * CHUNK), 0)
    c64 = lax.broadcasted_iota(jnp.int32, (CHUNK, 3 * CHUNK), 1) % CHUNK
    tri3 = (jnp.where(c64 <= r64, 1.0, 0.0).astype(BF16), jnp.where(c64 >= r64, 1.0, 0.0).astype(BF16))

    off_blk = ([], [])
    n_lvl = CHUNK.bit_length() - 1
    for lvl in range(n_lvl):
        s = 1 << lvl
        same2 = (ri // (2 * s)) == (lj // (2 * s))
        off_blk[0].append(jnp.logical_and(same2, (ri // s) == (lj // s) + 1))
        off_blk[1].append(jnp.logical_and(same2, (lj // s) == (ri // s) + 1))

    def stack_heads(x, lo_mask):
        return jnp.concatenate([jnp.where(lo_mask, x, 0.0), jnp.where(lo_mask, 0.0, x)], axis=0).astype(BF16)

    def chunk_rows(c):
        return pl.ds(c * CHUNK if isinstance(c, int) else pl.multiple_of(c * CHUNK, CHUNK), CHUNK)

    def chunk_load(c):
        rows = chunk_rows(c)
        return (q_ref[rows, :].astype(F32), k_ref[rows, :].astype(F32), v_ref[rows, :].astype(F32),
                [beta_s[d, rows, :] for d in range(2)], [g_s[d, rows, :] for d in range(2)])

    def group_a(loaded):
        nc = len(loaded)
        cd = [(i, d) for i in range(nc) for d in range(2)]
        units = [(i, h, d) for i in range(nc) for h in range(hps) for d in range(2)]
        qcs = [ld[0] * (DN_DK ** -0.5) for ld in loaded]
        kcs = [ld[1] for ld in loaded]
        vcs = [ld[2] for ld in loaded]
        kt_bd = {}
        for i in range(nc):
            for h in range(hps):
                kc = kcs[i][:, hp_sl[h]]
                ktt = jnp.transpose(jnp.concatenate([kc, kc], axis=0))
                kt_bd[i, h] = jnp.where(bd_mask, ktt, 0.0)
        gc = {(i, d): _bdot(tri3[d], jnp.concatenate(_split3(loaded[i][4][d]), axis=0)) for i, d in cd}
        rrow = {u: jnp.sum(jnp.where(diag_w, gc[u], 0.0), axis=0, keepdims=True) for u in cd}
        dec = {u: jnp.exp(jnp.minimum(gc[u] - rrow[u], 0.0)) for u in cd}
        eg = {u: jnp.exp(gc[u]) for u in cd}
        kbeta = {(i, d): kcs[i] * loaded[i][3][d] for i, d in cd}
        vbeta = {(i, d): vcs[i] * loaded[i][3][d] for i, d in cd}
        kbeg = {u: kbeta[u] * eg[u] for u in cd}
        m1 = {(i, h): _bdot(jnp.concatenate([qcs[i][:, hp_sl[h]], kbeta[i, 0][:, hp_sl[h]],
                                             kbeta[i, 1][:, hp_sl[h]]], axis=0).astype(BF16),
                            kt_bd[i, h].astype(BF16))
              for i in range(nc) for h in range(hps)}
        amat = {(i, h, d): jnp.where(strict[d], m1[i, h][CHUNK * (1 + d):CHUNK * (2 + d)] * dec[i, d][:, hp_sl[h]], 0.0)
                for i, h, d in units}
        o_st = {(i, h, d): stack_heads(jnp.where(off_blk[d][0], amat[i, h, d], 0.0), lo_half) for i, h, d in units}
        tmat = {(i, h, d): jnp.where(diag, 1.0, 0.0) - jnp.where(off_blk[d][0], amat[i, h, d], 0.0) for i, h, d in units}
        prod = {u: _bdot(amat[u].astype(BF16), o_st[u]) for u in units}
        wmat = {u: amat[u] - prod[u] for u in units}
        for lvl in range(1, n_lvl - 1):
            o_st = {(i, h, d): stack_heads(jnp.where(off_blk[d][lvl], wmat[i, h, d], 0.0), lo_half) for i, h, d in units}
            prod = {u: _bdot(jnp.concatenate([tmat[u], wmat[u]], axis=0).astype(BF16), o_st[u]) for u in units}
            tmat = {u: tmat[u] - prod[u][0:CHUNK] for u in units}
            wmat = {u: wmat[u] - prod[u][CHUNK:] for u in units}
        o_st = {(i, h, d): stack_heads(jnp.where(off_blk[d][n_lvl - 1], wmat[i, h, d], 0.0), lo_half)
                for i, h, d in units}
        prod = {u: _bdot(tmat[u].astype(BF16), o_st[u]) for u in units}
        tmat = {u: tmat[u] - prod[u] for u in units}
        x = {}
        for i, h, d in units:
            rhs = jnp.concatenate([vbeta[i, d][:, hp_sl[h]], kbeg[i, d][:, hp_sl[h]]], axis=1)
            x[i, h, d] = _bdot(tmat[i, h, d].astype(BF16), stack_heads(rhs, lo_half2))
        results = []
        for i in range(nc):
            out = []
            for d in range(2):
                u = lanes_cat([x[i, h, d][:, :LANES] for h in range(hps)])
                w = lanes_cat([x[i, h, d][:, LANES:] for h in range(hps)])
                qk = jnp.where(incl_w[d], lanes_cat([m1[i, h][0:CHUNK] for h in range(hps)]) * dec[i, d], 0.0)
                qg = qcs[i] * eg[i, d]
                glast = gc[i, d][CHUNK - 1:CHUNK, :] if d == 0 else gc[i, d][0:1, :]
                kmul = jnp.exp(glast - rrow[i, d])
                kd_bd = lanes_cat([kt_bd[i, h] * kmul[:, hp_sl[h]] for h in range(hps)])
                out.append((u,
                            jnp.concatenate([w, qg], axis=0).astype(BF16),
                            jnp.concatenate([qk, kd_bd], axis=0).astype(BF16),
                            jnp.broadcast_to(jnp.exp(glast), (SUBLANES, wl))))
            results.append(out)
        return results

    group = min(a_chunks, nt)

    def phase_a(gi, carry):
        cs = [gi * group + j for j in range(group)]
        loaded = [chunk_load(c) for c in cs]
        results = group_a(loaded)
        for c, res in zip(cs, results):
            for d in range(2):
                u_s[d, c], wq_s[d, c], lb_s[d, c], el_s[d, c] = res[d]
        return carry

    if nt // group <= 2:
        for gi in range(nt // group):
            phase_a(gi, 0)
    else:
        lax.fori_loop(0, nt // group, phase_a, 0)

    dh = [(j, d, h) for j in range(nb) for d in range(2) for h in range(hps)]
    if has_s0:
        zero = jnp.zeros((DN_DK, DN_DV), F32)

        def bd_state(j, d, h):
            top = jnp.concatenate([s0_ref[j, 0, d, 2 * h], zero], axis=1)
            bot = jnp.concatenate([zero, s0_ref[j, 0, d, 2 * h + 1]], axis=1)
            return jnp.concatenate([top, bot], axis=0)
        s_init = tuple(bd_state(j, d, h) for j, d, h in dh)
    else:
        s_init = tuple(jnp.zeros((LANES, LANES), F32) for _ in dh)

    def phase_b(s, states):
        jd = [(j, d) for j in range(nb) for d in range(2)]
        cs = {(j, d): j * n + (s if d == 0 else n - 1 - s) for j, d in jd}
        ld = {u: (wq_s[u[1], cs[u]], u_s[u[1], cs[u]], lb_s[u[1], cs[u]], el_s[u[1], cs[u], 0:1, :]) for u in jd}
        st = dict(zip(dh, states))
        ws = {(j, d, h): _bdot(ld[j, d][0][:, hp_sl[h]], st[j, d, h].astype(BF16)) for j, d, h in dh}
        vn = {(j, d, h): ld[j, d][1][:, hp_sl[h]] - ws[j, d, h][0:CHUNK] for j, d, h in dh}
        r2 = {(j, d, h): _bdot(ld[j, d][2][:, hp_sl[h]], stack_heads(vn[j, d, h], lo_half)) for j, d, h in dh}
        new_states = tuple(st[j, d, h] * ld[j, d][3][:, hp_sl[h]] + r2[j, d, h][CHUNK:] for j, d, h in dh)
        for j, d in jd:
            odir_s[d, chunk_rows(cs[j, d]), :] = lanes_cat([ws[j, d, h][CHUNK:] + r2[j, d, h][0:CHUNK]
                                                            for h in range(hps)])
        return new_states

    s_fin = dict(zip(dh, lax.fori_loop(0, n, phase_b, s_init)))

    for h in range(hps):
        o = odir_s[0, :, hp_sl[h]] + odir_s[1, :, hp_sl[h]]
        msq = _dot2_l(o * o, bd_ones) * (1.0 / DN_DV)
        o_ref[:, hp_sl[h]] = (o * lax.rsqrt(msq + EPS) * cst_ref[2:3, :]).astype(BF16)

    if emit_state:
        for j, d, h in dh:
            st_ref[j, 0, d, 2 * h] = s_fin[j, d, h][0:DN_DK, 0:DN_DV]
            st_ref[j, 0, d, 2 * h + 1] = s_fin[j, d, h][DN_DK:, DN_DV:]


def _deltanet(dqkv, ba, cst, batch, seq, nb, hps, a_chunks, s0=None, emit_state=False):
    assert batch % nb == 0
    rows = nb * seq
    nt = rows // CHUNK
    wl = hps * LANES
    groups = N_PAIRS // hps
    has_s0 = s0 is not None

    def col(cbase):
        return pl.BlockSpec((rows, wl), lambda b, hg: (b, cbase // wl + hg))

    st_spec = pl.BlockSpec((nb, 1, 2, 2 * hps, DN_DK, DN_DV), lambda b, hg: (b, 0, 0, hg, 0, 0))
    in_specs = [col(0), col(DN_WIDTH), col(2 * DN_WIDTH),
                pl.BlockSpec((rows, LANES), lambda b, hg: (b, 0)),
                pl.BlockSpec((SUBLANES, LANES), lambda b, hg: (0, 0))]
    args = [dqkv, dqkv, dqkv, ba, cst]
    if has_s0:
        in_specs.append(st_spec)
        args.append(s0)
    out_specs = [pl.BlockSpec((rows, wl), lambda b, hg: (b, hg))]
    out_shape = [jax.ShapeDtypeStruct((batch * seq, DN_HEADS * DN_DV), BF16)]
    if emit_state:
        out_specs.append(st_spec)
        out_shape.append(jax.ShapeDtypeStruct((batch, 1, 2, DN_HEADS, DN_DK, DN_DV), F32))
    scratch = [pltpu.VMEM((2, rows, wl), F32), pltpu.VMEM((2, rows, wl), F32),
               pltpu.VMEM((2, nt, CHUNK, wl), F32),
               pltpu.VMEM((2, nt, 2 * CHUNK, wl), BF16),
               pltpu.VMEM((2, nt, 3 * CHUNK, wl), BF16),
               pltpu.VMEM((2, nt, SUBLANES, wl), F32),
               pltpu.VMEM((2, rows, wl), F32)]
    res = pl.pallas_call(
        functools.partial(_dn_kernel, seq=seq, nb=nb, hps=hps, a_chunks=a_chunks, has_s0=has_s0,
                          emit_state=emit_state),
        grid=(batch // nb, groups),
        in_specs=in_specs,
        out_specs=out_specs,
        out_shape=out_shape,
        scratch_shapes=scratch,
        compiler_params=_params(2),
        name="deltanet_lat" if has_s0 else "deltanet_ctx",
    )(*args)
    return res


TM_F = 2 * TM


def _merge_ffn_kernel(x_ref, ona_ref, odn_ref, gz_ref, mod_ref, wpost1_ref, wpre2_ref, wpost2_ref,
                      wao_ref, wdo_ref, wout_ref, w1_ref, w2_ref, o_ref):
    mod = mod_ref[0]
    subs = [slice(s * TM, (s + 1) * TM) for s in range(TM_F // TM)]
    z_sl, gna_sl, gdn_sl = (slice(0, DN_WIDTH), slice(DN_WIDTH, DN_WIDTH + D_MODEL),
                            slice(DN_WIDTH + D_MODEL, GZ_W))
    odn = [odn_ref[sl, :].astype(F32) * _silu(gz_ref[sl, z_sl].astype(F32)) for sl in subs]
    br_na = [_bdot(ona_ref[sl, :], wao_ref[...]) for sl in subs]
    br_dn = [_bdot(t.astype(BF16), wdo_ref[...]) for t in odn]
    m = [_sigmoid(gz_ref[sl, gna_sl].astype(F32)) * a + _sigmoid(gz_ref[sl, gdn_sl].astype(F32)) * b
         for sl, a, b in zip(subs, br_na, br_dn)]
    y = [_bdot(t.astype(BF16), wout_ref[...]) for t in m]
    x1 = [x_ref[sl, :] + mod[2:3] * _rms_rows(t, wpost1_ref[...]) for sl, t in zip(subs, y)]
    h = [_rms_rows(t, wpre2_ref[...]) * (1.0 + mod[4:5]) + mod[3:4] for t in x1]
    f1 = [_bdot(t.astype(BF16), w1_ref[...]) for t in h]
    r = [jnp.maximum(t, 0.0) for t in f1]
    f = [_bdot((t * t).astype(BF16), w2_ref[...]) for t in r]
    for sl, a, b in zip(subs, x1, f):
        o_ref[sl, :] = a + mod[5:6] * _rms_rows(b, wpost2_ref[...])


def _merge_ffn(x2d, o_na, o_dn, gz, mod3, mod_row_of_tile, wpost1, wpre2, wpost2, w_ao, w_do, w_out, w1, w2):
    n_tok = x2d.shape[0]
    row = lambda i: (i, 0)
    const = lambda i: (0, 0)

    def resident(shape):
        return pl.BlockSpec(shape, const, pipeline_mode=pl.Buffered(1))

    return pl.pallas_call(
        _merge_ffn_kernel,
        grid=(n_tok // TM_F,),
        in_specs=[pl.BlockSpec((TM_F, D_MODEL), row),
                  pl.BlockSpec((TM_F, NA_WIDTH), row),
                  pl.BlockSpec((TM_F, DN_WIDTH), row),
                  pl.BlockSpec((TM_F, GZ_W), row),
                  pl.BlockSpec((1, N_MOD, D_MODEL), lambda i: (mod_row_of_tile(i), 0, 0)),
                  pl.BlockSpec((1, D_MODEL), const),
                  pl.BlockSpec((1, D_MODEL), const),
                  pl.BlockSpec((1, D_MODEL), const),
                  resident((NA_WIDTH, D_MODEL)),
                  resident((DN_WIDTH, D_MODEL)),
                  resident((D_MODEL, D_MODEL)),
                  resident((D_MODEL, D_FF)),
                  resident((D_FF, D_MODEL))],
        out_specs=pl.BlockSpec((TM_F, D_MODEL), row),
        out_shape=jax.ShapeDtypeStruct((n_tok, D_MODEL), F32),
        compiler_params=_params(1),
        name="merge_ffn",
    )(x2d, o_na, o_dn, gz, mod3, wpost1, wpre2, wpost2, w_ao, w_do, w_out, w1, w2)


def _pack_w_in(w):
    assert w.shape[1] == IN_COLS
    wb = w.astype(BF16)
    return wb, jnp.pad(wb[:, MAIN_COLS:], ((0, 0), (0, LANES - (IN_COLS - MAIN_COLS))))


def kernel(x_prompt, x_sample, c, cache_na_k, cache_na_v, state_delta, c_ctx, w_ada, b_ada, norm_pre1, norm_post1, norm_pre2, norm_post2, w_in, conv_w, a_log, dt_bias, dn_norm, na_rpb, w_ao, w_do, w_out, w_ff1, w_ff2):
    batch, seq, _ = x_prompt.shape
    dec_batch, dec_seq, _ = x_sample.shape
    depth = w_in.shape[0]
    assert depth == 1 and seq == TM and dec_seq % TM_F == 0 and (batch * seq) % TM_F == 0 and dec_batch < SUBLANES

    xp = x_prompt.reshape(batch * seq, D_MODEL)
    xs = x_sample.reshape(dec_batch * dec_seq, D_MODEL)
    l = 0

    cvecs = jnp.concatenate([c, c_ctx[None], jnp.zeros((SUBLANES - dec_batch - 1, D_MODEL), F32)], axis=0)
    mod3 = _adaln(cvecs, w_ada[l], b_ada[l]).reshape(SUBLANES, N_MOD, D_MODEL)
    ctx_row = lambda i: dec_batch
    lat_row = lambda i: i // (dec_seq // TMI)

    w_in_p = _pack_w_in(w_in[l])
    w_ao_b = w_ao[l].astype(BF16)
    w_do_b = w_do[l].astype(BF16)
    w_out_b = w_out[l].astype(BF16)
    w1_b = w_ff1[l].astype(BF16)
    w2_b = w_ff2[l].astype(BF16)
    wpre1 = norm_pre1[l].reshape(1, D_MODEL)
    wpost1 = norm_post1[l].reshape(1, D_MODEL)
    wpre2 = norm_pre2[l].reshape(1, D_MODEL)
    wpost2 = norm_post2[l].reshape(1, D_MODEL)

    conv_w8 = jnp.pad(conv_w[l], ((0, SUBLANES - CONV_W), (0, 0)))
    gate_rows = jnp.pad(jnp.stack([a_log[l].reshape(-1), dt_bias[l].reshape(-1)]),
                        ((0, 0), (2 * DN_HEADS, LANES - 4 * DN_HEADS)))
    cst = jnp.concatenate([gate_rows, jnp.tile(dn_norm[l], 2)[None],
                           jnp.zeros((SUBLANES - 3, LANES), F32)], axis=0)

    qkv_c, dqkv_c, gz_c, ba_c, new_k, new_v = _inproj(xp, mod3, ctx_row, wpre1, w_in_p, conv_w8, seq, batch)
    o_na_c = _ctx_attn(qkv_c, batch, seq, nb=2)
    o_dn_c, new_s = _deltanet(dqkv_c, ba_c, cst, batch, seq, nb=2, hps=4, a_chunks=4, emit_state=True)
    y_c = _merge_ffn(xp, o_na_c, o_dn_c, gz_c, mod3, ctx_row, wpost1, wpre2, wpost2,
                     w_ao_b, w_do_b, w_out_b, w1_b, w2_b)

    qkv_l, dqkv_l, gz_l, ba_l = _inproj(xs, mod3, lat_row, wpre1, w_in_p, conv_w8, dec_seq, dec_batch,
                                        rope_tabs=_rope_tables(dec_seq))
    past = cache_na_k.shape[3]
    ck_tm = jnp.transpose(cache_na_k[:, l], (0, 2, 1, 3)).reshape(dec_batch, past, NA_WIDTH)
    cv_tm = jnp.transpose(cache_na_v[:, l], (0, 2, 1, 3)).reshape(dec_batch, past, NA_WIDTH)
    o_na_l = _lat_attn(qkv_l, ck_tm, cv_tm, na_rpb[l], dec_batch, dec_seq)
    (o_dn_l,) = _deltanet(dqkv_l, ba_l, cst, dec_batch, dec_seq, nb=1, hps=4, a_chunks=2, s0=state_delta)
    y_l = _merge_ffn(xs, o_na_l, o_dn_l, gz_l, mod3, lambda i: i // (dec_seq // TM_F), wpost1, wpre2, wpost2,
                     w_ao_b, w_do_b, w_out_b, w1_b, w2_b)

    return (y_c.reshape(batch, seq, D_MODEL), y_l.reshape(dec_batch, dec_seq, D_MODEL),
            new_k, new_v, new_s)
```

```python
import functools

import numpy as np
import jax
import jax.numpy as jnp
from jax import lax
from jax.experimental import pallas as pl
from jax.experimental.pallas import tpu as pltpu

F32 = jnp.float32
BF16 = jnp.bfloat16

D_MODEL = 1024
N_MOD = 6
GRID_W = 64
NA_HEADS = 8
HEAD_DIM = 64
NA_WIDTH = NA_HEADS * HEAD_DIM
WIN_R = 8
WIN_C = 16
ROPE_BASE = 10000.0
DN_HEADS = 8
DN_DK = 64
DN_DV = 64
DN_WIDTH = DN_HEADS * DN_DK
CONV_W = 5
CHUNK = 64
D_FF = 4 * D_MODEL
EPS = 1e-6
NEG_INF = -1e30

LANES = 128
SUBLANES = 8
HALF = LANES // 2
assert HEAD_DIM == HALF and DN_DK == HALF and DN_DV == HALF and CHUNK == HALF
N_PAIRS = NA_HEADS // 2
TM = 256
TMI = 2 * TM
VMEM_LIMIT = 56 * 1024 * 1024

QKV_W = 3 * NA_WIDTH
DQKV_W = 3 * DN_WIDTH
BA_W = 4 * DN_HEADS
GZ_W = DN_WIDTH + 2 * D_MODEL
COL_QKV, COL_DQKV, COL_TAIL = 0, QKV_W, QKV_W + DQKV_W
IN_COLS = COL_TAIL + BA_W + GZ_W
MAIN_COLS = (IN_COLS // LANES) * LANES
TAIL_W = MAIN_COLS - COL_TAIL + LANES


def _params(n_grid):
    return pltpu.CompilerParams(dimension_semantics=("arbitrary",) * n_grid,
                                vmem_limit_bytes=VMEM_LIMIT)


def _bdot(a, b):
    return jnp.dot(a, b, preferred_element_type=F32)


def _split3(a):
    hi = a.astype(BF16)
    r1 = a - hi.astype(F32)
    mid = r1.astype(BF16)
    lo = (r1 - mid.astype(F32)).astype(BF16)
    return hi, mid, lo


def _dot2_l(a, b_exact):
    hi = a.astype(BF16)
    lo = (a - hi.astype(F32)).astype(BF16)
    return _bdot(jnp.concatenate([hi, lo], axis=1), jnp.concatenate([b_exact, b_exact], axis=0))


def _sigmoid(x):
    return 0.5 * jnp.tanh(0.5 * x) + 0.5


def _silu(x):
    return x * _sigmoid(x)


def _softplus(x):
    return jnp.maximum(x, 0.0) + jnp.log(1.0 + jnp.exp(-jnp.abs(x)))


def _rms_rows(x, w):
    ms = jnp.mean(x * x, axis=-1, keepdims=True)
    return x * lax.rsqrt(ms + EPS) * w


def _pair_block_ones():
    r = lax.broadcasted_iota(jnp.int32, (LANES, LANES), 0)
    c = lax.broadcasted_iota(jnp.int32, (LANES, LANES), 1)
    return jnp.where((r // HALF) == (c // HALF), 1.0, 0.0).astype(BF16)


def _adaln_kernel(c_ref, w_ref, b_ref, o_ref):
    c = c_ref[...]
    s = _silu(c).astype(BF16)
    o_ref[...] = _bdot(s, w_ref[...].astype(BF16)) + b_ref[...]


def _adaln(cvecs, w_ada, b_ada):
    tn = 1024
    n = w_ada.shape[1]
    return pl.pallas_call(
        _adaln_kernel,
        grid=(n // tn,),
        in_specs=[pl.BlockSpec((SUBLANES, D_MODEL), lambda j: (0, 0)),
                  pl.BlockSpec((D_MODEL, tn), lambda j: (0, j)),
                  pl.BlockSpec((1, tn), lambda j: (0, j))],
        out_specs=pl.BlockSpec((SUBLANES, tn), lambda j: (0, j)),
        out_shape=jax.ShapeDtypeStruct((SUBLANES, n), F32),
        compiler_params=_params(1),
        name="adaln",
    )(cvecs, w_ada, b_ada.reshape(1, n))


def _inproj_kernel(*refs, latent, tiles_per_seq):
    it = iter(refs)
    x_ref, mod_ref, wpre_ref, w_ref, wlast_ref, cw_ref = (next(it), next(it), next(it), next(it), next(it),
                                                          next(it))
    if latent:
        xprev_ref, xnext_ref, cos_ref, sin_ref = next(it), next(it), next(it), next(it)
    qkv_ref, dqkv_ref, gz_ref, ba_ref = next(it), next(it), next(it), next(it)
    if not latent:
        kc_ref, vc_ref = next(it), next(it)

    mod = mod_ref[0]
    n_sub = TMI // TM
    bd_ones = _pair_block_ones()

    def modulated(x):
        return (_rms_rows(x, wpre_ref[...]) * (1.0 + mod[1:2]) + mod[0:1]).astype(BF16)

    for sub in range(n_sub):
        rs = slice(sub * TM, (sub + 1) * TM)
        hb = modulated(x_ref[rs, :])

        w_dn = w_ref[:, COL_DQKV:COL_DQKV + DQKV_W]
        halo_zero = jnp.zeros((SUBLANES, DQKV_W), F32)
        if latent:
            t = (pl.program_id(0) * n_sub + sub) % tiles_per_seq
            prev = x_ref[sub * TM - SUBLANES:sub * TM, :] if sub > 0 else xprev_ref[...]
            nxt = x_ref[(sub + 1) * TM:(sub + 1) * TM + SUBLANES, :] if sub < n_sub - 1 else xnext_ref[...]
            hb_ext = jnp.concatenate([modulated(prev), hb, modulated(nxt)], axis=0)
            pe = _bdot(hb_ext, w_dn)
            pe = jnp.concatenate([jnp.where(t > 0, pe[:SUBLANES], halo_zero), pe[SUBLANES:SUBLANES + TM],
                                  jnp.where(t < tiles_per_seq - 1, pe[SUBLANES + TM:], halo_zero)], axis=0)
        else:
            pe = jnp.concatenate([halo_zero, _bdot(hb, w_dn), halo_zero], axis=0)
        acc = None
        for j in sorted(range(CONV_W), key=lambda s: abs(s - CONV_W // 2)):
            lo = SUBLANES - CONV_W // 2 + j
            term = pe[lo:lo + TM] * cw_ref[j:j + 1, :]
            acc = term if acc is None else acc + term
        y = _silu(acc)

        p = _bdot(hb, w_ref[:, COL_QKV:COL_QKV + QKV_W])
        if latent:
            qk = p[:, :2 * NA_WIDTH]
            lane = lax.broadcasted_iota(jnp.int32, qk.shape, 1)
            nf = HEAD_DIM // 4
            first = (lane % (2 * nf)) < nf
            partner = jnp.where(first, pltpu.roll(qk, 2 * NA_WIDTH - nf, 1), pltpu.roll(qk, nf, 1))
            qkv_ref[rs, :2 * NA_WIDTH] = (qk * cos_ref[rs, :] + partner * sin_ref[rs, :]).astype(BF16)
            qkv_ref[rs, 2 * NA_WIDTH:] = p[:, 2 * NA_WIDTH:].astype(BF16)
        else:
            qkv_ref[rs, :] = p.astype(BF16)
            for h in range(NA_HEADS):
                kc_ref[sub, 0, h] = p[:, NA_WIDTH + h * HEAD_DIM:NA_WIDTH + (h + 1) * HEAD_DIM]
                vc_ref[sub, 0, h] = p[:, 2 * NA_WIDTH + h * HEAD_DIM:2 * NA_WIDTH + (h + 1) * HEAD_DIM]

        split = COL_TAIL + ((MAIN_COLS - COL_TAIL) // LANES // 2) * LANES
        tail = jnp.concatenate([_bdot(hb, w_ref[:, COL_TAIL:split]), _bdot(hb, w_ref[:, split:MAIN_COLS]),
                                _bdot(hb, wlast_ref[...])], axis=1)
        ba_ref[rs, :] = tail[:, :LANES]
        gz_ref[rs, :] = pltpu.roll(tail, TAIL_W - BA_W, 1)[:, :GZ_W].astype(BF16)

        for s in range(DQKV_W // LANES):
            ys = y[:, s * LANES:(s + 1) * LANES]
            if s < 2 * DN_WIDTH // LANES:
                ys = ys * lax.rsqrt(_dot2_l(ys * ys, bd_ones) + EPS)
            dqkv_ref[rs, s * LANES:(s + 1) * LANES] = ys.astype(BF16)


def _inproj(x2d, mod3, mod_row_of_tile, wpre, w_packed, conv_w8, seq, batch, rope_tabs=None):
    n_tok = x2d.shape[0]
    latent = rope_tabs is not None
    assert n_tok % TMI == 0 and seq % TM == 0 and (seq % TMI == 0 or TMI % seq == 0)
    rows8 = TMI // SUBLANES
    n_blk8 = n_tok // SUBLANES

    def resident(shape):
        return pl.BlockSpec(shape, lambda i: (0, 0), pipeline_mode=pl.Buffered(1))

    in_specs = [pl.BlockSpec((TMI, D_MODEL), lambda i: (i, 0)),
                pl.BlockSpec((1, N_MOD, D_MODEL), lambda i: (mod_row_of_tile(i), 0, 0)),
                pl.BlockSpec((1, D_MODEL), lambda i: (0, 0)),
                resident((D_MODEL, MAIN_COLS)),
                resident((D_MODEL, LANES)),
                pl.BlockSpec((SUBLANES, DQKV_W), lambda i: (0, 0))]
    w_main, w_last = w_packed
    args = [x2d, mod3, wpre, w_main, w_last, conv_w8]
    out_specs = [pl.BlockSpec((TMI, QKV_W), lambda i: (i, 0)),
                 pl.BlockSpec((TMI, DQKV_W), lambda i: (i, 0)),
                 pl.BlockSpec((TMI, GZ_W), lambda i: (i, 0)),
                 pl.BlockSpec((TMI, LANES), lambda i: (i, 0))]
    out_shape = [jax.ShapeDtypeStruct((n_tok, QKV_W), BF16),
                 jax.ShapeDtypeStruct((n_tok, DQKV_W), BF16),
                 jax.ShapeDtypeStruct((n_tok, GZ_W), BF16),
                 jax.ShapeDtypeStruct((n_tok, LANES), F32)]
    if latent:
        steps_per_seq = seq // TMI
        in_specs += [pl.BlockSpec((SUBLANES, D_MODEL), lambda i: (jnp.maximum(i * rows8 - 1, 0), 0)),
                     pl.BlockSpec((SUBLANES, D_MODEL), lambda i: (jnp.minimum((i + 1) * rows8, n_blk8 - 1), 0))]
        args += [x2d, x2d]
        for tab in rope_tabs:
            in_specs.append(pl.BlockSpec((TMI, 2 * NA_WIDTH), lambda i: (i % steps_per_seq, 0)))
            args.append(tab)
    else:
        assert seq == TM
        seqs = TMI // seq
        cache_spec = pl.BlockSpec((seqs, 1, NA_HEADS, seq, HEAD_DIM), lambda i: (i, 0, 0, 0, 0))
        cache_shape = jax.ShapeDtypeStruct((batch, 1, NA_HEADS, seq, HEAD_DIM), F32)
        out_specs += [cache_spec, cache_spec]
        out_shape += [cache_shape, cache_shape]
    return pl.pallas_call(
        functools.partial(_inproj_kernel, latent=latent, tiles_per_seq=seq // TM),
        grid=(n_tok // TMI,),
        in_specs=in_specs,
        out_specs=out_specs,
        out_shape=out_shape,
        compiler_params=_params(1),
        name="inproj_lat" if latent else "inproj_ctx",
    )(*args)


def _rope_tables(seq):
    half = HEAD_DIM // 2
    nf = half // 2
    pos = np.arange(seq)
    inv = (np.float32(ROPE_BASE) ** (-np.arange(nf, dtype=np.float32) / np.float32(nf))).astype(np.float32)
    ang_r = ((pos // GRID_W).astype(np.float32)[:, None] * inv).astype(np.float64)
    ang_c = ((pos % GRID_W).astype(np.float32)[:, None] * inv).astype(np.float64)

    def grp(ang):
        return (np.concatenate([np.cos(ang), np.cos(ang)], 1),
                np.concatenate([-np.sin(ang), np.sin(ang)], 1))

    cr, sr = grp(ang_r)
    cc, sc = grp(ang_c)
    cos_h = np.concatenate([cr, cc], 1)
    sin_h = np.concatenate([sr, sc], 1)
    reps = 2 * NA_HEADS
    return (jnp.asarray(np.tile(cos_h, (1, reps)), F32),
            jnp.asarray(np.tile(sin_h, (1, reps)), F32))


def _ctx_attn_kernel(q_ref, k_ref, v_ref, o_ref, *, seq, nb):
    lane = lax.broadcasted_iota(jnp.int32, (1, LANES), 1)
    scale = HEAD_DIM ** -0.5
    pairs = [(j, hp) for j in range(nb) for hp in range(N_PAIRS)]
    heads = [(j, hp, e) for j, hp in pairs for e in range(2)]
    rows = [slice(j * seq, (j + 1) * seq) for j in range(nb)]
    sls = [slice(hp * LANES, (hp + 1) * LANES) for hp in range(N_PAIRS)]
    q = {(j, hp): q_ref[rows[j], sls[hp]] for j, hp in pairs}
    k = {(j, hp): k_ref[rows[j], sls[hp]] for j, hp in pairs}
    v = {(j, hp): v_ref[rows[j], sls[hp]] for j, hp in pairs}
    m_e = (lane < HALF, lane >= HALF)
    zero = jnp.zeros((), BF16)
    s = {(j, hp, e): lax.dot_general(jnp.where(m_e[e], q[j, hp] * scale, zero), k[j, hp],
                                     (((1,), (1,)), ((), ())), preferred_element_type=F32) for j, hp, e in heads}
    p = {u: jnp.exp(s[u] - jnp.max(s[u], axis=-1, keepdims=True)) for u in heads}
    l = {u: jnp.sum(p[u], axis=-1, keepdims=True) for u in heads}
    o = {(j, hp, e): _bdot(p[j, hp, e].astype(BF16), jnp.where(m_e[e], v[j, hp], zero)) for j, hp, e in heads}
    for j, hp in pairs:
        o_ref[rows[j], sls[hp]] = (o[j, hp, 0] / l[j, hp, 0] + o[j, hp, 1] / l[j, hp, 1]).astype(BF16)


def _ctx_attn(qkv, batch, seq, nb):
    assert batch % nb == 0

    def col(cb):
        return pl.BlockSpec((nb * seq, NA_WIDTH), lambda b: (b, cb))
    return pl.pallas_call(
        functools.partial(_ctx_attn_kernel, seq=seq, nb=nb),
        grid=(batch // nb,),
        in_specs=[col(0), col(1), col(2)],
        out_specs=pl.BlockSpec((nb * seq, NA_WIDTH), lambda b: (b, 0)),
        out_shape=jax.ShapeDtypeStruct((batch * seq, NA_WIDTH), BF16),
        compiler_params=_params(1),
        name="ctx_attn",
    )(qkv, qkv, qkv)


Q_ROWS = 4


def _lat_windows(rows):
    kr_n = min(WIN_R, rows)
    out = []
    for qb in range(rows // Q_ROWS):
        starts = [min(max(r - kr_n // 2, 0), rows - kr_n) for r in range(qb * Q_ROWS, (qb + 1) * Q_ROWS)]
        k_lo, k_hi = min(starts), max(starts) + kr_n
        if (k_hi - k_lo) % 2:
            if k_hi < rows:
                k_hi += 1
            else:
                k_lo -= 1
        out.append((k_lo, k_hi - k_lo))
    return out


def _lat_attn_kernel(rpb_ref, q_ref, k_ref, v_ref, ck_ref, cv_ref, o_ref, bias_ref, *, rows):
    hp = pl.program_id(0)
    b = pl.program_id(1)
    kr_n = min(WIN_R, rows)
    windows = _lat_windows(rows)
    n_dr = 2 * WIN_R - 1
    n_dc = 2 * WIN_C - 1
    qn = Q_ROWS * GRID_W

    @pl.when(b == 0)
    def _build_bias():
        qc = lax.broadcasted_iota(jnp.int32, (GRID_W, LANES), 0)
        ln = lax.broadcasted_iota(jnp.int32, (GRID_W, LANES), 1)
        kc = ln % GRID_W
        cs = jnp.clip(qc - WIN_C // 2, 0, GRID_W - WIN_C)
        valid = jnp.logical_and(kc >= cs, kc < cs + WIN_C)
        dc = jnp.clip(kc - qc + (WIN_C - 1), 0, n_dc - 1)
        dcv = jnp.where(valid, dc, -1)
        neg = jnp.full((GRID_W, LANES), NEG_INF, F32)
        for e in range(2):
            blocks = []
            for a in range(n_dr):
                blk = neg
                for j in range(n_dc):
                    blk = jnp.where(dcv == j, rpb_ref[((2 * hp + e) * n_dr + a) * n_dc + j], blk)
                blocks.append(blk)
            for qb, (k_lo, nk) in enumerate(windows):
                for rl in range(Q_ROWS):
                    r = qb * Q_ROWS + rl
                    rs = min(max(r - kr_n // 2, 0), rows - kr_n)
                    for kp in range(nk // 2):
                        parts = []
                        for kr in (k_lo + 2 * kp, k_lo + 2 * kp + 1):
                            parts.append(blocks[kr - r + WIN_R - 1] if rs <= kr < rs + kr_n else neg)
                        blk = jnp.where(ln < GRID_W, parts[0], parts[1])
                        bias_ref[e, qb, rl * GRID_W:(rl + 1) * GRID_W, kp * LANES:(kp + 1) * LANES] = blk

    lane = lax.broadcasted_iota(jnp.int32, (1, LANES), 1)
    scale = HEAD_DIM ** -0.5
    zero = jnp.zeros((), BF16)
    ck = ck_ref[0].astype(BF16)
    cv = cv_ref[0].astype(BF16)
    m_e = (lane < HALF, lane >= HALF)
    cve = [jnp.where(m, cv, zero) for m in m_e]
    nt = (((1,), (1,)), ((), ()))
    for qb, (k_lo, nk) in enumerate(windows):
        qrows = slice(qb * qn, (qb + 1) * qn)
        krows = slice(k_lo * GRID_W, (k_lo + nk) * GRID_W)
        kw = k_ref[krows, :]
        vw = v_ref[krows, :]
        qe = [jnp.where(m, q_ref[qrows, :] * scale, zero) for m in m_e]
        s_loc = [lax.dot_general(qe[e], kw, nt, preferred_element_type=F32) + bias_ref[e, qb, :, :nk * GRID_W]
                 for e in range(2)]
        s_ctx = [lax.dot_general(qe[e], ck, nt, preferred_element_type=F32) for e in range(2)]
        m = [jnp.maximum(jnp.max(s_loc[e], axis=-1, keepdims=True), jnp.max(s_ctx[e], axis=-1, keepdims=True))
             for e in range(2)]
        p_loc = [jnp.exp(s_loc[e] - m[e]) for e in range(2)]
        p_ctx = [jnp.exp(s_ctx[e] - m[e]) for e in range(2)]
        l = [jnp.sum(p_loc[e], axis=-1, keepdims=True) + jnp.sum(p_ctx[e], axis=-1, keepdims=True)
             for e in range(2)]
        o = [(_bdot(p_loc[e].astype(BF16), jnp.where(m_e[e], vw, zero)) + _bdot(p_ctx[e].astype(BF16), cve[e]))
             / l[e] for e in range(2)]
        o_ref[qrows, :] = (o[0] + o[1]).astype(BF16)


def _lat_attn(qkv, ck_tm, cv_tm, rpb, batch, seq):
    rows = seq // GRID_W
    assert rows * GRID_W == seq and rows % Q_ROWS == 0
    k_max = max(nk for _, nk in _lat_windows(rows)) * GRID_W
    past = ck_tm.shape[1]
    kernel = functools.partial(_lat_attn_kernel, rows=rows)
    return pl.pallas_call(
        kernel,
        grid=(N_PAIRS, batch),
        in_specs=[pl.BlockSpec(memory_space=pltpu.SMEM),
                  pl.BlockSpec((seq, LANES), lambda hp, b: (b, hp)),
                  pl.BlockSpec((seq, LANES), lambda hp, b: (b, N_PAIRS + hp)),
                  pl.BlockSpec((seq, LANES), lambda hp, b: (b, 2 * N_PAIRS + hp)),
                  pl.BlockSpec((1, past, LANES), lambda hp, b: (b, 0, hp)),
                  pl.BlockSpec((1, past, LANES), lambda hp, b: (b, 0, hp))],
        out_specs=pl.BlockSpec((seq, LANES), lambda hp, b: (b, hp)),
        out_shape=jax.ShapeDtypeStruct((batch * seq, NA_WIDTH), BF16),
        scratch_shapes=[pltpu.VMEM((2, rows // Q_ROWS, Q_ROWS * GRID_W, k_max), F32)],
        compiler_params=_params(2),
        name="lat_attn",
    )(rpb.reshape(-1), qkv, qkv, qkv, ck_tm, cv_tm)


def _dn_kernel(*refs, seq, nb, hps, a_chunks, has_s0, emit_state):
    it = iter(refs)
    q_ref, k_ref, v_ref, ba_ref, cst_ref = next(it), next(it), next(it), next(it), next(it)
    s0_ref = next(it) if has_s0 else None
    o_ref = next(it)
    st_ref = next(it) if emit_state else None
    (beta_s, g_s, u_s, wq_s, lb_s, el_s, odir_s) = it

    hg = pl.program_id(1)
    wl = hps * LANES
    n = seq // CHUNK
    nt = nb * n
    hp_sl = [slice(h * LANES, (h + 1) * LANES) for h in range(hps)]

    def lanes_cat(parts):
        return parts[0] if len(parts) == 1 else jnp.concatenate(parts, axis=1)

    r128 = lax.broadcasted_iota(jnp.int32, (LANES, LANES), 0)
    c128 = lax.broadcasted_iota(jnp.int32, (LANES, LANES), 1)
    bd_mask = (r128 // HALF) == (c128 // HALF)
    bd_ones = _pair_block_ones()

    ba = ba_ref[...]
    lane_g = lax.broadcasted_iota(jnp.int32, (1, LANES), 1)
    gates = jnp.where(lane_g < 2 * DN_HEADS, _sigmoid(ba),
                      -jnp.exp(cst_ref[0:1, :]) * _softplus(ba + cst_ref[1:2, :]))
    hi, mid, lo = _split3(jnp.where(lane_g < BA_W, gates, 0.0))
    packed = (hi.astype(F32) + pltpu.roll(mid.astype(F32), BA_W, 1)
              + pltpu.roll(lo.astype(F32), 2 * BA_W, 1)).astype(BF16)
    src = lax.broadcasted_iota(jnp.int32, (LANES, 4 * wl), 0)
    col = lax.broadcasted_iota(jnp.int32, (LANES, 4 * wl), 1)
    want = DN_HEADS * (col // wl) + 2 * hps * hg + (col % wl) // HALF
    sel = jnp.where(jnp.logical_and(src < 3 * BA_W, src % BA_W == want), 1.0, 0.0).astype(BF16)
    spread = _bdot(packed, sel)
    for d in range(2):
        beta_s[d] = spread[:, d * wl:(d + 1) * wl]
        g_s[d] = spread[:, (2 + d) * wl:(3 + d) * wl]

    ri = lax.broadcasted_iota(jnp.int32, (CHUNK, LANES), 0)
    lj = lax.broadcasted_iota(jnp.int32, (CHUNK, LANES), 1) % HALF
    lo_half = lax.broadcasted_iota(jnp.int32, (CHUNK, LANES), 1) < HALF
    lo_half2 = (lax.broadcasted_iota(jnp.int32, (CHUNK, 2 * LANES), 1) % LANES) < HALF
    diag = ri == lj
    strict = (ri > lj, ri < lj)
    riw = lax.broadcasted_iota(jnp.int32, (CHUNK, wl), 0)
    ljw = lax.broadcasted_iota(jnp.int32, (CHUNK, wl), 1) % HALF
    diag_w = riw == ljw
    incl_w = (riw >= ljw, riw <= ljw)
    r64 = lax.broadcasted_iota(jnp.int32, (CHUNK, 3 * CHUNK), 0)
    c64 = lax.broadcasted_iota(jnp.int32, (CHUNK, 3 * CHUNK), 1) % CHUNK
    tri3 = (jnp.where(c64 <= r64, 1.0, 0.0).astype(BF16), jnp.where(c64 >= r64, 1.0, 0.0).astype(BF16))

    off_blk = ([], [])
    n_lvl = CHUNK.bit_length() - 1
    for lvl in range(n_lvl):
        s = 1 << lvl
        same2 = (ri // (2 * s)) == (lj // (2 * s))
        off_blk[0].append(jnp.logical_and(same2, (ri // s) == (lj // s) + 1))
        off_blk[1].append(jnp.logical_and(same2, (lj // s) == (ri // s) + 1))

    def stack_heads(x, lo_mask):
        return jnp.concatenate([jnp.where(lo_mask, x, 0.0), jnp.where(lo_mask, 0.0, x)], axis=0).astype(BF16)

    def chunk_rows(c):
        return pl.ds(c * CHUNK if isinstance(c, int) else pl.multiple_of(c * CHUNK, CHUNK), CHUNK)

    def chunk_load(c):
        rows = chunk_rows(c)
        return (q_ref[rows, :].astype(F32), k_ref[rows, :].astype(F32), v_ref[rows, :].astype(F32),
                [beta_s[d, rows, :] for d in range(2)], [g_s[d, rows, :] for d in range(2)])

    def group_a(loaded):
        nc = len(loaded)
        cd = [(i, d) for i in range(nc) for d in range(2)]
        units = [(i, h, d) for i in range(nc) for h in range(hps) for d in range(2)]
        qcs = [ld[0] * (DN_DK ** -0.5) for ld in loaded]
        kcs = [ld[1] for ld in loaded]
        vcs = [ld[2] for ld in loaded]
        kt_bd = {}
        for i in range(nc):
            for h in range(hps):
                kc = kcs[i][:, hp_sl[h]]
                ktt = jnp.transpose(jnp.concatenate([kc, kc], axis=0))
                kt_bd[i, h] = jnp.where(bd_mask, ktt, 0.0)
        gc = {(i, d): _bdot(tri3[d], jnp.concatenate(_split3(loaded[i][4][d]), axis=0)) for i, d in cd}
        rrow = {u: jnp.sum(jnp.where(diag_w, gc[u], 0.0), axis=0, keepdims=True) for u in cd}
        dec = {u: jnp.exp(jnp.minimum(gc[u] - rrow[u], 0.0)) for u in cd}
        eg = {u: jnp.exp(gc[u]) for u in cd}
        kbeta = {(i, d): kcs[i] * loaded[i][3][d] for i, d in cd}
        vbeta = {(i, d): vcs[i] * loaded[i][3][d] for i, d in cd}
        kbeg = {u: kbeta[u] * eg[u] for u in cd}
        m1 = {(i, h): _bdot(jnp.concatenate([qcs[i][:, hp_sl[h]], kbeta[i, 0][:, hp_sl[h]],
                                             kbeta[i, 1][:, hp_sl[h]]], axis=0).astype(BF16),
                            kt_bd[i, h].astype(BF16))
              for i in range(nc) for h in range(hps)}
        amat = {(i, h, d): jnp.where(strict[d], m1[i, h][CHUNK * (1 + d):CHUNK * (2 + d)] * dec[i, d][:, hp_sl[h]], 0.0)
                for i, h, d in units}
        o_st = {(i, h, d): stack_heads(jnp.where(off_blk[d][0], amat[i, h, d], 0.0), lo_half) for i, h, d in units}
        tmat = {(i, h, d): jnp.where(diag, 1.0, 0.0) - jnp.where(off_blk[d][0], amat[i, h, d], 0.0) for i, h, d in units}
        prod = {u: _bdot(amat[u].astype(BF16), o_st[u]) for u in units}
        wmat = {u: amat[u] - prod[u] for u in units}
        for lvl in range(1, n_lvl - 1):
            o_st = {(i, h, d): stack_heads(jnp.where(off_blk[d][lvl], wmat[i, h, d], 0.0), lo_half) for i, h, d in units}
            prod = {u: _bdot(jnp.concatenate([tmat[u], wmat[u]], axis=0).astype(BF16), o_st[u]) for u in units}
            tmat = {u: tmat[u] - prod[u][0:CHUNK] for u in units}
            wmat = {u: wmat[u] - prod[u][CHUNK:] for u in units}
        o_st = {(i, h, d): stack_heads(jnp.where(off_blk[d][n_lvl - 1], wmat[i, h, d], 0.0), lo_half)
                for i, h, d in units}
        prod = {u: _bdot(tmat[u].astype(BF16), o_st[u]) for u in units}
        tmat = {u: tmat[u] - prod[u] for u in units}
        x = {}
        for i, h, d in units:
            rhs = jnp.concatenate([vbeta[i, d][:, hp_sl[h]], kbeg[i, d][:, hp_sl[h]]], axis=1)
            x[i, h, d] = _bdot(tmat[i, h, d].astype(BF16), stack_heads(rhs, lo_half2))
        results = []
        for i in range(nc):
            out = []
            for d in range(2):
                u = lanes_cat([x[i, h, d][:, :LANES] for h in range(hps)])
                w = lanes_cat([x[i, h, d][:, LANES:] for h in range(hps)])
                qk = jnp.where(incl_w[d], lanes_cat([m1[i, h][0:CHUNK] for h in range(hps)]) * dec[i, d], 0.0)
                qg = qcs[i] * eg[i, d]
                glast = gc[i, d][CHUNK - 1:CHUNK, :] if d == 0 else gc[i, d][0:1, :]
                kmul = jnp.exp(glast - rrow[i, d])
                kd_bd = lanes_cat([kt_bd[i, h] * kmul[:, hp_sl[h]] for h in range(hps)])
                out.append((u,
                            jnp.concatenate([w, qg], axis=0).astype(BF16),
                            jnp.concatenate([qk, kd_bd], axis=0).astype(BF16),
                            jnp.broadcast_to(jnp.exp(glast), (SUBLANES, wl))))
            results.append(out)
        return results

    group = min(a_chunks, nt)

    def phase_a(gi, carry):
        cs = [gi * group + j for j in range(group)]
        loaded = [chunk_load(c) for c in cs]
        results = group_a(loaded)
        for c, res in zip(cs, results):
            for d in range(2):
                u_s[d, c], wq_s[d, c], lb_s[d, c], el_s[d, c] = res[d]
        return carry

    if nt // group <= 2:
        for gi in range(nt // group):
            phase_a(gi, 0)
    else:
        lax.fori_loop(0, nt // group, phase_a, 0)

    dh = [(j, d, h) for j in range(nb) for d in range(2) for h in range(hps)]
    if has_s0:
        zero = jnp.zeros((DN_DK, DN_DV), F32)

        def bd_state(j, d, h):
            top = jnp.concatenate([s0_ref[j, 0, d, 2 * h], zero], axis=1)
            bot = jnp.concatenate([zero, s0_ref[j, 0, d, 2 * h + 1]], axis=1)
            return jnp.concatenate([top, bot], axis=0)
        s_init = tuple(bd_state(j, d, h) for j, d, h in dh)
    else:
        s_init = tuple(jnp.zeros((LANES, LANES), F32) for _ in dh)

    def phase_b(s, states):
        jd = [(j, d) for j in range(nb) for d in range(2)]
        cs = {(j, d): j * n + (s if d == 0 else n - 1 - s) for j, d in jd}
        ld = {u: (wq_s[u[1], cs[u]], u_s[u[1], cs[u]], lb_s[u[1], cs[u]], el_s[u[1], cs[u], 0:1, :]) for u in jd}
        st = dict(zip(dh, states))
        ws = {(j, d, h): _bdot(ld[j, d][0][:, hp_sl[h]], st[j, d, h].astype(BF16)) for j, d, h in dh}
        vn = {(j, d, h): ld[j, d][1][:, hp_sl[h]] - ws[j, d, h][0:CHUNK] for j, d, h in dh}
        r2 = {(j, d, h): _bdot(ld[j, d][2][:, hp_sl[h]], stack_heads(vn[j, d, h], lo_half)) for j, d, h in dh}
        new_states = tuple(st[j, d, h] * ld[j, d][3][:, hp_sl[h]] + r2[j, d, h][CHUNK:] for j, d, h in dh)
        for j, d in jd:
            odir_s[d, chunk_rows(cs[j, d]), :] = lanes_cat([ws[j, d, h][CHUNK:] + r2[j, d, h][0:CHUNK]
                                                            for h in range(hps)])
        return new_states

    s_fin = dict(zip(dh, lax.fori_loop(0, n, phase_b, s_init)))

    for h in range(hps):
        o = odir_s[0, :, hp_sl[h]] + odir_s[1, :, hp_sl[h]]
        msq = _dot2_l(o * o, bd_ones) * (1.0 / DN_DV)
        o_ref[:, hp_sl[h]] = (o * lax.rsqrt(msq + EPS) * cst_ref[2:3, :]).astype(BF16)

    if emit_state:
        for j, d, h in dh:
            st_ref[j, 0, d, 2 * h] = s_fin[j, d, h][0:DN_DK, 0:DN_DV]
            st_ref[j, 0, d, 2 * h + 1] = s_fin[j, d, h][DN_DK:, DN_DV:]


def _deltanet(dqkv, ba, cst, batch, seq, nb, hps, a_chunks, s0=None, emit_state=False):
    assert batch % nb == 0
    rows = nb * seq
    nt = rows // CHUNK
    wl = hps * LANES
    groups = N_PAIRS // hps
    has_s0 = s0 is not None

    def col(cbase):
        return pl.BlockSpec((rows, wl), lambda b, hg: (b, cbase // wl + hg))

    st_spec = pl.BlockSpec((nb, 1, 2, 2 * hps, DN_DK, DN_DV), lambda b, hg: (b, 0, 0, hg, 0, 0))
    in_specs = [col(0), col(DN_WIDTH), col(2 * DN_WIDTH),
                pl.BlockSpec((rows, LANES), lambda b, hg: (b, 0)),
                pl.BlockSpec((SUBLANES, LANES), lambda b, hg: (0, 0))]
    args = [dqkv, dqkv, dqkv, ba, cst]
    if has_s0:
        in_specs.append(st_spec)
        args.append(s0)
    out_specs = [pl.BlockSpec((rows, wl), lambda b, hg: (b, hg))]
    out_shape = [jax.ShapeDtypeStruct((batch * seq, DN_HEADS * DN_DV), BF16)]
    if emit_state:
        out_specs.append(st_spec)
        out_shape.append(jax.ShapeDtypeStruct((batch, 1, 2, DN_HEADS, DN_DK, DN_DV), F32))
    scratch = [pltpu.VMEM((2, rows, wl), F32), pltpu.VMEM((2, rows, wl), F32),
               pltpu.VMEM((2, nt, CHUNK, wl), F32),
               pltpu.VMEM((2, nt, 2 * CHUNK, wl), BF16),
               pltpu.VMEM((2, nt, 3 * CHUNK, wl), BF16),
               pltpu.VMEM((2, nt, SUBLANES, wl), F32),
               pltpu.VMEM((2, rows, wl), F32)]
    res = pl.pallas_call(
        functools.partial(_dn_kernel, seq=seq, nb=nb, hps=hps, a_chunks=a_chunks, has_s0=has_s0,
                          emit_state=emit_state),
        grid=(batch // nb, groups),
        in_specs=in_specs,
        out_specs=out_specs,
        out_shape=out_shape,
        scratch_shapes=scratch,
        compiler_params=_params(2),
        name="deltanet_lat" if has_s0 else "deltanet_ctx",
    )(*args)
    return res


TM_F = 2 * TM


def _merge_ffn_kernel(x_ref, ona_ref, odn_ref, gz_ref, mod_ref, wpost1_ref, wpre2_ref, wpost2_ref,
                      wao_ref, wdo_ref, wout_ref, w1_ref, w2_ref, o_ref):
    mod = mod_ref[0]
    subs = [slice(s * TM, (s + 1) * TM) for s in range(TM_F // TM)]
    z_sl, gna_sl, gdn_sl = (slice(0, DN_WIDTH), slice(DN_WIDTH, DN_WIDTH + D_MODEL),
                            slice(DN_WIDTH + D_MODEL, GZ_W))
    odn = [odn_ref[sl, :].astype(F32) * _silu(gz_ref[sl, z_sl].astype(F32)) for sl in subs]
    br_na = [_bdot(ona_ref[sl, :], wao_ref[...]) for sl in subs]
    br_dn = [_bdot(t.astype(BF16), wdo_ref[...]) for t in odn]
    m = [_sigmoid(gz_ref[sl, gna_sl].astype(F32)) * a + _sigmoid(gz_ref[sl, gdn_sl].astype(F32)) * b
         for sl, a, b in zip(subs, br_na, br_dn)]
    y = [_bdot(t.astype(BF16), wout_ref[...]) for t in m]
    x1 = [x_ref[sl, :] + mod[2:3] * _rms_rows(t, wpost1_ref[...]) for sl, t in zip(subs, y)]
    h = [_rms_rows(t, wpre2_ref[...]) * (1.0 + mod[4:5]) + mod[3:4] for t in x1]
    f1 = [_bdot(t.astype(BF16), w1_ref[...]) for t in h]
    r = [jnp.maximum(t, 0.0) for t in f1]
    f = [_bdot((t * t).astype(BF16), w2_ref[...]) for t in r]
    for sl, a, b in zip(subs, x1, f):
        o_ref[sl, :] = a + mod[5:6] * _rms_rows(b, wpost2_ref[...])


def _merge_ffn(x2d, o_na, o_dn, gz, mod3, mod_row_of_tile, wpost1, wpre2, wpost2, w_ao, w_do, w_out, w1, w2):
    n_tok = x2d.shape[0]
    row = lambda i: (i, 0)
    const = lambda i: (0, 0)

    def resident(shape):
        return pl.BlockSpec(shape, const, pipeline_mode=pl.Buffered(1))

    return pl.pallas_call(
        _merge_ffn_kernel,
        grid=(n_tok // TM_F,),
        in_specs=[pl.BlockSpec((TM_F, D_MODEL), row),
                  pl.BlockSpec((TM_F, NA_WIDTH), row),
                  pl.BlockSpec((TM_F, DN_WIDTH), row),
                  pl.BlockSpec((TM_F, GZ_W), row),
                  pl.BlockSpec((1, N_MOD, D_MODEL), lambda i: (mod_row_of_tile(i), 0, 0)),
                  pl.BlockSpec((1, D_MODEL), const),
                  pl.BlockSpec((1, D_MODEL), const),
                  pl.BlockSpec((1, D_MODEL), const),
                  resident((NA_WIDTH, D_MODEL)),
                  resident((DN_WIDTH, D_MODEL)),
                  resident((D_MODEL, D_MODEL)),
                  resident((D_MODEL, D_FF)),
                  resident((D_FF, D_MODEL))],
        out_specs=pl.BlockSpec((TM_F, D_MODEL), row),
        out_shape=jax.ShapeDtypeStruct((n_tok, D_MODEL), F32),
        compiler_params=_params(1),
        name="merge_ffn",
    )(x2d, o_na, o_dn, gz, mod3, wpost1, wpre2, wpost2, w_ao, w_do, w_out, w1, w2)


def _pack_w_in(w):
    assert w.shape[1] == IN_COLS
    wb = w.astype(BF16)
    return wb, jnp.pad(wb[:, MAIN_COLS:], ((0, 0), (0, LANES - (IN_COLS - MAIN_COLS))))


def kernel(x_prompt, x_sample, c, cache_na_k, cache_na_v, state_delta, c_ctx, w_ada, b_ada, norm_pre1, norm_post1, norm_pre2, norm_post2, w_in, conv_w, a_log, dt_bias, dn_norm, na_rpb, w_ao, w_do, w_out, w_ff1, w_ff2):
    batch, seq, _ = x_prompt.shape
    dec_batch, dec_seq, _ = x_sample.shape
    depth = w_in.shape[0]
    assert depth == 1 and seq == TM and dec_seq % TM_F == 0 and (batch * seq) % TM_F == 0 and dec_batch < SUBLANES

    xp = x_prompt.reshape(batch * seq, D_MODEL)
    xs = x_sample.reshape(dec_batch * dec_seq, D_MODEL)
    l = 0

    cvecs = jnp.concatenate([c, c_ctx[None], jnp.zeros((SUBLANES - dec_batch - 1, D_MODEL), F32)], axis=0)
    mod3 = _adaln(cvecs, w_ada[l], b_ada[l]).reshape(SUBLANES, N_MOD, D_MODEL)
    ctx_row = lambda i: dec_batch
    lat_row = lambda i: i // (dec_seq // TMI)

    w_in_p = _pack_w_in(w_in[l])
    w_ao_b = w_ao[l].astype(BF16)
    w_do_b = w_do[l].astype(BF16)
    w_out_b = w_out[l].astype(BF16)
    w1_b = w_ff1[l].astype(BF16)
    w2_b = w_ff2[l].astype(BF16)
    wpre1 = norm_pre1[l].reshape(1, D_MODEL)
    wpost1 = norm_post1[l].reshape(1, D_MODEL)
    wpre2 = norm_pre2[l].reshape(1, D_MODEL)
    wpost2 = norm_post2[l].reshape(1, D_MODEL)

    conv_w8 = jnp.pad(conv_w[l], ((0, SUBLANES - CONV_W), (0, 0)))
    gate_rows = jnp.pad(jnp.stack([a_log[l].reshape(-1), dt_bias[l].reshape(-1)]),
                        ((0, 0), (2 * DN_HEADS, LANES - 4 * DN_HEADS)))
    cst = jnp.concatenate([gate_rows, jnp.tile(dn_norm[l], 2)[None],
                           jnp.zeros((SUBLANES - 3, LANES), F32)], axis=0)

    qkv_c, dqkv_c, gz_c, ba_c, new_k, new_v = _inproj(xp, mod3, ctx_row, wpre1, w_in_p, conv_w8, seq, batch)
    o_na_c = _ctx_attn(qkv_c, batch, seq, nb=4)
    o_dn_c, new_s = _deltanet(dqkv_c, ba_c, cst, batch, seq, nb=2, hps=4, a_chunks=4, emit_state=True)
    y_c = _merge_ffn(xp, o_na_c, o_dn_c, gz_c, mod3, ctx_row, wpost1, wpre2, wpost2,
                     w_ao_b, w_do_b, w_out_b, w1_b, w2_b)

    qkv_l, dqkv_l, gz_l, ba_l = _inproj(xs, mod3, lat_row, wpre1, w_in_p, conv_w8, dec_seq, dec_batch,
                                        rope_tabs=_rope_tables(dec_seq))
    past = cache_na_k.shape[3]
    ck_tm = jnp.transpose(cache_na_k[:, l], (0, 2, 1, 3)).reshape(dec_batch, past, NA_WIDTH)
    cv_tm = jnp.transpose(cache_na_v[:, l], (0, 2, 1, 3)).reshape(dec_batch, past, NA_WIDTH)
    o_na_l = _lat_attn(qkv_l, ck_tm, cv_tm, na_rpb[l], dec_batch, dec_seq)
    (o_dn_l,) = _deltanet(dqkv_l, ba_l, cst, dec_batch, dec_seq, nb=1, hps=4, a_chunks=2, s0=state_delta)
    y_l = _merge_ffn(xs, o_na_l, o_dn_l, gz_l, mod3, lambda i: i // (dec_seq // TM_F), wpost1, wpre2, wpost2,
                     w_ao_b, w_do_b, w_out_b, w1_b, w2_b)

    return (y_c.reshape(batch, seq, D_MODEL), y_l.reshape(dec_batch, dec_seq, D_MODEL),
            new_k, new_v, new_s)
```
